```python
import jax, jax.numpy as jnp
from jax import lax
import numpy as np

D_MODEL = 2048
BATCH = 2
SEQ = 4096
DEPTH = 4

HEAD_DIM = 128
CONV_CH = 512
CONV_WIDTH = 3
DSA_HEADS = 6
IDX_HEADS = 16
IDX_DIM = 64
TOPK_MAX = 256
DIL_PATTERNS = ((128, 1), (512, 4), (2048, 16))
DIL_GROUPS = 3
DIL_HEADS_PER_GROUP = 2
DIL_HEADS = DIL_GROUPS * DIL_HEADS_PER_GROUP
D_MIX = CONV_CH + DSA_HEADS * HEAD_DIM + DIL_HEADS * HEAD_DIM
SPLITS = (
    CONV_CH, CONV_CH, CONV_CH,
    DSA_HEADS * HEAD_DIM, HEAD_DIM, HEAD_DIM,
    IDX_HEADS * IDX_DIM, IDX_DIM, IDX_HEADS,
    DIL_HEADS * HEAD_DIM, DIL_HEADS * HEAD_DIM, DIL_HEADS * HEAD_DIM,
)
D_IN = sum(SPLITS)
D_FF = 5632
ROPE_THETA = 10000.0
RMS_EPS = 1e-6
Q_BLOCK = 128

kernel_name = "hybrid_conv_dsa_dilated_macaron"


def rmsnorm(x, g):
    x32 = x.astype(jnp.float32)
    y = x32 * lax.rsqrt(jnp.mean(x32 * x32, axis=-1, keepdims=True) + RMS_EPS)
    return (y * g.astype(jnp.float32)).astype(x.dtype)


def swiglu(x, w_gate, w_up, w_down):
    return (jax.nn.silu(x @ w_gate) * (x @ w_up)) @ w_down


def rope_tables(positions, dim):
    inv = 1.0 / (ROPE_THETA ** (jnp.arange(0, dim, 2, dtype=jnp.float32) / dim))
    ang = positions.astype(jnp.float32)[..., None] * inv
    return jnp.cos(ang), jnp.sin(ang)


def apply_rope(x, cos, sin):
    extra = x.ndim - 3
    shp = cos.shape[:2] + (1,) * extra + cos.shape[-1:]
    c, s = cos.reshape(shp), sin.reshape(shp)
    x1, x2 = jnp.split(x.astype(jnp.float32), 2, axis=-1)
    return jnp.concatenate([x1 * c - x2 * s, x2 * c + x1 * s], axis=-1).astype(x.dtype)


def to_blocks(a):
    b, s = a.shape[:2]
    return jnp.moveaxis(a.reshape((b, s // Q_BLOCK, Q_BLOCK) + a.shape[2:]), 1, 0)


def from_blocks(a):
    nb, b, qb, f = a.shape
    return jnp.moveaxis(a, 0, 1).reshape(b, nb * qb, f)


def short_conv_mixer(h, gate_b, gate_c, conv_w):
    u = gate_c * h
    s = u.shape[1]
    u_pad = jnp.pad(u, ((0, 0), (CONV_WIDTH - 1, 0), (0, 0)))
    y = conv_w[0] * u_pad[:, 0:s]
    for j in range(1, CONV_WIDTH):
        y = y + conv_w[j] * u_pad[:, j:j + s]
    return gate_b * y


def dsa_attention(q, k, v, q_idx, k_idx, w_idx):
    b, s = q.shape[:2]
    topk = min(TOPK_MAX, s // 4)
    scale = HEAD_DIM ** -0.5
    key_pos = jnp.arange(s)
    gather = jax.vmap(lambda kk, ii: kk[ii])

    def block(args):
        q_blk, qi_blk, w_blk, start = args
        t = start + jnp.arange(Q_BLOCK)
        causal = key_pos[None, :] <= t[:, None]
        logits = jnp.einsum('bqhd,bsd->bqhs', qi_blk, k_idx)
        score = jnp.einsum('bqh,bqhs->bqs', w_blk, jax.nn.relu(logits)).astype(jnp.float32)
        score = jnp.where(causal[None], score, -jnp.inf)
        _, sel = lax.top_k(score, topk)
        valid = sel <= t[None, :, None]
        k_sel = gather(k, sel)
        v_sel = gather(v, sel)
        att = jnp.einsum('bqhd,bqkd->bqhk', q_blk, k_sel).astype(jnp.float32) * scale
        att = jnp.where(valid[:, :, None, :], att, -jnp.inf)
        p = jax.nn.softmax(att, axis=-1)
        o = jnp.einsum('bqhk,bqkd->bqhd', p.astype(v.dtype), v_sel)
        return o.reshape(b, Q_BLOCK, DSA_HEADS * HEAD_DIM)

    starts = jnp.arange(s // Q_BLOCK) * Q_BLOCK
    out = lax.map(block, (to_blocks(q), to_blocks(q_idx), to_blocks(w_idx), starts))
    return from_blocks(out)


def dilated_attention(q, k, v):
    b, s = q.shape[:2]
    scale = HEAD_DIM ** -0.5
    k_groups = [k[:, :, g] for g in range(DIL_GROUPS)]
    v_groups = [v[:, :, g] for g in range(DIL_GROUPS)]

    def block(args):
        q_blk, start = args
        t = start + jnp.arange(Q_BLOCK)
        outs, lses = [], []
        for g, (win, dil) in enumerate(DIL_PATTERNS):
            taps = jnp.arange(win // dil + 1) * dil
            idx = t[:, None] - taps[None, :]
            valid = idx >= 0
            idx = jnp.maximum(idx, 0)
            kg = k_groups[g][:, idx]
            vg = v_groups[g][:, idx]
            sc = jnp.einsum('bqhd,bqjhd->bqhj', q_blk[:, :, g], kg).astype(jnp.float32) * scale
            sc = jnp.where(valid[None, :, None, :], sc, -jnp.inf)
            lse = jax.nn.logsumexp(sc, axis=-1)
            p = jnp.exp(sc - lse[..., None])
            outs.append(jnp.einsum('bqhj,bqjhd->bqhd', p.astype(vg.dtype), vg))
            lses.append(lse)
        o = jnp.stack(outs, axis=2)
        alpha = jax.nn.softmax(jnp.stack(lses, axis=2), axis=2)
        o = o * alpha[..., None].astype(o.dtype)
        return o.reshape(b, Q_BLOCK, DIL_HEADS * HEAD_DIM)

    starts = jnp.arange(s // Q_BLOCK) * Q_BLOCK
    out = lax.map(block, (to_blocks(q), starts))
    return from_blocks(out)


def hybrid_mixer(xn, w_in, conv_w, w_out, cos_h, sin_h, cos_i, sin_i):
    b, s, _ = xn.shape
    proj = xn @ w_in
    offsets = np.cumsum(np.array(SPLITS))[:-1].tolist()
    (h, g_b, g_c, q_b, k_b, v_b, q_i, k_i, w_i, q_c, k_c, v_c) = jnp.split(proj, offsets, axis=-1)

    y_a = short_conv_mixer(h, g_b, g_c, conv_w)

    q_b = apply_rope(q_b.reshape(b, s, DSA_HEADS, HEAD_DIM), cos_h, sin_h)
    k_b = apply_rope(k_b, cos_h, sin_h)
    q_i = apply_rope(q_i.reshape(b, s, IDX_HEADS, IDX_DIM), cos_i, sin_i) * (IDX_DIM ** -0.5)
    k_i = apply_rope(k_i, cos_i, sin_i)
    w_i = w_i * (IDX_HEADS ** -0.5)
    y_b = dsa_attention(q_b, k_b, v_b, q_i, k_i, w_i)

    gshape = (b, s, DIL_GROUPS, DIL_HEADS_PER_GROUP, HEAD_DIM)
    q_c = apply_rope(q_c.reshape(gshape), cos_h, sin_h)
    k_c = apply_rope(k_c.reshape(gshape), cos_h, sin_h)
    y_c = dilated_attention(q_c, k_c, v_c.reshape(gshape))

    return jnp.concatenate([y_a, y_b, y_c], axis=-1) @ w_out


def setup_inputs(seed: int = 0) -> dict:
    key = jax.random.key(seed)
    ks = jax.random.split(key, 16)
    nrm = lambda k, shp, fan: jax.random.normal(k, shp, jnp.float32) * (fan ** -0.5)
    gain = lambda k, shp: 1.0 + 0.02 * jax.random.normal(k, shp, jnp.float32)
    x = jax.random.normal(ks[0], (BATCH, SEQ, D_MODEL), jnp.float32)
    offs = jax.random.randint(ks[1], (BATCH, 1), 0, 1024, dtype=jnp.int32)
    positions = offs + jnp.arange(SEQ, dtype=jnp.int32)[None, :]
    return {
        "x": x,
        "positions": positions,
        "norm_ffn1": gain(ks[2], (DEPTH, D_MODEL)),
        "ffn1_gate": nrm(ks[3], (DEPTH, D_MODEL, D_FF), D_MODEL),
        "ffn1_up": nrm(ks[4], (DEPTH, D_MODEL, D_FF), D_MODEL),
        "ffn1_down": nrm(ks[5], (DEPTH, D_FF, D_MODEL), D_FF),
        "norm_mix": gain(ks[6], (DEPTH, D_MODEL)),
        "w_in": nrm(ks[7], (DEPTH, D_MODEL, D_IN), D_MODEL),
        "conv_w": nrm(ks[8], (DEPTH, CONV_WIDTH, CONV_CH), CONV_WIDTH),
        "w_out": nrm(ks[9], (DEPTH, D_MIX, D_MODEL), D_MIX),
        "norm_ffn2": gain(ks[10], (DEPTH, D_MODEL)),
        "ffn2_gate": nrm(ks[11], (DEPTH, D_MODEL, D_FF), D_MODEL),
        "ffn2_up": nrm(ks[12], (DEPTH, D_MODEL, D_FF), D_MODEL),
        "ffn2_down": nrm(ks[13], (DEPTH, D_FF, D_MODEL), D_FF),
        "norm_final": gain(ks[14], (D_MODEL,)),
    }


def reference(x, positions, norm_ffn1, ffn1_gate, ffn1_up, ffn1_down, norm_mix, w_in, conv_w, w_out,
              norm_ffn2, ffn2_gate, ffn2_up, ffn2_down, norm_final):
    cos_h, sin_h = rope_tables(positions, HEAD_DIM)
    cos_i, sin_i = rope_tables(positions, IDX_DIM)
    for i in range(DEPTH):
        x = x + 0.5 * swiglu(rmsnorm(x, norm_ffn1[i]), ffn1_gate[i], ffn1_up[i], ffn1_down[i])
        x = x + hybrid_mixer(rmsnorm(x, norm_mix[i]), w_in[i], conv_w[i], w_out[i],
                             cos_h, sin_h, cos_i, sin_i)
        x = x + 0.5 * swiglu(rmsnorm(x, norm_ffn2[i]), ffn2_gate[i], ffn2_up[i], ffn2_down[i])
    return rmsnorm(x, norm_final)
```

```python
import functools

import jax
import jax.numpy as jnp
from jax import lax
from jax.experimental import pallas as pl
from jax.experimental.pallas import tpu as pltpu

F32 = jnp.float32
BF16 = jnp.bfloat16
I32 = jnp.int32

HEAD_DIM = 128
CONV_CH = 512
CONV_WIDTH = 3
DSA_HEADS = 6
IDX_HEADS = 16
IDX_DIM = 64
TOPK_MAX = 256
DIL_PATTERNS = ((128, 1), (512, 4), (2048, 16))
DIL_HEADS_PER_GROUP = 2
DIL_HEADS = len(DIL_PATTERNS) * DIL_HEADS_PER_GROUP
ROPE_THETA = 10000.0
RMS_EPS = 1e-6

LANES = 128
VMEM_LIMIT = 56 * 1024 * 1024
NEG = -1e30
INT_MIN = -2 ** 31

SEG_CONV = (0, 3 * CONV_CH)
SEG_QB = (SEG_CONV[1], SEG_CONV[1] + DSA_HEADS * HEAD_DIM)
SEG_KV = (SEG_QB[1], SEG_QB[1] + 2 * HEAD_DIM)
SEG_QI = (SEG_KV[1], SEG_KV[1] + IDX_HEADS * IDX_DIM)
SEG_KI = (SEG_QI[1], SEG_QI[1] + LANES)
SEG_WI = (SEG_KI[1], SEG_KI[1] + LANES)
SEG_QC = (SEG_WI[1], SEG_WI[1] + DIL_HEADS * HEAD_DIM)
SEG_KC = (SEG_QC[1], SEG_QC[1] + DIL_HEADS * HEAD_DIM)
SEG_VC = (SEG_KC[1], SEG_KC[1] + DIL_HEADS * HEAD_DIM)
D_IN_ALIGNED = SEG_VC[1]


def _params(*sem):
    return pltpu.CompilerParams(dimension_semantics=sem, vmem_limit_bytes=VMEM_LIMIT)


def _rms(x, g):
    ms = jnp.mean(x * x, axis=-1, keepdims=True)
    return x * lax.rsqrt(ms + RMS_EPS) * g


def _ffn_up_body(x_ref, g_ref, wg_ref, wu_ref, h_ref, xn_ref):
    @pl.when(pl.program_id(1) == 0)
    def _():
        xn_ref[...] = _rms(x_ref[...], g_ref[...]).astype(BF16)

    xn = xn_ref[...]
    a = jnp.dot(xn, wg_ref[...], preferred_element_type=F32)
    b = jnp.dot(xn, wu_ref[...], preferred_element_type=F32)
    h_ref[...] = (a * jax.nn.sigmoid(a) * b).astype(BF16)


def _ffn_up(x, g, wg, wu, tm, tn):
    t, d = x.shape
    f = wg.shape[1]
    return pl.pallas_call(
        _ffn_up_body,
        grid=(t // tm, f // tn),
        in_specs=[
            pl.BlockSpec((tm, d), lambda m, n: (m, 0)),
            pl.BlockSpec((1, d), lambda m, n: (0, 0)),
            pl.BlockSpec((d, tn), lambda m, n: (0, n)),
            pl.BlockSpec((d, tn), lambda m, n: (0, n)),
        ],
        out_specs=pl.BlockSpec((tm, tn), lambda m, n: (m, n)),
        out_shape=jax.ShapeDtypeStruct((t, f), BF16),
        scratch_shapes=[pltpu.VMEM((tm, d), BF16)],
        compiler_params=_params("parallel", "arbitrary"),
        name="ffn_up",
    )(x, g, wg, wu)


def _ffn_down_body(h_ref, w_ref, x_ref, o_ref):
    y = jnp.dot(h_ref[...], w_ref[...], preferred_element_type=F32)
    o_ref[...] = x_ref[...] + 0.5 * y


def _ffn_down(h, wd, x, tm, tn):
    t, f = h.shape
    d = wd.shape[1]
    return pl.pallas_call(
        _ffn_down_body,
        grid=(t // tm, d // tn),
        in_specs=[
            pl.BlockSpec((tm, f), lambda m, n: (m, 0)),
            pl.BlockSpec((f, tn), lambda m, n: (0, n)),
            pl.BlockSpec((tm, tn), lambda m, n: (m, n)),
        ],
        out_specs=pl.BlockSpec((tm, tn), lambda m, n: (m, n)),
        out_shape=jax.ShapeDtypeStruct((t, d), F32),
        compiler_params=_params("parallel", "arbitrary"),
        name="ffn_down",
    )(h, wd, x)


def _rope_table_body(pos_ref, inv_ref, sgn_ref, cos_ref, sin_ref):
    ang = pos_ref[...] * inv_ref[...]
    cos_ref[...] = jnp.cos(ang)
    sin_ref[...] = jnp.sin(ang) * sgn_ref[...]


def _rope_tables(pos, dim, tm):
    t = pos.shape[0]
    half = dim // 2
    inv = 1.0 / (ROPE_THETA ** (jnp.arange(0, dim, 2, dtype=F32) / dim))
    lane = jnp.arange(LANES)
    inv_l = inv[lane % half].reshape(1, LANES)
    sgn_l = jnp.where(lane % dim < half, -1.0, 1.0).astype(F32).reshape(1, LANES)
    row = pl.BlockSpec((1, LANES), lambda m: (0, 0))
    tab = pl.BlockSpec((tm, LANES), lambda m: (m, 0))
    return pl.pallas_call(
        _rope_table_body,
        grid=(t // tm,),
        in_specs=[pl.BlockSpec((tm, 1), lambda m: (m, 0)), row, row],
        out_specs=[tab, tab],
        out_shape=[jax.ShapeDtypeStruct((t, LANES), F32)] * 2,
        compiler_params=_params("parallel"),
        name="rope_tables",
    )(pos, inv_l, sgn_l)


def _rope128(x, cos, sin):
    return x * cos + pltpu.roll(x, HEAD_DIM // 2, 1) * sin


def _rope64(x, cos, sin, lo_half):
    partner = jnp.where(lo_half, pltpu.roll(x, LANES - IDX_DIM // 2, 1), pltpu.roll(x, IDX_DIM // 2, 1))
    return x * cos + partner * sin


def _in_proj_body(seq_tiles, x_ref, g_ref, w_ref, cw_ref, ch_ref, sh_ref, ci_ref, si_ref,
                  ya_ref, qb_ref, kb_ref, vb_ref, qi_ref, ki_ref, wi_ref, qc_ref, kc_ref, vc_ref,
                  u_ref):
    tm = x_ref.shape[0]
    xn = _rms(x_ref[...], g_ref[...]).astype(BF16)

    def proj(seg):
        return jnp.dot(xn, w_ref[:, seg[0]:seg[1]], preferred_element_type=F32)

    p = proj(SEG_CONV)
    h, gate_b, gate_c = p[:, :CONV_CH], p[:, CONV_CH:2 * CONV_CH], p[:, 2 * CONV_CH:]
    u = gate_c * h

    @pl.when(pl.program_id(0) % seq_tiles == 0)
    def _():
        u_ref[0:8, :] = jnp.zeros((8, CONV_CH), F32)

    u_ref[8:8 + tm, :] = u
    cw = cw_ref[...]
    y = cw[2:3, :] * u + cw[1:2, :] * u_ref[7:7 + tm, :] + cw[0:1, :] * u_ref[6:6 + tm, :]
    u_ref[0:8, :] = u[tm - 8:, :]
    ya_ref[...] = (gate_b * y).astype(BF16)

    ch, sh, ci, si = ch_ref[...], sh_ref[...], ci_ref[...], si_ref[...]
    scale = HEAD_DIM ** -0.5

    def rope_heads(p, n, mul):
        return jnp.concatenate(
            [_rope128(p[:, j * LANES:(j + 1) * LANES], ch, sh) * mul for j in range(n)], axis=1)

    qb_ref[...] = rope_heads(proj(SEG_QB), DSA_HEADS, scale).astype(BF16)
    p = proj(SEG_KV)
    kb_ref[...] = _rope128(p[:, :HEAD_DIM], ch, sh).astype(BF16)
    vb_ref[...] = p[:, HEAD_DIM:].astype(BF16)

    lo_half = lax.broadcasted_iota(I32, (tm, LANES), 1) % IDX_DIM < IDX_DIM // 2
    p = proj(SEG_QI)
    qi_ref[...] = jnp.concatenate(
        [_rope64(p[:, j * LANES:(j + 1) * LANES], ci, si, lo_half) * (IDX_DIM ** -0.5)
         for j in range(IDX_HEADS * IDX_DIM // LANES)], axis=1).astype(BF16)
    ki_ref[...] = _rope64(proj(SEG_KI), ci, si, lo_half)[:, :IDX_DIM].astype(BF16)
    wi_ref[...] = proj(SEG_WI) * (IDX_HEADS ** -0.5)

    qc_ref[...] = rope_heads(proj(SEG_QC), DIL_HEADS, scale).astype(BF16)
    kc_ref[...] = rope_heads(proj(SEG_KC), DIL_HEADS, 1.0).astype(BF16)
    vc_ref[...] = proj(SEG_VC).astype(BF16)


def _in_proj(x, g, w, cw, tabs, seq, tm):
    t, d = x.shape
    n = w.shape[1]
    row = lambda width: pl.BlockSpec((tm, width), lambda m: (m, 0))
    const = lambda shape: pl.BlockSpec(shape, lambda m: (0, 0))
    widths = [CONV_CH, DSA_HEADS * HEAD_DIM, HEAD_DIM, HEAD_DIM, IDX_HEADS * IDX_DIM, IDX_DIM, LANES,
              DIL_HEADS * HEAD_DIM, DIL_HEADS * HEAD_DIM, DIL_HEADS * HEAD_DIM]
    dtypes = [BF16, BF16, BF16, BF16, BF16, BF16, F32, BF16, BF16, BF16]
    return pl.pallas_call(
        functools.partial(_in_proj_body, seq // tm),
        grid=(t // tm,),
        in_specs=[row(d), const((1, d)),
                  pl.BlockSpec((d, n), lambda m: (0, 0), pipeline_mode=pl.Buffered(1)),
                  const((CONV_WIDTH, CONV_CH)), row(LANES), row(LANES), row(LANES), row(LANES)],
        out_specs=[row(wd) for wd in widths],
        out_shape=[jax.ShapeDtypeStruct((t, wd), dt) for wd, dt in zip(widths, dtypes)],
        scratch_shapes=[pltpu.VMEM((tm + 8, CONV_CH), F32)],
        compiler_params=_params("arbitrary"),
        name="in_proj",
    )(x, g, w, cw, *tabs)


def _dsa_body(topk, qb_ref, kb_ref, vb_ref, qi_ref, ki_ref, wi_ref, o_ref, keys_ref, qt_ref):
    tq = qb_ref.shape[1]
    seq = kb_ref.shape[1]
    i = pl.program_id(1)
    nch = i + 1

    def chunk(c):
        return pl.ds(pl.multiple_of(c * tq, tq), tq)

    qt_ref[...] = qi_ref[0].astype(F32).T.astype(BF16)
    wt = wi_ref[0].T
    qpos = lax.broadcasted_iota(I32, (tq, tq), 1) + i * tq
    krow = lax.broadcasted_iota(I32, (tq, tq), 0)

    def score_chunk(c, carry):
        kc = ki_ref[0, chunk(c), :]
        acc = jnp.zeros((tq, tq), F32)
        for h in range(IDX_HEADS):
            lg = jnp.dot(kc, qt_ref[h * IDX_DIM:(h + 1) * IDX_DIM, :], preferred_element_type=F32)
            acc = acc + jnp.maximum(lg, 0.0) * wt[h:h + 1, :]
        bits = pltpu.bitcast(acc, I32)
        key = bits ^ ((bits >> 31) & 0x7FFFFFFF)
        keys_ref[chunk(c), :] = jnp.where(krow + c * tq <= qpos, key, INT_MIN)
        return carry

    lax.fori_loop(0, nch, score_chunk, 0)

    def count(pred):
        def body(c, acc):
            hit = pred(keys_ref[chunk(c), :], krow + c * tq).astype(I32)
            return acc + jnp.sum(hit.reshape(tq // 8, 8, tq), axis=0)
        acc = lax.fori_loop(0, nch, body, jnp.zeros((8, tq), I32))
        return jnp.sum(acc, axis=0, keepdims=True)

    def search_bit(b, t_u):
        cand = t_u | (jnp.int32(1) << (31 - b))
        cand_s = cand ^ INT_MIN
        return jnp.where(count(lambda k, _: k >= cand_s) >= topk, cand, t_u)

    t_u = lax.fori_loop(0, 32, search_bit, jnp.zeros((1, tq), I32))
    thr = jnp.maximum(t_u ^ INT_MIN, INT_MIN + 1)
    n_ge = count(lambda k, _: k >= thr)

    idx_bits = seq.bit_length() - 1

    def tie_bound():
        need = topk - count(lambda k, _: k > thr)
        def bit(b, j):
            cand = j | (jnp.int32(1) << (idx_bits - 1 - b))
            below = count(lambda k, idx: (k == thr) & (idx < cand))
            return jnp.where(below < need, cand, j)
        j = lax.fori_loop(0, idx_bits, bit, jnp.zeros((1, tq), I32))
        return jnp.where(n_ge > topk, j, seq)

    bound = lax.cond(jnp.max(n_ge) > topk, tie_bound, lambda: jnp.full((1, tq), seq, I32))

    nh = DSA_HEADS
    qs = jnp.concatenate([qb_ref[0, :, h * HEAD_DIM:(h + 1) * HEAD_DIM] for h in range(nh)], axis=0)

    def attend(c, carry):
        m, l, acc = carry
        kblk = keys_ref[chunk(c), :]
        sel = (kblk > thr) | ((kblk == thr) & (krow + c * tq <= bound))
        bias = jnp.where(sel, 0.0, NEG).T
        s = lax.dot_general(qs, kb_ref[0, chunk(c), :], (((1,), (1,)), ((), ())),
                            preferred_element_type=F32)
        s = (s.reshape(nh, tq, tq) + bias[None]).reshape(nh * tq, tq)
        m_new = jnp.maximum(m, jnp.max(s, axis=-1, keepdims=True))
        alpha = jnp.exp(m - m_new)
        p = jnp.exp(s - m_new)
        l = alpha * l + jnp.sum(p, axis=-1, keepdims=True)
        acc = alpha * acc + jnp.dot(p.astype(BF16), vb_ref[0, chunk(c), :], preferred_element_type=F32)
        return m_new, l, acc

    init = (jnp.full((nh * tq, 1), NEG, F32), jnp.zeros((nh * tq, 1), F32),
            jnp.zeros((nh * tq, HEAD_DIM), F32))
    _, l, acc = lax.fori_loop(0, nch, attend, init)
    out = acc / l
    o_ref[0] = jnp.concatenate([out[h * tq:(h + 1) * tq] for h in range(nh)], axis=1).astype(BF16)


def _dsa(qb, kb, vb, qi, ki, wi, tq):
    b, s, _ = qb.shape
    topk = min(TOPK_MAX, s // 4)
    blk = lambda width: pl.BlockSpec((1, tq, width), lambda bb, i: (bb, i, 0))
    full = lambda width: pl.BlockSpec((1, s, width), lambda bb, i: (bb, 0, 0))
    return pl.pallas_call(
        functools.partial(_dsa_body, topk),
        grid=(b, s // tq),
        in_specs=[blk(DSA_HEADS * HEAD_DIM), full(HEAD_DIM), full(HEAD_DIM),
                  blk(IDX_HEADS * IDX_DIM), full(IDX_DIM), blk(LANES)],
        out_specs=blk(DSA_HEADS * HEAD_DIM),
        out_shape=jax.ShapeDtypeStruct((b, s, DSA_HEADS * HEAD_DIM), BF16),
        scratch_shapes=[pltpu.VMEM((s, tq), I32), pltpu.VMEM((IDX_HEADS * IDX_DIM, tq), BF16)],
        compiler_params=_params("parallel", "arbitrary"),
        name="dsa",
    )(qb, kb, vb, qi, ki, wi)


def _dilated_body(q_ref, k_ref, v_ref, o_ref):
    tq = q_ref.shape[1]
    t0 = pl.program_id(1) * tq
    hg = DIL_HEADS_PER_GROUP
    stats = []
    for g, (win, dil) in enumerate(DIL_PATTERNS):
        nk = min(win + tq, k_ref.shape[1])
        k0 = pl.multiple_of(jnp.clip(t0 - win, 0, k_ref.shape[1] - nk), tq)
        delta = (t0 - k0) + lax.broadcasted_iota(I32, (tq, nk), 0) - lax.broadcasted_iota(I32, (tq, nk), 1)
        valid = (delta >= 0) & (delta <= win) & ((delta & (dil - 1)) == 0)
        for j in range(hg):
            cols = slice((g * hg + j) * HEAD_DIM, (g * hg + j + 1) * HEAD_DIM)
            s = lax.dot_general(q_ref[0, :, cols], k_ref[0, pl.ds(k0, nk), cols],
                                (((1,), (1,)), ((), ())), preferred_element_type=F32)
            s = jnp.where(valid, s, NEG)
            m = jnp.max(s, axis=-1, keepdims=True)
            p = jnp.exp(s - m)
            l = jnp.sum(p, axis=-1, keepdims=True)
            acc = jnp.dot(p.astype(BF16), v_ref[0, pl.ds(k0, nk), cols], preferred_element_type=F32)
            stats.append((m, l, acc))
    outs = [None] * DIL_HEADS
    for j in range(hg):
        sl = [stats[g * hg + j] for g in range(len(DIL_PATTERNS))]
        m_all = functools.reduce(jnp.maximum, [m for m, _, _ in sl])
        ws = [jnp.exp(m - m_all) for m, _, _ in sl]
        den = sum(w * l for w, (_, l, _) in zip(ws, sl))
        for g, (w, (_, _, acc)) in enumerate(zip(ws, sl)):
            outs[g * hg + j] = acc * (w / den)
    o_ref[0] = jnp.concatenate(outs, axis=1).astype(BF16)


def _dilated(qc, kc, vc, tq):
    b, s, w = qc.shape
    assert all(win % tq == 0 and dil & (dil - 1) == 0 for win, dil in DIL_PATTERNS)
    blk = pl.BlockSpec((1, tq, w), lambda bb, i: (bb, i, 0))
    full = pl.BlockSpec((1, s, w), lambda bb, i: (bb, 0, 0))
    return pl.pallas_call(
        _dilated_body,
        grid=(b, s // tq),
        in_specs=[blk, full, full],
        out_specs=blk,
        out_shape=jax.ShapeDtypeStruct((b, s, w), BF16),
        compiler_params=_params("parallel", "arbitrary"),
        name="dilated",
    )(qc, kc, vc)


def _out_proj_body(ya_ref, yb_ref, yc_ref, w_ref, x_ref, o_ref):
    na, nb = ya_ref.shape[1], yb_ref.shape[1]
    y = jnp.dot(ya_ref[...], w_ref[0:na, :], preferred_element_type=F32)
    y = y + jnp.dot(yb_ref[...], w_ref[na:na + nb, :], preferred_element_type=F32)
    y = y + jnp.dot(yc_ref[...], w_ref[na + nb:, :], preferred_element_type=F32)
    o_ref[...] = x_ref[...] + y


def _out_proj(ya, yb, yc, w, x, tm, tn):
    t, d = x.shape
    row = lambda a: pl.BlockSpec((tm, a.shape[1]), lambda m, n: (m, 0))
    return pl.pallas_call(
        _out_proj_body,
        grid=(t // tm, d // tn),
        in_specs=[row(ya), row(yb), row(yc),
                  pl.BlockSpec((w.shape[0], tn), lambda m, n: (0, n)),
                  pl.BlockSpec((tm, tn), lambda m, n: (m, n))],
        out_specs=pl.BlockSpec((tm, tn), lambda m, n: (m, n)),
        out_shape=jax.ShapeDtypeStruct((t, d), F32),
        compiler_params=_params("parallel", "arbitrary"),
        name="out_proj",
    )(ya, yb, yc, w, x)


def _final_norm_body(x_ref, g_ref, o_ref):
    o_ref[...] = _rms(x_ref[...], g_ref[...])


def _final_norm(x, g, tm):
    t, d = x.shape
    return pl.pallas_call(
        _final_norm_body,
        grid=(t // tm,),
        in_specs=[pl.BlockSpec((tm, d), lambda m: (m, 0)), pl.BlockSpec((1, d), lambda m: (0, 0))],
        out_specs=pl.BlockSpec((tm, d), lambda m: (m, 0)),
        out_shape=jax.ShapeDtypeStruct((t, d), F32),
        compiler_params=_params("parallel"),
        name="final_norm",
    )(x, g)


def _align_w_in(w):
    d = w.shape[0]
    o_ki = SEG_QI[1]
    o_wi = o_ki + IDX_DIM
    o_qc = o_wi + IDX_HEADS
    zeros = lambda n: jnp.zeros((d, n), w.dtype)
    return jnp.concatenate(
        [w[:, :o_wi], zeros(LANES - IDX_DIM), w[:, o_wi:o_qc], zeros(LANES - IDX_HEADS), w[:, o_qc:]],
        axis=1).astype(BF16)


def _tile(n, want):
    while n % want:
        want //= 2
    return want


def kernel(x, positions, norm_ffn1, ffn1_gate, ffn1_up, ffn1_down, norm_mix, w_in, conv_w, w_out,
           norm_ffn2, ffn2_gate, ffn2_up, ffn2_down, norm_final):
    b, s, d = x.shape
    t = b * s
    depth = w_in.shape[0]
    tm_big = _tile(t, 1024)
    tm_proj = _tile(s, 512)
    tq = _tile(s, 128)

    pos = positions.astype(F32).reshape(t, 1)
    tabs = (*_rope_tables(pos, HEAD_DIM, tm_big), *_rope_tables(pos, IDX_DIM, tm_big))

    def ffn(xf, g, wg, wu, wd):
        h = _ffn_up(xf, g.reshape(1, d), wg.astype(BF16), wu.astype(BF16), tm_big, 512)
        return _ffn_down(h, wd.astype(BF16), xf, tm_big, 512)

    xf = x.reshape(t, d)
    for i in range(depth):
        xf = ffn(xf, norm_ffn1[i], ffn1_gate[i], ffn1_up[i], ffn1_down[i])
        ya, qb, kb, vb, qi, ki, wi, qc, kc, vc = _in_proj(
            xf, norm_mix[i].reshape(1, d), _align_w_in(w_in[i]), conv_w[i], tabs, s, tm_proj)
        r3 = lambda a: a.reshape(b, s, a.shape[-1])
        yb = _dsa(r3(qb), r3(kb), r3(vb), r3(qi), r3(ki), r3(wi), tq)
        yc = _dilated(r3(qc), r3(kc), r3(vc), tq)
        xf = _out_proj(ya, yb.reshape(t, -1), yc.reshape(t, -1), w_out[i].astype(BF16), xf, tm_big, 512)
        xf = ffn(xf, norm_ffn2[i], ffn2_gate[i], ffn2_up[i], ffn2_down[i])
    return _final_norm(xf, norm_final.reshape(1, d), tm_big).reshape(b, s, d)
```

```python
import functools

import jax
import jax.numpy as jnp
from jax import lax
from jax.experimental import pallas as pl
from jax.experimental.pallas import tpu as pltpu

F32 = jnp.float32
BF16 = jnp.bfloat16
I32 = jnp.int32

HEAD_DIM = 128
CONV_CH = 512
CONV_WIDTH = 3
DSA_HEADS = 6
IDX_HEADS = 16
IDX_DIM = 64
TOPK_MAX = 256
DIL_PATTERNS = ((128, 1), (512, 4), (2048, 16))
DIL_HEADS_PER_GROUP = 2
DIL_HEADS = len(DIL_PATTERNS) * DIL_HEADS_PER_GROUP
ROPE_THETA = 10000.0
RMS_EPS = 1e-6

LANES = 128
VMEM_LIMIT = 56 * 1024 * 1024
NEG = -1e30
INT_MIN = -2 ** 31
LOG2E = 1.4426950408889634

SEG_CONV = (0, 3 * CONV_CH)
SEG_QB = (SEG_CONV[1], SEG_CONV[1] + DSA_HEADS * HEAD_DIM)
SEG_KV = (SEG_QB[1], SEG_QB[1] + 2 * HEAD_DIM)
SEG_QI = (SEG_KV[1], SEG_KV[1] + IDX_HEADS * IDX_DIM)
SEG_KI = (SEG_QI[1], SEG_QI[1] + LANES)
SEG_WI = (SEG_KI[1], SEG_KI[1] + LANES)
SEG_QC = (SEG_WI[1], SEG_WI[1] + DIL_HEADS * HEAD_DIM)
SEG_KC = (SEG_QC[1], SEG_QC[1] + DIL_HEADS * HEAD_DIM)
SEG_VC = (SEG_KC[1], SEG_KC[1] + DIL_HEADS * HEAD_DIM)
D_IN_ALIGNED = SEG_VC[1]


def _params(*sem):
    return pltpu.CompilerParams(dimension_semantics=sem, vmem_limit_bytes=VMEM_LIMIT)


def _rms(x, g):
    ms = jnp.mean(x * x, axis=-1, keepdims=True)
    return x * lax.rsqrt(ms + RMS_EPS) * g


def _ffn_up_body(x_ref, g_ref, wg_ref, wu_ref, h_ref, xn_ref):
    @pl.when(pl.program_id(1) == 0)
    def _():
        xn_ref[...] = _rms(x_ref[...], g_ref[...]).astype(BF16)

    xn = xn_ref[...]
    a = jnp.dot(xn, wg_ref[...], preferred_element_type=F32)
    b = jnp.dot(xn, wu_ref[...], preferred_element_type=F32)
    h_ref[...] = (a * jax.nn.sigmoid(a) * b).astype(BF16)


def _ffn_up(x, g, wg, wu, tm, tn):
    t, d = x.shape
    f = wg.shape[1]
    return pl.pallas_call(
        _ffn_up_body,
        grid=(t // tm, f // tn),
        in_specs=[
            pl.BlockSpec((tm, d), lambda m, n: (m, 0)),
            pl.BlockSpec((1, d), lambda m, n: (0, 0)),
            pl.BlockSpec((d, tn), lambda m, n: (0, n)),
            pl.BlockSpec((d, tn), lambda m, n: (0, n)),
        ],
        out_specs=pl.BlockSpec((tm, tn), lambda m, n: (m, n)),
        out_shape=jax.ShapeDtypeStruct((t, f), BF16),
        scratch_shapes=[pltpu.VMEM((tm, d), BF16)],
        compiler_params=_params("parallel", "arbitrary"),
        name="ffn_up",
    )(x, g, wg, wu)


def _ffn_down_body(h_ref, w_ref, x_ref, o_ref):
    y = jnp.dot(h_ref[...], w_ref[...], preferred_element_type=F32)
    o_ref[...] = x_ref[...] + 0.5 * y


def _ffn_down(h, wd, x, tm, tn):
    t, f = h.shape
    d = wd.shape[1]
    return pl.pallas_call(
        _ffn_down_body,
        grid=(t // tm, d // tn),
        in_specs=[
            pl.BlockSpec((tm, f), lambda m, n: (m, 0)),
            pl.BlockSpec((f, tn), lambda m, n: (0, n)),
            pl.BlockSpec((tm, tn), lambda m, n: (m, n)),
        ],
        out_specs=pl.BlockSpec((tm, tn), lambda m, n: (m, n)),
        out_shape=jax.ShapeDtypeStruct((t, d), F32),
        compiler_params=_params("parallel", "arbitrary"),
        name="ffn_down",
    )(h, wd, x)


def _rope_table_body(pos_ref, inv_ref, sgn_ref, cos_ref, sin_ref):
    ang = pos_ref[...] * inv_ref[...]
    cos_ref[...] = jnp.cos(ang)
    sin_ref[...] = jnp.sin(ang) * sgn_ref[...]


def _rope_tables(pos, dim, tm):
    t = pos.shape[0]
    half = dim // 2
    inv = 1.0 / (ROPE_THETA ** (jnp.arange(0, dim, 2, dtype=F32) / dim))
    lane = jnp.arange(LANES)
    inv_l = inv[lane % half].reshape(1, LANES)
    sgn_l = jnp.where(lane % dim < half, -1.0, 1.0).astype(F32).reshape(1, LANES)
    row = pl.BlockSpec((1, LANES), lambda m: (0, 0))
    tab = pl.BlockSpec((tm, LANES), lambda m: (m, 0))
    return pl.pallas_call(
        _rope_table_body,
        grid=(t // tm,),
        in_specs=[pl.BlockSpec((tm, 1), lambda m: (m, 0)), row, row],
        out_specs=[tab, tab],
        out_shape=[jax.ShapeDtypeStruct((t, LANES), F32)] * 2,
        compiler_params=_params("parallel"),
        name="rope_tables",
    )(pos, inv_l, sgn_l)


def _rope128(x, cos, sin):
    return x * cos + pltpu.roll(x, HEAD_DIM // 2, 1) * sin


def _rope64(x, cos, sin, lo_half):
    partner = jnp.where(lo_half, pltpu.roll(x, LANES - IDX_DIM // 2, 1), pltpu.roll(x, IDX_DIM // 2, 1))
    return x * cos + partner * sin


def _in_proj_body(seq_tiles, x_ref, g_ref, w_ref, cw_ref, ch_ref, sh_ref, ci_ref, si_ref,
                  ya_ref, qb_ref, kb_ref, vb_ref, qi_ref, ki_ref, wi_ref, qc_ref, kc_ref, vc_ref,
                  u_ref):
    tm = x_ref.shape[0]
    xn = _rms(x_ref[...], g_ref[...]).astype(BF16)

    def proj(seg):
        return jnp.dot(xn, w_ref[:, seg[0]:seg[1]], preferred_element_type=F32)

    p = proj(SEG_CONV)
    h, gate_b, gate_c = p[:, :CONV_CH], p[:, CONV_CH:2 * CONV_CH], p[:, 2 * CONV_CH:]
    u = gate_c * h

    @pl.when(pl.program_id(0) % seq_tiles == 0)
    def _():
        u_ref[0:8, :] = jnp.zeros((8, CONV_CH), F32)

    u_ref[8:8 + tm, :] = u
    cw = cw_ref[...]
    y = cw[2:3, :] * u + cw[1:2, :] * u_ref[7:7 + tm, :] + cw[0:1, :] * u_ref[6:6 + tm, :]
    u_ref[0:8, :] = u[tm - 8:, :]
    ya_ref[...] = (gate_b * y).astype(BF16)

    ch, sh, ci, si = ch_ref[...], sh_ref[...], ci_ref[...], si_ref[...]
    scale = HEAD_DIM ** -0.5 * LOG2E

    def rope_heads(p, n, mul):
        return jnp.concatenate(
            [_rope128(p[:, j * LANES:(j + 1) * LANES], ch, sh) * mul for j in range(n)], axis=1)

    qb_ref[...] = rope_heads(proj(SEG_QB), DSA_HEADS, scale).astype(BF16)
    p = proj(SEG_KV)
    kb_ref[...] = _rope128(p[:, :HEAD_DIM], ch, sh).astype(BF16)
    vb_ref[...] = p[:, HEAD_DIM:].astype(BF16)

    lo_half = lax.broadcasted_iota(I32, (tm, LANES), 1) % IDX_DIM < IDX_DIM // 2
    p = proj(SEG_QI)
    qi_ref[...] = jnp.concatenate(
        [_rope64(p[:, j * LANES:(j + 1) * LANES], ci, si, lo_half) * (IDX_DIM ** -0.5)
         for j in range(IDX_HEADS * IDX_DIM // LANES)], axis=1).astype(BF16)
    ki_ref[...] = _rope64(proj(SEG_KI), ci, si, lo_half)[:, :IDX_DIM].astype(BF16)
    wi_ref[...] = proj(SEG_WI) * (IDX_HEADS ** -0.5)

    qc_ref[...] = rope_heads(proj(SEG_QC), DIL_HEADS, scale).astype(BF16)
    kc_ref[...] = rope_heads(proj(SEG_KC), DIL_HEADS, 1.0).astype(BF16)
    vc_ref[...] = proj(SEG_VC).astype(BF16)


def _in_proj(x, g, w, cw, tabs, seq, tm):
    t, d = x.shape
    n = w.shape[1]
    row = lambda width: pl.BlockSpec((tm, width), lambda m: (m, 0))
    const = lambda shape: pl.BlockSpec(shape, lambda m: (0, 0))
    widths = [CONV_CH, DSA_HEADS * HEAD_DIM, HEAD_DIM, HEAD_DIM, IDX_HEADS * IDX_DIM, IDX_DIM, LANES,
              DIL_HEADS * HEAD_DIM, DIL_HEADS * HEAD_DIM, DIL_HEADS * HEAD_DIM]
    dtypes = [BF16, BF16, BF16, BF16, BF16, BF16, F32, BF16, BF16, BF16]
    return pl.pallas_call(
        functools.partial(_in_proj_body, seq // tm),
        grid=(t // tm,),
        in_specs=[row(d), const((1, d)),
                  pl.BlockSpec((d, n), lambda m: (0, 0), pipeline_mode=pl.Buffered(1)),
                  const((CONV_WIDTH, CONV_CH)), row(LANES), row(LANES), row(LANES), row(LANES)],
        out_specs=[row(wd) for wd in widths],
        out_shape=[jax.ShapeDtypeStruct((t, wd), dt) for wd, dt in zip(widths, dtypes)],
        scratch_shapes=[pltpu.VMEM((tm + 8, CONV_CH), F32)],
        compiler_params=_params("arbitrary"),
        name="in_proj",
    )(x, g, w, cw, *tabs)


def _dsa_body(topk, qb_ref, kb_ref, vb_ref, qi_ref, ki_ref, wi_ref, o_ref,
              keys_ref, qt_ref, qbt_ref, vt_ref, acc_ref):
    tq = qb_ref.shape[1]
    seq = kb_ref.shape[1]
    nh = DSA_HEADS
    i = pl.program_id(1)
    nch = i + 1

    def chunk(c):
        return pl.ds(pl.multiple_of(c * tq, tq), tq)

    def transpose_bf16(a):
        return a.astype(F32).T.astype(BF16)

    @pl.when(i == 0)
    def _():
        for c in range(seq // tq):
            vt_ref[c] = transpose_bf16(vb_ref[0, c * tq:(c + 1) * tq, :])

    qt_ref[...] = transpose_bf16(qi_ref[0])
    for h in range(nh):
        qbt_ref[:, h * tq:(h + 1) * tq] = transpose_bf16(qb_ref[0, :, h * HEAD_DIM:(h + 1) * HEAD_DIM])
    wt = wi_ref[0].T
    qpos = lax.broadcasted_iota(I32, (tq, tq), 1) + i * tq
    krow = lax.broadcasted_iota(I32, (tq, tq), 0)

    def score_chunk(c, carry):
        kc = ki_ref[0, chunk(c), :]
        acc = jnp.zeros((tq, tq), F32)
        for h in range(IDX_HEADS):
            lg = jnp.dot(kc, qt_ref[h * IDX_DIM:(h + 1) * IDX_DIM, :], preferred_element_type=F32)
            acc = acc + jnp.maximum(lg, 0.0) * wt[h:h + 1, :]
        bits = pltpu.bitcast(acc, I32)
        key = bits ^ ((bits >> 31) & 0x7FFFFFFF)
        keys_ref[chunk(c), :] = jnp.where(krow + c * tq <= qpos, key, INT_MIN)
        return carry

    lax.fori_loop(0, nch, score_chunk, 0)

    def count(pred):
        def body(c, acc):
            hit = pred(keys_ref[chunk(c), :], krow + c * tq).astype(I32)
            return acc + jnp.sum(hit.reshape(tq // 8, 8, tq), axis=0)
        acc = lax.fori_loop(0, nch, body, jnp.zeros((8, tq), I32))
        return jnp.sum(acc, axis=0, keepdims=True)

    def search_bit(b, t_u):
        cand = t_u | (jnp.int32(1) << (31 - b))
        cand_s = cand ^ INT_MIN
        return jnp.where(count(lambda k, _: k >= cand_s) >= topk, cand, t_u)

    t_u = lax.fori_loop(0, 32, search_bit, jnp.zeros((1, tq), I32))
    thr = jnp.maximum(t_u ^ INT_MIN, INT_MIN + 1)
    n_ge = count(lambda k, _: k >= thr)

    idx_bits = seq.bit_length() - 1

    def tie_bound():
        need = topk - count(lambda k, _: k > thr)
        def bit(b, j):
            cand = j | (jnp.int32(1) << (idx_bits - 1 - b))
            below = count(lambda k, idx: (k == thr) & (idx < cand))
            return jnp.where(below < need, cand, j)
        j = lax.fori_loop(0, idx_bits, bit, jnp.zeros((1, tq), I32))
        return jnp.where(n_ge > topk, j, seq)

    bound = lax.cond(jnp.max(n_ge) > topk, tie_bound, lambda: jnp.full((1, tq), seq, I32))

    acc_ref[...] = jnp.zeros(acc_ref.shape, F32)

    def attend(c, carry):
        m, l = carry
        kblk = keys_ref[chunk(c), :]
        sel = (kblk > thr) | ((kblk == thr) & (krow + c * tq <= bound))
        bias = jnp.where(sel, 0.0, NEG)
        st = jnp.dot(kb_ref[0, chunk(c), :], qbt_ref[...], preferred_element_type=F32)
        st = st + jnp.concatenate([bias] * nh, axis=1)
        m_new = jnp.maximum(m, jnp.max(st, axis=0, keepdims=True))
        alpha = jnp.exp2(m - m_new)
        p = jnp.exp2(st - m_new)
        l = alpha * l + jnp.sum(p, axis=0, keepdims=True)
        acc_ref[...] = acc_ref[...] * alpha + jnp.dot(vt_ref[c], p.astype(BF16), preferred_element_type=F32)
        return m_new, l

    init = (jnp.full((1, nh * tq), NEG, F32), jnp.zeros((1, nh * tq), F32))
    _, l = lax.fori_loop(0, nch, attend, init)
    out = acc_ref[...] / l
    for h in range(nh):
        o_ref[0, :, h * HEAD_DIM:(h + 1) * HEAD_DIM] = out[:, h * tq:(h + 1) * tq].T.astype(BF16)


def _dsa(qb, kb, vb, qi, ki, wi, tq):
    b, s, _ = qb.shape
    topk = min(TOPK_MAX, s // 4)
    blk = lambda width: pl.BlockSpec((1, tq, width), lambda bb, i: (bb, i, 0))
    full = lambda width: pl.BlockSpec((1, s, width), lambda bb, i: (bb, 0, 0))
    return pl.pallas_call(
        functools.partial(_dsa_body, topk),
        grid=(b, s // tq),
        in_specs=[blk(DSA_HEADS * HEAD_DIM), full(HEAD_DIM), full(HEAD_DIM),
                  blk(IDX_HEADS * IDX_DIM), full(IDX_DIM), blk(LANES)],
        out_specs=blk(DSA_HEADS * HEAD_DIM),
        out_shape=jax.ShapeDtypeStruct((b, s, DSA_HEADS * HEAD_DIM), BF16),
        scratch_shapes=[pltpu.VMEM((s, tq), I32),
                        pltpu.VMEM((IDX_HEADS * IDX_DIM, tq), BF16),
                        pltpu.VMEM((HEAD_DIM, DSA_HEADS * tq), BF16),
                        pltpu.VMEM((s // tq, HEAD_DIM, tq), BF16),
                        pltpu.VMEM((HEAD_DIM, DSA_HEADS * tq), F32)],
        compiler_params=_params("parallel", "arbitrary"),
        name="dsa",
    )(qb, kb, vb, qi, ki, wi)


def _dilated_body(q_ref, k_ref, v_ref, o_ref):
    tq = q_ref.shape[1]
    t0 = pl.program_id(1) * tq
    hg = DIL_HEADS_PER_GROUP
    stats = []
    for g, (win, dil) in enumerate(DIL_PATTERNS):
        nk = min(win + tq, k_ref.shape[1])
        k0 = pl.multiple_of(jnp.clip(t0 - win, 0, k_ref.shape[1] - nk), tq)
        delta = (t0 - k0) + lax.broadcasted_iota(I32, (tq, nk), 0) - lax.broadcasted_iota(I32, (tq, nk), 1)
        valid = (delta >= 0) & (delta <= win) & ((delta & (dil - 1)) == 0)
        for j in range(hg):
            cols = slice((g * hg + j) * HEAD_DIM, (g * hg + j + 1) * HEAD_DIM)
            s = lax.dot_general(q_ref[0, :, cols], k_ref[0, pl.ds(k0, nk), cols],
                                (((1,), (1,)), ((), ())), preferred_element_type=F32)
            s = jnp.where(valid, s, NEG)
            m = jnp.max(s, axis=-1, keepdims=True)
            p = jnp.exp2(s - m)
            l = jnp.sum(p, axis=-1, keepdims=True)
            acc = jnp.dot(p.astype(BF16), v_ref[0, pl.ds(k0, nk), cols], preferred_element_type=F32)
            stats.append((m, l, acc))
    outs = [None] * DIL_HEADS
    for j in range(hg):
        sl = [stats[g * hg + j] for g in range(len(DIL_PATTERNS))]
        m_all = functools.reduce(jnp.maximum, [m for m, _, _ in sl])
        ws = [jnp.exp2(m - m_all) for m, _, _ in sl]
        den = sum(w * l for w, (_, l, _) in zip(ws, sl))
        for g, (w, (_, _, acc)) in enumerate(zip(ws, sl)):
            outs[g * hg + j] = acc * (w / den)
    o_ref[0] = jnp.concatenate(outs, axis=1).astype(BF16)


def _dilated(qc, kc, vc, tq):
    b, s, w = qc.shape
    assert all(win % tq == 0 and dil & (dil - 1) == 0 for win, dil in DIL_PATTERNS)
    blk = pl.BlockSpec((1, tq, w), lambda bb, i: (bb, i, 0))
    full = pl.BlockSpec((1, s, w), lambda bb, i: (bb, 0, 0))
    return pl.pallas_call(
        _dilated_body,
        grid=(b, s // tq),
        in_specs=[blk, full, full],
        out_specs=blk,
        out_shape=jax.ShapeDtypeStruct((b, s, w), BF16),
        compiler_params=_params("parallel", "arbitrary"),
        name="dilated",
    )(qc, kc, vc)


def _out_proj_body(ya_ref, yb_ref, yc_ref, w_ref, x_ref, o_ref):
    na, nb = ya_ref.shape[1], yb_ref.shape[1]
    y = jnp.dot(ya_ref[...], w_ref[0:na, :], preferred_element_type=F32)
    y = y + jnp.dot(yb_ref[...], w_ref[na:na + nb, :], preferred_element_type=F32)
    y = y + jnp.dot(yc_ref[...], w_ref[na + nb:, :], preferred_element_type=F32)
    o_ref[...] = x_ref[...] + y


def _out_proj(ya, yb, yc, w, x, tm, tn):
    t, d = x.shape
    row = lambda a: pl.BlockSpec((tm, a.shape[1]), lambda m, n: (m, 0))
    return pl.pallas_call(
        _out_proj_body,
        grid=(t // tm, d // tn),
        in_specs=[row(ya), row(yb), row(yc),
                  pl.BlockSpec((w.shape[0], tn), lambda m, n: (0, n)),
                  pl.BlockSpec((tm, tn), lambda m, n: (m, n))],
        out_specs=pl.BlockSpec((tm, tn), lambda m, n: (m, n)),
        out_shape=jax.ShapeDtypeStruct((t, d), F32),
        compiler_params=_params("parallel", "arbitrary"),
        name="out_proj",
    )(ya, yb, yc, w, x)


def _final_norm_body(x_ref, g_ref, o_ref):
    o_ref[...] = _rms(x_ref[...], g_ref[...])


def _final_norm(x, g, tm):
    t, d = x.shape
    return pl.pallas_call(
        _final_norm_body,
        grid=(t // tm,),
        in_specs=[pl.BlockSpec((tm, d), lambda m: (m, 0)), pl.BlockSpec((1, d), lambda m: (0, 0))],
        out_specs=pl.BlockSpec((tm, d), lambda m: (m, 0)),
        out_shape=jax.ShapeDtypeStruct((t, d), F32),
        compiler_params=_params("parallel"),
        name="final_norm",
    )(x, g)


def _align_w_in(w):
    d = w.shape[0]
    o_ki = SEG_QI[1]
    o_wi = o_ki + IDX_DIM
    o_qc = o_wi + IDX_HEADS
    zeros = lambda n: jnp.zeros((d, n), w.dtype)
    return jnp.concatenate(
        [w[:, :o_wi], zeros(LANES - IDX_DIM), w[:, o_wi:o_qc], zeros(LANES - IDX_HEADS), w[:, o_qc:]],
        axis=1).astype(BF16)


def _tile(n, want):
    while n % want:
        want //= 2
    return want


def kernel(x, positions, norm_ffn1, ffn1_gate, ffn1_up, ffn1_down, norm_mix, w_in, conv_w, w_out,
           norm_ffn2, ffn2_gate, ffn2_up, ffn2_down, norm_final):
    b, s, d = x.shape
    t = b * s
    depth = w_in.shape[0]
    tm_big = _tile(t, 1024)
    tm_proj = _tile(s, 512)
    tq = _tile(s, 128)
    tq_dsa = _tile(s, 256)

    pos = positions.astype(F32).reshape(t, 1)
    tabs = (*_rope_tables(pos, HEAD_DIM, tm_big), *_rope_tables(pos, IDX_DIM, tm_big))

    def ffn(xf, g, wg, wu, wd):
        h = _ffn_up(xf, g.reshape(1, d), wg.astype(BF16), wu.astype(BF16), tm_big, 512)
        return _ffn_down(h, wd.astype(BF16), xf, tm_big, 512)

    xf = x.reshape(t, d)
    for i in range(depth):
        xf = ffn(xf, norm_ffn1[i], ffn1_gate[i], ffn1_up[i], ffn1_down[i])
        ya, qb, kb, vb, qi, ki, wi, qc, kc, vc = _in_proj(
            xf, norm_mix[i].reshape(1, d), _align_w_in(w_in[i]), conv_w[i], tabs, s, tm_proj)
        r3 = lambda a: a.reshape(b, s, a.shape[-1])
        yb = _dsa(r3(qb), r3(kb), r3(vb), r3(qi), r3(ki), r3(wi), tq_dsa)
        yc = _dilated(r3(qc), r3(kc), r3(vc), tq)
        xf = _out_proj(ya, yb.reshape(t, -1), yc.reshape(t, -1), w_out[i].astype(BF16), xf, tm_big, 512)
        xf = ffn(xf, norm_ffn2[i], ffn2_gate[i], ffn2_up[i], ffn2_down[i])
    return _final_norm(xf, norm_final.reshape(1, d), tm_big).reshape(b, s, d)
```

```python
import functools

import jax
import jax.numpy as jnp
from jax import lax
from jax.experimental import pallas as pl
from jax.experimental.pallas import tpu as pltpu

F32 = jnp.float32
BF16 = jnp.bfloat16
I32 = jnp.int32
I16 = jnp.int16
HALF16 = 1 << 15

HEAD_DIM = 128
CONV_CH = 512
CONV_WIDTH = 3
DSA_HEADS = 6
IDX_HEADS = 16
IDX_DIM = 64
TOPK_MAX = 256
DIL_PATTERNS = ((128, 1), (512, 4), (2048, 16))
DIL_HEADS_PER_GROUP = 2
DIL_HEADS = len(DIL_PATTERNS) * DIL_HEADS_PER_GROUP
ROPE_THETA = 10000.0
RMS_EPS = 1e-6

LANES = 128
VMEM_LIMIT = 56 * 1024 * 1024
NEG = -1e30
INT_MIN = -2 ** 31
LOG2E = 1.4426950408889634

SEG_CONV = (0, 3 * CONV_CH)
SEG_QB = (SEG_CONV[1], SEG_CONV[1] + DSA_HEADS * HEAD_DIM)
SEG_KV = (SEG_QB[1], SEG_QB[1] + 2 * HEAD_DIM)
SEG_QI = (SEG_KV[1], SEG_KV[1] + IDX_HEADS * IDX_DIM)
SEG_KI = (SEG_QI[1], SEG_QI[1] + LANES)
SEG_WI = (SEG_KI[1], SEG_KI[1] + LANES)
SEG_QC = (SEG_WI[1], SEG_WI[1] + DIL_HEADS * HEAD_DIM)
SEG_KC = (SEG_QC[1], SEG_QC[1] + DIL_HEADS * HEAD_DIM)
SEG_VC = (SEG_KC[1], SEG_KC[1] + DIL_HEADS * HEAD_DIM)
D_IN_ALIGNED = SEG_VC[1]


def _params(*sem):
    return pltpu.CompilerParams(dimension_semantics=sem, vmem_limit_bytes=VMEM_LIMIT)


def _rms(x, g):
    ms = jnp.mean(x * x, axis=-1, keepdims=True)
    return x * lax.rsqrt(ms + RMS_EPS) * g


def _ffn_up_body(x_ref, g_ref, wg_ref, wu_ref, h_ref, xn_ref):
    @pl.when(pl.program_id(1) == 0)
    def _():
        xn_ref[...] = _rms(x_ref[...], g_ref[...]).astype(BF16)

    xn = xn_ref[...]
    a = jnp.dot(xn, wg_ref[...].astype(BF16), preferred_element_type=F32)
    b = jnp.dot(xn, wu_ref[...].astype(BF16), preferred_element_type=F32)
    h_ref[...] = (a * jax.nn.sigmoid(a) * b).astype(BF16)


def _layer_block(layer, rows, cols):
    return pl.BlockSpec((None, rows, cols), lambda m, n: (layer, 0, n))


def _ffn_up(x, g, wg, wu, layer, tm, tn):
    t, d = x.shape
    f = wg.shape[2]
    return pl.pallas_call(
        _ffn_up_body,
        grid=(t // tm, f // tn),
        in_specs=[
            pl.BlockSpec((tm, d), lambda m, n: (m, 0)),
            pl.BlockSpec((None, 1, d), lambda m, n: (layer, 0, 0)),
            _layer_block(layer, d, tn),
            _layer_block(layer, d, tn),
        ],
        out_specs=pl.BlockSpec((tm, tn), lambda m, n: (m, n)),
        out_shape=jax.ShapeDtypeStruct((t, f), BF16),
        scratch_shapes=[pltpu.VMEM((tm, d), BF16)],
        compiler_params=_params("parallel", "arbitrary"),
        name="ffn_up",
    )(x, g, wg, wu)


def _ffn_down_body(h_ref, w_ref, x_ref, o_ref):
    y = jnp.dot(h_ref[...], w_ref[...].astype(BF16), preferred_element_type=F32)
    o_ref[...] = x_ref[...] + 0.5 * y


def _ffn_down(h, wd, x, layer, tm, tn):
    t, f = h.shape
    d = wd.shape[2]
    return pl.pallas_call(
        _ffn_down_body,
        grid=(t // tm, d // tn),
        in_specs=[
            pl.BlockSpec((tm, f), lambda m, n: (m, 0)),
            _layer_block(layer, f, tn),
            pl.BlockSpec((tm, tn), lambda m, n: (m, n)),
        ],
        out_specs=pl.BlockSpec((tm, tn), lambda m, n: (m, n)),
        out_shape=jax.ShapeDtypeStruct((t, d), F32),
        compiler_params=_params("parallel", "arbitrary"),
        name="ffn_down",
    )(h, wd, x)


def _rope_table_body(pos_ref, inv_ref, sgn_ref, cos_ref, sin_ref):
    ang = pos_ref[...] * inv_ref[...]
    cos_ref[...] = jnp.cos(ang)
    sin_ref[...] = jnp.sin(ang) * sgn_ref[...]


def _rope_tables(pos, dim, tm):
    t = pos.shape[0]
    half = dim // 2
    inv = 1.0 / (ROPE_THETA ** (jnp.arange(0, dim, 2, dtype=F32) / dim))
    lane = jnp.arange(LANES)
    inv_l = inv[lane % half].reshape(1, LANES)
    sgn_l = jnp.where(lane % dim < half, -1.0, 1.0).astype(F32).reshape(1, LANES)
    row = pl.BlockSpec((1, LANES), lambda m: (0, 0))
    tab = pl.BlockSpec((tm, LANES), lambda m: (m, 0))
    return pl.pallas_call(
        _rope_table_body,
        grid=(t // tm,),
        in_specs=[pl.BlockSpec((tm, 1), lambda m: (m, 0)), row, row],
        out_specs=[tab, tab],
        out_shape=[jax.ShapeDtypeStruct((t, LANES), F32)] * 2,
        compiler_params=_params("parallel"),
        name="rope_tables",
    )(pos, inv_l, sgn_l)


def _rope128(x, cos, sin):
    return x * cos + pltpu.roll(x, HEAD_DIM // 2, 1) * sin


def _rope64(x, cos, sin, lo_half):
    partner = jnp.where(lo_half, pltpu.roll(x, LANES - IDX_DIM // 2, 1), pltpu.roll(x, IDX_DIM // 2, 1))
    return x * cos + partner * sin


def _in_proj_body(seq_tiles, x_ref, g_ref, w_ref, cw_ref, ch_ref, sh_ref, ci_ref, si_ref,
                  ya_ref, qb_ref, kb_ref, vb_ref, qi_ref, ki_ref, wi_ref, qc_ref, kc_ref, vc_ref,
                  u_ref):
    tm = x_ref.shape[0]
    xn = _rms(x_ref[...], g_ref[...]).astype(BF16)

    def proj(seg):
        return jnp.dot(xn, w_ref[:, seg[0]:seg[1]], preferred_element_type=F32)

    p = proj(SEG_CONV)
    h, gate_b, gate_c = p[:, :CONV_CH], p[:, CONV_CH:2 * CONV_CH], p[:, 2 * CONV_CH:]
    u = gate_c * h

    @pl.when(pl.program_id(0) % seq_tiles == 0)
    def _():
        u_ref[0:8, :] = jnp.zeros((8, CONV_CH), F32)

    u_ref[8:8 + tm, :] = u
    cw = cw_ref[...]
    y = cw[2:3, :] * u + cw[1:2, :] * u_ref[7:7 + tm, :] + cw[0:1, :] * u_ref[6:6 + tm, :]
    u_ref[0:8, :] = u[tm - 8:, :]
    ya_ref[...] = (gate_b * y).astype(BF16)

    ch, sh, ci, si = ch_ref[...], sh_ref[...], ci_ref[...], si_ref[...]
    scale = HEAD_DIM ** -0.5 * LOG2E

    def rope_heads(p, n, mul):
        return jnp.concatenate(
            [_rope128(p[:, j * LANES:(j + 1) * LANES], ch, sh) * mul for j in range(n)], axis=1)

    qb_ref[...] = rope_heads(proj(SEG_QB), DSA_HEADS, scale).astype(BF16)
    p = proj(SEG_KV)
    kb_ref[...] = _rope128(p[:, :HEAD_DIM], ch, sh).astype(BF16)
    vb_ref[...] = p[:, HEAD_DIM:].astype(BF16)

    lo_half = lax.broadcasted_iota(I32, (tm, LANES), 1) % IDX_DIM < IDX_DIM // 2
    p = proj(SEG_QI)
    qi_ref[...] = jnp.concatenate(
        [_rope64(p[:, j * LANES:(j + 1) * LANES], ci, si, lo_half) * (IDX_DIM ** -0.5)
         for j in range(IDX_HEADS * IDX_DIM // LANES)], axis=1).astype(BF16)
    ki_ref[...] = _rope64(proj(SEG_KI), ci, si, lo_half)[:, :IDX_DIM].astype(BF16)
    wi_ref[...] = proj(SEG_WI) * (IDX_HEADS ** -0.5)

    qc_ref[...] = rope_heads(proj(SEG_QC), DIL_HEADS, scale).astype(BF16)
    kc_ref[...] = rope_heads(proj(SEG_KC), DIL_HEADS, 1.0).astype(BF16)
    vc_ref[...] = proj(SEG_VC).astype(BF16)


def _in_proj(x, g, w, cw, tabs, seq, tm):
    t, d = x.shape
    n = w.shape[1]
    row = lambda width: pl.BlockSpec((tm, width), lambda m: (m, 0))
    const = lambda shape: pl.BlockSpec(shape, lambda m: (0, 0))
    widths = [CONV_CH, DSA_HEADS * HEAD_DIM, HEAD_DIM, HEAD_DIM, IDX_HEADS * IDX_DIM, IDX_DIM, LANES,
              DIL_HEADS * HEAD_DIM, DIL_HEADS * HEAD_DIM, DIL_HEADS * HEAD_DIM]
    dtypes = [BF16, BF16, BF16, BF16, BF16, BF16, F32, BF16, BF16, BF16]
    return pl.pallas_call(
        functools.partial(_in_proj_body, seq // tm),
        grid=(t // tm,),
        in_specs=[row(d), const((1, d)),
                  pl.BlockSpec((d, n), lambda m: (0, 0), pipeline_mode=pl.Buffered(1)),
                  const((CONV_WIDTH, CONV_CH)), row(LANES), row(LANES), row(LANES), row(LANES)],
        out_specs=[row(wd) for wd in widths],
        out_shape=[jax.ShapeDtypeStruct((t, wd), dt) for wd, dt in zip(widths, dtypes)],
        scratch_shapes=[pltpu.VMEM((tm + 8, CONV_CH), F32)],
        compiler_params=_params("arbitrary"),
        name="in_proj",
    )(x, g, w, cw, *tabs)


def _dsa_body(topk, qb_ref, kb_ref, vb_ref, qi_ref, ki_ref, wi_ref, o_ref,
              keys_ref, hi_ref, lo_ref, lo2_ref, qt_ref, qbt_ref, vt_ref, acc_ref):
    tq = qb_ref.shape[1]
    seq = kb_ref.shape[1]
    nh = DSA_HEADS
    i = pl.program_id(1)
    nch = i + 1

    def chunk(c):
        return pl.ds(pl.multiple_of(c * tq, tq), tq)

    def transpose_bf16(a):
        return a.astype(F32).T.astype(BF16)

    @pl.when(i == 0)
    def _():
        for c in range(seq // tq):
            vt_ref[c] = transpose_bf16(vb_ref[0, c * tq:(c + 1) * tq, :])

    qt_ref[...] = transpose_bf16(qi_ref[0])
    for h in range(nh):
        qbt_ref[:, h * tq:(h + 1) * tq] = transpose_bf16(qb_ref[0, :, h * HEAD_DIM:(h + 1) * HEAD_DIM])
    wt = wi_ref[0].T
    qpos = lax.broadcasted_iota(I32, (tq, tq), 1) + i * tq
    krow = lax.broadcasted_iota(I32, (tq, tq), 0)

    def score_chunk(c, carry):
        kc = ki_ref[0, chunk(c), :]
        acc = jnp.zeros((tq, tq), F32)
        for h in range(IDX_HEADS):
            lg = jnp.dot(kc, qt_ref[h * IDX_DIM:(h + 1) * IDX_DIM, :], preferred_element_type=F32)
            acc = acc + jnp.maximum(lg, 0.0) * wt[h:h + 1, :]
        bits = pltpu.bitcast(acc, I32)
        key = bits ^ ((bits >> 31) & 0x7FFFFFFF)
        key = jnp.where(krow + c * tq <= qpos, key, INT_MIN)
        keys_ref[chunk(c), :] = key
        hi_ref[chunk(c), :] = (key >> 16).astype(I16)
        lo_ref[chunk(c), :] = ((key & 0xFFFF) - HALF16).astype(I16)
        return carry

    lax.fori_loop(0, nch, score_chunk, 0)

    def count(pred):
        def body(c, acc):
            hit = pred(keys_ref[chunk(c), :], krow + c * tq).astype(I32)
            return acc + jnp.sum(hit.reshape(tq // 8, 8, tq), axis=0)
        acc = lax.fori_loop(0, nch, body, jnp.zeros((8, tq), I32))
        return jnp.sum(acc, axis=0, keepdims=True)

    def count16(ref, pred):
        def body(c, acc):
            hit = jnp.where(pred(ref[chunk(c), :]), jnp.ones((), BF16), jnp.zeros((), BF16))
            parts = [hit[r * 16:(r + 1) * 16] for r in range(tq // 16)]
            while len(parts) > 1:
                parts = [parts[j] + parts[j + 1] for j in range(0, len(parts), 2)]
            return acc + parts[0].astype(F32)
        acc = lax.fori_loop(0, nch, body, jnp.zeros((16, tq), F32))
        return jnp.sum(acc, axis=0, keepdims=True)

    def kth_largest16(ref, k):
        def search_bit(b, t_u):
            cand = t_u | (jnp.int32(1) << (15 - b))
            cand16 = (cand - HALF16).astype(I16)
            return jnp.where(count16(ref, lambda v: v >= cand16) >= k, cand, t_u)
        return lax.fori_loop(0, 16, search_bit, jnp.zeros((1, tq), I32)) - HALF16

    t_hi = kth_largest16(hi_ref, float(topk))
    t_hi16 = t_hi.astype(I16)
    need_lo = float(topk) - count16(hi_ref, lambda v: v > t_hi16)

    def bucket(c, carry):
        lo2_ref[chunk(c), :] = jnp.where(hi_ref[chunk(c), :] == t_hi16, lo_ref[chunk(c), :],
                                         jnp.full((), -HALF16, I16))
        return carry

    lax.fori_loop(0, nch, bucket, 0)
    t_lo = kth_largest16(lo2_ref, need_lo)
    thr = jnp.maximum((t_hi << 16) | (t_lo + HALF16), INT_MIN + 1)
    n_ge = count(lambda k, _: k >= thr)

    idx_bits = seq.bit_length() - 1

    def tie_bound():
        need = topk - count(lambda k, _: k > thr)
        def bit(b, j):
            cand = j | (jnp.int32(1) << (idx_bits - 1 - b))
            below = count(lambda k, idx: (k == thr) & (idx < cand))
            return jnp.where(below < need, cand, j)
        j = lax.fori_loop(0, idx_bits, bit, jnp.zeros((1, tq), I32))
        return jnp.where(n_ge > topk, j, seq)

    bound = lax.cond(jnp.max(n_ge) > topk, tie_bound, lambda: jnp.full((1, tq), seq, I32))

    acc_ref[...] = jnp.zeros(acc_ref.shape, F32)

    def attend(c, carry):
        m, l = carry
        kblk = keys_ref[chunk(c), :]
        sel = (kblk > thr) | ((kblk == thr) & (krow + c * tq <= bound))
        bias = jnp.where(sel, 0.0, NEG)
        st = jnp.dot(kb_ref[0, chunk(c), :], qbt_ref[...], preferred_element_type=F32)
        st = st + jnp.concatenate([bias] * nh, axis=1)
        m_new = jnp.maximum(m, jnp.max(st, axis=0, keepdims=True))
        alpha = jnp.exp2(m - m_new)
        p = jnp.exp2(st - m_new)
        l = alpha * l + jnp.sum(p, axis=0, keepdims=True)
        acc_ref[...] = acc_ref[...] * alpha + jnp.dot(vt_ref[c], p.astype(BF16), preferred_element_type=F32)
        return m_new, l

    init = (jnp.full((1, nh * tq), NEG, F32), jnp.zeros((1, nh * tq), F32))
    _, l = lax.fori_loop(0, nch, attend, init)
    out = acc_ref[...] / l
    for h in range(nh):
        o_ref[0, :, h * HEAD_DIM:(h + 1) * HEAD_DIM] = out[:, h * tq:(h + 1) * tq].T.astype(BF16)


def _dsa(qb, kb, vb, qi, ki, wi, tq):
    b, s, _ = qb.shape
    topk = min(TOPK_MAX, s // 4)
    blk = lambda width: pl.BlockSpec((1, tq, width), lambda bb, i: (bb, i, 0))
    full = lambda width: pl.BlockSpec((1, s, width), lambda bb, i: (bb, 0, 0))
    return pl.pallas_call(
        functools.partial(_dsa_body, topk),
        grid=(b, s // tq),
        in_specs=[blk(DSA_HEADS * HEAD_DIM), full(HEAD_DIM), full(HEAD_DIM),
                  blk(IDX_HEADS * IDX_DIM), full(IDX_DIM), blk(LANES)],
        out_specs=blk(DSA_HEADS * HEAD_DIM),
        out_shape=jax.ShapeDtypeStruct((b, s, DSA_HEADS * HEAD_DIM), BF16),
        scratch_shapes=[pltpu.VMEM((s, tq), I32),
                        pltpu.VMEM((s, tq), I16),
                        pltpu.VMEM((s, tq), I16),
                        pltpu.VMEM((s, tq), I16),
                        pltpu.VMEM((IDX_HEADS * IDX_DIM, tq), BF16),
                        pltpu.VMEM((HEAD_DIM, DSA_HEADS * tq), BF16),
                        pltpu.VMEM((s // tq, HEAD_DIM, tq), BF16),
                        pltpu.VMEM((HEAD_DIM, DSA_HEADS * tq), F32)],
        compiler_params=_params("parallel", "arbitrary"),
        name="dsa",
    )(qb, kb, vb, qi, ki, wi)


def _dilated_body(q_ref, k_ref, v_ref, o_ref):
    tq = q_ref.shape[1]
    t0 = pl.program_id(1) * tq
    hg = DIL_HEADS_PER_GROUP
    stats = []
    for g, (win, dil) in enumerate(DIL_PATTERNS):
        nk = min(win + tq, k_ref.shape[1])
        k0 = pl.multiple_of(jnp.clip(t0 - win, 0, k_ref.shape[1] - nk), tq)
        delta = (t0 - k0) + lax.broadcasted_iota(I32, (tq, nk), 0) - lax.broadcasted_iota(I32, (tq, nk), 1)
        valid = (delta >= 0) & (delta <= win) & ((delta & (dil - 1)) == 0)
        for j in range(hg):
            cols = slice((g * hg + j) * HEAD_DIM, (g * hg + j + 1) * HEAD_DIM)
            s = lax.dot_general(q_ref[0, :, cols], k_ref[0, pl.ds(k0, nk), cols],
                                (((1,), (1,)), ((), ())), preferred_element_type=F32)
            s = jnp.where(valid, s, NEG)
            m = jnp.max(s, axis=-1, keepdims=True)
            p = jnp.exp2(s - m)
            l = jnp.sum(p, axis=-1, keepdims=True)
            acc = jnp.dot(p.astype(BF16), v_ref[0, pl.ds(k0, nk), cols], preferred_element_type=F32)
            stats.append((m, l, acc))
    outs = [None] * DIL_HEADS
    for j in range(hg):
        sl = [stats[g * hg + j] for g in range(len(DIL_PATTERNS))]
        m_all = functools.reduce(jnp.maximum, [m for m, _, _ in sl])
        ws = [jnp.exp2(m - m_all) for m, _, _ in sl]
        den = sum(w * l for w, (_, l, _) in zip(ws, sl))
        for g, (w, (_, _, acc)) in enumerate(zip(ws, sl)):
            outs[g * hg + j] = acc * (w / den)
    o_ref[0] = jnp.concatenate(outs, axis=1).astype(BF16)


def _dilated(qc, kc, vc, tq):
    b, s, w = qc.shape
    assert all(win % tq == 0 and dil & (dil - 1) == 0 for win, dil in DIL_PATTERNS)
    blk = pl.BlockSpec((1, tq, w), lambda bb, i: (bb, i, 0))
    full = pl.BlockSpec((1, s, w), lambda bb, i: (bb, 0, 0))
    return pl.pallas_call(
        _dilated_body,
        grid=(b, s // tq),
        in_specs=[blk, full, full],
        out_specs=blk,
        out_shape=jax.ShapeDtypeStruct((b, s, w), BF16),
        compiler_params=_params("parallel", "arbitrary"),
        name="dilated",
    )(qc, kc, vc)


def _out_proj_body(ya_ref, yb_ref, yc_ref, w_ref, x_ref, o_ref):
    na, nb = ya_ref.shape[1], yb_ref.shape[1]
    y = jnp.dot(ya_ref[...], w_ref[0:na, :].astype(BF16), preferred_element_type=F32)
    y = y + jnp.dot(yb_ref[...], w_ref[na:na + nb, :].astype(BF16), preferred_element_type=F32)
    y = y + jnp.dot(yc_ref[...], w_ref[na + nb:, :].astype(BF16), preferred_element_type=F32)
    o_ref[...] = x_ref[...] + y


def _out_proj(ya, yb, yc, w, x, layer, tm, tn):
    t, d = x.shape
    row = lambda a: pl.BlockSpec((tm, a.shape[1]), lambda m, n: (m, 0))
    return pl.pallas_call(
        _out_proj_body,
        grid=(t // tm, d // tn),
        in_specs=[row(ya), row(yb), row(yc),
                  _layer_block(layer, w.shape[1], tn),
                  pl.BlockSpec((tm, tn), lambda m, n: (m, n))],
        out_specs=pl.BlockSpec((tm, tn), lambda m, n: (m, n)),
        out_shape=jax.ShapeDtypeStruct((t, d), F32),
        compiler_params=_params("parallel", "arbitrary"),
        name="out_proj",
    )(ya, yb, yc, w, x)


def _final_norm_body(x_ref, g_ref, o_ref):
    o_ref[...] = _rms(x_ref[...], g_ref[...])


def _final_norm(x, g, tm):
    t, d = x.shape
    return pl.pallas_call(
        _final_norm_body,
        grid=(t // tm,),
        in_specs=[pl.BlockSpec((tm, d), lambda m: (m, 0)), pl.BlockSpec((1, d), lambda m: (0, 0))],
        out_specs=pl.BlockSpec((tm, d), lambda m: (m, 0)),
        out_shape=jax.ShapeDtypeStruct((t, d), F32),
        compiler_params=_params("parallel"),
        name="final_norm",
    )(x, g)


def _align_w_in(w):
    d = w.shape[0]
    o_ki = SEG_QI[1]
    o_wi = o_ki + IDX_DIM
    o_qc = o_wi + IDX_HEADS
    zeros = lambda n: jnp.zeros((d, n), w.dtype)
    return jnp.concatenate(
        [w[:, :o_wi], zeros(LANES - IDX_DIM), w[:, o_wi:o_qc], zeros(LANES - IDX_HEADS), w[:, o_qc:]],
        axis=1).astype(BF16)


def _tile(n, want):
    while n % want:
        want //= 2
    return want


def kernel(x, positions, norm_ffn1, ffn1_gate, ffn1_up, ffn1_down, norm_mix, w_in, conv_w, w_out,
           norm_ffn2, ffn2_gate, ffn2_up, ffn2_down, norm_final):
    b, s, d = x.shape
    t = b * s
    depth = w_in.shape[0]
    tm_big = _tile(t, 1024)
    tm_proj = _tile(s, 512)
    tq = _tile(s, 128)
    tq_dsa = _tile(s, 256)

    pos = positions.astype(F32).reshape(t, 1)
    tabs = (*_rope_tables(pos, HEAD_DIM, tm_big), *_rope_tables(pos, IDX_DIM, tm_big))

    def ffn(xf, g, wg, wu, wd, layer):
        h = _ffn_up(xf, g.reshape(depth, 1, d), wg, wu, layer, tm_big, 512)
        return _ffn_down(h, wd, xf, layer, tm_big, 256)

    xf = x.reshape(t, d)
    for i in range(depth):
        xf = ffn(xf, norm_ffn1, ffn1_gate, ffn1_up, ffn1_down, i)
        ya, qb, kb, vb, qi, ki, wi, qc, kc, vc = _in_proj(
            xf, norm_mix[i].reshape(1, d), _align_w_in(w_in[i]), conv_w[i], tabs, s, tm_proj)
        r3 = lambda a: a.reshape(b, s, a.shape[-1])
        yb = _dsa(r3(qb), r3(kb), r3(vb), r3(qi), r3(ki), r3(wi), tq_dsa)
        yc = _dilated(r3(qc), r3(kc), r3(vc), tq)
        xf = _out_proj(ya, yb.reshape(t, -1), yc.reshape(t, -1), w_out, xf, i, tm_big, 512)
        xf = ffn(xf, norm_ffn2, ffn2_gate, ffn2_up, ffn2_down, i)
    return _final_norm(xf, norm_final.reshape(1, d), tm_big).reshape(b, s, d)
```

```python
import functools

import jax
import jax.numpy as jnp
from jax import lax
from jax.experimental import pallas as pl
from jax.experimental.pallas import tpu as pltpu

F32 = jnp.float32
BF16 = jnp.bfloat16
I32 = jnp.int32
I16 = jnp.int16
HALF16 = 1 << 15

HEAD_DIM = 128
CONV_CH = 512
CONV_WIDTH = 3
DSA_HEADS = 6
IDX_HEADS = 16
IDX_DIM = 64
TOPK_MAX = 256
DIL_PATTERNS = ((128, 1), (512, 4), (2048, 16))
DIL_HEADS_PER_GROUP = 2
DIL_HEADS = len(DIL_PATTERNS) * DIL_HEADS_PER_GROUP
ROPE_THETA = 10000.0
RMS_EPS = 1e-6

LANES = 128
VMEM_LIMIT = 56 * 1024 * 1024
NEG = -1e30
INT_MIN = -2 ** 31
LOG2E = 1.4426950408889634

SEG_CONV = (0, 3 * CONV_CH)
SEG_QB = (SEG_CONV[1], SEG_CONV[1] + DSA_HEADS * HEAD_DIM)
SEG_KV = (SEG_QB[1], SEG_QB[1] + 2 * HEAD_DIM)
SEG_QI = (SEG_KV[1], SEG_KV[1] + IDX_HEADS * IDX_DIM)
SEG_KI = (SEG_QI[1], SEG_QI[1] + LANES)
SEG_WI = (SEG_KI[1], SEG_KI[1] + LANES)
SEG_QC = (SEG_WI[1], SEG_WI[1] + DIL_HEADS * HEAD_DIM)
SEG_KC = (SEG_QC[1], SEG_QC[1] + DIL_HEADS * HEAD_DIM)
SEG_VC = (SEG_KC[1], SEG_KC[1] + DIL_HEADS * HEAD_DIM)
D_IN_ALIGNED = SEG_VC[1]


def _params(*sem):
    return pltpu.CompilerParams(dimension_semantics=sem, vmem_limit_bytes=VMEM_LIMIT)


def _rms(x, g):
    ms = jnp.mean(x * x, axis=-1, keepdims=True)
    return x * lax.rsqrt(ms + RMS_EPS) * g


def _ffn_up_body(x_ref, g_ref, wg_ref, wu_ref, h_ref, xn_ref):
    @pl.when(pl.program_id(1) == 0)
    def _():
        xn_ref[...] = _rms(x_ref[...], g_ref[...]).astype(BF16)

    xn = xn_ref[...]
    a = jnp.dot(xn, wg_ref[...].astype(BF16), preferred_element_type=F32)
    b = jnp.dot(xn, wu_ref[...].astype(BF16), preferred_element_type=F32)
    h_ref[...] = (a * jax.nn.sigmoid(a) * b).astype(BF16)


def _layer_block(layer, rows, cols):
    return pl.BlockSpec((None, rows, cols), lambda m, n: (layer, 0, n))


def _ffn_up(x, g, wg, wu, layer, tm, tn):
    t, d = x.shape
    f = wg.shape[2]
    return pl.pallas_call(
        _ffn_up_body,
        grid=(t // tm, f // tn),
        in_specs=[
            pl.BlockSpec((tm, d), lambda m, n: (m, 0)),
            pl.BlockSpec((None, 1, d), lambda m, n: (layer, 0, 0)),
            _layer_block(layer, d, tn),
            _layer_block(layer, d, tn),
        ],
        out_specs=pl.BlockSpec((tm, tn), lambda m, n: (m, n)),
        out_shape=jax.ShapeDtypeStruct((t, f), BF16),
        scratch_shapes=[pltpu.VMEM((tm, d), BF16)],
        compiler_params=_params("parallel", "arbitrary"),
        name="ffn_up",
    )(x, g, wg, wu)


def _ffn_down_body(h_ref, w_ref, x_ref, o_ref, wb_ref):
    @pl.when(pl.program_id(1) == 0)
    def _():
        wb_ref[...] = w_ref[...].astype(BF16)

    y = jnp.dot(h_ref[...], wb_ref[...], preferred_element_type=F32)
    o_ref[...] = x_ref[...] + 0.5 * y


def _ffn_down(h, wd, x, layer, tm, tn):
    t, f = h.shape
    d = wd.shape[2]
    return pl.pallas_call(
        _ffn_down_body,
        grid=(d // tn, t // tm),
        in_specs=[
            pl.BlockSpec((tm, f), lambda n, m: (m, 0)),
            pl.BlockSpec((None, f, tn), lambda n, m: (layer, 0, n)),
            pl.BlockSpec((tm, tn), lambda n, m: (m, n)),
        ],
        out_specs=pl.BlockSpec((tm, tn), lambda n, m: (m, n)),
        out_shape=jax.ShapeDtypeStruct((t, d), F32),
        scratch_shapes=[pltpu.VMEM((f, tn), BF16)],
        compiler_params=_params("parallel", "arbitrary"),
        name="ffn_down",
    )(h, wd, x)


def _rope_table_body(pos_ref, inv_ref, sgn_ref, cos_ref, sin_ref):
    ang = pos_ref[...] * inv_ref[...]
    cos_ref[...] = jnp.cos(ang)
    sin_ref[...] = jnp.sin(ang) * sgn_ref[...]


def _rope_tables(pos, dim, tm):
    t = pos.shape[0]
    half = dim // 2
    inv = 1.0 / (ROPE_THETA ** (jnp.arange(0, dim, 2, dtype=F32) / dim))
    lane = jnp.arange(LANES)
    inv_l = inv[lane % half].reshape(1, LANES)
    sgn_l = jnp.where(lane % dim < half, -1.0, 1.0).astype(F32).reshape(1, LANES)
    row = pl.BlockSpec((1, LANES), lambda m: (0, 0))
    tab = pl.BlockSpec((tm, LANES), lambda m: (m, 0))
    return pl.pallas_call(
        _rope_table_body,
        grid=(t // tm,),
        in_specs=[pl.BlockSpec((tm, 1), lambda m: (m, 0)), row, row],
        out_specs=[tab, tab],
        out_shape=[jax.ShapeDtypeStruct((t, LANES), F32)] * 2,
        compiler_params=_params("parallel"),
        name="rope_tables",
    )(pos, inv_l, sgn_l)


def _rope128(x, cos, sin):
    return x * cos + pltpu.roll(x, HEAD_DIM // 2, 1) * sin


def _rope64(x, cos, sin, lo_half):
    partner = jnp.where(lo_half, pltpu.roll(x, LANES - IDX_DIM // 2, 1), pltpu.roll(x, IDX_DIM // 2, 1))
    return x * cos + partner * sin


def _in_proj_body(seq_tiles, x_ref, g_ref, w_ref, cw_ref, ch_ref, sh_ref, ci_ref, si_ref,
                  ya_ref, qb_ref, kb_ref, vb_ref, qi_ref, ki_ref, wi_ref, qc_ref, kc_ref, vc_ref,
                  u_ref):
    tm = x_ref.shape[0]
    xn = _rms(x_ref[...], g_ref[...]).astype(BF16)

    def proj(seg):
        return jnp.dot(xn, w_ref[:, seg[0]:seg[1]], preferred_element_type=F32)

    p = proj(SEG_CONV)
    h, gate_b, gate_c = p[:, :CONV_CH], p[:, CONV_CH:2 * CONV_CH], p[:, 2 * CONV_CH:]
    u = gate_c * h

    @pl.when(pl.program_id(0) % seq_tiles == 0)
    def _():
        u_ref[0:8, :] = jnp.zeros((8, CONV_CH), F32)

    u_ref[8:8 + tm, :] = u
    cw = cw_ref[...]
    y = cw[2:3, :] * u + cw[1:2, :] * u_ref[7:7 + tm, :] + cw[0:1, :] * u_ref[6:6 + tm, :]
    u_ref[0:8, :] = u[tm - 8:, :]
    ya_ref[...] = (gate_b * y).astype(BF16)

    ch, sh, ci, si = ch_ref[...], sh_ref[...], ci_ref[...], si_ref[...]
    scale = HEAD_DIM ** -0.5 * LOG2E

    def rope_heads(p, n, mul):
        return jnp.concatenate(
            [_rope128(p[:, j * LANES:(j + 1) * LANES], ch, sh) * mul for j in range(n)], axis=1)

    qb_ref[...] = rope_heads(proj(SEG_QB), DSA_HEADS, scale).astype(BF16)
    p = proj(SEG_KV)
    kb_ref[...] = _rope128(p[:, :HEAD_DIM], ch, sh).astype(BF16)
    vb_ref[...] = p[:, HEAD_DIM:].astype(BF16)

    lo_half = lax.broadcasted_iota(I32, (tm, LANES), 1) % IDX_DIM < IDX_DIM // 2
    p = proj(SEG_QI)
    qi_ref[...] = jnp.concatenate(
        [_rope64(p[:, j * LANES:(j + 1) * LANES], ci, si, lo_half) * (IDX_DIM ** -0.5)
         for j in range(IDX_HEADS * IDX_DIM // LANES)], axis=1).astype(BF16)
    ki_ref[...] = _rope64(proj(SEG_KI), ci, si, lo_half)[:, :IDX_DIM].astype(BF16)
    wi_ref[...] = proj(SEG_WI) * (IDX_HEADS ** -0.5)

    qc_ref[...] = rope_heads(proj(SEG_QC), DIL_HEADS, scale)
    kc_ref[...] = rope_heads(proj(SEG_KC), DIL_HEADS, 1.0)
    vc_ref[...] = proj(SEG_VC)


def _in_proj(x, g, w, cw, tabs, layer, seq, tm):
    t, d = x.shape
    n = w.shape[2]
    row = lambda width: pl.BlockSpec((tm, width), lambda m: (m, 0))
    const = lambda shape: pl.BlockSpec((None,) + shape, lambda m: (layer, 0, 0))
    widths = [CONV_CH, DSA_HEADS * HEAD_DIM, HEAD_DIM, HEAD_DIM, IDX_HEADS * IDX_DIM, IDX_DIM, LANES,
              DIL_HEADS * HEAD_DIM, DIL_HEADS * HEAD_DIM, DIL_HEADS * HEAD_DIM]
    dtypes = [BF16, BF16, BF16, BF16, BF16, BF16, F32, F32, F32, F32]
    return pl.pallas_call(
        functools.partial(_in_proj_body, seq // tm),
        grid=(t // tm,),
        in_specs=[row(d), const((1, d)),
                  pl.BlockSpec((None, d, n), lambda m: (layer, 0, 0), pipeline_mode=pl.Buffered(1)),
                  const((CONV_WIDTH, CONV_CH)), row(LANES), row(LANES), row(LANES), row(LANES)],
        out_specs=[row(wd) for wd in widths],
        out_shape=[jax.ShapeDtypeStruct((t, wd), dt) for wd, dt in zip(widths, dtypes)],
        scratch_shapes=[pltpu.VMEM((tm + 8, CONV_CH), F32)],
        compiler_params=_params("arbitrary"),
        name="in_proj",
    )(x, g, w, cw, *tabs)


def _dsa_body(topk, qb_ref, kb_ref, vb_ref, qi_ref, ki_ref, wi_ref, o_ref,
              keys_ref, hi_ref, lo_ref, lo2_ref, qt_ref, qbt_ref, vt_ref, acc_ref):
    tq = qb_ref.shape[1]
    seq = kb_ref.shape[1]
    nh = DSA_HEADS
    i = pl.program_id(1)
    nch = i + 1

    def chunk(c):
        return pl.ds(pl.multiple_of(c * tq, tq), tq)

    def transpose_bf16(a):
        return a.astype(F32).T.astype(BF16)

    @pl.when(i == 0)
    def _():
        for c in range(seq // tq):
            vt_ref[c] = transpose_bf16(vb_ref[0, c * tq:(c + 1) * tq, :])

    qt_ref[...] = transpose_bf16(qi_ref[0])
    for h in range(nh):
        qbt_ref[:, h * tq:(h + 1) * tq] = transpose_bf16(qb_ref[0, :, h * HEAD_DIM:(h + 1) * HEAD_DIM])
    wt = wi_ref[0].T
    qpos = lax.broadcasted_iota(I32, (tq, tq), 1) + i * tq
    krow = lax.broadcasted_iota(I32, (tq, tq), 0)

    def score_chunk(c, carry):
        kc = ki_ref[0, chunk(c), :]
        acc = jnp.zeros((tq, tq), F32)
        for h in range(IDX_HEADS):
            lg = jnp.dot(kc, qt_ref[h * IDX_DIM:(h + 1) * IDX_DIM, :], preferred_element_type=F32)
            acc = acc + jnp.maximum(lg, 0.0) * wt[h:h + 1, :]
        bits = pltpu.bitcast(acc, I32)
        key = bits ^ ((bits >> 31) & 0x7FFFFFFF)
        key = jnp.where(krow + c * tq <= qpos, key, INT_MIN)
        keys_ref[chunk(c), :] = key
        hi_ref[chunk(c), :] = (key >> 16).astype(I16)
        lo_ref[chunk(c), :] = ((key & 0xFFFF) - HALF16).astype(I16)
        return carry

    lax.fori_loop(0, nch, score_chunk, 0)

    def count(pred):
        def body(c, acc):
            hit = pred(keys_ref[chunk(c), :], krow + c * tq).astype(I32)
            return acc + jnp.sum(hit.reshape(tq // 8, 8, tq), axis=0)
        acc = lax.fori_loop(0, nch, body, jnp.zeros((8, tq), I32))
        return jnp.sum(acc, axis=0, keepdims=True)

    def count16(ref, pred):
        def body(c, acc):
            hit = jnp.where(pred(ref[chunk(c), :]), jnp.ones((), BF16), jnp.zeros((), BF16))
            parts = [hit[r * 16:(r + 1) * 16] for r in range(tq // 16)]
            while len(parts) > 1:
                parts = [parts[j] + parts[j + 1] for j in range(0, len(parts), 2)]
            return acc + parts[0].astype(F32)
        acc = lax.fori_loop(0, nch, body, jnp.zeros((16, tq), F32))
        return jnp.sum(acc, axis=0, keepdims=True)

    def kth_largest16(ref, k):
        def search_bit(b, t_u):
            cand = t_u | (jnp.int32(1) << (15 - b))
            cand16 = (cand - HALF16).astype(I16)
            return jnp.where(count16(ref, lambda v: v >= cand16) >= k, cand, t_u)
        return lax.fori_loop(0, 16, search_bit, jnp.zeros((1, tq), I32)) - HALF16

    t_hi = kth_largest16(hi_ref, float(topk))
    t_hi16 = t_hi.astype(I16)
    need_lo = float(topk) - count16(hi_ref, lambda v: v > t_hi16)

    def bucket(c, carry):
        lo2_ref[chunk(c), :] = jnp.where(hi_ref[chunk(c), :] == t_hi16, lo_ref[chunk(c), :],
                                         jnp.full((), -HALF16, I16))
        return carry

    lax.fori_loop(0, nch, bucket, 0)
    t_lo = kth_largest16(lo2_ref, need_lo)
    thr = jnp.maximum((t_hi << 16) | (t_lo + HALF16), INT_MIN + 1)
    n_ge = count(lambda k, _: k >= thr)

    idx_bits = seq.bit_length() - 1

    def tie_bound():
        need = topk - count(lambda k, _: k > thr)
        def bit(b, j):
            cand = j | (jnp.int32(1) << (idx_bits - 1 - b))
            below = count(lambda k, idx: (k == thr) & (idx < cand))
            return jnp.where(below < need, cand, j)
        j = lax.fori_loop(0, idx_bits, bit, jnp.zeros((1, tq), I32))
        return jnp.where(n_ge > topk, j, seq)

    bound = lax.cond(jnp.max(n_ge) > topk, tie_bound, lambda: jnp.full((1, tq), seq, I32))

    acc_ref[...] = jnp.zeros(acc_ref.shape, F32)

    def attend(c, carry):
        m, l = carry
        kblk = keys_ref[chunk(c), :]
        sel = (kblk > thr) | ((kblk == thr) & (krow + c * tq <= bound))
        bias = jnp.where(sel, 0.0, NEG)
        st = jnp.dot(kb_ref[0, chunk(c), :], qbt_ref[...], preferred_element_type=F32)
        st = st + jnp.concatenate([bias] * nh, axis=1)
        m_new = jnp.maximum(m, jnp.max(st, axis=0, keepdims=True))
        alpha = jnp.exp2(m - m_new)
        p = jnp.exp2(st - m_new)
        l = alpha * l + jnp.sum(p, axis=0, keepdims=True)
        acc_ref[...] = acc_ref[...] * alpha + jnp.dot(vt_ref[c], p.astype(BF16), preferred_element_type=F32)
        return m_new, l

    init = (jnp.full((1, nh * tq), NEG, F32), jnp.zeros((1, nh * tq), F32))
    _, l = lax.fori_loop(0, nch, attend, init)
    out = acc_ref[...] / l
    for h in range(nh):
        o_ref[0, :, h * HEAD_DIM:(h + 1) * HEAD_DIM] = out[:, h * tq:(h + 1) * tq].T.astype(BF16)


def _dsa(qb, kb, vb, qi, ki, wi, tq):
    b, s, _ = qb.shape
    topk = min(TOPK_MAX, s // 4)
    blk = lambda width: pl.BlockSpec((1, tq, width), lambda bb, i: (bb, i, 0))
    full = lambda width: pl.BlockSpec((1, s, width), lambda bb, i: (bb, 0, 0))
    return pl.pallas_call(
        functools.partial(_dsa_body, topk),
        grid=(b, s // tq),
        in_specs=[blk(DSA_HEADS * HEAD_DIM), full(HEAD_DIM), full(HEAD_DIM),
                  blk(IDX_HEADS * IDX_DIM), full(IDX_DIM), blk(LANES)],
        out_specs=blk(DSA_HEADS * HEAD_DIM),
        out_shape=jax.ShapeDtypeStruct((b, s, DSA_HEADS * HEAD_DIM), BF16),
        scratch_shapes=[pltpu.VMEM((s, tq), I32),
                        pltpu.VMEM((s, tq), I16),
                        pltpu.VMEM((s, tq), I16),
                        pltpu.VMEM((s, tq), I16),
                        pltpu.VMEM((IDX_HEADS * IDX_DIM, tq), BF16),
                        pltpu.VMEM((HEAD_DIM, DSA_HEADS * tq), BF16),
                        pltpu.VMEM((s // tq, HEAD_DIM, tq), BF16),
                        pltpu.VMEM((HEAD_DIM, DSA_HEADS * tq), F32)],
        compiler_params=_params("parallel", "arbitrary"),
        name="dsa",
    )(qb, kb, vb, qi, ki, wi)


def _dilated_body(*refs):
    ng = len(DIL_PATTERNS)
    q_refs, k_refs, v_refs = refs[0:ng], refs[ng:2 * ng], refs[2 * ng:3 * ng]
    o_refs = refs[3 * ng:4 * ng]
    acc_ref, m_ref, l_ref = refs[4 * ng:]
    sb_tokens = q_refs[0].shape[1]
    t0 = pl.program_id(2) * sb_tokens
    blk = HEAD_DIM
    UNITS = 4
    rq = lax.broadcasted_iota(I32, (blk, blk), 0)
    ck = lax.broadcasted_iota(I32, (blk, blk), 1)
    bias_cur = jnp.where(ck <= rq, 0.0, NEG)
    bias_prev = jnp.where(ck >= rq, 0.0, NEG)
    nt = (((1,), (1,)), ((), ()))

    for g, (win, dil) in enumerate(DIL_PATTERNS):
        q_ref, k_ref, v_ref = q_refs[g], k_refs[g], v_refs[g]
        per_res = sb_tokens // dil // blk

        def rows(start, dil=dil):
            return pl.ds(start, blk, stride=dil) if dil > 1 else pl.ds(start, blk)

        def units(it, carry, g=g, dil=dil, per_res=per_res, rows=rows,
                  q_ref=q_ref, k_ref=k_ref, v_ref=v_ref):
            q0s, kcs, kps, hps = [], [], [], []
            for n in range(UNITS):
                idx = it * UNITS + n
                q0 = idx // per_res + (idx % per_res) * (blk * dil)
                k_cur = t0 + q0
                has_prev = k_cur >= blk * dil
                q0s.append(q0)
                kcs.append(k_cur)
                hps.append(has_prev)
                kps.append(jnp.where(has_prev, k_cur - blk * dil, k_cur))
            ss = []
            for q0, kc, kp, hp in zip(q0s, kcs, kps, hps):
                k2 = jnp.concatenate([k_ref[0, rows(kp), :], k_ref[0, rows(kc), :]], axis=0).astype(BF16)
                s = lax.dot_general(q_ref[0, rows(q0), :].astype(BF16), k2, nt, preferred_element_type=F32)
                ss.append(s + jnp.concatenate([jnp.where(hp, bias_prev, NEG), bias_cur], axis=1))
            ms = [jnp.max(jnp.maximum(s[:, :blk], s[:, blk:]), axis=-1, keepdims=True) for s in ss]
            ps = [jnp.exp2(s - m).astype(BF16) for s, m in zip(ss, ms)]
            for q0, kc, kp, m, p in zip(q0s, kcs, kps, ms, ps):
                v2 = jnp.concatenate([v_ref[0, rows(kp), :], v_ref[0, rows(kc), :]], axis=0).astype(BF16)
                av = jnp.dot(p, jnp.concatenate([v2, jnp.ones_like(v2)], axis=1), preferred_element_type=F32)
                acc_ref[g, rows(q0), :] = av[:, :HEAD_DIM]
                l_ref[g, rows(q0), :] = av[:, HEAD_DIM:]
                m_ref[g, rows(q0), :] = jnp.broadcast_to(m, (blk, LANES))
            return carry

        lax.fori_loop(0, dil * per_res // UNITS, units, 0)

    step = 256

    def merge(c, carry):
        sl = pl.ds(pl.multiple_of(c * step, step), step)
        ms = [m_ref[g, sl, :] for g in range(ng)]
        m_all = functools.reduce(jnp.maximum, ms)
        ws = [jnp.exp2(m - m_all) for m in ms]
        den = sum(w * l_ref[g, sl, :] for g, w in enumerate(ws))
        for g, w in enumerate(ws):
            o_refs[g][0, sl, :] = (acc_ref[g, sl, :] * (w / den)).astype(BF16)
        return carry

    lax.fori_loop(0, sb_tokens // step, merge, 0)


def _dilated(qc, kc, vc):
    b, s, _ = qc.shape
    ng, hg = len(DIL_PATTERNS), DIL_HEADS_PER_GROUP
    sb_tokens = HEAD_DIM * max(dil for _, dil in DIL_PATTERNS)
    assert s % sb_tokens == 0 and all(win == HEAD_DIM * dil for win, dil in DIL_PATTERNS)
    head = lambda g: (lambda bb, j, sb: (bb, sb, g * hg + j))
    head_full = lambda g: (lambda bb, j, sb: (bb, 0, g * hg + j))
    q_specs = [pl.BlockSpec((1, sb_tokens, HEAD_DIM), head(g)) for g in range(ng)]
    kv_specs = [pl.BlockSpec((1, s, HEAD_DIM), head_full(g)) for g in range(ng)]
    out_spec = pl.BlockSpec((1, sb_tokens, HEAD_DIM), lambda bb, j, sb: (bb, sb, j))
    return pl.pallas_call(
        _dilated_body,
        grid=(b, hg, s // sb_tokens),
        in_specs=q_specs + kv_specs + kv_specs,
        out_specs=[out_spec] * ng,
        out_shape=[jax.ShapeDtypeStruct((b, s, hg * HEAD_DIM), BF16)] * ng,
        scratch_shapes=[pltpu.VMEM((ng, sb_tokens, HEAD_DIM), F32)] * 3,
        compiler_params=_params("parallel", "parallel", "arbitrary"),
        name="dilated",
    )(*([qc] * ng + [kc] * ng + [vc] * ng))


def _out_proj_body(*refs):
    *y_refs, w_ref, x_ref, o_ref = refs
    y, row = x_ref[...], 0
    for y_ref in y_refs:
        n = y_ref.shape[1]
        y = y + jnp.dot(y_ref[...], w_ref[row:row + n, :].astype(BF16), preferred_element_type=F32)
        row += n
    o_ref[...] = y


def _out_proj(parts, w, x, layer, tm, tn):
    t, d = x.shape
    assert sum(a.shape[1] for a in parts) == w.shape[1]
    return pl.pallas_call(
        _out_proj_body,
        grid=(t // tm, d // tn),
        in_specs=[pl.BlockSpec((tm, a.shape[1]), lambda m, n: (m, 0)) for a in parts]
        + [_layer_block(layer, w.shape[1], tn), pl.BlockSpec((tm, tn), lambda m, n: (m, n))],
        out_specs=pl.BlockSpec((tm, tn), lambda m, n: (m, n)),
        out_shape=jax.ShapeDtypeStruct((t, d), F32),
        compiler_params=_params("parallel", "arbitrary"),
        name="out_proj",
    )(*parts, w, x)


def _final_norm_body(x_ref, g_ref, o_ref):
    o_ref[...] = _rms(x_ref[...], g_ref[...])


def _final_norm(x, g, tm):
    t, d = x.shape
    return pl.pallas_call(
        _final_norm_body,
        grid=(t // tm,),
        in_specs=[pl.BlockSpec((tm, d), lambda m: (m, 0)), pl.BlockSpec((1, d), lambda m: (0, 0))],
        out_specs=pl.BlockSpec((tm, d), lambda m: (m, 0)),
        out_shape=jax.ShapeDtypeStruct((t, d), F32),
        compiler_params=_params("parallel"),
        name="final_norm",
    )(x, g)


def _align_w_in(w):
    w = w.astype(BF16)
    o_ki = SEG_QI[1]
    o_wi = o_ki + IDX_DIM
    o_qc = o_wi + IDX_HEADS
    zeros = lambda n: jnp.zeros(w.shape[:2] + (n,), BF16)
    return jnp.concatenate(
        [w[..., :o_wi], zeros(LANES - IDX_DIM), w[..., o_wi:o_qc], zeros(LANES - IDX_HEADS), w[..., o_qc:]],
        axis=-1)


def _tile(n, want):
    while n % want:
        want //= 2
    return want


def kernel(x, positions, norm_ffn1, ffn1_gate, ffn1_up, ffn1_down, norm_mix, w_in, conv_w, w_out,
           norm_ffn2, ffn2_gate, ffn2_up, ffn2_down, norm_final):
    b, s, d = x.shape
    t = b * s
    depth = w_in.shape[0]
    tm_big = _tile(t, 1024)
    tm_proj = _tile(s, 512)
    tq_dsa = _tile(s, 256)

    pos = positions.astype(F32).reshape(t, 1)
    tabs = (*_rope_tables(pos, HEAD_DIM, tm_big), *_rope_tables(pos, IDX_DIM, tm_big))

    def ffn(xf, g, wg, wu, wd, layer):
        h = _ffn_up(xf, g.reshape(depth, 1, d), wg, wu, layer, tm_big, 512)
        return _ffn_down(h, wd, xf, layer, _tile(t, 512), 512)

    w_in_al = _align_w_in(w_in)
    xf = x.reshape(t, d)
    for i in range(depth):
        xf = ffn(xf, norm_ffn1, ffn1_gate, ffn1_up, ffn1_down, i)
        ya, qb, kb, vb, qi, ki, wi, qc, kc, vc = _in_proj(
            xf, norm_mix.reshape(depth, 1, d), w_in_al, conv_w, tabs, i, s, tm_proj)
        r3 = lambda a: a.reshape(b, s, a.shape[-1])
        yb = _dsa(r3(qb), r3(kb), r3(vb), r3(qi), r3(ki), r3(wi), tq_dsa)
        ycs = _dilated(r3(qc), r3(kc), r3(vc))
        parts = [ya, yb.reshape(t, -1)] + [yc.reshape(t, -1) for yc in ycs]
        xf = _out_proj(parts, w_out, xf, i, tm_big, 512)
        xf = ffn(xf, norm_ffn2, ffn2_gate, ffn2_up, ffn2_down, i)
    return _final_norm(xf, norm_final.reshape(1, d), tm_big).reshape(b, s, d)
```

```python
import functools

import jax
import jax.numpy as jnp
from jax import lax
from jax.experimental import pallas as pl
from jax.experimental.pallas import tpu as pltpu

F32 = jnp.float32
BF16 = jnp.bfloat16
I32 = jnp.int32
I16 = jnp.int16
HALF16 = 1 << 15
ONES_ROWS = 16

HEAD_DIM = 128
CONV_CH = 512
CONV_WIDTH = 3
DSA_HEADS = 6
IDX_HEADS = 16
IDX_DIM = 64
TOPK_MAX = 256
DIL_PATTERNS = ((128, 1), (512, 4), (2048, 16))
DIL_HEADS_PER_GROUP = 2
DIL_HEADS = len(DIL_PATTERNS) * DIL_HEADS_PER_GROUP
ROPE_THETA = 10000.0
RMS_EPS = 1e-6

LANES = 128
VMEM_LIMIT = 56 * 1024 * 1024
NEG = -1e30
INT_MIN = -2 ** 31
LOG2E = 1.4426950408889634

SEG_CONV = (0, 3 * CONV_CH)
SEG_QB = (SEG_CONV[1], SEG_CONV[1] + DSA_HEADS * HEAD_DIM)
SEG_KV = (SEG_QB[1], SEG_QB[1] + 2 * HEAD_DIM)
SEG_QI = (SEG_KV[1], SEG_KV[1] + IDX_HEADS * IDX_DIM)
SEG_KI = (SEG_QI[1], SEG_QI[1] + LANES)
SEG_WI = (SEG_KI[1], SEG_KI[1] + LANES)
SEG_QC = (SEG_WI[1], SEG_WI[1] + DIL_HEADS * HEAD_DIM)
SEG_KC = (SEG_QC[1], SEG_QC[1] + DIL_HEADS * HEAD_DIM)
SEG_VC = (SEG_KC[1], SEG_KC[1] + DIL_HEADS * HEAD_DIM)
D_IN_ALIGNED = SEG_VC[1]


def _params(*sem):
    return pltpu.CompilerParams(dimension_semantics=sem, vmem_limit_bytes=VMEM_LIMIT)


def _rms(x, g):
    ms = jnp.mean(x * x, axis=-1, keepdims=True)
    return x * lax.rsqrt(ms + RMS_EPS) * g


def _ffn_up_body(x_ref, g_ref, wg_ref, wu_ref, h_ref, xn_ref):
    @pl.when(pl.program_id(1) == 0)
    def _():
        xn_ref[...] = _rms(x_ref[...], g_ref[...]).astype(BF16)

    xn = xn_ref[...]
    a = jnp.dot(xn, wg_ref[...].astype(BF16), preferred_element_type=F32)
    b = jnp.dot(xn, wu_ref[...].astype(BF16), preferred_element_type=F32)
    h_ref[...] = (a * jax.nn.sigmoid(a) * b).astype(BF16)


def _layer_block(layer, rows, cols):
    return pl.BlockSpec((None, rows, cols), lambda m, n: (layer, 0, n))


def _ffn_up(x, g, wg, wu, layer, tm, tn):
    t, d = x.shape
    f = wg.shape[2]
    return pl.pallas_call(
        _ffn_up_body,
        grid=(t // tm, f // tn),
        in_specs=[
            pl.BlockSpec((tm, d), lambda m, n: (m, 0)),
            pl.BlockSpec((None, 1, d), lambda m, n: (layer, 0, 0)),
            _layer_block(layer, d, tn),
            _layer_block(layer, d, tn),
        ],
        out_specs=pl.BlockSpec((tm, tn), lambda m, n: (m, n)),
        out_shape=jax.ShapeDtypeStruct((t, f), BF16),
        scratch_shapes=[pltpu.VMEM((tm, d), BF16)],
        compiler_params=_params("parallel", "arbitrary"),
        name="ffn_up",
    )(x, g, wg, wu)


def _ffn_down_body(h_ref, w_ref, x_ref, o_ref, wb_ref):
    @pl.when(pl.program_id(1) == 0)
    def _():
        wb_ref[...] = w_ref[...].astype(BF16)

    y = jnp.dot(h_ref[...], wb_ref[...], preferred_element_type=F32)
    o_ref[...] = x_ref[...] + 0.5 * y


def _ffn_down(h, wd, x, layer, tm, tn):
    t, f = h.shape
    d = wd.shape[2]
    return pl.pallas_call(
        _ffn_down_body,
        grid=(d // tn, t // tm),
        in_specs=[
            pl.BlockSpec((tm, f), lambda n, m: (m, 0)),
            pl.BlockSpec((None, f, tn), lambda n, m: (layer, 0, n)),
            pl.BlockSpec((tm, tn), lambda n, m: (m, n)),
        ],
        out_specs=pl.BlockSpec((tm, tn), lambda n, m: (m, n)),
        out_shape=jax.ShapeDtypeStruct((t, d), F32),
        scratch_shapes=[pltpu.VMEM((f, tn), BF16)],
        compiler_params=_params("parallel", "arbitrary"),
        name="ffn_down",
    )(h, wd, x)


def _rope_table_body(pos_ref, inv_ref, sgn_ref, cos_ref, sin_ref):
    ang = pos_ref[...] * inv_ref[...]
    cos_ref[...] = jnp.cos(ang)
    sin_ref[...] = jnp.sin(ang) * sgn_ref[...]


def _rope_tables(pos, dim, tm):
    t = pos.shape[0]
    half = dim // 2
    inv = 1.0 / (ROPE_THETA ** (jnp.arange(0, dim, 2, dtype=F32) / dim))
    lane = jnp.arange(LANES)
    inv_l = inv[lane % half].reshape(1, LANES)
    sgn_l = jnp.where(lane % dim < half, -1.0, 1.0).astype(F32).reshape(1, LANES)
    row = pl.BlockSpec((1, LANES), lambda m: (0, 0))
    tab = pl.BlockSpec((tm, LANES), lambda m: (m, 0))
    return pl.pallas_call(
        _rope_table_body,
        grid=(t // tm,),
        in_specs=[pl.BlockSpec((tm, 1), lambda m: (m, 0)), row, row],
        out_specs=[tab, tab],
        out_shape=[jax.ShapeDtypeStruct((t, LANES), F32)] * 2,
        compiler_params=_params("parallel"),
        name="rope_tables",
    )(pos, inv_l, sgn_l)


def _rope128(x, cos, sin):
    return x * cos + pltpu.roll(x, HEAD_DIM // 2, 1) * sin


def _rope64(x, cos, sin, lo_half):
    partner = jnp.where(lo_half, pltpu.roll(x, LANES - IDX_DIM // 2, 1), pltpu.roll(x, IDX_DIM // 2, 1))
    return x * cos + partner * sin


def _in_proj_body(seq_tiles, x_ref, g_ref, w_ref, cw_ref, ch_ref, sh_ref, ci_ref, si_ref,
                  ya_ref, qb_ref, kb_ref, vb_ref, qi_ref, ki_ref, wi_ref, qc_ref, kc_ref, vc_ref,
                  u_ref):
    tm = x_ref.shape[0]
    xn = _rms(x_ref[...], g_ref[...]).astype(BF16)

    def proj(seg):
        return jnp.dot(xn, w_ref[:, seg[0]:seg[1]], preferred_element_type=F32)

    p = proj(SEG_CONV)
    h, gate_b, gate_c = p[:, :CONV_CH], p[:, CONV_CH:2 * CONV_CH], p[:, 2 * CONV_CH:]
    u = gate_c * h

    @pl.when(pl.program_id(0) % seq_tiles == 0)
    def _():
        u_ref[0:8, :] = jnp.zeros((8, CONV_CH), F32)

    u_ref[8:8 + tm, :] = u
    cw = cw_ref[...]
    y = cw[2:3, :] * u + cw[1:2, :] * u_ref[7:7 + tm, :] + cw[0:1, :] * u_ref[6:6 + tm, :]
    u_ref[0:8, :] = u[tm - 8:, :]
    ya_ref[...] = (gate_b * y).astype(BF16)

    ch, sh, ci, si = ch_ref[...], sh_ref[...], ci_ref[...], si_ref[...]
    scale = HEAD_DIM ** -0.5 * LOG2E

    def rope_heads(p, n, mul):
        return jnp.concatenate(
            [_rope128(p[:, j * LANES:(j + 1) * LANES], ch, sh) * mul for j in range(n)], axis=1)

    qb_ref[...] = rope_heads(proj(SEG_QB), DSA_HEADS, scale).astype(BF16)
    p = proj(SEG_KV)
    kb_ref[...] = _rope128(p[:, :HEAD_DIM], ch, sh).astype(BF16)
    vb_ref[...] = p[:, HEAD_DIM:].astype(BF16)

    lo_half = lax.broadcasted_iota(I32, (tm, LANES), 1) % IDX_DIM < IDX_DIM // 2
    p = proj(SEG_QI)
    qi_ref[...] = jnp.concatenate(
        [_rope64(p[:, j * LANES:(j + 1) * LANES], ci, si, lo_half) * (IDX_DIM ** -0.5)
         for j in range(IDX_HEADS * IDX_DIM // LANES)], axis=1).astype(BF16)
    ki_ref[...] = _rope64(proj(SEG_KI), ci, si, lo_half)[:, :IDX_DIM].astype(BF16)
    wi_ref[...] = proj(SEG_WI) * (IDX_HEADS ** -0.5)

    qc_ref[...] = rope_heads(proj(SEG_QC), DIL_HEADS, scale)
    kc_ref[...] = rope_heads(proj(SEG_KC), DIL_HEADS, 1.0)
    vc_ref[...] = proj(SEG_VC)


def _in_proj(x, g, w, cw, tabs, layer, seq, tm):
    t, d = x.shape
    n = w.shape[2]
    row = lambda width: pl.BlockSpec((tm, width), lambda m: (m, 0))
    const = lambda shape: pl.BlockSpec((None,) + shape, lambda m: (layer, 0, 0))
    widths = [CONV_CH, DSA_HEADS * HEAD_DIM, HEAD_DIM, HEAD_DIM, IDX_HEADS * IDX_DIM, IDX_DIM, LANES,
              DIL_HEADS * HEAD_DIM, DIL_HEADS * HEAD_DIM, DIL_HEADS * HEAD_DIM]
    dtypes = [BF16, BF16, BF16, BF16, BF16, BF16, F32, F32, F32, F32]
    return pl.pallas_call(
        functools.partial(_in_proj_body, seq // tm),
        grid=(t // tm,),
        in_specs=[row(d), const((1, d)),
                  pl.BlockSpec((None, d, n), lambda m: (layer, 0, 0), pipeline_mode=pl.Buffered(1)),
                  const((CONV_WIDTH, CONV_CH)), row(LANES), row(LANES), row(LANES), row(LANES)],
        out_specs=[row(wd) for wd in widths],
        out_shape=[jax.ShapeDtypeStruct((t, wd), dt) for wd, dt in zip(widths, dtypes)],
        scratch_shapes=[pltpu.VMEM((tm + 8, CONV_CH), F32)],
        compiler_params=_params("arbitrary"),
        name="in_proj",
    )(x, g, w, cw, *tabs)


def _dsa_body(topk, qb_ref, kb_ref, vb_ref, qi_ref, ki_ref, wi_ref, o_ref,
              keys_ref, hi_ref, lo_ref, lo2_ref, qt_ref, qbt_ref, vt_ref, acc_ref):
    tq = qb_ref.shape[1]
    seq = kb_ref.shape[1]
    nh = DSA_HEADS
    i = pl.program_id(1)
    nch = i + 1

    def chunk(c):
        return pl.ds(pl.multiple_of(c * tq, tq), tq)

    def transpose_bf16(a):
        return a.astype(F32).T.astype(BF16)

    @pl.when(i == 0)
    def _():
        for c2 in range(seq // (2 * tq)):
            vt_ref[c2, :HEAD_DIM, :] = transpose_bf16(vb_ref[0, c2 * 2 * tq:(c2 + 1) * 2 * tq, :])
            vt_ref[c2, HEAD_DIM:, :] = jnp.ones((ONES_ROWS, 2 * tq), BF16)

    qt_ref[...] = transpose_bf16(qi_ref[0])
    for h in range(nh):
        qbt_ref[:, h * tq:(h + 1) * tq] = transpose_bf16(qb_ref[0, :, h * HEAD_DIM:(h + 1) * HEAD_DIM])
    wt = wi_ref[0].T
    qpos = lax.broadcasted_iota(I32, (tq, tq), 1) + i * tq
    krow = lax.broadcasted_iota(I32, (tq, tq), 0)

    def score_chunk(c, carry):
        kc = ki_ref[0, chunk(c), :]
        acc = jnp.zeros((tq, tq), F32)
        for h in range(IDX_HEADS):
            lg = jnp.dot(kc, qt_ref[h * IDX_DIM:(h + 1) * IDX_DIM, :], preferred_element_type=F32)
            acc = acc + jnp.maximum(lg, 0.0) * wt[h:h + 1, :]
        bits = pltpu.bitcast(acc, I32)
        key = bits ^ ((bits >> 31) & 0x7FFFFFFF)
        key = jnp.where(krow + c * tq <= qpos, key, INT_MIN)
        keys_ref[chunk(c), :] = key
        hi_ref[chunk(c), :] = (key >> 16).astype(I16)
        lo_ref[chunk(c), :] = ((key & 0xFFFF) - HALF16).astype(I16)
        return carry

    lax.fori_loop(0, nch, score_chunk, 0)

    npair = (nch + 1) // 2

    def pair(c2):
        return pl.ds(pl.multiple_of(c2 * (2 * tq), 2 * tq), 2 * tq)

    @pl.when(nch % 2 == 1)
    def _():
        keys_ref[chunk(nch), :] = jnp.full((tq, tq), INT_MIN, I32)
        hi_ref[chunk(nch), :] = jnp.full((tq, tq), -HALF16, I16)
        lo_ref[chunk(nch), :] = jnp.full((tq, tq), -HALF16, I16)

    def count(pred):
        def body(c, acc):
            hit = pred(keys_ref[chunk(c), :], krow + c * tq).astype(I32)
            return acc + jnp.sum(hit.reshape(tq // 8, 8, tq), axis=0)
        acc = lax.fori_loop(0, nch, body, jnp.zeros((8, tq), I32))
        return jnp.sum(acc, axis=0, keepdims=True)

    def count16(ref, pred):
        def body(c2, acc):
            hit = jnp.where(pred(ref[pair(c2), :]), jnp.ones((), BF16), jnp.zeros((), BF16))
            parts = [hit[r * 16:(r + 1) * 16] for r in range(2 * tq // 16)]
            while len(parts) > 1:
                parts = [parts[j] + parts[j + 1] for j in range(0, len(parts), 2)]
            return acc + parts[0].astype(F32)
        acc = lax.fori_loop(0, npair, body, jnp.zeros((16, tq), F32))
        return jnp.sum(acc, axis=0, keepdims=True)

    def kth_largest16(ref, k):
        def search_bit(b, t_u):
            cand = t_u | (jnp.int32(1) << (15 - b))
            cand16 = (cand - HALF16).astype(I16)
            return jnp.where(count16(ref, lambda v: v >= cand16) >= k, cand, t_u)
        return lax.fori_loop(0, 16, search_bit, jnp.zeros((1, tq), I32)) - HALF16

    t_hi = kth_largest16(hi_ref, float(topk))
    t_hi16 = t_hi.astype(I16)
    need_lo = float(topk) - count16(hi_ref, lambda v: v > t_hi16)

    def bucket(c2, carry):
        lo2_ref[pair(c2), :] = jnp.where(hi_ref[pair(c2), :] == t_hi16, lo_ref[pair(c2), :],
                                         jnp.full((), -HALF16, I16))
        return carry

    lax.fori_loop(0, npair, bucket, 0)
    t_lo = kth_largest16(lo2_ref, need_lo)
    thr = jnp.maximum((t_hi << 16) | (t_lo + HALF16), INT_MIN + 1)
    n_ge = count(lambda k, _: k >= thr)

    idx_bits = seq.bit_length() - 1

    def tie_bound():
        need = topk - count(lambda k, _: k > thr)
        def bit(b, j):
            cand = j | (jnp.int32(1) << (idx_bits - 1 - b))
            below = count(lambda k, idx: (k == thr) & (idx < cand))
            return jnp.where(below < need, cand, j)
        j = lax.fori_loop(0, idx_bits, bit, jnp.zeros((1, tq), I32))
        return jnp.where(n_ge > topk, j, seq)

    bound = lax.cond(jnp.max(n_ge) > topk, tie_bound, lambda: jnp.full((1, tq), seq, I32))

    acc_ref[...] = jnp.zeros(acc_ref.shape, F32)

    krow2 = lax.broadcasted_iota(I32, (2 * tq, tq), 0)

    def attend(c2, m):
        kblk = keys_ref[pair(c2), :]
        sel = (kblk > thr) | ((kblk == thr) & (krow2 + c2 * (2 * tq) <= bound))
        bias = jnp.where(sel, 0.0, NEG)
        st = jnp.dot(kb_ref[0, pair(c2), :], qbt_ref[...], preferred_element_type=F32)
        st = st + jnp.concatenate([bias] * nh, axis=1)
        m_new = jnp.maximum(m, jnp.max(st, axis=0, keepdims=True))
        p = jnp.exp2(st - m_new).astype(BF16)
        acc_ref[...] = acc_ref[...] * jnp.exp2(m - m_new) + jnp.dot(vt_ref[c2], p, preferred_element_type=F32)
        return m_new

    lax.fori_loop(0, npair, attend, jnp.full((1, nh * tq), NEG, F32))
    out = acc_ref[:HEAD_DIM, :] / acc_ref[HEAD_DIM:HEAD_DIM + 1, :]
    for h in range(nh):
        o_ref[0, :, h * HEAD_DIM:(h + 1) * HEAD_DIM] = out[:, h * tq:(h + 1) * tq].T.astype(BF16)


def _dsa(qb, kb, vb, qi, ki, wi, tq):
    b, s, _ = qb.shape
    topk = min(TOPK_MAX, s // 4)
    assert s % (2 * tq) == 0
    blk = lambda width: pl.BlockSpec((1, tq, width), lambda bb, i: (bb, i, 0))
    full = lambda width: pl.BlockSpec((1, s, width), lambda bb, i: (bb, 0, 0))
    return pl.pallas_call(
        functools.partial(_dsa_body, topk),
        grid=(b, s // tq),
        in_specs=[blk(DSA_HEADS * HEAD_DIM), full(HEAD_DIM), full(HEAD_DIM),
                  blk(IDX_HEADS * IDX_DIM), full(IDX_DIM), blk(LANES)],
        out_specs=blk(DSA_HEADS * HEAD_DIM),
        out_shape=jax.ShapeDtypeStruct((b, s, DSA_HEADS * HEAD_DIM), BF16),
        scratch_shapes=[pltpu.VMEM((s, tq), I32),
                        pltpu.VMEM((s, tq), I16),
                        pltpu.VMEM((s, tq), I16),
                        pltpu.VMEM((s, tq), I16),
                        pltpu.VMEM((IDX_HEADS * IDX_DIM, tq), BF16),
                        pltpu.VMEM((HEAD_DIM, DSA_HEADS * tq), BF16),
                        pltpu.VMEM((s // (2 * tq), HEAD_DIM + ONES_ROWS, 2 * tq), BF16),
                        pltpu.VMEM((HEAD_DIM + ONES_ROWS, DSA_HEADS * tq), F32)],
        compiler_params=_params("parallel", "arbitrary"),
        name="dsa",
    )(qb, kb, vb, qi, ki, wi)


def _dilated_body(*refs):
    ng = len(DIL_PATTERNS)
    q_refs, k_refs, v_refs = refs[0:ng], refs[ng:2 * ng], refs[2 * ng:3 * ng]
    o_refs = refs[3 * ng:4 * ng]
    acc_ref, m_ref, l_ref = refs[4 * ng:]
    sb_tokens = q_refs[0].shape[1]
    t0 = pl.program_id(2) * sb_tokens
    blk = HEAD_DIM
    UNITS = 4
    rq = lax.broadcasted_iota(I32, (blk, blk), 0)
    ck = lax.broadcasted_iota(I32, (blk, blk), 1)
    bias_cur = jnp.where(ck <= rq, 0.0, NEG)
    bias_prev = jnp.where(ck >= rq, 0.0, NEG)
    nt = (((1,), (1,)), ((), ()))

    for g, (win, dil) in enumerate(DIL_PATTERNS):
        q_ref, k_ref, v_ref = q_refs[g], k_refs[g], v_refs[g]
        per_res = sb_tokens // dil // blk

        def rows(start, dil=dil):
            return pl.ds(start, blk, stride=dil) if dil > 1 else pl.ds(start, blk)

        def units(it, carry, g=g, dil=dil, per_res=per_res, rows=rows,
                  q_ref=q_ref, k_ref=k_ref, v_ref=v_ref):
            q0s, kcs, kps, hps = [], [], [], []
            for n in range(UNITS):
                idx = it * UNITS + n
                q0 = idx // per_res + (idx % per_res) * (blk * dil)
                k_cur = t0 + q0
                has_prev = k_cur >= blk * dil
                q0s.append(q0)
                kcs.append(k_cur)
                hps.append(has_prev)
                kps.append(jnp.where(has_prev, k_cur - blk * dil, k_cur))
            ss = []
            for q0, kc, kp, hp in zip(q0s, kcs, kps, hps):
                k2 = jnp.concatenate([k_ref[0, rows(kp), :], k_ref[0, rows(kc), :]], axis=0).astype(BF16)
                s = lax.dot_general(q_ref[0, rows(q0), :].astype(BF16), k2, nt, preferred_element_type=F32)
                ss.append(s + jnp.concatenate([jnp.where(hp, bias_prev, NEG), bias_cur], axis=1))
            ms = [jnp.max(jnp.maximum(s[:, :blk], s[:, blk:]), axis=-1, keepdims=True) for s in ss]
            ps = [jnp.exp2(s - m).astype(BF16) for s, m in zip(ss, ms)]
            for q0, kc, kp, m, p in zip(q0s, kcs, kps, ms, ps):
                v2 = jnp.concatenate([v_ref[0, rows(kp), :], v_ref[0, rows(kc), :]], axis=0).astype(BF16)
                av = jnp.dot(p, jnp.concatenate([v2, jnp.ones_like(v2)], axis=1), preferred_element_type=F32)
                acc_ref[g, rows(q0), :] = av[:, :HEAD_DIM]
                l_ref[g, rows(q0), :] = av[:, HEAD_DIM:]
                m_ref[g, rows(q0), :] = jnp.broadcast_to(m, (blk, LANES))
            return carry

        lax.fori_loop(0, dil * per_res // UNITS, units, 0)

    step = 256

    def merge(c, carry):
        sl = pl.ds(pl.multiple_of(c * step, step), step)
        ms = [m_ref[g, sl, :] for g in range(ng)]
        m_all = functools.reduce(jnp.maximum, ms)
        ws = [jnp.exp2(m - m_all) for m in ms]
        den = sum(w * l_ref[g, sl, :] for g, w in enumerate(ws))
        for g, w in enumerate(ws):
            o_refs[g][0, sl, :] = (acc_ref[g, sl, :] * (w / den)).astype(BF16)
        return carry

    lax.fori_loop(0, sb_tokens // step, merge, 0)


def _dilated(qc, kc, vc):
    b, s, _ = qc.shape
    ng, hg = len(DIL_PATTERNS), DIL_HEADS_PER_GROUP
    sb_tokens = HEAD_DIM * max(dil for _, dil in DIL_PATTERNS)
    assert s % sb_tokens == 0 and all(win == HEAD_DIM * dil for win, dil in DIL_PATTERNS)
    head = lambda g: (lambda bb, j, sb: (bb, sb, g * hg + j))
    head_full = lambda g: (lambda bb, j, sb: (bb, 0, g * hg + j))
    q_specs = [pl.BlockSpec((1, sb_tokens, HEAD_DIM), head(g)) for g in range(ng)]
    kv_specs = [pl.BlockSpec((1, s, HEAD_DIM), head_full(g)) for g in range(ng)]
    out_spec = pl.BlockSpec((1, sb_tokens, HEAD_DIM), lambda bb, j, sb: (bb, sb, j))
    return pl.pallas_call(
        _dilated_body,
        grid=(b, hg, s // sb_tokens),
        in_specs=q_specs + kv_specs + kv_specs,
        out_specs=[out_spec] * ng,
        out_shape=[jax.ShapeDtypeStruct((b, s, hg * HEAD_DIM), BF16)] * ng,
        scratch_shapes=[pltpu.VMEM((ng, sb_tokens, HEAD_DIM), F32)] * 3,
        compiler_params=_params("parallel", "parallel", "arbitrary"),
        name="dilated",
    )(*([qc] * ng + [kc] * ng + [vc] * ng))


def _out_proj_body(*refs):
    *y_refs, w_ref, x_ref, o_ref = refs
    y, row = x_ref[...], 0
    for y_ref in y_refs:
        n = y_ref.shape[1]
        y = y + jnp.dot(y_ref[...], w_ref[row:row + n, :].astype(BF16), preferred_element_type=F32)
        row += n
    o_ref[...] = y


def _out_proj(parts, w, x, layer, tm, tn):
    t, d = x.shape
    assert sum(a.shape[1] for a in parts) == w.shape[1]
    return pl.pallas_call(
        _out_proj_body,
        grid=(t // tm, d // tn),
        in_specs=[pl.BlockSpec((tm, a.shape[1]), lambda m, n: (m, 0)) for a in parts]
        + [_layer_block(layer, w.shape[1], tn), pl.BlockSpec((tm, tn), lambda m, n: (m, n))],
        out_specs=pl.BlockSpec((tm, tn), lambda m, n: (m, n)),
        out_shape=jax.ShapeDtypeStruct((t, d), F32),
        compiler_params=_params("parallel", "arbitrary"),
        name="out_proj",
    )(*parts, w, x)


def _final_norm_body(x_ref, g_ref, o_ref):
    o_ref[...] = _rms(x_ref[...], g_ref[...])


def _final_norm(x, g, tm):
    t, d = x.shape
    return pl.pallas_call(
        _final_norm_body,
        grid=(t // tm,),
        in_specs=[pl.BlockSpec((tm, d), lambda m: (m, 0)), pl.BlockSpec((1, d), lambda m: (0, 0))],
        out_specs=pl.BlockSpec((tm, d), lambda m: (m, 0)),
        out_shape=jax.ShapeDtypeStruct((t, d), F32),
        compiler_params=_params("parallel"),
        name="final_norm",
    )(x, g)


def _align_w_in_body(x_ref, o_ref):
    o_wi = SEG_QI[1] + IDX_DIM
    o_qc = o_wi + IDX_HEADS
    rows = x_ref.shape[0]
    o_ref[:, :o_wi] = x_ref[:, :o_wi].astype(BF16)
    o_ref[:, o_wi:SEG_WI[0]] = jnp.zeros((rows, SEG_WI[0] - o_wi), BF16)
    o_ref[:, SEG_WI[0]:SEG_WI[0] + IDX_HEADS] = x_ref[:, o_wi:o_qc].astype(BF16)
    o_ref[:, SEG_WI[0] + IDX_HEADS:SEG_QC[0]] = jnp.zeros((rows, LANES - IDX_HEADS), BF16)
    o_ref[:, SEG_QC[0]:] = x_ref[:, o_qc:].astype(BF16)


def _align_w_in(w, tr):
    depth, d, n = w.shape
    assert n - (SEG_QI[1] + IDX_DIM + IDX_HEADS) == D_IN_ALIGNED - SEG_QC[0]
    return pl.pallas_call(
        _align_w_in_body,
        grid=(depth, d // tr),
        in_specs=[pl.BlockSpec((None, tr, n), lambda l, r: (l, r, 0))],
        out_specs=pl.BlockSpec((None, tr, D_IN_ALIGNED), lambda l, r: (l, r, 0)),
        out_shape=jax.ShapeDtypeStruct((depth, d, D_IN_ALIGNED), BF16),
        compiler_params=_params("parallel", "parallel"),
        name="align_w_in",
    )(w)


def _tile(n, want):
    while n % want:
        want //= 2
    return want


def kernel(x, positions, norm_ffn1, ffn1_gate, ffn1_up, ffn1_down, norm_mix, w_in, conv_w, w_out,
           norm_ffn2, ffn2_gate, ffn2_up, ffn2_down, norm_final):
    b, s, d = x.shape
    t = b * s
    depth = w_in.shape[0]
    tm_big = _tile(t, 1024)
    tm_proj = _tile(s, 512)
    tq_dsa = _tile(s, 256)

    pos = positions.astype(F32).reshape(t, 1)
    tabs = (*_rope_tables(pos, HEAD_DIM, tm_big), *_rope_tables(pos, IDX_DIM, tm_big))

    def ffn(xf, g, wg, wu, wd, layer):
        h = _ffn_up(xf, g.reshape(depth, 1, d), wg, wu, layer, tm_big, 512)
        return _ffn_down(h, wd, xf, layer, _tile(t, 512), 512)

    w_in_al = _align_w_in(w_in, _tile(d, 256))
    xf = x.reshape(t, d)
    for i in range(depth):
        xf = ffn(xf, norm_ffn1, ffn1_gate, ffn1_up, ffn1_down, i)
        ya, qb, kb, vb, qi, ki, wi, qc, kc, vc = _in_proj(
            xf, norm_mix.reshape(depth, 1, d), w_in_al, conv_w, tabs, i, s, tm_proj)
        r3 = lambda a: a.reshape(b, s, a.shape[-1])
        yb = _dsa(r3(qb), r3(kb), r3(vb), r3(qi), r3(ki), r3(wi), tq_dsa)
        ycs = _dilated(r3(qc), r3(kc), r3(vc))
        parts = [ya, yb.reshape(t, -1)] + [yc.reshape(t, -1) for yc in ycs]
        xf = _out_proj(parts, w_out, xf, i, tm_big, 512)
        xf = ffn(xf, norm_ffn2, ffn2_gate, ffn2_up, ffn2_down, i)
    return _final_norm(xf, norm_final.reshape(1, d), tm_big).reshape(b, s, d)
```

```python
import functools

import jax
import jax.numpy as jnp
from jax import lax
from jax.experimental import pallas as pl
from jax.experimental.pallas import tpu as pltpu

F32 = jnp.float32
BF16 = jnp.bfloat16
I32 = jnp.int32
I16 = jnp.int16
HALF16 = 1 << 15
ONES_ROWS = 16

HEAD_DIM = 128
CONV_CH = 512
CONV_WIDTH = 3
DSA_HEADS = 6
IDX_HEADS = 16
IDX_DIM = 64
TOPK_MAX = 256
DIL_PATTERNS = ((128, 1), (512, 4), (2048, 16))
DIL_HEADS_PER_GROUP = 2
DIL_HEADS = len(DIL_PATTERNS) * DIL_HEADS_PER_GROUP
ROPE_THETA = 10000.0
RMS_EPS = 1e-6

LANES = 128
VMEM_LIMIT = 56 * 1024 * 1024
NEG = -1e30
INT_MIN = -2 ** 31
LOG2E = 1.4426950408889634

SEG_CONV = (0, 3 * CONV_CH)
SEG_QB = (SEG_CONV[1], SEG_CONV[1] + DSA_HEADS * HEAD_DIM)
SEG_KV = (SEG_QB[1], SEG_QB[1] + 2 * HEAD_DIM)
SEG_QI = (SEG_KV[1], SEG_KV[1] + IDX_HEADS * IDX_DIM)
SEG_KI = (SEG_QI[1], SEG_QI[1] + LANES)
SEG_WI = (SEG_KI[1], SEG_KI[1] + LANES)
SEG_QC = (SEG_WI[1], SEG_WI[1] + DIL_HEADS * HEAD_DIM)
SEG_KC = (SEG_QC[1], SEG_QC[1] + DIL_HEADS * HEAD_DIM)
SEG_VC = (SEG_KC[1], SEG_KC[1] + DIL_HEADS * HEAD_DIM)
D_IN_ALIGNED = SEG_VC[1]


def _params(*sem):
    return pltpu.CompilerParams(dimension_semantics=sem, vmem_limit_bytes=VMEM_LIMIT)


def _rms(x, g):
    ms = jnp.mean(x * x, axis=-1, keepdims=True)
    return x * lax.rsqrt(ms + RMS_EPS) * g


def _ffn_up_body(x_ref, g_ref, wg_ref, wu_ref, h_ref, xn_ref):
    @pl.when(pl.program_id(1) == 0)
    def _():
        xn_ref[...] = _rms(x_ref[...], g_ref[...]).astype(BF16)

    xn = xn_ref[...]
    a = jnp.dot(xn, wg_ref[...].astype(BF16), preferred_element_type=F32)
    b = jnp.dot(xn, wu_ref[...].astype(BF16), preferred_element_type=F32)
    h_ref[...] = (a * jax.nn.sigmoid(a) * b).astype(BF16)


def _layer_block(layer, rows, cols):
    return pl.BlockSpec((None, rows, cols), lambda m, n: (layer, 0, n))


def _ffn_up(x, g, wg, wu, layer, tm, tn):
    t, d = x.shape
    f = wg.shape[2]
    return pl.pallas_call(
        _ffn_up_body,
        grid=(t // tm, f // tn),
        in_specs=[
            pl.BlockSpec((tm, d), lambda m, n: (m, 0)),
            pl.BlockSpec((None, 1, d), lambda m, n: (layer, 0, 0)),
            _layer_block(layer, d, tn),
            _layer_block(layer, d, tn),
        ],
        out_specs=pl.BlockSpec((tm, tn), lambda m, n: (m, n)),
        out_shape=jax.ShapeDtypeStruct((t, f), BF16),
        scratch_shapes=[pltpu.VMEM((tm, d), BF16)],
        compiler_params=_params("parallel", "arbitrary"),
        name="ffn_up",
    )(x, g, wg, wu)


def _ffn_down_body(h_ref, w_ref, x_ref, o_ref, wb_ref):
    @pl.when(pl.program_id(1) == 0)
    def _():
        wb_ref[...] = w_ref[...].astype(BF16)

    y = jnp.dot(h_ref[...], wb_ref[...], preferred_element_type=F32)
    o_ref[...] = x_ref[...] + 0.5 * y


def _ffn_down(h, wd, x, layer, tm, tn):
    t, f = h.shape
    d = wd.shape[2]
    return pl.pallas_call(
        _ffn_down_body,
        grid=(d // tn, t // tm),
        in_specs=[
            pl.BlockSpec((tm, f), lambda n, m: (m, 0)),
            pl.BlockSpec((None, f, tn), lambda n, m: (layer, 0, n)),
            pl.BlockSpec((tm, tn), lambda n, m: (m, n)),
        ],
        out_specs=pl.BlockSpec((tm, tn), lambda n, m: (m, n)),
        out_shape=jax.ShapeDtypeStruct((t, d), F32),
        scratch_shapes=[pltpu.VMEM((f, tn), BF16)],
        compiler_params=_params("parallel", "arbitrary"),
        name="ffn_down",
    )(h, wd, x)


def _rope_table_body(pos_ref, inv_ref, sgn_ref, cos_ref, sin_ref):
    ang = pos_ref[...] * inv_ref[...]
    cos_ref[...] = jnp.cos(ang)
    sin_ref[...] = jnp.sin(ang) * sgn_ref[...]


def _rope_tables(pos, dim, tm):
    t = pos.shape[0]
    half = dim // 2
    inv = 1.0 / (ROPE_THETA ** (jnp.arange(0, dim, 2, dtype=F32) / dim))
    lane = jnp.arange(LANES)
    inv_l = inv[lane % half].reshape(1, LANES)
    sgn_l = jnp.where(lane % dim < half, -1.0, 1.0).astype(F32).reshape(1, LANES)
    row = pl.BlockSpec((1, LANES), lambda m: (0, 0))
    tab = pl.BlockSpec((tm, LANES), lambda m: (m, 0))
    return pl.pallas_call(
        _rope_table_body,
        grid=(t // tm,),
        in_specs=[pl.BlockSpec((tm, 1), lambda m: (m, 0)), row, row],
        out_specs=[tab, tab],
        out_shape=[jax.ShapeDtypeStruct((t, LANES), F32)] * 2,
        compiler_params=_params("parallel"),
        name="rope_tables",
    )(pos, inv_l, sgn_l)


def _rope128(x, cos, sin):
    return x * cos + pltpu.roll(x, HEAD_DIM // 2, 1) * sin


def _rope64(x, cos, sin, lo_half):
    partner = jnp.where(lo_half, pltpu.roll(x, LANES - IDX_DIM // 2, 1), pltpu.roll(x, IDX_DIM // 2, 1))
    return x * cos + partner * sin


def _in_proj_body(seq_tiles, x_ref, g_ref, w_ref, cw_ref, ch_ref, sh_ref, ci_ref, si_ref,
                  ya_ref, qb_ref, kb_ref, vb_ref, qi_ref, ki_ref, wi_ref, qc_ref, kc_ref, vc_ref,
                  u_ref):
    tm = x_ref.shape[0]
    xn = _rms(x_ref[...], g_ref[...]).astype(BF16)

    def proj(seg):
        return jnp.dot(xn, w_ref[:, seg[0]:seg[1]], preferred_element_type=F32)

    p = proj(SEG_CONV)
    h, gate_b, gate_c = p[:, :CONV_CH], p[:, CONV_CH:2 * CONV_CH], p[:, 2 * CONV_CH:]
    u = gate_c * h

    @pl.when(pl.program_id(0) % seq_tiles == 0)
    def _():
        u_ref[0:8, :] = jnp.zeros((8, CONV_CH), F32)

    u_ref[8:8 + tm, :] = u
    cw = cw_ref[...]
    y = cw[2:3, :] * u + cw[1:2, :] * u_ref[7:7 + tm, :] + cw[0:1, :] * u_ref[6:6 + tm, :]
    u_ref[0:8, :] = u[tm - 8:, :]
    ya_ref[...] = (gate_b * y).astype(BF16)

    ch, sh, ci, si = ch_ref[...], sh_ref[...], ci_ref[...], si_ref[...]
    scale = HEAD_DIM ** -0.5 * LOG2E

    def rope_heads(p, n, mul):
        return jnp.concatenate(
            [_rope128(p[:, j * LANES:(j + 1) * LANES], ch, sh) * mul for j in range(n)], axis=1)

    qb_ref[...] = rope_heads(proj(SEG_QB), DSA_HEADS, scale).astype(BF16)
    p = proj(SEG_KV)
    kb_ref[...] = _rope128(p[:, :HEAD_DIM], ch, sh).astype(BF16)
    vb_ref[...] = p[:, HEAD_DIM:].astype(BF16)

    lo_half = lax.broadcasted_iota(I32, (tm, LANES), 1) % IDX_DIM < IDX_DIM // 2
    p = proj(SEG_QI)
    qi_ref[...] = jnp.concatenate(
        [_rope64(p[:, j * LANES:(j + 1) * LANES], ci, si, lo_half) * (IDX_DIM ** -0.5)
         for j in range(IDX_HEADS * IDX_DIM // LANES)], axis=1).astype(BF16)
    ki_ref[...] = _rope64(proj(SEG_KI), ci, si, lo_half)[:, :IDX_DIM].astype(BF16)
    wi_ref[...] = proj(SEG_WI) * (IDX_HEADS ** -0.5)

    qc_ref[...] = rope_heads(proj(SEG_QC), DIL_HEADS, scale)
    kc_ref[...] = rope_heads(proj(SEG_KC), DIL_HEADS, 1.0)
    vc_ref[...] = proj(SEG_VC)


def _in_proj(x, g, w, cw, tabs, layer, seq, tm):
    t, d = x.shape
    n = w.shape[2]
    row = lambda width: pl.BlockSpec((tm, width), lambda m: (m, 0))
    const = lambda shape: pl.BlockSpec((None,) + shape, lambda m: (layer, 0, 0))
    widths = [CONV_CH, DSA_HEADS * HEAD_DIM, HEAD_DIM, HEAD_DIM, IDX_HEADS * IDX_DIM, IDX_DIM, LANES,
              DIL_HEADS * HEAD_DIM, DIL_HEADS * HEAD_DIM, DIL_HEADS * HEAD_DIM]
    dtypes = [BF16, BF16, BF16, BF16, BF16, BF16, F32, F32, F32, F32]
    return pl.pallas_call(
        functools.partial(_in_proj_body, seq // tm),
        grid=(t // tm,),
        in_specs=[row(d), const((1, d)),
                  pl.BlockSpec((None, d, n), lambda m: (layer, 0, 0), pipeline_mode=pl.Buffered(1)),
                  const((CONV_WIDTH, CONV_CH)), row(LANES), row(LANES), row(LANES), row(LANES)],
        out_specs=[row(wd) for wd in widths],
        out_shape=[jax.ShapeDtypeStruct((t, wd), dt) for wd, dt in zip(widths, dtypes)],
        scratch_shapes=[pltpu.VMEM((tm + 8, CONV_CH), F32)],
        compiler_params=_params("arbitrary"),
        name="in_proj",
    )(x, g, w, cw, *tabs)


def _dsa_body(topk, qb_ref, kb_ref, vb_ref, qi_ref, ki_ref, wi_ref, o_ref,
              keys_ref, hi_ref, lo_ref, lo2_ref, qt_ref, qbt_ref, vt_ref, acc_ref):
    tq = qb_ref.shape[1]
    seq = kb_ref.shape[1]
    nh = DSA_HEADS
    i = pl.program_id(1)
    nch = i + 1

    def chunk(c):
        return pl.ds(pl.multiple_of(c * tq, tq), tq)

    def transpose_bf16(a):
        return a.astype(F32).T.astype(BF16)

    @pl.when(i == 0)
    def _():
        for c2 in range(seq // (2 * tq)):
            vt_ref[c2, :HEAD_DIM, :] = transpose_bf16(vb_ref[0, c2 * 2 * tq:(c2 + 1) * 2 * tq, :])
            vt_ref[c2, HEAD_DIM:, :] = jnp.ones((ONES_ROWS, 2 * tq), BF16)

    qt_ref[...] = transpose_bf16(qi_ref[0])
    for h in range(nh):
        qbt_ref[:, h * tq:(h + 1) * tq] = transpose_bf16(qb_ref[0, :, h * HEAD_DIM:(h + 1) * HEAD_DIM])
    wt = wi_ref[0].T
    qpos = lax.broadcasted_iota(I32, (tq, tq), 1) + i * tq
    krow = lax.broadcasted_iota(I32, (tq, tq), 0)

    def score_chunk(c, carry):
        kc = ki_ref[0, chunk(c), :]
        acc = jnp.zeros((tq, tq), F32)
        for h in range(IDX_HEADS):
            lg = jnp.dot(kc, qt_ref[h * IDX_DIM:(h + 1) * IDX_DIM, :], preferred_element_type=F32)
            acc = acc + jnp.maximum(lg, 0.0) * wt[h:h + 1, :]
        bits = pltpu.bitcast(acc, I32)
        key = bits ^ ((bits >> 31) & 0x7FFFFFFF)
        key = jnp.where(krow + c * tq <= qpos, key, INT_MIN)
        keys_ref[chunk(c), :] = key
        hi_ref[chunk(c), :] = (key >> 16).astype(I16)
        lo_ref[chunk(c), :] = ((key & 0xFFFF) - HALF16).astype(I16)
        return carry

    lax.fori_loop(0, nch, score_chunk, 0)

    npair = (nch + 1) // 2

    def pair(c2):
        return pl.ds(pl.multiple_of(c2 * (2 * tq), 2 * tq), 2 * tq)

    @pl.when(nch % 2 == 1)
    def _():
        keys_ref[chunk(nch), :] = jnp.full((tq, tq), INT_MIN, I32)
        hi_ref[chunk(nch), :] = jnp.full((tq, tq), -HALF16, I16)
        lo_ref[chunk(nch), :] = jnp.full((tq, tq), -HALF16, I16)

    def count(pred):
        def body(c, acc):
            hit = pred(keys_ref[chunk(c), :], krow + c * tq).astype(I32)
            return acc + jnp.sum(hit.reshape(tq // 8, 8, tq), axis=0)
        acc = lax.fori_loop(0, nch, body, jnp.zeros((8, tq), I32))
        return jnp.sum(acc, axis=0, keepdims=True)

    def count16(ref, pred):
        def body(c2, acc):
            hit = jnp.where(pred(ref[pair(c2), :]), jnp.ones((), BF16), jnp.zeros((), BF16))
            parts = [hit[r * 16:(r + 1) * 16] for r in range(2 * tq // 16)]
            while len(parts) > 1:
                parts = [parts[j] + parts[j + 1] for j in range(0, len(parts), 2)]
            return acc + parts[0].astype(F32)
        acc = lax.fori_loop(0, npair, body, jnp.zeros((16, tq), F32))
        return jnp.sum(acc, axis=0, keepdims=True)

    def kth_largest16(ref, k):
        def search_bit(b, t_u):
            cand = t_u | (jnp.int32(1) << (15 - b))
            cand16 = (cand - HALF16).astype(I16)
            return jnp.where(count16(ref, lambda v: v >= cand16) >= k, cand, t_u)
        return lax.fori_loop(0, 16, search_bit, jnp.zeros((1, tq), I32)) - HALF16

    t_hi = kth_largest16(hi_ref, float(topk))
    t_hi16 = t_hi.astype(I16)
    need_lo = float(topk) - count16(hi_ref, lambda v: v > t_hi16)

    def bucket(c2, carry):
        lo2_ref[pair(c2), :] = jnp.where(hi_ref[pair(c2), :] == t_hi16, lo_ref[pair(c2), :],
                                         jnp.full((), -HALF16, I16))
        return carry

    lax.fori_loop(0, npair, bucket, 0)
    t_lo = kth_largest16(lo2_ref, need_lo)
    thr = jnp.maximum((t_hi << 16) | (t_lo + HALF16), INT_MIN + 1)
    n_ge = count(lambda k, _: k >= thr)

    idx_bits = seq.bit_length() - 1

    def tie_bound():
        need = topk - count(lambda k, _: k > thr)
        def bit(b, j):
            cand = j | (jnp.int32(1) << (idx_bits - 1 - b))
            below = count(lambda k, idx: (k == thr) & (idx < cand))
            return jnp.where(below < need, cand, j)
        j = lax.fori_loop(0, idx_bits, bit, jnp.zeros((1, tq), I32))
        return jnp.where(n_ge > topk, j, seq)

    bound = lax.cond(jnp.max(n_ge) > topk, tie_bound, lambda: jnp.full((1, tq), seq, I32))

    acc_ref[...] = jnp.zeros(acc_ref.shape, F32)

    krow2 = lax.broadcasted_iota(I32, (2 * tq, tq), 0)

    def attend(c2, m):
        kblk = keys_ref[pair(c2), :]
        sel = (kblk > thr) | ((kblk == thr) & (krow2 + c2 * (2 * tq) <= bound))
        bias = jnp.where(sel, 0.0, NEG)
        st = jnp.dot(kb_ref[0, pair(c2), :], qbt_ref[...], preferred_element_type=F32)
        st = st + jnp.concatenate([bias] * nh, axis=1)
        m_new = jnp.maximum(m, jnp.max(st, axis=0, keepdims=True))
        p = jnp.exp2(st - m_new).astype(BF16)
        acc_ref[...] = acc_ref[...] * jnp.exp2(m - m_new) + jnp.dot(vt_ref[c2], p, preferred_element_type=F32)
        return m_new

    lax.fori_loop(0, npair, attend, jnp.full((1, nh * tq), NEG, F32))
    out = acc_ref[:HEAD_DIM, :] / acc_ref[HEAD_DIM:HEAD_DIM + 1, :]
    for h in range(nh):
        o_ref[0, :, h * HEAD_DIM:(h + 1) * HEAD_DIM] = out[:, h * tq:(h + 1) * tq].T.astype(BF16)


def _dsa(qb, kb, vb, qi, ki, wi, tq):
    b, s, _ = qb.shape
    topk = min(TOPK_MAX, s // 4)
    assert s % (2 * tq) == 0
    blk = lambda width: pl.BlockSpec((1, tq, width), lambda bb, i: (bb, i, 0))
    full = lambda width: pl.BlockSpec((1, s, width), lambda bb, i: (bb, 0, 0))
    return pl.pallas_call(
        functools.partial(_dsa_body, topk),
        grid=(b, s // tq),
        in_specs=[blk(DSA_HEADS * HEAD_DIM), full(HEAD_DIM), full(HEAD_DIM),
                  blk(IDX_HEADS * IDX_DIM), full(IDX_DIM), blk(LANES)],
        out_specs=blk(DSA_HEADS * HEAD_DIM),
        out_shape=jax.ShapeDtypeStruct((b, s, DSA_HEADS * HEAD_DIM), BF16),
        scratch_shapes=[pltpu.VMEM((s, tq), I32),
                        pltpu.VMEM((s, tq), I16),
                        pltpu.VMEM((s, tq), I16),
                        pltpu.VMEM((s, tq), I16),
                        pltpu.VMEM((IDX_HEADS * IDX_DIM, tq), BF16),
                        pltpu.VMEM((HEAD_DIM, DSA_HEADS * tq), BF16),
                        pltpu.VMEM((s // (2 * tq), HEAD_DIM + ONES_ROWS, 2 * tq), BF16),
                        pltpu.VMEM((HEAD_DIM + ONES_ROWS, DSA_HEADS * tq), F32)],
        compiler_params=_params("parallel", "arbitrary"),
        name="dsa",
    )(qb, kb, vb, qi, ki, wi)


def _dilated_body(*refs):
    ng = len(DIL_PATTERNS)
    q_refs, k_refs, v_refs = refs[0:ng], refs[ng:2 * ng], refs[2 * ng:3 * ng]
    o_refs = refs[3 * ng:4 * ng]
    acc_ref, m_ref, l_ref = refs[4 * ng:]
    sb_tokens = q_refs[0].shape[1]
    t0 = pl.program_id(2) * sb_tokens
    blk = HEAD_DIM
    UNITS = 4
    rq = lax.broadcasted_iota(I32, (blk, blk), 0)
    ck = lax.broadcasted_iota(I32, (blk, blk), 1)
    bias_cur = jnp.where(ck <= rq, 0.0, NEG)
    bias_prev = jnp.where(ck >= rq, 0.0, NEG)
    nt = (((1,), (1,)), ((), ()))

    for g, (win, dil) in enumerate(DIL_PATTERNS):
        q_ref, k_ref, v_ref = q_refs[g], k_refs[g], v_refs[g]
        per_res = sb_tokens // dil // blk

        def rows(start, dil=dil):
            return pl.ds(start, blk, stride=dil) if dil > 1 else pl.ds(start, blk)

        def units(it, carry, g=g, dil=dil, per_res=per_res, rows=rows,
                  q_ref=q_ref, k_ref=k_ref, v_ref=v_ref):
            q0s, kcs, kps, hps = [], [], [], []
            for n in range(UNITS):
                idx = it * UNITS + n
                q0 = idx // per_res + (idx % per_res) * (blk * dil)
                k_cur = t0 + q0
                has_prev = k_cur >= blk * dil
                q0s.append(q0)
                kcs.append(k_cur)
                hps.append(has_prev)
                kps.append(jnp.where(has_prev, k_cur - blk * dil, k_cur))
            ss = []
            for q0, kc, kp, hp in zip(q0s, kcs, kps, hps):
                k2 = jnp.concatenate([k_ref[0, rows(kp), :], k_ref[0, rows(kc), :]], axis=0).astype(BF16)
                s = lax.dot_general(q_ref[0, rows(q0), :].astype(BF16), k2, nt, preferred_element_type=F32)
                ss.append(s + jnp.concatenate([jnp.where(hp, bias_prev, NEG), bias_cur], axis=1))
            ms = [jnp.max(jnp.maximum(s[:, :blk], s[:, blk:]), axis=-1, keepdims=True) for s in ss]
            ps = [jnp.exp2(s - m).astype(BF16) for s, m in zip(ss, ms)]
            for q0, kc, kp, m, p in zip(q0s, kcs, kps, ms, ps):
                v2 = jnp.concatenate([v_ref[0, rows(kp), :], v_ref[0, rows(kc), :]], axis=0).astype(BF16)
                av = jnp.dot(p, jnp.concatenate([v2, jnp.ones_like(v2)], axis=1), preferred_element_type=F32)
                acc_ref[g, rows(q0), :] = av[:, :HEAD_DIM]
                l_ref[g, rows(q0), :] = av[:, HEAD_DIM:]
                m_ref[g, rows(q0), :] = jnp.broadcast_to(m, (blk, LANES))
            return carry

        lax.fori_loop(0, dil * per_res // UNITS, units, 0)

    step = 256

    def merge(c, carry):
        sl = pl.ds(pl.multiple_of(c * step, step), step)
        ms = [m_ref[g, sl, :] for g in range(ng)]
        m_all = functools.reduce(jnp.maximum, ms)
        ws = [jnp.exp2(m - m_all) for m in ms]
        den = sum(w * l_ref[g, sl, :] for g, w in enumerate(ws))
        for g, w in enumerate(ws):
            o_refs[g][0, sl, :] = (acc_ref[g, sl, :] * (w / den)).astype(BF16)
        return carry

    lax.fori_loop(0, sb_tokens // step, merge, 0)


def _dilated(qc, kc, vc):
    b, s, _ = qc.shape
    ng, hg = len(DIL_PATTERNS), DIL_HEADS_PER_GROUP
    sb_tokens = HEAD_DIM * max(dil for _, dil in DIL_PATTERNS)
    assert s % sb_tokens == 0 and all(win == HEAD_DIM * dil for win, dil in DIL_PATTERNS)
    head = lambda g: (lambda bb, j, sb: (bb, sb, g * hg + j))
    head_full = lambda g: (lambda bb, j, sb: (bb, 0, g * hg + j))
    q_specs = [pl.BlockSpec((1, sb_tokens, HEAD_DIM), head(g)) for g in range(ng)]
    kv_specs = [pl.BlockSpec((1, s, HEAD_DIM), head_full(g)) for g in range(ng)]
    out_spec = pl.BlockSpec((1, sb_tokens, HEAD_DIM), lambda bb, j, sb: (bb, sb, j))
    return pl.pallas_call(
        _dilated_body,
        grid=(b, hg, s // sb_tokens),
        in_specs=q_specs + kv_specs + kv_specs,
        out_specs=[out_spec] * ng,
        out_shape=[jax.ShapeDtypeStruct((b, s, hg * HEAD_DIM), BF16)] * ng,
        scratch_shapes=[pltpu.VMEM((ng, sb_tokens, HEAD_DIM), F32)] * 3,
        compiler_params=_params("parallel", "parallel", "arbitrary"),
        name="dilated",
    )(*([qc] * ng + [kc] * ng + [vc] * ng))


def _out_proj_body(*refs):
    *y_refs, w_ref, x_ref, o_ref, wb_ref = refs

    @pl.when(pl.program_id(1) == 0)
    def _():
        wb_ref[...] = w_ref[...].astype(BF16)

    y = jnp.concatenate([y_ref[...] for y_ref in y_refs], axis=1)
    o_ref[...] = x_ref[...] + jnp.dot(y, wb_ref[...], preferred_element_type=F32)


def _out_proj(parts, w, x, layer, tm, tn):
    t, d = x.shape
    assert sum(a.shape[1] for a in parts) == w.shape[1]
    return pl.pallas_call(
        _out_proj_body,
        grid=(d // tn, t // tm),
        in_specs=[pl.BlockSpec((tm, a.shape[1]), lambda n, m: (m, 0)) for a in parts]
        + [pl.BlockSpec((None, w.shape[1], tn), lambda n, m: (layer, 0, n)),
           pl.BlockSpec((tm, tn), lambda n, m: (m, n))],
        out_specs=pl.BlockSpec((tm, tn), lambda n, m: (m, n)),
        out_shape=jax.ShapeDtypeStruct((t, d), F32),
        scratch_shapes=[pltpu.VMEM((w.shape[1], tn), BF16)],
        compiler_params=_params("parallel", "arbitrary"),
        name="out_proj",
    )(*parts, w, x)


def _final_norm_body(x_ref, g_ref, o_ref):
    o_ref[...] = _rms(x_ref[...], g_ref[...])


def _final_norm(x, g, tm):
    t, d = x.shape
    return pl.pallas_call(
        _final_norm_body,
        grid=(t // tm,),
        in_specs=[pl.BlockSpec((tm, d), lambda m: (m, 0)), pl.BlockSpec((1, d), lambda m: (0, 0))],
        out_specs=pl.BlockSpec((tm, d), lambda m: (m, 0)),
        out_shape=jax.ShapeDtypeStruct((t, d), F32),
        compiler_params=_params("parallel"),
        name="final_norm",
    )(x, g)


def _align_w_in_body(x_ref, o_ref):
    o_wi = SEG_QI[1] + IDX_DIM
    o_qc = o_wi + IDX_HEADS
    rows = x_ref.shape[0]
    n_c = D_IN_ALIGNED - SEG_QC[0]
    o_ref[:, :o_wi] = x_ref[:, :o_wi]
    o_ref[:, o_wi:SEG_WI[0]] = jnp.zeros((rows, SEG_WI[0] - o_wi), BF16)
    o_ref[:, SEG_WI[0]:SEG_WI[0] + IDX_HEADS] = x_ref[:, o_wi:o_qc]
    o_ref[:, SEG_WI[0] + IDX_HEADS:SEG_QC[0]] = jnp.zeros((rows, LANES - IDX_HEADS), BF16)
    o_ref[:, SEG_QC[0]:] = x_ref[:, o_qc:o_qc + n_c]


def _align_w_in(w, tr):
    depth, d, n = w.shape
    assert n - (SEG_QI[1] + IDX_DIM + IDX_HEADS) == D_IN_ALIGNED - SEG_QC[0]
    w = jnp.pad(w.astype(BF16), ((0, 0), (0, 0), (0, D_IN_ALIGNED - n)))
    return pl.pallas_call(
        _align_w_in_body,
        grid=(depth, d // tr),
        in_specs=[pl.BlockSpec((None, tr, D_IN_ALIGNED), lambda l, r: (l, r, 0))],
        out_specs=pl.BlockSpec((None, tr, D_IN_ALIGNED), lambda l, r: (l, r, 0)),
        out_shape=jax.ShapeDtypeStruct((depth, d, D_IN_ALIGNED), BF16),
        compiler_params=_params("parallel", "parallel"),
        name="align_w_in",
    )(w)


def _tile(n, want):
    while n % want:
        want //= 2
    return want


def kernel(x, positions, norm_ffn1, ffn1_gate, ffn1_up, ffn1_down, norm_mix, w_in, conv_w, w_out,
           norm_ffn2, ffn2_gate, ffn2_up, ffn2_down, norm_final):
    b, s, d = x.shape
    t = b * s
    depth = w_in.shape[0]
    tm_big = _tile(t, 1024)
    tm_proj = _tile(s, 512)
    tq_dsa = _tile(s, 256)

    pos = positions.astype(F32).reshape(t, 1)
    tabs = (*_rope_tables(pos, HEAD_DIM, tm_big), *_rope_tables(pos, IDX_DIM, tm_big))

    def ffn(xf, g, wg, wu, wd, layer):
        h = _ffn_up(xf, g.reshape(depth, 1, d), wg, wu, layer, tm_big, 512)
        return _ffn_down(h, wd, xf, layer, _tile(t, 512), 512)

    w_in_al = _align_w_in(w_in, _tile(d, 256))
    xf = x.reshape(t, d)
    for i in range(depth):
        xf = ffn(xf, norm_ffn1, ffn1_gate, ffn1_up, ffn1_down, i)
        ya, qb, kb, vb, qi, ki, wi, qc, kc, vc = _in_proj(
            xf, norm_mix.reshape(depth, 1, d), w_in_al, conv_w, tabs, i, s, tm_proj)
        r3 = lambda a: a.reshape(b, s, a.shape[-1])
        yb = _dsa(r3(qb), r3(kb), r3(vb), r3(qi), r3(ki), r3(wi), tq_dsa)
        ycs = _dilated(r3(qc), r3(kc), r3(vc))
        parts = [ya, yb.reshape(t, -1)] + [yc.reshape(t, -1) for yc in ycs]
        xf = _out_proj(parts, w_out, xf, i, tm_big, 512)
        xf = ffn(xf, norm_ffn2, ffn2_gate, ffn2_up, ffn2_down, i)
    return _final_norm(xf, norm_final.reshape(1, d), tm_big).reshape(b, s, d)
```

```python
import functools

import jax
import jax.numpy as jnp
from jax import lax
from jax.experimental import pallas as pl
from jax.experimental.pallas import tpu as pltpu

F32 = jnp.float32
BF16 = jnp.bfloat16
I32 = jnp.int32
I16 = jnp.int16
HALF16 = 1 << 15
ONES_ROWS = 16

HEAD_DIM = 128
CONV_CH = 512
CONV_WIDTH = 3
DSA_HEADS = 6
IDX_HEADS = 16
IDX_DIM = 64
TOPK_MAX = 256
DIL_PATTERNS = ((128, 1), (512, 4), (2048, 16))
DIL_HEADS_PER_GROUP = 2
DIL_HEADS = len(DIL_PATTERNS) * DIL_HEADS_PER_GROUP
ROPE_THETA = 10000.0
RMS_EPS = 1e-6

LANES = 128
VMEM_LIMIT = 56 * 1024 * 1024
NEG = -1e30
INT_MIN = -2 ** 31
LOG2E = 1.4426950408889634

SEG_CONV = (0, 3 * CONV_CH)
SEG_QB = (SEG_CONV[1], SEG_CONV[1] + DSA_HEADS * HEAD_DIM)
SEG_KV = (SEG_QB[1], SEG_QB[1] + 2 * HEAD_DIM)
SEG_QI = (SEG_KV[1], SEG_KV[1] + IDX_HEADS * IDX_DIM)
SEG_KI = (SEG_QI[1], SEG_QI[1] + LANES)
SEG_WI = (SEG_KI[1], SEG_KI[1] + LANES)
SEG_QC = (SEG_WI[1], SEG_WI[1] + DIL_HEADS * HEAD_DIM)
SEG_KC = (SEG_QC[1], SEG_QC[1] + DIL_HEADS * HEAD_DIM)
SEG_VC = (SEG_KC[1], SEG_KC[1] + DIL_HEADS * HEAD_DIM)
D_IN_ALIGNED = SEG_VC[1]


def _params(*sem):
    return pltpu.CompilerParams(dimension_semantics=sem, vmem_limit_bytes=VMEM_LIMIT)


def _rms(x, g):
    ms = jnp.mean(x * x, axis=-1, keepdims=True)
    return x * lax.rsqrt(ms + RMS_EPS) * g


def _ffn_up_body(x_ref, g_ref, wg_ref, wu_ref, h_ref, xn_ref):
    @pl.when(pl.program_id(1) == 0)
    def _():
        xn_ref[...] = _rms(x_ref[...], g_ref[...]).astype(BF16)

    xn = xn_ref[...]
    a = jnp.dot(xn, wg_ref[...].astype(BF16), preferred_element_type=F32)
    b = jnp.dot(xn, wu_ref[...].astype(BF16), preferred_element_type=F32)
    h_ref[...] = (a * jax.nn.sigmoid(a) * b).astype(BF16)


def _layer_block(layer, rows, cols):
    return pl.BlockSpec((None, rows, cols), lambda m, n: (layer, 0, n))


def _ffn_up(x, g, wg, wu, layer, tm, tn):
    t, d = x.shape
    f = wg.shape[2]
    return pl.pallas_call(
        _ffn_up_body,
        grid=(t // tm, f // tn),
        in_specs=[
            pl.BlockSpec((tm, d), lambda m, n: (m, 0)),
            pl.BlockSpec((None, 1, d), lambda m, n: (layer, 0, 0)),
            _layer_block(layer, d, tn),
            _layer_block(layer, d, tn),
        ],
        out_specs=pl.BlockSpec((tm, tn), lambda m, n: (m, n)),
        out_shape=jax.ShapeDtypeStruct((t, f), BF16),
        scratch_shapes=[pltpu.VMEM((tm, d), BF16)],
        compiler_params=_params("parallel", "arbitrary"),
        name="ffn_up",
    )(x, g, wg, wu)


def _ffn_down_body(h_ref, w_ref, x_ref, o_ref, wb_ref):
    @pl.when(pl.program_id(1) == 0)
    def _():
        wb_ref[...] = w_ref[...].astype(BF16)

    y = jnp.dot(h_ref[...], wb_ref[...], preferred_element_type=F32)
    o_ref[...] = x_ref[...] + 0.5 * y


def _ffn_down(h, wd, x, layer, tm, tn):
    t, f = h.shape
    d = wd.shape[2]
    return pl.pallas_call(
        _ffn_down_body,
        grid=(d // tn, t // tm),
        in_specs=[
            pl.BlockSpec((tm, f), lambda n, m: (m, 0)),
            pl.BlockSpec((None, f, tn), lambda n, m: (layer, 0, n)),
            pl.BlockSpec((tm, tn), lambda n, m: (m, n)),
        ],
        out_specs=pl.BlockSpec((tm, tn), lambda n, m: (m, n)),
        out_shape=jax.ShapeDtypeStruct((t, d), F32),
        scratch_shapes=[pltpu.VMEM((f, tn), BF16)],
        compiler_params=_params("parallel", "arbitrary"),
        name="ffn_down",
    )(h, wd, x)


def _rope_table_body(pos_ref, inv_ref, sgn_ref, cos_ref, sin_ref):
    ang = pos_ref[...] * inv_ref[...]
    cos_ref[...] = jnp.cos(ang)
    sin_ref[...] = jnp.sin(ang) * sgn_ref[...]


def _rope_tables(pos, dim, tm):
    t = pos.shape[0]
    half = dim // 2
    inv = 1.0 / (ROPE_THETA ** (jnp.arange(0, dim, 2, dtype=F32) / dim))
    lane = jnp.arange(LANES)
    inv_l = inv[lane % half].reshape(1, LANES)
    sgn_l = jnp.where(lane % dim < half, -1.0, 1.0).astype(F32).reshape(1, LANES)
    row = pl.BlockSpec((1, LANES), lambda m: (0, 0))
    tab = pl.BlockSpec((tm, LANES), lambda m: (m, 0))
    return pl.pallas_call(
        _rope_table_body,
        grid=(t // tm,),
        in_specs=[pl.BlockSpec((tm, 1), lambda m: (m, 0)), row, row],
        out_specs=[tab, tab],
        out_shape=[jax.ShapeDtypeStruct((t, LANES), F32)] * 2,
        compiler_params=_params("parallel"),
        name="rope_tables",
    )(pos, inv_l, sgn_l)


def _rope128(x, cos, sin):
    return x * cos + pltpu.roll(x, HEAD_DIM // 2, 1) * sin


def _rope64(x, cos, sin, lo_half):
    partner = jnp.where(lo_half, pltpu.roll(x, LANES - IDX_DIM // 2, 1), pltpu.roll(x, IDX_DIM // 2, 1))
    return x * cos + partner * sin


def _in_proj_body(seq_tiles, x_ref, g_ref, w_ref, cw_ref, ch_ref, sh_ref, ci_ref, si_ref,
                  ya_ref, qb_ref, kb_ref, vb_ref, qi_ref, ki_ref, wi_ref, qc_ref, kc_ref, vc_ref,
                  u_ref):
    tm = x_ref.shape[0]
    xn = _rms(x_ref[...], g_ref[...]).astype(BF16)

    def proj(seg):
        return jnp.dot(xn, w_ref[:, seg[0]:seg[1]], preferred_element_type=F32)

    p = proj(SEG_CONV)
    h, gate_b, gate_c = p[:, :CONV_CH], p[:, CONV_CH:2 * CONV_CH], p[:, 2 * CONV_CH:]
    u = gate_c * h

    @pl.when(pl.program_id(0) % seq_tiles == 0)
    def _():
        u_ref[0:8, :] = jnp.zeros((8, CONV_CH), F32)

    u_ref[8:8 + tm, :] = u
    cw = cw_ref[...]
    y = cw[2:3, :] * u + cw[1:2, :] * u_ref[7:7 + tm, :] + cw[0:1, :] * u_ref[6:6 + tm, :]
    u_ref[0:8, :] = u[tm - 8:, :]
    ya_ref[...] = (gate_b * y).astype(BF16)

    ch, sh, ci, si = ch_ref[...], sh_ref[...], ci_ref[...], si_ref[...]
    scale = HEAD_DIM ** -0.5 * LOG2E

    def rope_heads(p, n, mul):
        return jnp.concatenate(
            [_rope128(p[:, j * LANES:(j + 1) * LANES], ch, sh) * mul for j in range(n)], axis=1)

    qb_ref[...] = rope_heads(proj(SEG_QB), DSA_HEADS, scale).astype(BF16)
    p = proj(SEG_KV)
    kb_ref[...] = _rope128(p[:, :HEAD_DIM], ch, sh).astype(BF16)
    vb_ref[...] = p[:, HEAD_DIM:].astype(BF16)

    lo_half = lax.broadcasted_iota(I32, (tm, LANES), 1) % IDX_DIM < IDX_DIM // 2
    p = proj(SEG_QI)
    qi_ref[...] = jnp.concatenate(
        [_rope64(p[:, j * LANES:(j + 1) * LANES], ci, si, lo_half) * (IDX_DIM ** -0.5)
         for j in range(IDX_HEADS * IDX_DIM // LANES)], axis=1).astype(BF16)
    ki_ref[...] = _rope64(proj(SEG_KI), ci, si, lo_half)[:, :IDX_DIM].astype(BF16)
    wi_ref[...] = proj(SEG_WI) * (IDX_HEADS ** -0.5)

    qc_ref[...] = rope_heads(proj(SEG_QC), DIL_HEADS, scale)
    kc_ref[...] = rope_heads(proj(SEG_KC), DIL_HEADS, 1.0)
    vc_ref[...] = proj(SEG_VC)


def _in_proj(x, g, w, cw, tabs, layer, seq, tm):
    t, d = x.shape
    n = w.shape[2]
    row = lambda width: pl.BlockSpec((tm, width), lambda m: (m, 0))
    const = lambda shape: pl.BlockSpec((None,) + shape, lambda m: (layer, 0, 0))
    widths = [CONV_CH, DSA_HEADS * HEAD_DIM, HEAD_DIM, HEAD_DIM, IDX_HEADS * IDX_DIM, IDX_DIM, LANES,
              DIL_HEADS * HEAD_DIM, DIL_HEADS * HEAD_DIM, DIL_HEADS * HEAD_DIM]
    dtypes = [BF16, BF16, BF16, BF16, BF16, BF16, F32, F32, F32, F32]
    return pl.pallas_call(
        functools.partial(_in_proj_body, seq // tm),
        grid=(t // tm,),
        in_specs=[row(d), const((1, d)),
                  pl.BlockSpec((None, d, n), lambda m: (layer, 0, 0), pipeline_mode=pl.Buffered(1)),
                  const((CONV_WIDTH, CONV_CH)), row(LANES), row(LANES), row(LANES), row(LANES)],
        out_specs=[row(wd) for wd in widths],
        out_shape=[jax.ShapeDtypeStruct((t, wd), dt) for wd, dt in zip(widths, dtypes)],
        scratch_shapes=[pltpu.VMEM((tm + 8, CONV_CH), F32)],
        compiler_params=_params("arbitrary"),
        name="in_proj",
    )(x, g, w, cw, *tabs)


def _dsa_body(topk, qb_ref, kb_ref, vb_ref, qi_ref, ki_ref, wi_ref, o_ref,
              keys_ref, hi_ref, lo_ref, lo2_ref, qt_ref, qbt_ref, vt_ref, acc_ref):
    tq = qb_ref.shape[1]
    seq = kb_ref.shape[1]
    nh = DSA_HEADS
    i = pl.program_id(1)
    nch = i + 1

    def chunk(c):
        return pl.ds(pl.multiple_of(c * tq, tq), tq)

    def transpose_bf16(a):
        return a.astype(F32).T.astype(BF16)

    @pl.when(i == 0)
    def _():
        for c2 in range(seq // (2 * tq)):
            vt_ref[c2, :HEAD_DIM, :] = transpose_bf16(vb_ref[0, c2 * 2 * tq:(c2 + 1) * 2 * tq, :])
            vt_ref[c2, HEAD_DIM:, :] = jnp.ones((ONES_ROWS, 2 * tq), BF16)

    qt_ref[...] = transpose_bf16(qi_ref[0])
    for h in range(nh):
        qbt_ref[:, h * tq:(h + 1) * tq] = transpose_bf16(qb_ref[0, :, h * HEAD_DIM:(h + 1) * HEAD_DIM])
    wt = wi_ref[0].T
    krow = lax.broadcasted_iota(I32, (tq, tq), 0)

    def pair(c2):
        return pl.ds(pl.multiple_of(c2 * (2 * tq), 2 * tq), 2 * tq)

    def score_rows(rows, n, first_key):
        kc = ki_ref[0, rows, :]
        acc = jnp.zeros((n, tq), F32)
        for h in range(IDX_HEADS):
            lg = jnp.dot(kc, qt_ref[h * IDX_DIM:(h + 1) * IDX_DIM, :], preferred_element_type=F32)
            acc = acc + jnp.maximum(lg, 0.0) * wt[h:h + 1, :]
        bits = pltpu.bitcast(acc, I32)
        key = bits ^ ((bits >> 31) & 0x7FFFFFFF)
        causal = lax.broadcasted_iota(I32, (n, tq), 0) + first_key <= lax.broadcasted_iota(I32, (n, tq), 1) + i * tq
        key = jnp.where(causal, key, INT_MIN)
        keys_ref[rows, :] = key
        hi_ref[rows, :] = (key >> 16).astype(I16)
        lo_ref[rows, :] = ((key & 0xFFFF) - HALF16).astype(I16)

    def score_pair(c2, carry):
        score_rows(pair(c2), 2 * tq, c2 * (2 * tq))
        return carry

    lax.fori_loop(0, nch // 2, score_pair, 0)

    npair = (nch + 1) // 2

    @pl.when(nch % 2 == 1)
    def _():
        score_rows(chunk(nch - 1), tq, (nch - 1) * tq)
        keys_ref[chunk(nch), :] = jnp.full((tq, tq), INT_MIN, I32)
        hi_ref[chunk(nch), :] = jnp.full((tq, tq), -HALF16, I16)
        lo_ref[chunk(nch), :] = jnp.full((tq, tq), -HALF16, I16)

    def count(pred):
        def body(c, acc):
            hit = pred(keys_ref[chunk(c), :], krow + c * tq).astype(I32)
            return acc + jnp.sum(hit.reshape(tq // 8, 8, tq), axis=0)
        acc = lax.fori_loop(0, nch, body, jnp.zeros((8, tq), I32))
        return jnp.sum(acc, axis=0, keepdims=True)

    def count16(ref, pred):
        def body(c2, acc):
            hit = jnp.where(pred(ref[pair(c2), :]), jnp.ones((), BF16), jnp.zeros((), BF16))
            parts = [hit[r * 16:(r + 1) * 16] for r in range(2 * tq // 16)]
            while len(parts) > 1:
                parts = [parts[j] + parts[j + 1] for j in range(0, len(parts), 2)]
            return acc + parts[0].astype(F32)
        acc = lax.fori_loop(0, npair, body, jnp.zeros((16, tq), F32))
        return jnp.sum(acc, axis=0, keepdims=True)

    def kth_largest16(ref, k):
        def search_bit(b, t_u):
            cand = t_u | (jnp.int32(1) << (15 - b))
            cand16 = (cand - HALF16).astype(I16)
            return jnp.where(count16(ref, lambda v: v >= cand16) >= k, cand, t_u)
        return lax.fori_loop(0, 16, search_bit, jnp.zeros((1, tq), I32)) - HALF16

    t_hi = kth_largest16(hi_ref, float(topk))
    t_hi16 = t_hi.astype(I16)
    need_lo = float(topk) - count16(hi_ref, lambda v: v > t_hi16)

    def bucket(c2, carry):
        lo2_ref[pair(c2), :] = jnp.where(hi_ref[pair(c2), :] == t_hi16, lo_ref[pair(c2), :],
                                         jnp.full((), -HALF16, I16))
        return carry

    lax.fori_loop(0, npair, bucket, 0)
    t_lo = kth_largest16(lo2_ref, need_lo)
    thr = jnp.maximum((t_hi << 16) | (t_lo + HALF16), INT_MIN + 1)
    n_ge = count(lambda k, _: k >= thr)

    idx_bits = seq.bit_length() - 1

    def tie_bound():
        need = topk - count(lambda k, _: k > thr)
        def bit(b, j):
            cand = j | (jnp.int32(1) << (idx_bits - 1 - b))
            below = count(lambda k, idx: (k == thr) & (idx < cand))
            return jnp.where(below < need, cand, j)
        j = lax.fori_loop(0, idx_bits, bit, jnp.zeros((1, tq), I32))
        return jnp.where(n_ge > topk, j, seq)

    bound = lax.cond(jnp.max(n_ge) > topk, tie_bound, lambda: jnp.full((1, tq), seq, I32))

    acc_ref[...] = jnp.zeros(acc_ref.shape, F32)

    krow2 = lax.broadcasted_iota(I32, (2 * tq, tq), 0)

    def qk(c2):
        return jnp.dot(kb_ref[0, pair(c2), :], qbt_ref[...], preferred_element_type=F32)

    def attend(c2, m):
        kblk = keys_ref[pair(c2), :]
        sel = (kblk > thr) | ((kblk == thr) & (krow2 + c2 * (2 * tq) <= bound))
        bias = jnp.where(sel, 0.0, NEG)
        st = qk(c2) + jnp.concatenate([bias] * nh, axis=1)
        m_new = jnp.maximum(m, jnp.max(st, axis=0, keepdims=True))
        p = jnp.exp2(st - m_new).astype(BF16)
        acc_ref[...] = acc_ref[...] * jnp.exp2(m - m_new) + jnp.dot(vt_ref[c2], p, preferred_element_type=F32)
        return m_new

    lax.fori_loop(0, npair, attend, jnp.full((1, nh * tq), NEG, F32))
    out = acc_ref[:HEAD_DIM, :] / acc_ref[HEAD_DIM:HEAD_DIM + 1, :]
    for h in range(nh):
        o_ref[0, :, h * HEAD_DIM:(h + 1) * HEAD_DIM] = out[:, h * tq:(h + 1) * tq].T.astype(BF16)


def _dsa(qb, kb, vb, qi, ki, wi, tq):
    b, s, _ = qb.shape
    topk = min(TOPK_MAX, s // 4)
    assert s % (2 * tq) == 0
    blk = lambda width: pl.BlockSpec((1, tq, width), lambda bb, i: (bb, i, 0))
    full = lambda width: pl.BlockSpec((1, s, width), lambda bb, i: (bb, 0, 0))
    return pl.pallas_call(
        functools.partial(_dsa_body, topk),
        grid=(b, s // tq),
        in_specs=[blk(DSA_HEADS * HEAD_DIM), full(HEAD_DIM), full(HEAD_DIM),
                  blk(IDX_HEADS * IDX_DIM), full(IDX_DIM), blk(LANES)],
        out_specs=blk(DSA_HEADS * HEAD_DIM),
        out_shape=jax.ShapeDtypeStruct((b, s, DSA_HEADS * HEAD_DIM), BF16),
        scratch_shapes=[pltpu.VMEM((s, tq), I32),
                        pltpu.VMEM((s, tq), I16),
                        pltpu.VMEM((s, tq), I16),
                        pltpu.VMEM((s, tq), I16),
                        pltpu.VMEM((IDX_HEADS * IDX_DIM, tq), BF16),
                        pltpu.VMEM((HEAD_DIM, DSA_HEADS * tq), BF16),
                        pltpu.VMEM((s // (2 * tq), HEAD_DIM + ONES_ROWS, 2 * tq), BF16),
                        pltpu.VMEM((HEAD_DIM + ONES_ROWS, DSA_HEADS * tq), F32)],
        compiler_params=_params("parallel", "arbitrary"),
        name="dsa",
    )(qb, kb, vb, qi, ki, wi)


def _dilated_body(*refs):
    ng = len(DIL_PATTERNS)
    q_refs, k_refs, v_refs = refs[0:ng], refs[ng:2 * ng], refs[2 * ng:3 * ng]
    o_refs = refs[3 * ng:4 * ng]
    acc_ref, m_ref, l_ref = refs[4 * ng:]
    sb_tokens = q_refs[0].shape[1]
    t0 = pl.program_id(2) * sb_tokens
    blk = HEAD_DIM
    UNITS = 4
    rq = lax.broadcasted_iota(I32, (blk, blk), 0)
    ck = lax.broadcasted_iota(I32, (blk, blk), 1)
    bias_cur = jnp.where(ck <= rq, 0.0, NEG)
    bias_prev = jnp.where(ck >= rq, 0.0, NEG)
    nt = (((1,), (1,)), ((), ()))

    for g, (win, dil) in enumerate(DIL_PATTERNS):
        q_ref, k_ref, v_ref = q_refs[g], k_refs[g], v_refs[g]
        per_res = sb_tokens // dil // blk

        def rows(start, dil=dil):
            return pl.ds(start, blk, stride=dil) if dil > 1 else pl.ds(start, blk)

        def units(it, carry, g=g, dil=dil, per_res=per_res, rows=rows,
                  q_ref=q_ref, k_ref=k_ref, v_ref=v_ref):
            q0s, kcs, kps, hps = [], [], [], []
            for n in range(UNITS):
                idx = it * UNITS + n
                q0 = idx // per_res + (idx % per_res) * (blk * dil)
                k_cur = t0 + q0
                has_prev = k_cur >= blk * dil
                q0s.append(q0)
                kcs.append(k_cur)
                hps.append(has_prev)
                kps.append(jnp.where(has_prev, k_cur - blk * dil, k_cur))
            ss = []
            for q0, kc, kp, hp in zip(q0s, kcs, kps, hps):
                k2 = jnp.concatenate([k_ref[0, rows(kp), :], k_ref[0, rows(kc), :]], axis=0).astype(BF16)
                s = lax.dot_general(q_ref[0, rows(q0), :].astype(BF16), k2, nt, preferred_element_type=F32)
                ss.append(s + jnp.concatenate([jnp.where(hp, bias_prev, NEG), bias_cur], axis=1))
            ms = [jnp.max(jnp.maximum(s[:, :blk], s[:, blk:]), axis=-1, keepdims=True) for s in ss]
            ps = [jnp.exp2(s - m).astype(BF16) for s, m in zip(ss, ms)]
            for q0, kc, kp, m, p in zip(q0s, kcs, kps, ms, ps):
                v2 = jnp.concatenate([v_ref[0, rows(kp), :], v_ref[0, rows(kc), :]], axis=0).astype(BF16)
                av = jnp.dot(p, jnp.concatenate([v2, jnp.ones_like(v2)], axis=1), preferred_element_type=F32)
                acc_ref[g, rows(q0), :] = av[:, :HEAD_DIM]
                l_ref[g, rows(q0), :] = av[:, HEAD_DIM:]
                m_ref[g, rows(q0), :] = jnp.broadcast_to(m, (blk, LANES))
            return carry

        lax.fori_loop(0, dil * per_res // UNITS, units, 0)

    step = 256

    def merge(c, carry):
        sl = pl.ds(pl.multiple_of(c * step, step), step)
        ms = [m_ref[g, sl, :] for g in range(ng)]
        m_all = functools.reduce(jnp.maximum, ms)
        ws = [jnp.exp2(m - m_all) for m in ms]
        den = sum(w * l_ref[g, sl, :] for g, w in enumerate(ws))
        for g, w in enumerate(ws):
            o_refs[g][0, sl, :] = (acc_ref[g, sl, :] * (w / den)).astype(BF16)
        return carry

    lax.fori_loop(0, sb_tokens // step, merge, 0)


def _dilated(qc, kc, vc):
    b, s, _ = qc.shape
    ng, hg = len(DIL_PATTERNS), DIL_HEADS_PER_GROUP
    sb_tokens = HEAD_DIM * max(dil for _, dil in DIL_PATTERNS)
    assert s % sb_tokens == 0 and all(win == HEAD_DIM * dil for win, dil in DIL_PATTERNS)
    head = lambda g: (lambda bb, j, sb: (bb, sb, g * hg + j))
    head_full = lambda g: (lambda bb, j, sb: (bb, 0, g * hg + j))
    q_specs = [pl.BlockSpec((1, sb_tokens, HEAD_DIM), head(g)) for g in range(ng)]
    kv_specs = [pl.BlockSpec((1, s, HEAD_DIM), head_full(g)) for g in range(ng)]
    out_spec = pl.BlockSpec((1, sb_tokens, HEAD_DIM), lambda bb, j, sb: (bb, sb, j))
    return pl.pallas_call(
        _dilated_body,
        grid=(b, hg, s // sb_tokens),
        in_specs=q_specs + kv_specs + kv_specs,
        out_specs=[out_spec] * ng,
        out_shape=[jax.ShapeDtypeStruct((b, s, hg * HEAD_DIM), BF16)] * ng,
        scratch_shapes=[pltpu.VMEM((ng, sb_tokens, HEAD_DIM), F32)] * 3,
        compiler_params=_params("parallel", "parallel", "arbitrary"),
        name="dilated",
    )(*([qc] * ng + [kc] * ng + [vc] * ng))


def _out_proj_body(*refs):
    *y_refs, w_ref, x_ref, o_ref, wb_ref = refs

    @pl.when(pl.program_id(1) == 0)
    def _():
        wb_ref[...] = w_ref[...].astype(BF16)

    y = jnp.concatenate([y_ref[...] for y_ref in y_refs], axis=1)
    o_ref[...] = x_ref[...] + jnp.dot(y, wb_ref[...], preferred_element_type=F32)


def _out_proj(parts, w, x, layer, tm, tn):
    t, d = x.shape
    assert sum(a.shape[1] for a in parts) == w.shape[1]
    return pl.pallas_call(
        _out_proj_body,
        grid=(d // tn, t // tm),
        in_specs=[pl.BlockSpec((tm, a.shape[1]), lambda n, m: (m, 0)) for a in parts]
        + [pl.BlockSpec((None, w.shape[1], tn), lambda n, m: (layer, 0, n)),
           pl.BlockSpec((tm, tn), lambda n, m: (m, n))],
        out_specs=pl.BlockSpec((tm, tn), lambda n, m: (m, n)),
        out_shape=jax.ShapeDtypeStruct((t, d), F32),
        scratch_shapes=[pltpu.VMEM((w.shape[1], tn), BF16)],
        compiler_params=_params("parallel", "arbitrary"),
        name="out_proj",
    )(*parts, w, x)


def _final_norm_body(x_ref, g_ref, o_ref):
    o_ref[...] = _rms(x_ref[...], g_ref[...])


def _final_norm(x, g, tm):
    t, d = x.shape
    return pl.pallas_call(
        _final_norm_body,
        grid=(t // tm,),
        in_specs=[pl.BlockSpec((tm, d), lambda m: (m, 0)), pl.BlockSpec((1, d), lambda m: (0, 0))],
        out_specs=pl.BlockSpec((tm, d), lambda m: (m, 0)),
        out_shape=jax.ShapeDtypeStruct((t, d), F32),
        compiler_params=_params("parallel"),
        name="final_norm",
    )(x, g)


def _align_w_in_body(x_ref, o_ref):
    o_wi = SEG_QI[1] + IDX_DIM
    o_qc = o_wi + IDX_HEADS
    rows = x_ref.shape[0]
    o_ref[:, :o_wi] = x_ref[:, :o_wi]
    o_ref[:, o_wi:SEG_WI[0]] = jnp.zeros((rows, SEG_WI[0] - o_wi), BF16)
    o_ref[:, SEG_WI[0]:SEG_WI[0] + IDX_HEADS] = x_ref[:, o_wi:o_qc]
    o_ref[:, SEG_WI[0] + IDX_HEADS:SEG_QC[0]] = jnp.zeros((rows, LANES - IDX_HEADS), BF16)
    o_ref[:, SEG_QC[0]:] = x_ref[:, o_qc:]


def _align_w_in(w, tr):
    depth, d, n = w.shape
    assert n - (SEG_QI[1] + IDX_DIM + IDX_HEADS) == D_IN_ALIGNED - SEG_QC[0]
    w = w.astype(BF16)
    return pl.pallas_call(
        _align_w_in_body,
        grid=(depth, d // tr),
        in_specs=[pl.BlockSpec((None, tr, n), lambda l, r: (l, r, 0))],
        out_specs=pl.BlockSpec((None, tr, D_IN_ALIGNED), lambda l, r: (l, r, 0)),
        out_shape=jax.ShapeDtypeStruct((depth, d, D_IN_ALIGNED), BF16),
        compiler_params=_params("parallel", "parallel"),
        name="align_w_in",
    )(w)


def _tile(n, want):
    while n % want:
        want //= 2
    return want


def kernel(x, positions, norm_ffn1, ffn1_gate, ffn1_up, ffn1_down, norm_mix, w_in, conv_w, w_out,
           norm_ffn2, ffn2_gate, ffn2_up, ffn2_down, norm_final):
    b, s, d = x.shape
    t = b * s
    depth = w_in.shape[0]
    tm_big = _tile(t, 1024)
    tm_proj = _tile(s, 512)
    tq_dsa = _tile(s, 256)

    pos = positions.astype(F32).reshape(t, 1)
    tabs = (*_rope_tables(pos, HEAD_DIM, tm_big), *_rope_tables(pos, IDX_DIM, tm_big))

    def ffn(xf, g, wg, wu, wd, layer):
        h = _ffn_up(xf, g.reshape(depth, 1, d), wg, wu, layer, tm_big, 512)
        return _ffn_down(h, wd, xf, layer, _tile(t, 512), 512)

    w_in_al = _align_w_in(w_in, _tile(d, 256))
    xf = x.reshape(t, d)
    for i in range(depth):
        xf = ffn(xf, norm_ffn1, ffn1_gate, ffn1_up, ffn1_down, i)
        ya, qb, kb, vb, qi, ki, wi, qc, kc, vc = _in_proj(
            xf, norm_mix.reshape(depth, 1, d), w_in_al, conv_w, tabs, i, s, tm_proj)
        r3 = lambda a: a.reshape(b, s, a.shape[-1])
        yb = _dsa(r3(qb), r3(kb), r3(vb), r3(qi), r3(ki), r3(wi), tq_dsa)
        ycs = _dilated(r3(qc), r3(kc), r3(vc))
        parts = [ya, yb.reshape(t, -1)] + [yc.reshape(t, -1) for yc in ycs]
        xf = _out_proj(parts, w_out, xf, i, tm_big, 1024)
        xf = ffn(xf, norm_ffn2, ffn2_gate, ffn2_up, ffn2_down, i)
    return _final_norm(xf, norm_final.reshape(1, d), tm_big).reshape(b, s, d)
```

```python
import functools

import jax
import jax.numpy as jnp
from jax import lax
from jax.experimental import pallas as pl
from jax.experimental.pallas import tpu as pltpu

F32 = jnp.float32
BF16 = jnp.bfloat16
I32 = jnp.int32
I16 = jnp.int16
HALF16 = 1 << 15
ONES_ROWS = 16

HEAD_DIM = 128
CONV_CH = 512
CONV_WIDTH = 3
DSA_HEADS = 6
IDX_HEADS = 16
IDX_DIM = 64
TOPK_MAX = 256
DIL_PATTERNS = ((128, 1), (512, 4), (2048, 16))
DIL_HEADS_PER_GROUP = 2
DIL_HEADS = len(DIL_PATTERNS) * DIL_HEADS_PER_GROUP
ROPE_THETA = 10000.0
RMS_EPS = 1e-6

LANES = 128
VMEM_LIMIT = 56 * 1024 * 1024
NEG = -1e30
INT_MIN = -2 ** 31
LOG2E = 1.4426950408889634

SEG_CONV = (0, 3 * CONV_CH)
SEG_QB = (SEG_CONV[1], SEG_CONV[1] + DSA_HEADS * HEAD_DIM)
SEG_KV = (SEG_QB[1], SEG_QB[1] + 2 * HEAD_DIM)
SEG_QI = (SEG_KV[1], SEG_KV[1] + IDX_HEADS * IDX_DIM)
SEG_KI = (SEG_QI[1], SEG_QI[1] + LANES)
SEG_WI = (SEG_KI[1], SEG_KI[1] + LANES)
SEG_QC = (SEG_WI[1], SEG_WI[1] + DIL_HEADS * HEAD_DIM)
SEG_KC = (SEG_QC[1], SEG_QC[1] + DIL_HEADS * HEAD_DIM)
SEG_VC = (SEG_KC[1], SEG_KC[1] + DIL_HEADS * HEAD_DIM)
D_IN_ALIGNED = SEG_VC[1]


def _params(*sem):
    return pltpu.CompilerParams(dimension_semantics=sem, vmem_limit_bytes=VMEM_LIMIT)


def _rms(x, g):
    ms = jnp.mean(x * x, axis=-1, keepdims=True)
    return x * lax.rsqrt(ms + RMS_EPS) * g


def _ffn_up_body(x_ref, g_ref, wg_ref, wu_ref, h_ref, xn_ref):
    @pl.when(pl.program_id(1) == 0)
    def _():
        xn_ref[...] = _rms(x_ref[...], g_ref[...]).astype(BF16)

    xn = xn_ref[...]
    a = jnp.dot(xn, wg_ref[...].astype(BF16), preferred_element_type=F32)
    b = jnp.dot(xn, wu_ref[...].astype(BF16), preferred_element_type=F32)
    h_ref[...] = (a * jax.nn.sigmoid(a) * b).astype(BF16)


def _layer_block(layer, rows, cols):
    return pl.BlockSpec((None, rows, cols), lambda m, n: (layer, 0, n))


def _ffn_up(x, g, wg, wu, layer, tm, tn):
    t, d = x.shape
    f = wg.shape[2]
    return pl.pallas_call(
        _ffn_up_body,
        grid=(t // tm, f // tn),
        in_specs=[
            pl.BlockSpec((tm, d), lambda m, n: (m, 0)),
            pl.BlockSpec((None, 1, d), lambda m, n: (layer, 0, 0)),
            _layer_block(layer, d, tn),
            _layer_block(layer, d, tn),
        ],
        out_specs=pl.BlockSpec((tm, tn), lambda m, n: (m, n)),
        out_shape=jax.ShapeDtypeStruct((t, f), BF16),
        scratch_shapes=[pltpu.VMEM((tm, d), BF16)],
        compiler_params=_params("parallel", "arbitrary"),
        name="ffn_up",
    )(x, g, wg, wu)


def _ffn_down_body(h_ref, w_ref, x_ref, o_ref, wb_ref):
    @pl.when(pl.program_id(1) == 0)
    def _():
        wb_ref[...] = w_ref[...].astype(BF16)

    y = jnp.dot(h_ref[...], wb_ref[...], preferred_element_type=F32)
    o_ref[...] = x_ref[...] + 0.5 * y


def _ffn_down(h, wd, x, layer, tm, tn):
    t, f = h.shape
    d = wd.shape[2]
    return pl.pallas_call(
        _ffn_down_body,
        grid=(d // tn, t // tm),
        in_specs=[
            pl.BlockSpec((tm, f), lambda n, m: (m, 0)),
            pl.BlockSpec((None, f, tn), lambda n, m: (layer, 0, n)),
            pl.BlockSpec((tm, tn), lambda n, m: (m, n)),
        ],
        out_specs=pl.BlockSpec((tm, tn), lambda n, m: (m, n)),
        out_shape=jax.ShapeDtypeStruct((t, d), F32),
        scratch_shapes=[pltpu.VMEM((f, tn), BF16)],
        compiler_params=_params("parallel", "arbitrary"),
        name="ffn_down",
    )(h, wd, x)


def _rope_table_body(pos_ref, inv_ref, sgn_ref, cos_ref, sin_ref):
    ang = pos_ref[...] * inv_ref[...]
    cos_ref[...] = jnp.cos(ang)
    sin_ref[...] = jnp.sin(ang) * sgn_ref[...]


def _rope_tables(pos, dim, tm):
    t = pos.shape[0]
    half = dim // 2
    inv = 1.0 / (ROPE_THETA ** (jnp.arange(0, dim, 2, dtype=F32) / dim))
    lane = jnp.arange(LANES)
    inv_l = inv[lane % half].reshape(1, LANES)
    sgn_l = jnp.where(lane % dim < half, -1.0, 1.0).astype(F32).reshape(1, LANES)
    row = pl.BlockSpec((1, LANES), lambda m: (0, 0))
    tab = pl.BlockSpec((tm, LANES), lambda m: (m, 0))
    return pl.pallas_call(
        _rope_table_body,
        grid=(t // tm,),
        in_specs=[pl.BlockSpec((tm, 1), lambda m: (m, 0)), row, row],
        out_specs=[tab, tab],
        out_shape=[jax.ShapeDtypeStruct((t, LANES), F32)] * 2,
        compiler_params=_params("parallel"),
        name="rope_tables",
    )(pos, inv_l, sgn_l)


def _rope128(x, cos, sin):
    return x * cos + pltpu.roll(x, HEAD_DIM // 2, 1) * sin


def _rope64(x, cos, sin, lo_half):
    partner = jnp.where(lo_half, pltpu.roll(x, LANES - IDX_DIM // 2, 1), pltpu.roll(x, IDX_DIM // 2, 1))
    return x * cos + partner * sin


def _in_proj_body(seq_tiles, x_ref, g_ref, w_ref, cw_ref, ch_ref, sh_ref, ci_ref, si_ref,
                  ya_ref, qb_ref, kb_ref, vb_ref, qi_ref, ki_ref, wi_ref, qc_ref, kc_ref, vc_ref,
                  u_ref):
    tm = x_ref.shape[0]
    xn = _rms(x_ref[...], g_ref[...]).astype(BF16)

    def proj(seg):
        return jnp.dot(xn, w_ref[:, seg[0]:seg[1]], preferred_element_type=F32)

    p = proj(SEG_CONV)
    h, gate_b, gate_c = p[:, :CONV_CH], p[:, CONV_CH:2 * CONV_CH], p[:, 2 * CONV_CH:]
    u = gate_c * h

    @pl.when(pl.program_id(0) % seq_tiles == 0)
    def _():
        u_ref[0:8, :] = jnp.zeros((8, CONV_CH), F32)

    u_ref[8:8 + tm, :] = u
    cw = cw_ref[...]
    y = cw[2:3, :] * u + cw[1:2, :] * u_ref[7:7 + tm, :] + cw[0:1, :] * u_ref[6:6 + tm, :]
    u_ref[0:8, :] = u[tm - 8:, :]
    ya_ref[...] = (gate_b * y).astype(BF16)

    ch, sh, ci, si = ch_ref[...], sh_ref[...], ci_ref[...], si_ref[...]
    scale = HEAD_DIM ** -0.5 * LOG2E

    def rope_heads(p, n, mul):
        return jnp.concatenate(
            [_rope128(p[:, j * LANES:(j + 1) * LANES], ch, sh) * mul for j in range(n)], axis=1)

    qb_ref[...] = rope_heads(proj(SEG_QB), DSA_HEADS, scale).T.astype(BF16)
    p = proj(SEG_KV)
    kb_ref[...] = _rope128(p[:, :HEAD_DIM], ch, sh).astype(BF16)
    vb_ref[...] = p[:, HEAD_DIM:].astype(BF16)

    lo_half = lax.broadcasted_iota(I32, (tm, LANES), 1) % IDX_DIM < IDX_DIM // 2
    p = proj(SEG_QI)
    qi_ref[...] = jnp.concatenate(
        [_rope64(p[:, j * LANES:(j + 1) * LANES], ci, si, lo_half) * (IDX_DIM ** -0.5)
         for j in range(IDX_HEADS * IDX_DIM // LANES)], axis=1).T.astype(BF16)
    ki_ref[...] = _rope64(proj(SEG_KI), ci, si, lo_half)[:, :IDX_DIM].astype(BF16)
    wi_ref[...] = (proj(SEG_WI) * (IDX_HEADS ** -0.5)).T[:IDX_HEADS, :]

    qc_ref[...] = rope_heads(proj(SEG_QC), DIL_HEADS, scale)
    kc_ref[...] = rope_heads(proj(SEG_KC), DIL_HEADS, 1.0)
    vc_ref[...] = proj(SEG_VC)


def _in_proj(x, g, w, cw, tabs, layer, seq, tm):
    t, d = x.shape
    n = w.shape[2]
    row = lambda width: pl.BlockSpec((tm, width), lambda m: (m, 0))
    col = lambda height: pl.BlockSpec((height, tm), lambda m: (0, m))
    const = lambda shape: pl.BlockSpec((None,) + shape, lambda m: (layer, 0, 0))
    outs = [(CONV_CH, BF16, True), (DSA_HEADS * HEAD_DIM, BF16, False), (HEAD_DIM, BF16, True),
            (HEAD_DIM, BF16, True), (IDX_HEADS * IDX_DIM, BF16, False), (IDX_DIM, BF16, True),
            (IDX_HEADS, F32, False), (DIL_HEADS * HEAD_DIM, F32, True), (DIL_HEADS * HEAD_DIM, F32, True),
            (DIL_HEADS * HEAD_DIM, F32, True)]
    return pl.pallas_call(
        functools.partial(_in_proj_body, seq // tm),
        grid=(t // tm,),
        in_specs=[row(d), const((1, d)),
                  pl.BlockSpec((None, d, n), lambda m: (layer, 0, 0), pipeline_mode=pl.Buffered(1)),
                  const((CONV_WIDTH, CONV_CH)), row(LANES), row(LANES), row(LANES), row(LANES)],
        out_specs=[row(wd) if tok else col(wd) for wd, _, tok in outs],
        out_shape=[jax.ShapeDtypeStruct((t, wd) if tok else (wd, t), dt) for wd, dt, tok in outs],
        scratch_shapes=[pltpu.VMEM((tm + 8, CONV_CH), F32)],
        compiler_params=_params("arbitrary"),
        name="in_proj",
    )(x, g, w, cw, *tabs)


def _dsa_body(topk, qbt_in_ref, kb_ref, vb_ref, qt_ref, ki_ref, wt_ref, o_ref,
              keys_ref, hi_ref, lo_ref, lo2_ref, qbt_ref, vt_ref, acc_ref):
    tq = o_ref.shape[1]
    seq = kb_ref.shape[1]
    nh = DSA_HEADS
    i = pl.program_id(1)
    nch = i + 1

    def chunk(c):
        return pl.ds(pl.multiple_of(c * tq, tq), tq)

    def transpose_bf16(a):
        return a.astype(F32).T.astype(BF16)

    @pl.when(i == 0)
    def _():
        for c2 in range(seq // (2 * tq)):
            vt_ref[c2, :HEAD_DIM, :] = transpose_bf16(vb_ref[0, c2 * 2 * tq:(c2 + 1) * 2 * tq, :])
            vt_ref[c2, HEAD_DIM:, :] = jnp.ones((ONES_ROWS, 2 * tq), BF16)

    for h in range(nh):
        qbt_ref[:, h * tq:(h + 1) * tq] = qbt_in_ref[h * HEAD_DIM:(h + 1) * HEAD_DIM, :]
    wt = wt_ref[...]
    krow = lax.broadcasted_iota(I32, (tq, tq), 0)

    def pair(c2):
        return pl.ds(pl.multiple_of(c2 * (2 * tq), 2 * tq), 2 * tq)

    def score_rows(rows, n, first_key):
        kc = ki_ref[0, rows, :]
        acc = jnp.zeros((n, tq), F32)
        for h in range(IDX_HEADS):
            lg = jnp.dot(kc, qt_ref[h * IDX_DIM:(h + 1) * IDX_DIM, :], preferred_element_type=F32)
            acc = acc + jnp.maximum(lg, 0.0) * wt[h:h + 1, :]
        bits = pltpu.bitcast(acc, I32)
        key = bits ^ ((bits >> 31) & 0x7FFFFFFF)
        causal = lax.broadcasted_iota(I32, (n, tq), 0) + first_key <= lax.broadcasted_iota(I32, (n, tq), 1) + i * tq
        key = jnp.where(causal, key, INT_MIN)
        keys_ref[rows, :] = key
        hi_ref[rows, :] = (key >> 16).astype(I16)
        lo_ref[rows, :] = ((key & 0xFFFF) - HALF16).astype(I16)

    def score_pair(c2, carry):
        score_rows(pair(c2), 2 * tq, c2 * (2 * tq))
        return carry

    lax.fori_loop(0, nch // 2, score_pair, 0)

    npair = (nch + 1) // 2

    @pl.when(nch % 2 == 1)
    def _():
        score_rows(chunk(nch - 1), tq, (nch - 1) * tq)
        keys_ref[chunk(nch), :] = jnp.full((tq, tq), INT_MIN, I32)
        hi_ref[chunk(nch), :] = jnp.full((tq, tq), -HALF16, I16)
        lo_ref[chunk(nch), :] = jnp.full((tq, tq), -HALF16, I16)

    def count(pred):
        def body(c, acc):
            hit = pred(keys_ref[chunk(c), :], krow + c * tq).astype(I32)
            return acc + jnp.sum(hit.reshape(tq // 8, 8, tq), axis=0)
        acc = lax.fori_loop(0, nch, body, jnp.zeros((8, tq), I32))
        return jnp.sum(acc, axis=0, keepdims=True)

    def count16(ref, pred):
        def body(c2, acc):
            hit = jnp.where(pred(ref[pair(c2), :]), jnp.ones((), BF16), jnp.zeros((), BF16))
            parts = [hit[r * 16:(r + 1) * 16] for r in range(2 * tq // 16)]
            while len(parts) > 1:
                parts = [parts[j] + parts[j + 1] for j in range(0, len(parts), 2)]
            return acc + parts[0].astype(F32)
        acc = lax.fori_loop(0, npair, body, jnp.zeros((16, tq), F32))
        return jnp.sum(acc, axis=0, keepdims=True)

    def kth_largest16(ref, k):
        def search_bit(b, t_u):
            cand = t_u | (jnp.int32(1) << (15 - b))
            cand16 = (cand - HALF16).astype(I16)
            return jnp.where(count16(ref, lambda v: v >= cand16) >= k, cand, t_u)
        return lax.fori_loop(0, 16, search_bit, jnp.zeros((1, tq), I32)) - HALF16

    t_hi = kth_largest16(hi_ref, float(topk))
    t_hi16 = t_hi.astype(I16)
    need_lo = float(topk) - count16(hi_ref, lambda v: v > t_hi16)

    def bucket(c2, carry):
        lo2_ref[pair(c2), :] = jnp.where(hi_ref[pair(c2), :] == t_hi16, lo_ref[pair(c2), :],
                                         jnp.full((), -HALF16, I16))
        return carry

    lax.fori_loop(0, npair, bucket, 0)
    t_lo = kth_largest16(lo2_ref, need_lo)
    thr = jnp.maximum((t_hi << 16) | (t_lo + HALF16), INT_MIN + 1)
    n_ge = count(lambda k, _: k >= thr)

    idx_bits = seq.bit_length() - 1

    def tie_bound():
        need = topk - count(lambda k, _: k > thr)
        def bit(b, j):
            cand = j | (jnp.int32(1) << (idx_bits - 1 - b))
            below = count(lambda k, idx: (k == thr) & (idx < cand))
            return jnp.where(below < need, cand, j)
        j = lax.fori_loop(0, idx_bits, bit, jnp.zeros((1, tq), I32))
        return jnp.where(n_ge > topk, j, seq)

    bound = lax.cond(jnp.max(n_ge) > topk, tie_bound, lambda: jnp.full((1, tq), seq, I32))

    acc_ref[...] = jnp.zeros(acc_ref.shape, F32)

    krow2 = lax.broadcasted_iota(I32, (2 * tq, tq), 0)

    def qk(c2):
        return jnp.dot(kb_ref[0, pair(c2), :], qbt_ref[...], preferred_element_type=F32)

    def attend(c2, m):
        kblk = keys_ref[pair(c2), :]
        sel = (kblk > thr) | ((kblk == thr) & (krow2 + c2 * (2 * tq) <= bound))
        bias = jnp.where(sel, 0.0, NEG)
        st = qk(c2) + jnp.concatenate([bias] * nh, axis=1)
        m_new = jnp.maximum(m, jnp.max(st, axis=0, keepdims=True))
        p = jnp.exp2(st - m_new).astype(BF16)
        acc_ref[...] = acc_ref[...] * jnp.exp2(m - m_new) + jnp.dot(vt_ref[c2], p, preferred_element_type=F32)
        return m_new

    lax.fori_loop(0, npair, attend, jnp.full((1, nh * tq), NEG, F32))
    out = acc_ref[:HEAD_DIM, :] / acc_ref[HEAD_DIM:HEAD_DIM + 1, :]
    for h in range(nh):
        o_ref[0, :, h * HEAD_DIM:(h + 1) * HEAD_DIM] = out[:, h * tq:(h + 1) * tq].T.astype(BF16)


def _dsa(qbt, kb, vb, qit, ki, wit, tq):
    b, s, _ = kb.shape
    topk = min(TOPK_MAX, s // 4)
    assert s % (2 * tq) == 0
    nq = s // tq
    blk = lambda width: pl.BlockSpec((1, tq, width), lambda bb, i: (bb, i, 0))
    full = lambda width: pl.BlockSpec((1, s, width), lambda bb, i: (bb, 0, 0))
    qcol = lambda a: pl.BlockSpec((a.shape[0], tq), lambda bb, i: (0, bb * nq + i))
    return pl.pallas_call(
        functools.partial(_dsa_body, topk),
        grid=(b, nq),
        in_specs=[qcol(qbt), full(HEAD_DIM), full(HEAD_DIM), qcol(qit), full(IDX_DIM), qcol(wit)],
        out_specs=blk(DSA_HEADS * HEAD_DIM),
        out_shape=jax.ShapeDtypeStruct((b, s, DSA_HEADS * HEAD_DIM), BF16),
        scratch_shapes=[pltpu.VMEM((s, tq), I32),
                        pltpu.VMEM((s, tq), I16),
                        pltpu.VMEM((s, tq), I16),
                        pltpu.VMEM((s, tq), I16),
                        pltpu.VMEM((HEAD_DIM, DSA_HEADS * tq), BF16),
                        pltpu.VMEM((s // (2 * tq), HEAD_DIM + ONES_ROWS, 2 * tq), BF16),
                        pltpu.VMEM((HEAD_DIM + ONES_ROWS, DSA_HEADS * tq), F32)],
        compiler_params=_params("parallel", "arbitrary"),
        name="dsa",
    )(qbt, kb, vb, qit, ki, wit)


def _dilated_body(*refs):
    ng = len(DIL_PATTERNS)
    q_refs, k_refs, v_refs = refs[0:ng], refs[ng:2 * ng], refs[2 * ng:3 * ng]
    o_refs = refs[3 * ng:4 * ng]
    acc_ref, m_ref, l_ref = refs[4 * ng:]
    sb_tokens = q_refs[0].shape[1]
    t0 = pl.program_id(2) * sb_tokens
    blk = HEAD_DIM
    UNITS = 4
    rq = lax.broadcasted_iota(I32, (blk, blk), 0)
    ck = lax.broadcasted_iota(I32, (blk, blk), 1)
    bias_cur = jnp.where(ck <= rq, 0.0, NEG)
    bias_prev = jnp.where(ck >= rq, 0.0, NEG)
    nt = (((1,), (1,)), ((), ()))

    for g, (win, dil) in enumerate(DIL_PATTERNS):
        q_ref, k_ref, v_ref = q_refs[g], k_refs[g], v_refs[g]
        per_res = sb_tokens // dil // blk

        def rows(start, dil=dil):
            return pl.ds(start, blk, stride=dil) if dil > 1 else pl.ds(start, blk)

        def units(it, carry, g=g, dil=dil, per_res=per_res, rows=rows,
                  q_ref=q_ref, k_ref=k_ref, v_ref=v_ref):
            q0s, kcs, kps, hps = [], [], [], []
            for n in range(UNITS):
                idx = it * UNITS + n
                q0 = idx // per_res + (idx % per_res) * (blk * dil)
                k_cur = t0 + q0
                has_prev = k_cur >= blk * dil
                q0s.append(q0)
                kcs.append(k_cur)
                hps.append(has_prev)
                kps.append(jnp.where(has_prev, k_cur - blk * dil, k_cur))
            ss = []
            for q0, kc, kp, hp in zip(q0s, kcs, kps, hps):
                k2 = jnp.concatenate([k_ref[0, rows(kp), :], k_ref[0, rows(kc), :]], axis=0).astype(BF16)
                s = lax.dot_general(q_ref[0, rows(q0), :].astype(BF16), k2, nt, preferred_element_type=F32)
                ss.append(s + jnp.concatenate([jnp.where(hp, bias_prev, NEG), bias_cur], axis=1))
            ms = [jnp.max(jnp.maximum(s[:, :blk], s[:, blk:]), axis=-1, keepdims=True) for s in ss]
            ps = [jnp.exp2(s - m).astype(BF16) for s, m in zip(ss, ms)]
            for q0, kc, kp, m, p in zip(q0s, kcs, kps, ms, ps):
                v2 = jnp.concatenate([v_ref[0, rows(kp), :], v_ref[0, rows(kc), :]], axis=0).astype(BF16)
                av = jnp.dot(p, jnp.concatenate([v2, jnp.ones_like(v2)], axis=1), preferred_element_type=F32)
                acc_ref[g, rows(q0), :] = av[:, :HEAD_DIM]
                l_ref[g, rows(q0), :] = av[:, HEAD_DIM:]
                m_ref[g, rows(q0), :] = jnp.broadcast_to(m, (blk, LANES))
            return carry

        lax.fori_loop(0, dil * per_res // UNITS, units, 0)

    step = 256

    def merge(c, carry):
        sl = pl.ds(pl.multiple_of(c * step, step), step)
        ms = [m_ref[g, sl, :] for g in range(ng)]
        m_all = functools.reduce(jnp.maximum, ms)
        ws = [jnp.exp2(m - m_all) for m in ms]
        den = sum(w * l_ref[g, sl, :] for g, w in enumerate(ws))
        for g, w in enumerate(ws):
            o_refs[g][0, sl, :] = (acc_ref[g, sl, :] * (w / den)).astype(BF16)
        return carry

    lax.fori_loop(0, sb_tokens // step, merge, 0)


def _dilated(qc, kc, vc):
    b, s, _ = qc.shape
    ng, hg = len(DIL_PATTERNS), DIL_HEADS_PER_GROUP
    sb_tokens = HEAD_DIM * max(dil for _, dil in DIL_PATTERNS)
    assert s % sb_tokens == 0 and all(win == HEAD_DIM * dil for win, dil in DIL_PATTERNS)
    head = lambda g: (lambda bb, j, sb: (bb, sb, g * hg + j))
    head_full = lambda g: (lambda bb, j, sb: (bb, 0, g * hg + j))
    q_specs = [pl.BlockSpec((1, sb_tokens, HEAD_DIM), head(g)) for g in range(ng)]
    kv_specs = [pl.BlockSpec((1, s, HEAD_DIM), head_full(g)) for g in range(ng)]
    out_spec = pl.BlockSpec((1, sb_tokens, HEAD_DIM), lambda bb, j, sb: (bb, sb, j))
    return pl.pallas_call(
        _dilated_body,
        grid=(b, hg, s // sb_tokens),
        in_specs=q_specs + kv_specs + kv_specs,
        out_specs=[out_spec] * ng,
        out_shape=[jax.ShapeDtypeStruct((b, s, hg * HEAD_DIM), BF16)] * ng,
        scratch_shapes=[pltpu.VMEM((ng, sb_tokens, HEAD_DIM), F32)] * 3,
        compiler_params=_params("parallel", "parallel", "arbitrary"),
        name="dilated",
    )(*([qc] * ng + [kc] * ng + [vc] * ng))


def _out_proj_body(*refs):
    *y_refs, w_ref, x_ref, o_ref, wb_ref = refs

    @pl.when(pl.program_id(1) == 0)
    def _():
        wb_ref[...] = w_ref[...].astype(BF16)

    y = jnp.concatenate([y_ref[...] for y_ref in y_refs], axis=1)
    o_ref[...] = x_ref[...] + jnp.dot(y, wb_ref[...], preferred_element_type=F32)


def _out_proj(parts, w, x, layer, tm, tn):
    t, d = x.shape
    assert sum(a.shape[1] for a in parts) == w.shape[1]
    return pl.pallas_call(
        _out_proj_body,
        grid=(d // tn, t // tm),
        in_specs=[pl.BlockSpec((tm, a.shape[1]), lambda n, m: (m, 0)) for a in parts]
        + [pl.BlockSpec((None, w.shape[1], tn), lambda n, m: (layer, 0, n)),
           pl.BlockSpec((tm, tn), lambda n, m: (m, n))],
        out_specs=pl.BlockSpec((tm, tn), lambda n, m: (m, n)),
        out_shape=jax.ShapeDtypeStruct((t, d), F32),
        scratch_shapes=[pltpu.VMEM((w.shape[1], tn), BF16)],
        compiler_params=_params("parallel", "arbitrary"),
        name="out_proj",
    )(*parts, w, x)


def _final_norm_body(x_ref, g_ref, o_ref):
    o_ref[...] = _rms(x_ref[...], g_ref[...])


def _final_norm(x, g, tm):
    t, d = x.shape
    return pl.pallas_call(
        _final_norm_body,
        grid=(t // tm,),
        in_specs=[pl.BlockSpec((tm, d), lambda m: (m, 0)), pl.BlockSpec((1, d), lambda m: (0, 0))],
        out_specs=pl.BlockSpec((tm, d), lambda m: (m, 0)),
        out_shape=jax.ShapeDtypeStruct((t, d), F32),
        compiler_params=_params("parallel"),
        name="final_norm",
    )(x, g)


def _align_w_in_body(x_ref, o_ref):
    o_wi = SEG_QI[1] + IDX_DIM
    o_qc = o_wi + IDX_HEADS
    rows = x_ref.shape[0]
    o_ref[:, :o_wi] = x_ref[:, :o_wi]
    o_ref[:, o_wi:SEG_WI[0]] = jnp.zeros((rows, SEG_WI[0] - o_wi), BF16)
    o_ref[:, SEG_WI[0]:SEG_WI[0] + IDX_HEADS] = x_ref[:, o_wi:o_qc]
    o_ref[:, SEG_WI[0] + IDX_HEADS:SEG_QC[0]] = jnp.zeros((rows, LANES - IDX_HEADS), BF16)
    o_ref[:, SEG_QC[0]:] = x_ref[:, o_qc:o_qc + D_IN_ALIGNED - SEG_QC[0]]


def _align_w_in(w, tr):
    depth, d, n = w.shape
    assert n - (SEG_QI[1] + IDX_DIM + IDX_HEADS) == D_IN_ALIGNED - SEG_QC[0]
    w = jnp.pad(w.astype(BF16), ((0, 0), (0, 0), (0, D_IN_ALIGNED - n)))
    return pl.pallas_call(
        _align_w_in_body,
        grid=(depth, d // tr),
        in_specs=[pl.BlockSpec((None, tr, D_IN_ALIGNED), lambda l, r: (l, r, 0))],
        out_specs=pl.BlockSpec((None, tr, D_IN_ALIGNED), lambda l, r: (l, r, 0)),
        out_shape=jax.ShapeDtypeStruct((depth, d, D_IN_ALIGNED), BF16),
        compiler_params=_params("parallel", "parallel"),
        name="align_w_in",
    )(w)


def _tile(n, want):
    while n % want:
        want //= 2
    return want


def kernel(x, positions, norm_ffn1, ffn1_gate, ffn1_up, ffn1_down, norm_mix, w_in, conv_w, w_out,
           norm_ffn2, ffn2_gate, ffn2_up, ffn2_down, norm_final):
    b, s, d = x.shape
    t = b * s
    depth = w_in.shape[0]
    tm_big = _tile(t, 1024)
    tm_proj = _tile(s, 512)
    tq_dsa = _tile(s, 256)

    pos = positions.astype(F32).reshape(t, 1)
    tabs = (*_rope_tables(pos, HEAD_DIM, tm_big), *_rope_tables(pos, IDX_DIM, tm_big))

    def ffn(xf, g, wg, wu, wd, layer):
        h = _ffn_up(xf, g.reshape(depth, 1, d), wg, wu, layer, tm_big, 512)
        return _ffn_down(h, wd, xf, layer, _tile(t, 512), 512)

    w_in_al = _align_w_in(w_in, _tile(d, 256))
    xf = x.reshape(t, d)
    for i in range(depth):
        xf = ffn(xf, norm_ffn1, ffn1_gate, ffn1_up, ffn1_down, i)
        ya, qb, kb, vb, qi, ki, wi, qc, kc, vc = _in_proj(
            xf, norm_mix.reshape(depth, 1, d), w_in_al, conv_w, tabs, i, s, tm_proj)
        r3 = lambda a: a.reshape(b, s, a.shape[-1])
        yb = _dsa(qb, r3(kb), r3(vb), qi, r3(ki), wi, tq_dsa)
        ycs = _dilated(r3(qc), r3(kc), r3(vc))
        parts = [ya, yb.reshape(t, -1)] + [yc.reshape(t, -1) for yc in ycs]
        xf = _out_proj(parts, w_out, xf, i, tm_big, 1024)
        xf = ffn(xf, norm_ffn2, ffn2_gate, ffn2_up, ffn2_down, i)
    return _final_norm(xf, norm_final.reshape(1, d), tm_big).reshape(b, s, d)
```

```python
import functools

import jax
import jax.numpy as jnp
from jax import lax
from jax.experimental import pallas as pl
from jax.experimental.pallas import tpu as pltpu

F32 = jnp.float32
BF16 = jnp.bfloat16
I32 = jnp.int32
I16 = jnp.int16
HALF16 = 1 << 15
ONES_ROWS = 16

HEAD_DIM = 128
CONV_CH = 512
CONV_WIDTH = 3
DSA_HEADS = 6
IDX_HEADS = 16
IDX_DIM = 64
TOPK_MAX = 256
DIL_PATTERNS = ((128, 1), (512, 4), (2048, 16))
DIL_HEADS_PER_GROUP = 2
DIL_HEADS = len(DIL_PATTERNS) * DIL_HEADS_PER_GROUP
ROPE_THETA = 10000.0
RMS_EPS = 1e-6

LANES = 128
VMEM_LIMIT = 56 * 1024 * 1024
NEG = -1e30
INT_MIN = -2 ** 31
LOG2E = 1.4426950408889634

SEG_CONV = (0, 3 * CONV_CH)
SEG_QB = (SEG_CONV[1], SEG_CONV[1] + DSA_HEADS * HEAD_DIM)
SEG_KV = (SEG_QB[1], SEG_QB[1] + 2 * HEAD_DIM)
SEG_QI = (SEG_KV[1], SEG_KV[1] + IDX_HEADS * IDX_DIM)
SEG_KI = (SEG_QI[1], SEG_QI[1] + LANES)
SEG_WI = (SEG_KI[1], SEG_KI[1] + LANES)
SEG_QC = (SEG_WI[1], SEG_WI[1] + DIL_HEADS * HEAD_DIM)
SEG_KC = (SEG_QC[1], SEG_QC[1] + DIL_HEADS * HEAD_DIM)
SEG_VC = (SEG_KC[1], SEG_KC[1] + DIL_HEADS * HEAD_DIM)
D_IN_ALIGNED = SEG_VC[1]


def _params(*sem):
    return pltpu.CompilerParams(dimension_semantics=sem, vmem_limit_bytes=VMEM_LIMIT)


def _rms(x, g):
    ms = jnp.mean(x * x, axis=-1, keepdims=True)
    return x * lax.rsqrt(ms + RMS_EPS) * g


def _ffn_up_body(x_ref, g_ref, wg_ref, wu_ref, h_ref, xn_ref):
    @pl.when(pl.program_id(1) == 0)
    def _():
        xn_ref[...] = _rms(x_ref[...], g_ref[...]).astype(BF16)

    xn = xn_ref[...]
    a = jnp.dot(xn, wg_ref[...], preferred_element_type=F32)
    b = jnp.dot(xn, wu_ref[...], preferred_element_type=F32)
    h_ref[...] = (a * jax.nn.sigmoid(a) * b).astype(BF16)


def _ffn_up(x, g, wg, wu, layer, tm, tn):
    t, d = x.shape
    f = wg.shape[1]
    return pl.pallas_call(
        _ffn_up_body,
        grid=(t // tm, f // tn),
        in_specs=[
            pl.BlockSpec((tm, d), lambda m, n: (m, 0)),
            pl.BlockSpec((None, 1, d), lambda m, n: (layer, 0, 0)),
            pl.BlockSpec((d, tn), lambda m, n: (0, n)),
            pl.BlockSpec((d, tn), lambda m, n: (0, n)),
        ],
        out_specs=pl.BlockSpec((tm, tn), lambda m, n: (m, n)),
        out_shape=jax.ShapeDtypeStruct((t, f), BF16),
        scratch_shapes=[pltpu.VMEM((tm, d), BF16)],
        compiler_params=_params("parallel", "arbitrary"),
        name="ffn_up",
    )(x, g, wg, wu)


def _ffn_down_body(h_ref, w_ref, x_ref, o_ref):
    y = jnp.dot(h_ref[...], w_ref[...], preferred_element_type=F32)
    o_ref[...] = x_ref[...] + 0.5 * y


def _ffn_down(h, wd, x, tm, tn):
    t, f = h.shape
    d = wd.shape[1]
    return pl.pallas_call(
        _ffn_down_body,
        grid=(t // tm, d // tn),
        in_specs=[
            pl.BlockSpec((tm, f), lambda m, n: (m, 0)),
            pl.BlockSpec((f, tn), lambda m, n: (0, n)),
            pl.BlockSpec((tm, tn), lambda m, n: (m, n)),
        ],
        out_specs=pl.BlockSpec((tm, tn), lambda m, n: (m, n)),
        out_shape=jax.ShapeDtypeStruct((t, d), F32),
        compiler_params=_params("parallel", "arbitrary"),
        name="ffn_down",
    )(h, wd, x)


def _cast_block_specs(w, layer, nsteps, step_of):
    _, rows, cols = w.shape
    nblk = nsteps
    while rows % nblk or (rows // nblk) % 16:
        nblk //= 2
    blk = rows // nblk
    return (pl.BlockSpec((None, blk, cols), lambda *g: (layer, step_of(*g) * nblk // nsteps, 0)),
            pl.BlockSpec((blk, cols), lambda *g: (step_of(*g) * nblk // nsteps, 0)),
            jax.ShapeDtypeStruct((rows, cols), BF16))


def _rope_table_body(pos_ref, inv_ref, sgn_ref, cos_ref, sin_ref):
    ang = pos_ref[...] * inv_ref[...]
    cos_ref[...] = jnp.cos(ang)
    sin_ref[...] = jnp.sin(ang) * sgn_ref[...]


def _rope_tables(pos, dim, tm):
    t = pos.shape[0]
    half = dim // 2
    inv = 1.0 / (ROPE_THETA ** (jnp.arange(0, dim, 2, dtype=F32) / dim))
    lane = jnp.arange(LANES)
    inv_l = inv[lane % half].reshape(1, LANES)
    sgn_l = jnp.where(lane % dim < half, -1.0, 1.0).astype(F32).reshape(1, LANES)
    row = pl.BlockSpec((1, LANES), lambda m: (0, 0))
    tab = pl.BlockSpec((tm, LANES), lambda m: (m, 0))
    return pl.pallas_call(
        _rope_table_body,
        grid=(t // tm,),
        in_specs=[pl.BlockSpec((tm, 1), lambda m: (m, 0)), row, row],
        out_specs=[tab, tab],
        out_shape=[jax.ShapeDtypeStruct((t, LANES), F32)] * 2,
        compiler_params=_params("parallel"),
        name="rope_tables",
    )(pos, inv_l, sgn_l)


def _rope128(x, cos, sin):
    return x * cos + pltpu.roll(x, HEAD_DIM // 2, 1) * sin


def _rope64(x, cos, sin, lo_half):
    partner = jnp.where(lo_half, pltpu.roll(x, LANES - IDX_DIM // 2, 1), pltpu.roll(x, IDX_DIM // 2, 1))
    return x * cos + partner * sin


def _in_proj_body(seq_tiles, x_ref, g_ref, w_ref, cw_ref, ch_ref, sh_ref, ci_ref, si_ref,
                  ya_ref, qb_ref, kb_ref, vb_ref, qi_ref, ki_ref, wi_ref, qc_ref, kc_ref, vc_ref,
                  u_ref):
    tm = x_ref.shape[0]
    xn = _rms(x_ref[...], g_ref[...]).astype(BF16)

    def proj(seg):
        return jnp.dot(xn, w_ref[:, seg[0]:seg[1]], preferred_element_type=F32)

    p = proj(SEG_CONV)
    h, gate_b, gate_c = p[:, :CONV_CH], p[:, CONV_CH:2 * CONV_CH], p[:, 2 * CONV_CH:]
    u = gate_c * h

    @pl.when(pl.program_id(0) % seq_tiles == 0)
    def _():
        u_ref[0:8, :] = jnp.zeros((8, CONV_CH), F32)

    u_ref[8:8 + tm, :] = u
    cw = cw_ref[...]
    y = cw[2:3, :] * u + cw[1:2, :] * u_ref[7:7 + tm, :] + cw[0:1, :] * u_ref[6:6 + tm, :]
    u_ref[0:8, :] = u[tm - 8:, :]
    ya_ref[...] = (gate_b * y).astype(BF16)

    ch, sh, ci, si = ch_ref[...], sh_ref[...], ci_ref[...], si_ref[...]
    scale = HEAD_DIM ** -0.5 * LOG2E

    def rope_heads(p, n, mul):
        return jnp.concatenate(
            [_rope128(p[:, j * LANES:(j + 1) * LANES], ch, sh) * mul for j in range(n)], axis=1)

    qb_ref[...] = rope_heads(proj(SEG_QB), DSA_HEADS, scale).T.astype(BF16)
    p = proj(SEG_KV)
    kb_ref[...] = _rope128(p[:, :HEAD_DIM], ch, sh).astype(BF16)
    vb_ref[...] = p[:, HEAD_DIM:].astype(BF16)

    lo_half = lax.broadcasted_iota(I32, (tm, LANES), 1) % IDX_DIM < IDX_DIM // 2
    p = proj(SEG_QI)
    qi_ref[...] = jnp.concatenate(
        [_rope64(p[:, j * LANES:(j + 1) * LANES], ci, si, lo_half) * (IDX_DIM ** -0.5)
         for j in range(IDX_HEADS * IDX_DIM // LANES)], axis=1).T.astype(BF16)
    ki_ref[...] = _rope64(proj(SEG_KI), ci, si, lo_half)[:, :IDX_DIM].astype(BF16)
    wi_ref[...] = (proj(SEG_WI) * (IDX_HEADS ** -0.5)).T[:IDX_HEADS, :]

    qc_ref[...] = rope_heads(proj(SEG_QC), DIL_HEADS, scale)
    kc_ref[...] = rope_heads(proj(SEG_KC), DIL_HEADS, 1.0)
    vc_ref[...] = proj(SEG_VC)


def _in_proj(x, g, w, cw, tabs, layer, seq, tm):
    t, d = x.shape
    n = w.shape[2]
    row = lambda width: pl.BlockSpec((tm, width), lambda m: (m, 0))
    col = lambda height: pl.BlockSpec((height, tm), lambda m: (0, m))
    const = lambda shape: pl.BlockSpec((None,) + shape, lambda m: (layer, 0, 0))
    outs = [(CONV_CH, BF16, True), (DSA_HEADS * HEAD_DIM, BF16, False), (HEAD_DIM, BF16, True),
            (HEAD_DIM, BF16, True), (IDX_HEADS * IDX_DIM, BF16, False), (IDX_DIM, BF16, True),
            (IDX_HEADS, F32, False), (DIL_HEADS * HEAD_DIM, F32, True), (DIL_HEADS * HEAD_DIM, F32, True),
            (DIL_HEADS * HEAD_DIM, F32, True)]
    return pl.pallas_call(
        functools.partial(_in_proj_body, seq // tm),
        grid=(t // tm,),
        in_specs=[row(d), const((1, d)),
                  pl.BlockSpec((None, d, n), lambda m: (layer, 0, 0), pipeline_mode=pl.Buffered(1)),
                  const((CONV_WIDTH, CONV_CH)), row(LANES), row(LANES), row(LANES), row(LANES)],
        out_specs=[row(wd) if tok else col(wd) for wd, _, tok in outs],
        out_shape=[jax.ShapeDtypeStruct((t, wd) if tok else (wd, t), dt) for wd, dt, tok in outs],
        scratch_shapes=[pltpu.VMEM((tm + 8, CONV_CH), F32)],
        compiler_params=_params("arbitrary"),
        name="in_proj",
    )(x, g, w, cw, *tabs)


def _dsa_body(topk, ncast, qbt_in_ref, kb_ref, vb_ref, qt_ref, ki_ref, wt_ref, *rest):
    cast_in, (o_ref, *cast_out) = rest[:ncast], rest[ncast:2 * ncast + 1]
    keys_ref, hi_ref, lo_ref, lo2_ref, qbt_ref, vt_ref, acc_ref = rest[2 * ncast + 1:]
    tq = o_ref.shape[1]
    seq = kb_ref.shape[1]

    for src_ref, dst_ref in zip(cast_in, cast_out):
        dst_ref[...] = src_ref[...].astype(BF16)

    nh = DSA_HEADS
    i = pl.program_id(1)
    nch = i + 1

    def chunk(c):
        return pl.ds(pl.multiple_of(c * tq, tq), tq)

    def transpose_bf16(a):
        return a.astype(F32).T.astype(BF16)

    @pl.when(i == 0)
    def _():
        for c2 in range(seq // (2 * tq)):
            vt_ref[c2, :HEAD_DIM, :] = transpose_bf16(vb_ref[0, c2 * 2 * tq:(c2 + 1) * 2 * tq, :])
            vt_ref[c2, HEAD_DIM:, :] = jnp.ones((ONES_ROWS, 2 * tq), BF16)

    for h in range(nh):
        qbt_ref[:, h * tq:(h + 1) * tq] = qbt_in_ref[h * HEAD_DIM:(h + 1) * HEAD_DIM, :]
    wt = wt_ref[...]
    krow = lax.broadcasted_iota(I32, (tq, tq), 0)

    def pair(c2):
        return pl.ds(pl.multiple_of(c2 * (2 * tq), 2 * tq), 2 * tq)

    def score_rows(rows, n, first_key):
        kc = ki_ref[0, rows, :]
        acc = jnp.zeros((n, tq), F32)
        for h in range(IDX_HEADS):
            lg = jnp.dot(kc, qt_ref[h * IDX_DIM:(h + 1) * IDX_DIM, :], preferred_element_type=F32)
            acc = acc + jnp.maximum(lg, 0.0) * wt[h:h + 1, :]
        bits = pltpu.bitcast(acc, I32)
        key = bits ^ ((bits >> 31) & 0x7FFFFFFF)
        causal = lax.broadcasted_iota(I32, (n, tq), 0) + first_key <= lax.broadcasted_iota(I32, (n, tq), 1) + i * tq
        key = jnp.where(causal, key, INT_MIN)
        keys_ref[rows, :] = key
        hi_ref[rows, :] = (key >> 16).astype(I16)
        lo_ref[rows, :] = ((key & 0xFFFF) - HALF16).astype(I16)

    def score_pair(c2, carry):
        score_rows(pair(c2), 2 * tq, c2 * (2 * tq))
        return carry

    lax.fori_loop(0, nch // 2, score_pair, 0)

    npair = (nch + 1) // 2

    @pl.when(nch % 2 == 1)
    def _():
        score_rows(chunk(nch - 1), tq, (nch - 1) * tq)
        keys_ref[chunk(nch), :] = jnp.full((tq, tq), INT_MIN, I32)
        hi_ref[chunk(nch), :] = jnp.full((tq, tq), -HALF16, I16)
        lo_ref[chunk(nch), :] = jnp.full((tq, tq), -HALF16, I16)

    def count(pred):
        def body(c, acc):
            hit = pred(keys_ref[chunk(c), :], krow + c * tq).astype(I32)
            return acc + jnp.sum(hit.reshape(tq // 8, 8, tq), axis=0)
        acc = lax.fori_loop(0, nch, body, jnp.zeros((8, tq), I32))
        return jnp.sum(acc, axis=0, keepdims=True)

    def count16(ref, pred):
        def body(c2, acc):
            hit = jnp.where(pred(ref[pair(c2), :]), jnp.ones((), BF16), jnp.zeros((), BF16))
            parts = [hit[r * 16:(r + 1) * 16] for r in range(2 * tq // 16)]
            while len(parts) > 1:
                parts = [parts[j] + parts[j + 1] for j in range(0, len(parts), 2)]
            return acc + parts[0].astype(F32)
        acc = lax.fori_loop(0, npair, body, jnp.zeros((16, tq), F32))
        return jnp.sum(acc, axis=0, keepdims=True)

    def kth_largest16(ref, k):
        def search_bit(b, t_u):
            cand = t_u | (jnp.int32(1) << (15 - b))
            cand16 = (cand - HALF16).astype(I16)
            return jnp.where(count16(ref, lambda v: v >= cand16) >= k, cand, t_u)
        return lax.fori_loop(0, 16, search_bit, jnp.zeros((1, tq), I32)) - HALF16

    t_hi = kth_largest16(hi_ref, float(topk))
    t_hi16 = t_hi.astype(I16)
    need_lo = float(topk) - count16(hi_ref, lambda v: v > t_hi16)

    def bucket(c2, carry):
        lo2_ref[pair(c2), :] = jnp.where(hi_ref[pair(c2), :] == t_hi16, lo_ref[pair(c2), :],
                                         jnp.full((), -HALF16, I16))
        return carry

    lax.fori_loop(0, npair, bucket, 0)
    t_lo = kth_largest16(lo2_ref, need_lo)
    thr = jnp.maximum((t_hi << 16) | (t_lo + HALF16), INT_MIN + 1)
    n_ge = count(lambda k, _: k >= thr)

    idx_bits = seq.bit_length() - 1

    def tie_bound():
        need = topk - count(lambda k, _: k > thr)
        def bit(b, j):
            cand = j | (jnp.int32(1) << (idx_bits - 1 - b))
            below = count(lambda k, idx: (k == thr) & (idx < cand))
            return jnp.where(below < need, cand, j)
        j = lax.fori_loop(0, idx_bits, bit, jnp.zeros((1, tq), I32))
        return jnp.where(n_ge > topk, j, seq)

    bound = lax.cond(jnp.max(n_ge) > topk, tie_bound, lambda: jnp.full((1, tq), seq, I32))

    acc_ref[...] = jnp.zeros(acc_ref.shape, F32)

    krow2 = lax.broadcasted_iota(I32, (2 * tq, tq), 0)

    def qk(c2):
        return jnp.dot(kb_ref[0, pair(c2), :], qbt_ref[...], preferred_element_type=F32)

    def attend(c2, m):
        kblk = keys_ref[pair(c2), :]
        sel = (kblk > thr) | ((kblk == thr) & (krow2 + c2 * (2 * tq) <= bound))
        bias = jnp.where(sel, 0.0, NEG)
        st = qk(c2) + jnp.concatenate([bias] * nh, axis=1)
        m_new = jnp.maximum(m, jnp.max(st, axis=0, keepdims=True))
        p = jnp.exp2(st - m_new).astype(BF16)
        acc_ref[...] = acc_ref[...] * jnp.exp2(m - m_new) + jnp.dot(vt_ref[c2], p, preferred_element_type=F32)
        return m_new

    lax.fori_loop(0, npair, attend, jnp.full((1, nh * tq), NEG, F32))
    out = acc_ref[:HEAD_DIM, :] / acc_ref[HEAD_DIM:HEAD_DIM + 1, :]
    for h in range(nh):
        o_ref[0, :, h * HEAD_DIM:(h + 1) * HEAD_DIM] = out[:, h * tq:(h + 1) * tq].T.astype(BF16)


def _dsa(qbt, kb, vb, qit, ki, wit, tq, to_cast):
    b, s, _ = kb.shape
    topk = min(TOPK_MAX, s // 4)
    assert s % (2 * tq) == 0
    nq = s // tq
    blk = lambda width: pl.BlockSpec((1, tq, width), lambda bb, i: (bb, i, 0))
    full = lambda width: pl.BlockSpec((1, s, width), lambda bb, i: (bb, 0, 0))
    qcol = lambda a: pl.BlockSpec((a.shape[0], tq), lambda bb, i: (0, bb * nq + i))
    casts = [_cast_block_specs(w, layer, b * nq, lambda bb, i: bb * nq + i) for w, layer in to_cast]
    out, *copies = pl.pallas_call(
        functools.partial(_dsa_body, topk, len(casts)),
        grid=(b, nq),
        in_specs=[qcol(qbt), full(HEAD_DIM), full(HEAD_DIM), qcol(qit), full(IDX_DIM), qcol(wit)]
        + [c[0] for c in casts],
        out_specs=[blk(DSA_HEADS * HEAD_DIM)] + [c[1] for c in casts],
        out_shape=[jax.ShapeDtypeStruct((b, s, DSA_HEADS * HEAD_DIM), BF16)] + [c[2] for c in casts],
        scratch_shapes=[pltpu.VMEM((s, tq), I32),
                        pltpu.VMEM((s, tq), I16),
                        pltpu.VMEM((s, tq), I16),
                        pltpu.VMEM((s, tq), I16),
                        pltpu.VMEM((HEAD_DIM, DSA_HEADS * tq), BF16),
                        pltpu.VMEM((s // (2 * tq), HEAD_DIM + ONES_ROWS, 2 * tq), BF16),
                        pltpu.VMEM((HEAD_DIM + ONES_ROWS, DSA_HEADS * tq), F32)],
        compiler_params=_params("parallel", "arbitrary"),
        name="dsa",
    )(qbt, kb, vb, qit, ki, wit, *[w for w, _ in to_cast])
    return out, copies


def _dilated_body(*refs):
    ng = len(DIL_PATTERNS)
    q_refs, k_refs, v_refs = refs[0:ng], refs[ng:2 * ng], refs[2 * ng:3 * ng]
    o_refs = refs[3 * ng:4 * ng]
    acc_ref, m_ref, l_ref = refs[4 * ng:]
    sb_tokens = q_refs[0].shape[1]
    t0 = pl.program_id(2) * sb_tokens
    blk = HEAD_DIM
    UNITS = 4
    rq = lax.broadcasted_iota(I32, (blk, blk), 0)
    ck = lax.broadcasted_iota(I32, (blk, blk), 1)
    bias_cur = jnp.where(ck <= rq, 0.0, NEG)
    bias_prev = jnp.where(ck >= rq, 0.0, NEG)
    nt = (((1,), (1,)), ((), ()))

    for g, (win, dil) in enumerate(DIL_PATTERNS):
        q_ref, k_ref, v_ref = q_refs[g], k_refs[g], v_refs[g]
        per_res = sb_tokens // dil // blk

        def rows(start, dil=dil):
            return pl.ds(start, blk, stride=dil) if dil > 1 else pl.ds(start, blk)

        def units(it, carry, g=g, dil=dil, per_res=per_res, rows=rows,
                  q_ref=q_ref, k_ref=k_ref, v_ref=v_ref):
            q0s, kcs, kps, hps = [], [], [], []
            for n in range(UNITS):
                idx = it * UNITS + n
                q0 = idx // per_res + (idx % per_res) * (blk * dil)
                k_cur = t0 + q0
                has_prev = k_cur >= blk * dil
                q0s.append(q0)
                kcs.append(k_cur)
                hps.append(has_prev)
                kps.append(jnp.where(has_prev, k_cur - blk * dil, k_cur))
            ss = []
            for q0, kc, kp, hp in zip(q0s, kcs, kps, hps):
                k2 = jnp.concatenate([k_ref[0, rows(kp), :], k_ref[0, rows(kc), :]], axis=0).astype(BF16)
                s = lax.dot_general(q_ref[0, rows(q0), :].astype(BF16), k2, nt, preferred_element_type=F32)
                ss.append(s + jnp.concatenate([jnp.where(hp, bias_prev, NEG), bias_cur], axis=1))
            ms = [jnp.max(jnp.maximum(s[:, :blk], s[:, blk:]), axis=-1, keepdims=True) for s in ss]
            ps = [jnp.exp2(s - m).astype(BF16) for s, m in zip(ss, ms)]
            for q0, kc, kp, m, p in zip(q0s, kcs, kps, ms, ps):
                v2 = jnp.concatenate([v_ref[0, rows(kp), :], v_ref[0, rows(kc), :]], axis=0).astype(BF16)
                av = jnp.dot(p, jnp.concatenate([v2, jnp.ones_like(v2)], axis=1), preferred_element_type=F32)
                acc_ref[g, rows(q0), :] = av[:, :HEAD_DIM]
                l_ref[g, rows(q0), :] = av[:, HEAD_DIM:]
                m_ref[g, rows(q0), :] = jnp.broadcast_to(m, (blk, LANES))
            return carry

        lax.fori_loop(0, dil * per_res // UNITS, units, 0)

    step = 256

    def merge(c, carry):
        sl = pl.ds(pl.multiple_of(c * step, step), step)
        ms = [m_ref[g, sl, :] for g in range(ng)]
        m_all = functools.reduce(jnp.maximum, ms)
        ws = [jnp.exp2(m - m_all) for m in ms]
        den = sum(w * l_ref[g, sl, :] for g, w in enumerate(ws))
        for g, w in enumerate(ws):
            o_refs[g][0, sl, :] = (acc_ref[g, sl, :] * (w / den)).astype(BF16)
        return carry

    lax.fori_loop(0, sb_tokens // step, merge, 0)


def _dilated(qc, kc, vc):
    b, s, _ = qc.shape
    ng, hg = len(DIL_PATTERNS), DIL_HEADS_PER_GROUP
    sb_tokens = HEAD_DIM * max(dil for _, dil in DIL_PATTERNS)
    assert s % sb_tokens == 0 and all(win == HEAD_DIM * dil for win, dil in DIL_PATTERNS)
    head = lambda g: (lambda bb, j, sb: (bb, sb, g * hg + j))
    head_full = lambda g: (lambda bb, j, sb: (bb, 0, g * hg + j))
    q_specs = [pl.BlockSpec((1, sb_tokens, HEAD_DIM), head(g)) for g in range(ng)]
    kv_specs = [pl.BlockSpec((1, s, HEAD_DIM), head_full(g)) for g in range(ng)]
    out_spec = pl.BlockSpec((1, sb_tokens, HEAD_DIM), lambda bb, j, sb: (bb, sb, j))
    return pl.pallas_call(
        _dilated_body,
        grid=(b, hg, s // sb_tokens),
        in_specs=q_specs + kv_specs + kv_specs,
        out_specs=[out_spec] * ng,
        out_shape=[jax.ShapeDtypeStruct((b, s, hg * HEAD_DIM), BF16)] * ng,
        scratch_shapes=[pltpu.VMEM((ng, sb_tokens, HEAD_DIM), F32)] * 3,
        compiler_params=_params("parallel", "parallel", "arbitrary"),
        name="dilated",
    )(*([qc] * ng + [kc] * ng + [vc] * ng))


def _out_proj_body(*refs):
    *y_refs, w_ref, x_ref, o_ref, wb_ref = refs

    @pl.when(pl.program_id(1) == 0)
    def _():
        wb_ref[...] = w_ref[...].astype(BF16)

    y = jnp.concatenate([y_ref[...] for y_ref in y_refs], axis=1)
    o_ref[...] = x_ref[...] + jnp.dot(y, wb_ref[...], preferred_element_type=F32)


def _out_proj(parts, w, x, layer, tm, tn):
    t, d = x.shape
    assert sum(a.shape[1] for a in parts) == w.shape[1]
    return pl.pallas_call(
        _out_proj_body,
        grid=(d // tn, t // tm),
        in_specs=[pl.BlockSpec((tm, a.shape[1]), lambda n, m: (m, 0)) for a in parts]
        + [pl.BlockSpec((None, w.shape[1], tn), lambda n, m: (layer, 0, n)),
           pl.BlockSpec((tm, tn), lambda n, m: (m, n))],
        out_specs=pl.BlockSpec((tm, tn), lambda n, m: (m, n)),
        out_shape=jax.ShapeDtypeStruct((t, d), F32),
        scratch_shapes=[pltpu.VMEM((w.shape[1], tn), BF16)],
        compiler_params=_params("parallel", "arbitrary"),
        name="out_proj",
    )(*parts, w, x)


def _final_norm_body(x_ref, g_ref, o_ref):
    o_ref[...] = _rms(x_ref[...], g_ref[...])


def _final_norm(x, g, tm):
    t, d = x.shape
    return pl.pallas_call(
        _final_norm_body,
        grid=(t // tm,),
        in_specs=[pl.BlockSpec((tm, d), lambda m: (m, 0)), pl.BlockSpec((1, d), lambda m: (0, 0))],
        out_specs=pl.BlockSpec((tm, d), lambda m: (m, 0)),
        out_shape=jax.ShapeDtypeStruct((t, d), F32),
        compiler_params=_params("parallel"),
        name="final_norm",
    )(x, g)


def _align_w_in_body(x_ref, o_ref):
    o_wi = SEG_QI[1] + IDX_DIM
    o_qc = o_wi + IDX_HEADS
    rows = x_ref.shape[0]
    o_ref[:, :o_wi] = x_ref[:, :o_wi]
    o_ref[:, o_wi:SEG_WI[0]] = jnp.zeros((rows, SEG_WI[0] - o_wi), BF16)
    o_ref[:, SEG_WI[0]:SEG_WI[0] + IDX_HEADS] = x_ref[:, o_wi:o_qc]
    o_ref[:, SEG_WI[0] + IDX_HEADS:SEG_QC[0]] = jnp.zeros((rows, LANES - IDX_HEADS), BF16)
    o_ref[:, SEG_QC[0]:] = x_ref[:, o_qc:o_qc + D_IN_ALIGNED - SEG_QC[0]]


def _align_w_in(w, tr):
    depth, d, n = w.shape
    assert n - (SEG_QI[1] + IDX_DIM + IDX_HEADS) == D_IN_ALIGNED - SEG_QC[0]
    w = jnp.pad(w.astype(BF16), ((0, 0), (0, 0), (0, D_IN_ALIGNED - n)))
    return pl.pallas_call(
        _align_w_in_body,
        grid=(depth, d // tr),
        in_specs=[pl.BlockSpec((None, tr, D_IN_ALIGNED), lambda l, r: (l, r, 0))],
        out_specs=pl.BlockSpec((None, tr, D_IN_ALIGNED), lambda l, r: (l, r, 0)),
        out_shape=jax.ShapeDtypeStruct((depth, d, D_IN_ALIGNED), BF16),
        compiler_params=_params("parallel", "parallel"),
        name="align_w_in",
    )(w)


def _tile(n, want):
    while n % want:
        want //= 2
    return want


def kernel(x, positions, norm_ffn1, ffn1_gate, ffn1_up, ffn1_down, norm_mix, w_in, conv_w, w_out,
           norm_ffn2, ffn2_gate, ffn2_up, ffn2_down, norm_final):
    b, s, d = x.shape
    t = b * s
    depth = w_in.shape[0]
    tm_big = _tile(t, 1024)
    tm_proj = _tile(s, 512)
    tq_dsa = _tile(s, 256)

    pos = positions.astype(F32).reshape(t, 1)
    tabs = (*_rope_tables(pos, HEAD_DIM, tm_big), *_rope_tables(pos, IDX_DIM, tm_big))

    def ffn(xf, g, weights, layer):
        wg, wu, wd = weights
        h = _ffn_up(xf, g.reshape(depth, 1, d), wg, wu, layer, tm_big, 512)
        return _ffn_down(h, wd, xf, tm_big, 512)

    w_in_al = _align_w_in(w_in, _tile(d, 256))
    ffn1_stacks, ffn2_stacks = (ffn1_gate, ffn1_up, ffn1_down), (ffn2_gate, ffn2_up, ffn2_down)
    w1 = [w[0].astype(BF16) for w in ffn1_stacks]
    xf = x.reshape(t, d)
    for i in range(depth):
        xf = ffn(xf, norm_ffn1, w1, i)
        ya, qb, kb, vb, qi, ki, wi, qc, kc, vc = _in_proj(
            xf, norm_mix.reshape(depth, 1, d), w_in_al, conv_w, tabs, i, s, tm_proj)
        r3 = lambda a: a.reshape(b, s, a.shape[-1])
        to_cast = [(w, i) for w in ffn2_stacks] + ([(w, i + 1) for w in ffn1_stacks] if i + 1 < depth else [])
        yb, copies = _dsa(qb, r3(kb), r3(vb), qi, r3(ki), wi, tq_dsa, to_cast)
        w2, w1 = copies[:3], copies[3:]
        ycs = _dilated(r3(qc), r3(kc), r3(vc))
        parts = [ya, yb.reshape(t, -1)] + [yc.reshape(t, -1) for yc in ycs]
        xf = _out_proj(parts, w_out, xf, i, tm_big, 1024)
        xf = ffn(xf, norm_ffn2, w2, i)
    return _final_norm(xf, norm_final.reshape(1, d), tm_big).reshape(b, s, d)
```

```python
import functools

import jax
import jax.numpy as jnp
from jax import lax
from jax.experimental import pallas as pl
from jax.experimental.pallas import tpu as pltpu

F32 = jnp.float32
BF16 = jnp.bfloat16
I32 = jnp.int32
I16 = jnp.int16
HALF16 = 1 << 15
ONES_ROWS = 16

HEAD_DIM = 128
CONV_CH = 512
CONV_WIDTH = 3
DSA_HEADS = 6
IDX_HEADS = 16
IDX_DIM = 64
TOPK_MAX = 256
DIL_PATTERNS = ((128, 1), (512, 4), (2048, 16))
DIL_HEADS_PER_GROUP = 2
DIL_HEADS = len(DIL_PATTERNS) * DIL_HEADS_PER_GROUP
ROPE_THETA = 10000.0
RMS_EPS = 1e-6

LANES = 128
VMEM_LIMIT = 56 * 1024 * 1024
NEG = -1e30
INT_MIN = -2 ** 31
LOG2E = 1.4426950408889634

SEG_CONV = (0, 3 * CONV_CH)
SEG_QB = (SEG_CONV[1], SEG_CONV[1] + DSA_HEADS * HEAD_DIM)
SEG_KV = (SEG_QB[1], SEG_QB[1] + 2 * HEAD_DIM)
SEG_QI = (SEG_KV[1], SEG_KV[1] + IDX_HEADS * IDX_DIM)
SEG_KI = (SEG_QI[1], SEG_QI[1] + LANES)
SEG_WI = (SEG_KI[1], SEG_KI[1] + LANES)
SEG_QC = (SEG_WI[1], SEG_WI[1] + DIL_HEADS * HEAD_DIM)
SEG_KC = (SEG_QC[1], SEG_QC[1] + DIL_HEADS * HEAD_DIM)
SEG_VC = (SEG_KC[1], SEG_KC[1] + DIL_HEADS * HEAD_DIM)
D_IN_ALIGNED = SEG_VC[1]


def _params(*sem):
    return pltpu.CompilerParams(dimension_semantics=sem, vmem_limit_bytes=VMEM_LIMIT)


def _rms(x, g):
    ms = jnp.mean(x * x, axis=-1, keepdims=True)
    return x * lax.rsqrt(ms + RMS_EPS) * g


def _ffn_up_body(x_ref, g_ref, wg_ref, wu_ref, h_ref, xn_ref):
    @pl.when(pl.program_id(1) == 0)
    def _():
        xn_ref[...] = _rms(x_ref[...], g_ref[...]).astype(BF16)

    xn = xn_ref[...]
    a = jnp.dot(xn, wg_ref[...], preferred_element_type=F32)
    b = jnp.dot(xn, wu_ref[...], preferred_element_type=F32)
    h_ref[...] = (a * jax.nn.sigmoid(a) * b).astype(BF16)


def _ffn_up(x, g, wg, wu, layer, tm, tn):
    t, d = x.shape
    f = wg.shape[1]
    return pl.pallas_call(
        _ffn_up_body,
        grid=(t // tm, f // tn),
        in_specs=[
            pl.BlockSpec((tm, d), lambda m, n: (m, 0)),
            pl.BlockSpec((None, 1, d), lambda m, n: (layer, 0, 0)),
            pl.BlockSpec((d, tn), lambda m, n: (0, n)),
            pl.BlockSpec((d, tn), lambda m, n: (0, n)),
        ],
        out_specs=pl.BlockSpec((tm, tn), lambda m, n: (m, n)),
        out_shape=jax.ShapeDtypeStruct((t, f), BF16),
        scratch_shapes=[pltpu.VMEM((tm, d), BF16)],
        compiler_params=_params("parallel", "arbitrary"),
        name="ffn_up",
    )(x, g, wg, wu)


def _ffn_down_body(h_ref, w_ref, x_ref, o_ref):
    y = jnp.dot(h_ref[...], w_ref[...], preferred_element_type=F32)
    o_ref[...] = x_ref[...] + 0.5 * y


def _ffn_down(h, wd, x, tm, tn):
    t, f = h.shape
    d = wd.shape[1]
    return pl.pallas_call(
        _ffn_down_body,
        grid=(t // tm, d // tn),
        in_specs=[
            pl.BlockSpec((tm, f), lambda m, n: (m, 0)),
            pl.BlockSpec((f, tn), lambda m, n: (0, n)),
            pl.BlockSpec((tm, tn), lambda m, n: (m, n)),
        ],
        out_specs=pl.BlockSpec((tm, tn), lambda m, n: (m, n)),
        out_shape=jax.ShapeDtypeStruct((t, d), F32),
        compiler_params=_params("parallel", "arbitrary"),
        name="ffn_down",
    )(h, wd, x)


def _cast_block_specs(w, layer, nsteps, step_of):
    _, rows, cols = w.shape
    nblk = nsteps
    while rows % nblk or (rows // nblk) % 16:
        nblk //= 2
    blk = rows // nblk
    return (pl.BlockSpec((None, blk, cols), lambda *g: (layer, step_of(*g) * nblk // nsteps, 0)),
            pl.BlockSpec((blk, cols), lambda *g: (step_of(*g) * nblk // nsteps, 0)),
            jax.ShapeDtypeStruct((rows, cols), BF16))


def _rope_table_body(pos_ref, inv_ref, ch_ref, sh_ref, ci_ref, si_ref):
    ang = pos_ref[...] * inv_ref[...]
    cos, sin = jnp.cos(ang), jnp.sin(ang)
    hh, hi = HEAD_DIM // 2, IDX_DIM // 2
    ch_ref[...] = jnp.concatenate([cos[:, :hh]] * 2, axis=1)
    sh_ref[...] = jnp.concatenate([-sin[:, :hh], sin[:, :hh]], axis=1)
    ci_ref[...] = jnp.concatenate([cos[:, hh:hh + hi]] * (LANES // hi), axis=1)
    si_ref[...] = jnp.concatenate([-sin[:, hh:hh + hi], sin[:, hh:hh + hi]] * (LANES // IDX_DIM), axis=1)


def _rope_tables(pos, tm):
    t = pos.shape[0]
    inv = lambda dim: 1.0 / (ROPE_THETA ** (jnp.arange(0, dim, 2, dtype=F32) / dim))
    inv_l = jnp.concatenate([inv(HEAD_DIM), inv(IDX_DIM), jnp.zeros((LANES - (HEAD_DIM + IDX_DIM) // 2,), F32)])
    tab = pl.BlockSpec((tm, LANES), lambda m: (m, 0))
    return pl.pallas_call(
        _rope_table_body,
        grid=(t // tm,),
        in_specs=[pl.BlockSpec((tm, 1), lambda m: (m, 0)), pl.BlockSpec((1, LANES), lambda m: (0, 0))],
        out_specs=[tab] * 4,
        out_shape=[jax.ShapeDtypeStruct((t, LANES), F32)] * 4,
        compiler_params=_params("parallel"),
        name="rope_tables",
    )(pos, inv_l.reshape(1, LANES))


def _rope128(x, cos, sin):
    return x * cos + pltpu.roll(x, HEAD_DIM // 2, 1) * sin


def _rope64(x, cos, sin, lo_half):
    partner = jnp.where(lo_half, pltpu.roll(x, LANES - IDX_DIM // 2, 1), pltpu.roll(x, IDX_DIM // 2, 1))
    return x * cos + partner * sin


def _in_proj_body(seq_tiles, x_ref, g_ref, w_ref, cw_ref, ch_ref, sh_ref, ci_ref, si_ref,
                  ya_ref, qb_ref, kb_ref, vb_ref, qi_ref, ki_ref, wi_ref, qc_ref, kc_ref, vc_ref,
                  u_ref):
    tm = x_ref.shape[0]
    xn = _rms(x_ref[...], g_ref[...]).astype(BF16)

    def proj(seg):
        return jnp.dot(xn, w_ref[:, seg[0]:seg[1]], preferred_element_type=F32)

    p = proj(SEG_CONV)
    h, gate_b, gate_c = p[:, :CONV_CH], p[:, CONV_CH:2 * CONV_CH], p[:, 2 * CONV_CH:]
    u = gate_c * h

    @pl.when(pl.program_id(0) % seq_tiles == 0)
    def _():
        u_ref[0:8, :] = jnp.zeros((8, CONV_CH), F32)

    u_ref[8:8 + tm, :] = u
    cw = cw_ref[...]
    y = cw[2:3, :] * u + cw[1:2, :] * u_ref[7:7 + tm, :] + cw[0:1, :] * u_ref[6:6 + tm, :]
    u_ref[0:8, :] = u[tm - 8:, :]
    ya_ref[...] = (gate_b * y).astype(BF16)

    ch, sh, ci, si = ch_ref[...], sh_ref[...], ci_ref[...], si_ref[...]
    scale = HEAD_DIM ** -0.5 * LOG2E

    def rope_heads(p, n, mul):
        return jnp.concatenate(
            [_rope128(p[:, j * LANES:(j + 1) * LANES], ch, sh) * mul for j in range(n)], axis=1)

    qb_ref[...] = rope_heads(proj(SEG_QB), DSA_HEADS, scale).T.astype(BF16)
    p = proj(SEG_KV)
    kb_ref[...] = _rope128(p[:, :HEAD_DIM], ch, sh).astype(BF16)
    vb_ref[...] = p[:, HEAD_DIM:].astype(BF16)

    lo_half = lax.broadcasted_iota(I32, (tm, LANES), 1) % IDX_DIM < IDX_DIM // 2
    p = proj(SEG_QI)
    qi_ref[...] = jnp.concatenate(
        [_rope64(p[:, j * LANES:(j + 1) * LANES], ci, si, lo_half) * (IDX_DIM ** -0.5)
         for j in range(IDX_HEADS * IDX_DIM // LANES)], axis=1).T.astype(BF16)
    ki_ref[...] = _rope64(proj(SEG_KI), ci, si, lo_half)[:, :IDX_DIM].astype(BF16)
    wi_ref[...] = (proj(SEG_WI) * (IDX_HEADS ** -0.5)).T[:IDX_HEADS, :]

    qc_ref[...] = rope_heads(proj(SEG_QC), DIL_HEADS, scale)
    kc_ref[...] = rope_heads(proj(SEG_KC), DIL_HEADS, 1.0)
    vc_ref[...] = proj(SEG_VC)


def _in_proj(x, g, w, cw, tabs, layer, seq, tm):
    t, d = x.shape
    n = w.shape[2]
    row = lambda width: pl.BlockSpec((tm, width), lambda m: (m, 0))
    col = lambda height: pl.BlockSpec((height, tm), lambda m: (0, m))
    const = lambda shape: pl.BlockSpec((None,) + shape, lambda m: (layer, 0, 0))
    outs = [(CONV_CH, BF16, True), (DSA_HEADS * HEAD_DIM, BF16, False), (HEAD_DIM, BF16, True),
            (HEAD_DIM, BF16, True), (IDX_HEADS * IDX_DIM, BF16, False), (IDX_DIM, BF16, True),
            (IDX_HEADS, F32, False), (DIL_HEADS * HEAD_DIM, F32, True), (DIL_HEADS * HEAD_DIM, F32, True),
            (DIL_HEADS * HEAD_DIM, F32, True)]
    return pl.pallas_call(
        functools.partial(_in_proj_body, seq // tm),
        grid=(t // tm,),
        in_specs=[row(d), const((1, d)),
                  pl.BlockSpec((None, d, n), lambda m: (layer, 0, 0), pipeline_mode=pl.Buffered(1)),
                  const((CONV_WIDTH, CONV_CH)), row(LANES), row(LANES), row(LANES), row(LANES)],
        out_specs=[row(wd) if tok else col(wd) for wd, _, tok in outs],
        out_shape=[jax.ShapeDtypeStruct((t, wd) if tok else (wd, t), dt) for wd, dt, tok in outs],
        scratch_shapes=[pltpu.VMEM((tm + 8, CONV_CH), F32)],
        compiler_params=_params("arbitrary"),
        name="in_proj",
    )(x, g, w, cw, *tabs)


def _dsa_body(topk, ncast, qbt_in_ref, kb_ref, vb_ref, qt_ref, ki_ref, wt_ref, *rest):
    cast_in, (o_ref, *cast_out) = rest[:ncast], rest[ncast:2 * ncast + 1]
    keys_ref, hi_ref, lo_ref, lo2_ref, qbt_ref, vt_ref, acc_ref = rest[2 * ncast + 1:]
    tq = o_ref.shape[1]
    seq = kb_ref.shape[1]

    for src_ref, dst_ref in zip(cast_in, cast_out):
        dst_ref[...] = src_ref[...].astype(BF16)

    nh = DSA_HEADS
    i = pl.program_id(1)
    nch = i + 1

    def chunk(c):
        return pl.ds(pl.multiple_of(c * tq, tq), tq)

    def transpose_bf16(a):
        return a.astype(F32).T.astype(BF16)

    @pl.when(i == 0)
    def _():
        for c2 in range(seq // (2 * tq)):
            vt_ref[c2, :HEAD_DIM, :] = transpose_bf16(vb_ref[0, c2 * 2 * tq:(c2 + 1) * 2 * tq, :])
            vt_ref[c2, HEAD_DIM:, :] = jnp.ones((ONES_ROWS, 2 * tq), BF16)

    for h in range(nh):
        qbt_ref[:, h * tq:(h + 1) * tq] = qbt_in_ref[h * HEAD_DIM:(h + 1) * HEAD_DIM, :]
    wt = wt_ref[...]
    krow = lax.broadcasted_iota(I32, (tq, tq), 0)

    def pair(c2):
        return pl.ds(pl.multiple_of(c2 * (2 * tq), 2 * tq), 2 * tq)

    def score_rows(rows, n, first_key):
        kc = ki_ref[0, rows, :]
        acc = jnp.zeros((n, tq), F32)
        for h in range(IDX_HEADS):
            lg = jnp.dot(kc, qt_ref[h * IDX_DIM:(h + 1) * IDX_DIM, :], preferred_element_type=F32)
            acc = acc + jnp.maximum(lg, 0.0) * wt[h:h + 1, :]
        bits = pltpu.bitcast(acc, I32)
        key = bits ^ ((bits >> 31) & 0x7FFFFFFF)
        causal = lax.broadcasted_iota(I32, (n, tq), 0) + first_key <= lax.broadcasted_iota(I32, (n, tq), 1) + i * tq
        key = jnp.where(causal, key, INT_MIN)
        keys_ref[rows, :] = key
        hi_ref[rows, :] = (key >> 16).astype(I16)
        lo_ref[rows, :] = ((key & 0xFFFF) - HALF16).astype(I16)

    def score_pair(c2, carry):
        score_rows(pair(c2), 2 * tq, c2 * (2 * tq))
        return carry

    lax.fori_loop(0, nch // 2, score_pair, 0)

    npair = (nch + 1) // 2

    @pl.when(nch % 2 == 1)
    def _():
        score_rows(chunk(nch - 1), tq, (nch - 1) * tq)
        keys_ref[chunk(nch), :] = jnp.full((tq, tq), INT_MIN, I32)
        hi_ref[chunk(nch), :] = jnp.full((tq, tq), -HALF16, I16)
        lo_ref[chunk(nch), :] = jnp.full((tq, tq), -HALF16, I16)

    def count(pred):
        def body(c, acc):
            hit = pred(keys_ref[chunk(c), :], krow + c * tq).astype(I32)
            return acc + jnp.sum(hit.reshape(tq // 8, 8, tq), axis=0)
        acc = lax.fori_loop(0, nch, body, jnp.zeros((8, tq), I32))
        return jnp.sum(acc, axis=0, keepdims=True)

    def count16(ref, pred):
        def body(c2, acc):
            hit = jnp.where(pred(ref[pair(c2), :]), jnp.ones((), BF16), jnp.zeros((), BF16))
            parts = [hit[r * 16:(r + 1) * 16] for r in range(2 * tq // 16)]
            while len(parts) > 1:
                parts = [parts[j] + parts[j + 1] for j in range(0, len(parts), 2)]
            return acc + parts[0].astype(F32)
        acc = lax.fori_loop(0, npair, body, jnp.zeros((16, tq), F32))
        return jnp.sum(acc, axis=0, keepdims=True)

    def kth_largest16(ref, k):
        def search_bit(b, t_u):
            cand = t_u | (jnp.int32(1) << (15 - b))
            cand16 = (cand - HALF16).astype(I16)
            return jnp.where(count16(ref, lambda v: v >= cand16) >= k, cand, t_u)
        return lax.fori_loop(0, 16, search_bit, jnp.zeros((1, tq), I32)) - HALF16

    t_hi = kth_largest16(hi_ref, float(topk))
    t_hi16 = t_hi.astype(I16)
    n_gt_hi = count16(hi_ref, lambda v: v > t_hi16)
    need_lo = float(topk) - n_gt_hi

    def bucket(c2, carry):
        lo2_ref[pair(c2), :] = jnp.where(hi_ref[pair(c2), :] == t_hi16, lo_ref[pair(c2), :],
                                         jnp.full((), -HALF16, I16))
        return carry

    lax.fori_loop(0, npair, bucket, 0)
    t_lo = kth_largest16(lo2_ref, need_lo)
    thr = jnp.maximum((t_hi << 16) | (t_lo + HALF16), INT_MIN + 1)
    t_lo16 = t_lo.astype(I16)
    n_ge = jnp.where(t_hi > -HALF16, n_gt_hi + count16(lo2_ref, lambda v: v >= t_lo16), 0.0)

    idx_bits = seq.bit_length() - 1

    @pl.when(jnp.max(n_ge) > topk)
    def _():
        need = topk - count(lambda k, _: k > thr)
        def bit(b, j):
            cand = j | (jnp.int32(1) << (idx_bits - 1 - b))
            below = count(lambda k, idx: (k == thr) & (idx < cand))
            return jnp.where(below < need, cand, j)
        bound = lax.fori_loop(0, idx_bits, bit, jnp.zeros((1, tq), I32))
        bound = jnp.where(n_ge > topk, bound, seq)

        def demote(c, carry):
            k = keys_ref[chunk(c), :]
            keys_ref[chunk(c), :] = jnp.where((k == thr) & (krow + c * tq > bound), thr - 1, k)
            return carry

        lax.fori_loop(0, nch, demote, 0)

    acc_ref[...] = jnp.zeros(acc_ref.shape, F32)

    def attend(c2, m):
        bias = jnp.where(keys_ref[pair(c2), :] >= thr, 0.0, NEG)
        st = jnp.dot(kb_ref[0, pair(c2), :], qbt_ref[...], preferred_element_type=F32)
        st = st + jnp.concatenate([bias] * nh, axis=1)
        m_new = jnp.maximum(m, jnp.max(st, axis=0, keepdims=True))
        p = jnp.exp2(st - m_new).astype(BF16)
        acc_ref[...] = acc_ref[...] * jnp.exp2(m - m_new) + jnp.dot(vt_ref[c2], p, preferred_element_type=F32)
        return m_new

    lax.fori_loop(0, npair, attend, jnp.full((1, nh * tq), NEG, F32))
    out = acc_ref[:HEAD_DIM, :] / acc_ref[HEAD_DIM:HEAD_DIM + 1, :]
    for h in range(nh):
        o_ref[0, :, h * HEAD_DIM:(h + 1) * HEAD_DIM] = out[:, h * tq:(h + 1) * tq].T.astype(BF16)


def _dsa(qbt, kb, vb, qit, ki, wit, tq, to_cast):
    b, s, _ = kb.shape
    topk = min(TOPK_MAX, s // 4)
    assert s % (2 * tq) == 0
    nq = s // tq
    blk = lambda width: pl.BlockSpec((1, tq, width), lambda bb, i: (bb, i, 0))
    full = lambda width: pl.BlockSpec((1, s, width), lambda bb, i: (bb, 0, 0))
    qcol = lambda a: pl.BlockSpec((a.shape[0], tq), lambda bb, i: (0, bb * nq + i))
    casts = [_cast_block_specs(w, layer, b * nq, lambda bb, i: bb * nq + i) for w, layer in to_cast]
    out, *copies = pl.pallas_call(
        functools.partial(_dsa_body, topk, len(casts)),
        grid=(b, nq),
        in_specs=[qcol(qbt), full(HEAD_DIM), full(HEAD_DIM), qcol(qit), full(IDX_DIM), qcol(wit)]
        + [c[0] for c in casts],
        out_specs=[blk(DSA_HEADS * HEAD_DIM)] + [c[1] for c in casts],
        out_shape=[jax.ShapeDtypeStruct((b, s, DSA_HEADS * HEAD_DIM), BF16)] + [c[2] for c in casts],
        scratch_shapes=[pltpu.VMEM((s, tq), I32),
                        pltpu.VMEM((s, tq), I16),
                        pltpu.VMEM((s, tq), I16),
                        pltpu.VMEM((s, tq), I16),
                        pltpu.VMEM((HEAD_DIM, DSA_HEADS * tq), BF16),
                        pltpu.VMEM((s // (2 * tq), HEAD_DIM + ONES_ROWS, 2 * tq), BF16),
                        pltpu.VMEM((HEAD_DIM + ONES_ROWS, DSA_HEADS * tq), F32)],
        compiler_params=_params("parallel", "arbitrary"),
        name="dsa",
    )(qbt, kb, vb, qit, ki, wit, *[w for w, _ in to_cast])
    return out, copies


def _dilated_body(*refs):
    ng = len(DIL_PATTERNS)
    q_refs, k_refs, v_refs = refs[0:ng], refs[ng:2 * ng], refs[2 * ng:3 * ng]
    o_refs = refs[3 * ng:4 * ng]
    acc_ref, m_ref, l_ref = refs[4 * ng:]
    sb_tokens = q_refs[0].shape[1]
    t0 = pl.program_id(2) * sb_tokens
    blk = HEAD_DIM
    UNITS = 4
    rq = lax.broadcasted_iota(I32, (blk, blk), 0)
    ck = lax.broadcasted_iota(I32, (blk, blk), 1)
    bias_cur = jnp.where(ck <= rq, 0.0, NEG)
    bias_prev = jnp.where(ck >= rq, 0.0, NEG)
    nt = (((1,), (1,)), ((), ()))

    for g, (win, dil) in enumerate(DIL_PATTERNS):
        q_ref, k_ref, v_ref = q_refs[g], k_refs[g], v_refs[g]
        per_res = sb_tokens // dil // blk

        def rows(start, dil=dil):
            return pl.ds(start, blk, stride=dil) if dil > 1 else pl.ds(start, blk)

        def units(it, carry, g=g, dil=dil, per_res=per_res, rows=rows,
                  q_ref=q_ref, k_ref=k_ref, v_ref=v_ref):
            q0s, kcs, kps, hps = [], [], [], []
            for n in range(UNITS):
                idx = it * UNITS + n
                q0 = idx // per_res + (idx % per_res) * (blk * dil)
                k_cur = t0 + q0
                has_prev = k_cur >= blk * dil
                q0s.append(q0)
                kcs.append(k_cur)
                hps.append(has_prev)
                kps.append(jnp.where(has_prev, k_cur - blk * dil, k_cur))
            ss = []
            for q0, kc, kp, hp in zip(q0s, kcs, kps, hps):
                k2 = jnp.concatenate([k_ref[0, rows(kp), :], k_ref[0, rows(kc), :]], axis=0).astype(BF16)
                s = lax.dot_general(q_ref[0, rows(q0), :].astype(BF16), k2, nt, preferred_element_type=F32)
                ss.append(s + jnp.concatenate([jnp.where(hp, bias_prev, NEG), bias_cur], axis=1))
            ms = [jnp.max(jnp.maximum(s[:, :blk], s[:, blk:]), axis=-1, keepdims=True) for s in ss]
            ps = [jnp.exp2(s - m).astype(BF16) for s, m in zip(ss, ms)]
            for q0, kc, kp, m, p in zip(q0s, kcs, kps, ms, ps):
                v2 = jnp.concatenate([v_ref[0, rows(kp), :], v_ref[0, rows(kc), :]], axis=0).astype(BF16)
                av = jnp.dot(p, jnp.concatenate([v2, jnp.ones_like(v2)], axis=1), preferred_element_type=F32)
                acc_ref[g, rows(q0), :] = av[:, :HEAD_DIM]
                l_ref[g, rows(q0), :] = av[:, HEAD_DIM:]
                m_ref[g, rows(q0), :] = jnp.broadcast_to(m, (blk, LANES))
            return carry

        lax.fori_loop(0, dil * per_res // UNITS, units, 0)

    step = 256

    def merge(c, carry):
        sl = pl.ds(pl.multiple_of(c * step, step), step)
        ms = [m_ref[g, sl, :] for g in range(ng)]
        m_all = functools.reduce(jnp.maximum, ms)
        ws = [jnp.exp2(m - m_all) for m in ms]
        den = sum(w * l_ref[g, sl, :] for g, w in enumerate(ws))
        for g, w in enumerate(ws):
            o_refs[g][0, sl, :] = (acc_ref[g, sl, :] * (w / den)).astype(BF16)
        return carry

    lax.fori_loop(0, sb_tokens // step, merge, 0)


def _dilated(qc, kc, vc):
    b, s, _ = qc.shape
    ng, hg = len(DIL_PATTERNS), DIL_HEADS_PER_GROUP
    sb_tokens = HEAD_DIM * max(dil for _, dil in DIL_PATTERNS)
    assert s % sb_tokens == 0 and all(win == HEAD_DIM * dil for win, dil in DIL_PATTERNS)
    head = lambda g: (lambda bb, j, sb: (bb, sb, g * hg + j))
    head_full = lambda g: (lambda bb, j, sb: (bb, 0, g * hg + j))
    q_specs = [pl.BlockSpec((1, sb_tokens, HEAD_DIM), head(g)) for g in range(ng)]
    kv_specs = [pl.BlockSpec((1, s, HEAD_DIM), head_full(g)) for g in range(ng)]
    out_spec = pl.BlockSpec((1, sb_tokens, HEAD_DIM), lambda bb, j, sb: (bb, sb, j))
    return pl.pallas_call(
        _dilated_body,
        grid=(b, hg, s // sb_tokens),
        in_specs=q_specs + kv_specs + kv_specs,
        out_specs=[out_spec] * ng,
        out_shape=[jax.ShapeDtypeStruct((b, s, hg * HEAD_DIM), BF16)] * ng,
        scratch_shapes=[pltpu.VMEM((ng, sb_tokens, HEAD_DIM), F32)] * 3,
        compiler_params=_params("parallel", "parallel", "arbitrary"),
        name="dilated",
    )(*([qc] * ng + [kc] * ng + [vc] * ng))


def _out_proj_body(*refs):
    *y_refs, w_ref, x_ref, o_ref, wb_ref = refs

    @pl.when(pl.program_id(1) == 0)
    def _():
        wb_ref[...] = w_ref[...].astype(BF16)

    y = jnp.concatenate([y_ref[...] for y_ref in y_refs], axis=1)
    o_ref[...] = x_ref[...] + jnp.dot(y, wb_ref[...], preferred_element_type=F32)


def _out_proj(parts, w, x, layer, tm, tn):
    t, d = x.shape
    assert sum(a.shape[1] for a in parts) == w.shape[1]
    return pl.pallas_call(
        _out_proj_body,
        grid=(d // tn, t // tm),
        in_specs=[pl.BlockSpec((tm, a.shape[1]), lambda n, m: (m, 0)) for a in parts]
        + [pl.BlockSpec((None, w.shape[1], tn), lambda n, m: (layer, 0, n)),
           pl.BlockSpec((tm, tn), lambda n, m: (m, n))],
        out_specs=pl.BlockSpec((tm, tn), lambda n, m: (m, n)),
        out_shape=jax.ShapeDtypeStruct((t, d), F32),
        scratch_shapes=[pltpu.VMEM((w.shape[1], tn), BF16)],
        compiler_params=_params("parallel", "arbitrary"),
        name="out_proj",
    )(*parts, w, x)


def _final_norm_body(x_ref, g_ref, o_ref):
    o_ref[...] = _rms(x_ref[...], g_ref[...])


def _final_norm(x, g, tm):
    t, d = x.shape
    return pl.pallas_call(
        _final_norm_body,
        grid=(t // tm,),
        in_specs=[pl.BlockSpec((tm, d), lambda m: (m, 0)), pl.BlockSpec((1, d), lambda m: (0, 0))],
        out_specs=pl.BlockSpec((tm, d), lambda m: (m, 0)),
        out_shape=jax.ShapeDtypeStruct((t, d), F32),
        compiler_params=_params("parallel"),
        name="final_norm",
    )(x, g)


def _align_w_in_body(x_ref, o_ref):
    o_wi = SEG_QI[1] + IDX_DIM
    o_qc = o_wi + IDX_HEADS
    rows = x_ref.shape[0]
    o_ref[:, :o_wi] = x_ref[:, :o_wi]
    o_ref[:, o_wi:SEG_WI[0]] = jnp.zeros((rows, SEG_WI[0] - o_wi), BF16)
    o_ref[:, SEG_WI[0]:SEG_WI[0] + IDX_HEADS] = x_ref[:, o_wi:o_qc]
    o_ref[:, SEG_WI[0] + IDX_HEADS:SEG_QC[0]] = jnp.zeros((rows, LANES - IDX_HEADS), BF16)
    o_ref[:, SEG_QC[0]:] = x_ref[:, o_qc:o_qc + D_IN_ALIGNED - SEG_QC[0]]


def _align_w_in(w, tr):
    depth, d, n = w.shape
    assert n - (SEG_QI[1] + IDX_DIM + IDX_HEADS) == D_IN_ALIGNED - SEG_QC[0]
    w = jnp.pad(w.astype(BF16), ((0, 0), (0, 0), (0, D_IN_ALIGNED - n)))
    return pl.pallas_call(
        _align_w_in_body,
        grid=(depth, d // tr),
        in_specs=[pl.BlockSpec((None, tr, D_IN_ALIGNED), lambda l, r: (l, r, 0))],
        out_specs=pl.BlockSpec((None, tr, D_IN_ALIGNED), lambda l, r: (l, r, 0)),
        out_shape=jax.ShapeDtypeStruct((depth, d, D_IN_ALIGNED), BF16),
        compiler_params=_params("parallel", "parallel"),
        name="align_w_in",
    )(w)


def _tile(n, want):
    while n % want:
        want //= 2
    return want


def kernel(x, positions, norm_ffn1, ffn1_gate, ffn1_up, ffn1_down, norm_mix, w_in, conv_w, w_out,
           norm_ffn2, ffn2_gate, ffn2_up, ffn2_down, norm_final):
    b, s, d = x.shape
    t = b * s
    depth = w_in.shape[0]
    tm_big = _tile(t, 1024)
    tm_proj = _tile(s, 512)
    tq_dsa = _tile(s, 256)

    pos = positions.astype(F32).reshape(t, 1)
    tabs = _rope_tables(pos, tm_big)

    def ffn(xf, g, weights, layer):
        wg, wu, wd = weights
        h = _ffn_up(xf, g.reshape(depth, 1, d), wg, wu, layer, tm_big, 512)
        return _ffn_down(h, wd, xf, tm_big, 512)

    w_in_al = _align_w_in(w_in, _tile(d, 256))
    ffn1_stacks, ffn2_stacks = (ffn1_gate, ffn1_up, ffn1_down), (ffn2_gate, ffn2_up, ffn2_down)
    w1 = [w[0].astype(BF16) for w in ffn1_stacks]
    xf = x.reshape(t, d)
    for i in range(depth):
        xf = ffn(xf, norm_ffn1, w1, i)
        ya, qb, kb, vb, qi, ki, wi, qc, kc, vc = _in_proj(
            xf, norm_mix.reshape(depth, 1, d), w_in_al, conv_w, tabs, i, s, tm_proj)
        r3 = lambda a: a.reshape(b, s, a.shape[-1])
        to_cast = [(w, i) for w in ffn2_stacks] + ([(w, i + 1) for w in ffn1_stacks] if i + 1 < depth else [])
        yb, copies = _dsa(qb, r3(kb), r3(vb), qi, r3(ki), wi, tq_dsa, to_cast)
        w2, w1 = copies[:3], copies[3:]
        ycs = _dilated(r3(qc), r3(kc), r3(vc))
        parts = [ya, yb.reshape(t, -1)] + [yc.reshape(t, -1) for yc in ycs]
        xf = _out_proj(parts, w_out, xf, i, tm_big, 1024)
        xf = ffn(xf, norm_ffn2, w2, i)
    return _final_norm(xf, norm_final.reshape(1, d), tm_big).reshape(b, s, d)
```

```python
import functools

import jax
import jax.numpy as jnp
from jax import lax
from jax.experimental import pallas as pl
from jax.experimental.pallas import tpu as pltpu

F32 = jnp.float32
BF16 = jnp.bfloat16
I32 = jnp.int32
I16 = jnp.int16
HALF16 = 1 << 15
ONES_ROWS = 16

HEAD_DIM = 128
CONV_CH = 512
CONV_WIDTH = 3
DSA_HEADS = 6
IDX_HEADS = 16
IDX_DIM = 64
TOPK_MAX = 256
DIL_PATTERNS = ((128, 1), (512, 4), (2048, 16))
DIL_HEADS_PER_GROUP = 2
DIL_HEADS = len(DIL_PATTERNS) * DIL_HEADS_PER_GROUP
ROPE_THETA = 10000.0
RMS_EPS = 1e-6

LANES = 128
VMEM_LIMIT = 56 * 1024 * 1024
NEG = -1e30
INT_MIN = -2 ** 31
LOG2E = 1.4426950408889634

SEG_CONV = (0, 3 * CONV_CH)
SEG_QB = (SEG_CONV[1], SEG_CONV[1] + DSA_HEADS * HEAD_DIM)
SEG_KV = (SEG_QB[1], SEG_QB[1] + 2 * HEAD_DIM)
SEG_QI = (SEG_KV[1], SEG_KV[1] + IDX_HEADS * IDX_DIM)
SEG_KI = (SEG_QI[1], SEG_QI[1] + LANES)
SEG_WI = (SEG_KI[1], SEG_KI[1] + LANES)
SEG_QC = (SEG_WI[1], SEG_WI[1] + DIL_HEADS * HEAD_DIM)
SEG_KC = (SEG_QC[1], SEG_QC[1] + DIL_HEADS * HEAD_DIM)
SEG_VC = (SEG_KC[1], SEG_KC[1] + DIL_HEADS * HEAD_DIM)
D_IN_ALIGNED = SEG_VC[1]


def _params(*sem):
    return pltpu.CompilerParams(dimension_semantics=sem, vmem_limit_bytes=VMEM_LIMIT)


def _rms(x, g):
    ms = jnp.mean(x * x, axis=-1, keepdims=True)
    return x * lax.rsqrt(ms + RMS_EPS) * g


def _ffn_up_body(x_ref, g_ref, wg_ref, wu_ref, h_ref, xn_ref):
    @pl.when(pl.program_id(1) == 0)
    def _():
        xn_ref[...] = _rms(x_ref[...], g_ref[...]).astype(BF16)

    xn = xn_ref[...]
    a = jnp.dot(xn, wg_ref[...], preferred_element_type=F32)
    b = jnp.dot(xn, wu_ref[...], preferred_element_type=F32)
    h_ref[...] = (a * jax.nn.sigmoid(a) * b).astype(BF16)


def _ffn_up(x, g, wg, wu, layer, tm, tn):
    t, d = x.shape
    f = wg.shape[1]
    return pl.pallas_call(
        _ffn_up_body,
        grid=(t // tm, f // tn),
        in_specs=[
            pl.BlockSpec((tm, d), lambda m, n: (m, 0)),
            pl.BlockSpec((None, 1, d), lambda m, n: (layer, 0, 0)),
            pl.BlockSpec((d, tn), lambda m, n: (0, n)),
            pl.BlockSpec((d, tn), lambda m, n: (0, n)),
        ],
        out_specs=pl.BlockSpec((tm, tn), lambda m, n: (m, n)),
        out_shape=jax.ShapeDtypeStruct((t, f), BF16),
        scratch_shapes=[pltpu.VMEM((tm, d), BF16)],
        compiler_params=_params("parallel", "arbitrary"),
        name="ffn_up",
    )(x, g, wg, wu)


def _ffn_down_body(h_ref, w_ref, x_ref, o_ref):
    y = jnp.dot(h_ref[...], w_ref[...], preferred_element_type=F32)
    o_ref[...] = x_ref[...] + 0.5 * y


def _ffn_down(h, wd, x, tm, tn):
    t, f = h.shape
    d = wd.shape[1]
    return pl.pallas_call(
        _ffn_down_body,
        grid=(t // tm, d // tn),
        in_specs=[
            pl.BlockSpec((tm, f), lambda m, n: (m, 0)),
            pl.BlockSpec((f, tn), lambda m, n: (0, n)),
            pl.BlockSpec((tm, tn), lambda m, n: (m, n)),
        ],
        out_specs=pl.BlockSpec((tm, tn), lambda m, n: (m, n)),
        out_shape=jax.ShapeDtypeStruct((t, d), F32),
        compiler_params=_params("parallel", "arbitrary"),
        name="ffn_down",
    )(h, wd, x)


def _cast_block_specs(w, layer, nsteps, step_of):
    _, rows, cols = w.shape
    nblk = nsteps
    while rows % nblk or (rows // nblk) % 16:
        nblk //= 2
    blk = rows // nblk
    return (pl.BlockSpec((None, blk, cols), lambda *g: (layer, step_of(*g) * nblk // nsteps, 0)),
            pl.BlockSpec((blk, cols), lambda *g: (step_of(*g) * nblk // nsteps, 0)),
            jax.ShapeDtypeStruct((rows, cols), BF16))


def _rope_table_body(ncast, pos_ref, inv_ref, *rest):
    cast_in, (ch_ref, sh_ref, ci_ref, si_ref, *cast_out) = rest[:ncast], rest[ncast:]
    for src_ref, dst_ref in zip(cast_in, cast_out):
        dst_ref[...] = src_ref[...].astype(BF16)

    ang = pos_ref[...] * inv_ref[...]
    cos, sin = jnp.cos(ang), jnp.sin(ang)
    hh, hi = HEAD_DIM // 2, IDX_DIM // 2
    ch_ref[...] = jnp.concatenate([cos[:, :hh]] * 2, axis=1)
    sh_ref[...] = jnp.concatenate([-sin[:, :hh], sin[:, :hh]], axis=1)
    ci_ref[...] = jnp.concatenate([cos[:, hh:hh + hi]] * (LANES // hi), axis=1)
    si_ref[...] = jnp.concatenate([-sin[:, hh:hh + hi], sin[:, hh:hh + hi]] * (LANES // IDX_DIM), axis=1)


def _rope_tables(pos, tm, to_cast):
    t = pos.shape[0]
    inv = lambda dim: 1.0 / (ROPE_THETA ** (jnp.arange(0, dim, 2, dtype=F32) / dim))
    inv_l = jnp.concatenate([inv(HEAD_DIM), inv(IDX_DIM), jnp.zeros((LANES - (HEAD_DIM + IDX_DIM) // 2,), F32)])
    tab = pl.BlockSpec((tm, LANES), lambda m: (m, 0))
    casts = [_cast_block_specs(w, layer, t // tm, lambda m: m) for w, layer in to_cast]
    *tabs, = pl.pallas_call(
        functools.partial(_rope_table_body, len(casts)),
        grid=(t // tm,),
        in_specs=[pl.BlockSpec((tm, 1), lambda m: (m, 0)), pl.BlockSpec((1, LANES), lambda m: (0, 0))]
        + [c[0] for c in casts],
        out_specs=[tab] * 4 + [c[1] for c in casts],
        out_shape=[jax.ShapeDtypeStruct((t, LANES), F32)] * 4 + [c[2] for c in casts],
        compiler_params=_params("parallel"),
        name="rope_tables",
    )(pos, inv_l.reshape(1, LANES), *[w for w, _ in to_cast])
    return tabs[:4], tabs[4:]


def _rope128(x, cos, sin):
    return x * cos + pltpu.roll(x, HEAD_DIM // 2, 1) * sin


def _rope64(x, cos, sin, lo_half):
    partner = jnp.where(lo_half, pltpu.roll(x, LANES - IDX_DIM // 2, 1), pltpu.roll(x, IDX_DIM // 2, 1))
    return x * cos + partner * sin


def _in_proj_body(seq_tiles, x_ref, g_ref, w_ref, cw_ref, ch_ref, sh_ref, ci_ref, si_ref,
                  ya_ref, qb_ref, kb_ref, vb_ref, qi_ref, ki_ref, wi_ref, qc_ref, kc_ref, vc_ref,
                  u_ref):
    tm = x_ref.shape[0]
    xn = _rms(x_ref[...], g_ref[...]).astype(BF16)

    def proj(seg):
        return jnp.dot(xn, w_ref[:, seg[0]:seg[1]], preferred_element_type=F32)

    p = proj(SEG_CONV)
    h, gate_b, gate_c = p[:, :CONV_CH], p[:, CONV_CH:2 * CONV_CH], p[:, 2 * CONV_CH:]
    u = gate_c * h

    @pl.when(pl.program_id(0) % seq_tiles == 0)
    def _():
        u_ref[0:8, :] = jnp.zeros((8, CONV_CH), F32)

    u_ref[8:8 + tm, :] = u
    cw = cw_ref[...]
    y = cw[2:3, :] * u + cw[1:2, :] * u_ref[7:7 + tm, :] + cw[0:1, :] * u_ref[6:6 + tm, :]
    u_ref[0:8, :] = u[tm - 8:, :]
    ya_ref[...] = (gate_b * y).astype(BF16)

    ch, sh, ci, si = ch_ref[...], sh_ref[...], ci_ref[...], si_ref[...]
    scale = HEAD_DIM ** -0.5 * LOG2E

    def rope_heads(p, n, mul):
        return jnp.concatenate(
            [_rope128(p[:, j * LANES:(j + 1) * LANES], ch, sh) * mul for j in range(n)], axis=1)

    qb_ref[...] = rope_heads(proj(SEG_QB), DSA_HEADS, scale).T.astype(BF16)
    p = proj(SEG_KV)
    kb_ref[...] = _rope128(p[:, :HEAD_DIM], ch, sh).astype(BF16)
    vb_ref[...] = p[:, HEAD_DIM:].astype(BF16)

    lo_half = lax.broadcasted_iota(I32, (tm, LANES), 1) % IDX_DIM < IDX_DIM // 2
    p = proj(SEG_QI)
    qi_ref[...] = jnp.concatenate(
        [_rope64(p[:, j * LANES:(j + 1) * LANES], ci, si, lo_half) * (IDX_DIM ** -0.5)
         for j in range(IDX_HEADS * IDX_DIM // LANES)], axis=1).T.astype(BF16)
    ki_ref[...] = _rope64(proj(SEG_KI), ci, si, lo_half)[:, :IDX_DIM].astype(BF16)
    wi_ref[...] = (proj(SEG_WI) * (IDX_HEADS ** -0.5)).T[:IDX_HEADS, :]

    qc_ref[...] = rope_heads(proj(SEG_QC), DIL_HEADS, scale)
    kc_ref[...] = rope_heads(proj(SEG_KC), DIL_HEADS, 1.0)
    vc_ref[...] = proj(SEG_VC)


def _in_proj(x, g, w, cw, tabs, layer, seq, tm):
    t, d = x.shape
    n = w.shape[2]
    row = lambda width: pl.BlockSpec((tm, width), lambda m: (m, 0))
    col = lambda height: pl.BlockSpec((height, tm), lambda m: (0, m))
    const = lambda shape: pl.BlockSpec((None,) + shape, lambda m: (layer, 0, 0))
    outs = [(CONV_CH, BF16, True), (DSA_HEADS * HEAD_DIM, BF16, False), (HEAD_DIM, BF16, True),
            (HEAD_DIM, BF16, True), (IDX_HEADS * IDX_DIM, BF16, False), (IDX_DIM, BF16, True),
            (IDX_HEADS, F32, False), (DIL_HEADS * HEAD_DIM, F32, True), (DIL_HEADS * HEAD_DIM, F32, True),
            (DIL_HEADS * HEAD_DIM, F32, True)]
    return pl.pallas_call(
        functools.partial(_in_proj_body, seq // tm),
        grid=(t // tm,),
        in_specs=[row(d), const((1, d)),
                  pl.BlockSpec((None, d, n), lambda m: (layer, 0, 0), pipeline_mode=pl.Buffered(1)),
                  const((CONV_WIDTH, CONV_CH)), row(LANES), row(LANES), row(LANES), row(LANES)],
        out_specs=[row(wd) if tok else col(wd) for wd, _, tok in outs],
        out_shape=[jax.ShapeDtypeStruct((t, wd) if tok else (wd, t), dt) for wd, dt, tok in outs],
        scratch_shapes=[pltpu.VMEM((tm + 8, CONV_CH), F32)],
        compiler_params=_params("arbitrary"),
        name="in_proj",
    )(x, g, w, cw, *tabs)


def _dsa_body(topk, ncast, qbt_in_ref, kb_ref, vb_ref, qt_ref, ki_ref, wt_ref, *rest):
    cast_in, (o_ref, *cast_out) = rest[:ncast], rest[ncast:2 * ncast + 1]
    keys_ref, hi_ref, lo_ref, lo2_ref, qbt_ref, vt_ref, acc_ref = rest[2 * ncast + 1:]
    tq = o_ref.shape[1]
    seq = kb_ref.shape[1]

    for src_ref, dst_ref in zip(cast_in, cast_out):
        dst_ref[...] = src_ref[...].astype(BF16)

    nh = DSA_HEADS
    i = pl.program_id(1)
    nch = i + 1

    def chunk(c):
        return pl.ds(pl.multiple_of(c * tq, tq), tq)

    def transpose_bf16(a):
        return a.astype(F32).T.astype(BF16)

    @pl.when(i == 0)
    def _():
        for c2 in range(seq // (2 * tq)):
            vt_ref[c2, :HEAD_DIM, :] = transpose_bf16(vb_ref[0, c2 * 2 * tq:(c2 + 1) * 2 * tq, :])
            vt_ref[c2, HEAD_DIM:, :] = jnp.ones((ONES_ROWS, 2 * tq), BF16)

    for h in range(nh):
        qbt_ref[:, h * tq:(h + 1) * tq] = qbt_in_ref[h * HEAD_DIM:(h + 1) * HEAD_DIM, :]
    wt = wt_ref[...]
    krow = lax.broadcasted_iota(I32, (tq, tq), 0)

    def pair(c2):
        return pl.ds(pl.multiple_of(c2 * (2 * tq), 2 * tq), 2 * tq)

    def score_rows(rows, n, first_key):
        kc = ki_ref[0, rows, :]
        acc = jnp.zeros((n, tq), F32)
        for h in range(IDX_HEADS):
            lg = jnp.dot(kc, qt_ref[h * IDX_DIM:(h + 1) * IDX_DIM, :], preferred_element_type=F32)
            acc = acc + jnp.maximum(lg, 0.0) * wt[h:h + 1, :]
        bits = pltpu.bitcast(acc, I32)
        key = bits ^ ((bits >> 31) & 0x7FFFFFFF)
        causal = lax.broadcasted_iota(I32, (n, tq), 0) + first_key <= lax.broadcasted_iota(I32, (n, tq), 1) + i * tq
        key = jnp.where(causal, key, INT_MIN)
        keys_ref[rows, :] = key
        hi_ref[rows, :] = (key >> 16).astype(I16)
        lo_ref[rows, :] = ((key & 0xFFFF) - HALF16).astype(I16)

    def score_pair(c2, carry):
        score_rows(pair(c2), 2 * tq, c2 * (2 * tq))
        return carry

    lax.fori_loop(0, nch // 2, score_pair, 0)

    npair = (nch + 1) // 2

    @pl.when(nch % 2 == 1)
    def _():
        score_rows(chunk(nch - 1), tq, (nch - 1) * tq)
        keys_ref[chunk(nch), :] = jnp.full((tq, tq), INT_MIN, I32)
        hi_ref[chunk(nch), :] = jnp.full((tq, tq), -HALF16, I16)
        lo_ref[chunk(nch), :] = jnp.full((tq, tq), -HALF16, I16)

    def count(pred):
        def body(c, acc):
            hit = pred(keys_ref[chunk(c), :], krow + c * tq).astype(I32)
            return acc + jnp.sum(hit.reshape(tq // 8, 8, tq), axis=0)
        acc = lax.fori_loop(0, nch, body, jnp.zeros((8, tq), I32))
        return jnp.sum(acc, axis=0, keepdims=True)

    def count16(ref, pred):
        def body(c2, acc):
            hit = jnp.where(pred(ref[pair(c2), :]), jnp.ones((), BF16), jnp.zeros((), BF16))
            parts = [hit[r * 16:(r + 1) * 16] for r in range(2 * tq // 16)]
            while len(parts) > 1:
                parts = [parts[j] + parts[j + 1] for j in range(0, len(parts), 2)]
            return acc + parts[0].astype(F32)
        acc = lax.fori_loop(0, npair, body, jnp.zeros((16, tq), F32))
        return jnp.sum(acc, axis=0, keepdims=True)

    def kth_largest16(ref, k):
        def search_bit(b, t_u):
            cand = t_u | (jnp.int32(1) << (15 - b))
            cand16 = (cand - HALF16).astype(I16)
            return jnp.where(count16(ref, lambda v: v >= cand16) >= k, cand, t_u)
        return lax.fori_loop(0, 16, search_bit, jnp.zeros((1, tq), I32)) - HALF16

    t_hi = kth_largest16(hi_ref, float(topk))
    t_hi16 = t_hi.astype(I16)
    n_gt_hi = count16(hi_ref, lambda v: v > t_hi16)
    need_lo = float(topk) - n_gt_hi

    def bucket(c2, carry):
        lo2_ref[pair(c2), :] = jnp.where(hi_ref[pair(c2), :] == t_hi16, lo_ref[pair(c2), :],
                                         jnp.full((), -HALF16, I16))
        return carry

    lax.fori_loop(0, npair, bucket, 0)
    t_lo = kth_largest16(lo2_ref, need_lo)
    thr = jnp.maximum((t_hi << 16) | (t_lo + HALF16), INT_MIN + 1)
    t_lo16 = t_lo.astype(I16)
    n_ge = jnp.where(t_hi > -HALF16, n_gt_hi + count16(lo2_ref, lambda v: v >= t_lo16), 0.0)

    idx_bits = seq.bit_length() - 1

    @pl.when(jnp.max(n_ge) > topk)
    def _():
        need = topk - count(lambda k, _: k > thr)
        def bit(b, j):
            cand = j | (jnp.int32(1) << (idx_bits - 1 - b))
            below = count(lambda k, idx: (k == thr) & (idx < cand))
            return jnp.where(below < need, cand, j)
        bound = lax.fori_loop(0, idx_bits, bit, jnp.zeros((1, tq), I32))
        bound = jnp.where(n_ge > topk, bound, seq)

        def demote(c, carry):
            k = keys_ref[chunk(c), :]
            keys_ref[chunk(c), :] = jnp.where((k == thr) & (krow + c * tq > bound), thr - 1, k)
            return carry

        lax.fori_loop(0, nch, demote, 0)

    acc_ref[...] = jnp.zeros(acc_ref.shape, F32)

    def attend(c2, m):
        bias = jnp.where(keys_ref[pair(c2), :] >= thr, 0.0, NEG)
        st = jnp.dot(kb_ref[0, pair(c2), :], qbt_ref[...], preferred_element_type=F32)
        st = st + jnp.concatenate([bias] * nh, axis=1)
        m_new = jnp.maximum(m, jnp.max(st, axis=0, keepdims=True))
        p = jnp.exp2(st - m_new).astype(BF16)
        acc_ref[...] = acc_ref[...] * jnp.exp2(m - m_new) + jnp.dot(vt_ref[c2], p, preferred_element_type=F32)
        return m_new

    lax.fori_loop(0, npair, attend, jnp.full((1, nh * tq), NEG, F32))
    out = acc_ref[:HEAD_DIM, :] / acc_ref[HEAD_DIM:HEAD_DIM + 1, :]
    for h in range(nh):
        o_ref[0, :, h * HEAD_DIM:(h + 1) * HEAD_DIM] = out[:, h * tq:(h + 1) * tq].T.astype(BF16)


def _dsa(qbt, kb, vb, qit, ki, wit, tq, to_cast):
    b, s, _ = kb.shape
    topk = min(TOPK_MAX, s // 4)
    assert s % (2 * tq) == 0
    nq = s // tq
    blk = lambda width: pl.BlockSpec((1, tq, width), lambda bb, i: (bb, i, 0))
    full = lambda width: pl.BlockSpec((1, s, width), lambda bb, i: (bb, 0, 0))
    qcol = lambda a: pl.BlockSpec((a.shape[0], tq), lambda bb, i: (0, bb * nq + i))
    casts = [_cast_block_specs(w, layer, b * nq, lambda bb, i: bb * nq + i) for w, layer in to_cast]
    out, *copies = pl.pallas_call(
        functools.partial(_dsa_body, topk, len(casts)),
        grid=(b, nq),
        in_specs=[qcol(qbt), full(HEAD_DIM), full(HEAD_DIM), qcol(qit), full(IDX_DIM), qcol(wit)]
        + [c[0] for c in casts],
        out_specs=[blk(DSA_HEADS * HEAD_DIM)] + [c[1] for c in casts],
        out_shape=[jax.ShapeDtypeStruct((b, s, DSA_HEADS * HEAD_DIM), BF16)] + [c[2] for c in casts],
        scratch_shapes=[pltpu.VMEM((s, tq), I32),
                        pltpu.VMEM((s, tq), I16),
                        pltpu.VMEM((s, tq), I16),
                        pltpu.VMEM((s, tq), I16),
                        pltpu.VMEM((HEAD_DIM, DSA_HEADS * tq), BF16),
                        pltpu.VMEM((s // (2 * tq), HEAD_DIM + ONES_ROWS, 2 * tq), BF16),
                        pltpu.VMEM((HEAD_DIM + ONES_ROWS, DSA_HEADS * tq), F32)],
        compiler_params=_params("parallel", "arbitrary"),
        name="dsa",
    )(qbt, kb, vb, qit, ki, wit, *[w for w, _ in to_cast])
    return out, copies


def _dilated_body(*refs):
    ng = len(DIL_PATTERNS)
    q_refs, k_refs, v_refs = refs[0:ng], refs[ng:2 * ng], refs[2 * ng:3 * ng]
    o_refs = refs[3 * ng:4 * ng]
    acc_ref, m_ref, l_ref = refs[4 * ng:]
    sb_tokens = q_refs[0].shape[1]
    t0 = pl.program_id(2) * sb_tokens
    blk = HEAD_DIM
    UNITS = 4
    rq = lax.broadcasted_iota(I32, (blk, blk), 0)
    ck = lax.broadcasted_iota(I32, (blk, blk), 1)
    bias_cur = jnp.where(ck <= rq, 0.0, NEG)
    bias_prev = jnp.where(ck >= rq, 0.0, NEG)
    nt = (((1,), (1,)), ((), ()))

    for g, (win, dil) in enumerate(DIL_PATTERNS):
        q_ref, k_ref, v_ref = q_refs[g], k_refs[g], v_refs[g]
        per_res = sb_tokens // dil // blk

        def rows(start, dil=dil):
            return pl.ds(start, blk, stride=dil) if dil > 1 else pl.ds(start, blk)

        def units(it, carry, g=g, dil=dil, per_res=per_res, rows=rows,
                  q_ref=q_ref, k_ref=k_ref, v_ref=v_ref):
            q0s, kcs, kps, hps = [], [], [], []
            for n in range(UNITS):
                idx = it * UNITS + n
                q0 = idx // per_res + (idx % per_res) * (blk * dil)
                k_cur = t0 + q0
                has_prev = k_cur >= blk * dil
                q0s.append(q0)
                kcs.append(k_cur)
                hps.append(has_prev)
                kps.append(jnp.where(has_prev, k_cur - blk * dil, k_cur))
            ss = []
            for q0, kc, kp, hp in zip(q0s, kcs, kps, hps):
                k2 = jnp.concatenate([k_ref[0, rows(kp), :], k_ref[0, rows(kc), :]], axis=0).astype(BF16)
                s = lax.dot_general(q_ref[0, rows(q0), :].astype(BF16), k2, nt, preferred_element_type=F32)
                ss.append(s + jnp.concatenate([jnp.where(hp, bias_prev, NEG), bias_cur], axis=1))
            ms = [jnp.max(jnp.maximum(s[:, :blk], s[:, blk:]), axis=-1, keepdims=True) for s in ss]
            ps = [jnp.exp2(s - m).astype(BF16) for s, m in zip(ss, ms)]
            for q0, kc, kp, m, p in zip(q0s, kcs, kps, ms, ps):
                v2 = jnp.concatenate([v_ref[0, rows(kp), :], v_ref[0, rows(kc), :]], axis=0).astype(BF16)
                av = jnp.dot(p, jnp.concatenate([v2, jnp.ones_like(v2)], axis=1), preferred_element_type=F32)
                acc_ref[g, rows(q0), :] = av[:, :HEAD_DIM]
                l_ref[g, rows(q0), :] = av[:, HEAD_DIM:]
                m_ref[g, rows(q0), :] = jnp.broadcast_to(m, (blk, LANES))
            return carry

        lax.fori_loop(0, dil * per_res // UNITS, units, 0)

    step = 256

    def merge(c, carry):
        sl = pl.ds(pl.multiple_of(c * step, step), step)
        ms = [m_ref[g, sl, :] for g in range(ng)]
        m_all = functools.reduce(jnp.maximum, ms)
        ws = [jnp.exp2(m - m_all) for m in ms]
        den = sum(w * l_ref[g, sl, :] for g, w in enumerate(ws))
        for g, w in enumerate(ws):
            o_refs[g][0, sl, :] = (acc_ref[g, sl, :] * (w / den)).astype(BF16)
        return carry

    lax.fori_loop(0, sb_tokens // step, merge, 0)


def _dilated(qc, kc, vc):
    b, s, _ = qc.shape
    ng, hg = len(DIL_PATTERNS), DIL_HEADS_PER_GROUP
    sb_tokens = HEAD_DIM * max(dil for _, dil in DIL_PATTERNS)
    assert s % sb_tokens == 0 and all(win == HEAD_DIM * dil for win, dil in DIL_PATTERNS)
    head = lambda g: (lambda bb, j, sb: (bb, sb, g * hg + j))
    head_full = lambda g: (lambda bb, j, sb: (bb, 0, g * hg + j))
    q_specs = [pl.BlockSpec((1, sb_tokens, HEAD_DIM), head(g)) for g in range(ng)]
    kv_specs = [pl.BlockSpec((1, s, HEAD_DIM), head_full(g)) for g in range(ng)]
    out_spec = pl.BlockSpec((1, sb_tokens, HEAD_DIM), lambda bb, j, sb: (bb, sb, j))
    return pl.pallas_call(
        _dilated_body,
        grid=(b, hg, s // sb_tokens),
        in_specs=q_specs + kv_specs + kv_specs,
        out_specs=[out_spec] * ng,
        out_shape=[jax.ShapeDtypeStruct((b, s, hg * HEAD_DIM), BF16)] * ng,
        scratch_shapes=[pltpu.VMEM((ng, sb_tokens, HEAD_DIM), F32)] * 3,
        compiler_params=_params("parallel", "parallel", "arbitrary"),
        name="dilated",
    )(*([qc] * ng + [kc] * ng + [vc] * ng))


def _out_proj_body(*refs):
    *y_refs, w_ref, x_ref, o_ref, wb_ref = refs

    @pl.when(pl.program_id(1) == 0)
    def _():
        wb_ref[...] = w_ref[...].astype(BF16)

    y = jnp.concatenate([y_ref[...] for y_ref in y_refs], axis=1)
    o_ref[...] = x_ref[...] + jnp.dot(y, wb_ref[...], preferred_element_type=F32)


def _out_proj(parts, w, x, layer, tm, tn):
    t, d = x.shape
    assert sum(a.shape[1] for a in parts) == w.shape[1]
    return pl.pallas_call(
        _out_proj_body,
        grid=(d // tn, t // tm),
        in_specs=[pl.BlockSpec((tm, a.shape[1]), lambda n, m: (m, 0)) for a in parts]
        + [pl.BlockSpec((None, w.shape[1], tn), lambda n, m: (layer, 0, n)),
           pl.BlockSpec((tm, tn), lambda n, m: (m, n))],
        out_specs=pl.BlockSpec((tm, tn), lambda n, m: (m, n)),
        out_shape=jax.ShapeDtypeStruct((t, d), F32),
        scratch_shapes=[pltpu.VMEM((w.shape[1], tn), BF16)],
        compiler_params=_params("parallel", "arbitrary"),
        name="out_proj",
    )(*parts, w, x)


def _final_norm_body(x_ref, g_ref, o_ref):
    o_ref[...] = _rms(x_ref[...], g_ref[...])


def _final_norm(x, g, tm):
    t, d = x.shape
    return pl.pallas_call(
        _final_norm_body,
        grid=(t // tm,),
        in_specs=[pl.BlockSpec((tm, d), lambda m: (m, 0)), pl.BlockSpec((1, d), lambda m: (0, 0))],
        out_specs=pl.BlockSpec((tm, d), lambda m: (m, 0)),
        out_shape=jax.ShapeDtypeStruct((t, d), F32),
        compiler_params=_params("parallel"),
        name="final_norm",
    )(x, g)


def _align_w_in_body(x_ref, o_ref):
    o_wi = SEG_QI[1] + IDX_DIM
    o_qc = o_wi + IDX_HEADS
    rows = x_ref.shape[0]
    o_ref[:, :o_wi] = x_ref[:, :o_wi]
    o_ref[:, o_wi:SEG_WI[0]] = jnp.zeros((rows, SEG_WI[0] - o_wi), BF16)
    o_ref[:, SEG_WI[0]:SEG_WI[0] + IDX_HEADS] = x_ref[:, o_wi:o_qc]
    o_ref[:, SEG_WI[0] + IDX_HEADS:SEG_QC[0]] = jnp.zeros((rows, LANES - IDX_HEADS), BF16)
    o_ref[:, SEG_QC[0]:] = x_ref[:, o_qc:o_qc + D_IN_ALIGNED - SEG_QC[0]]


def _align_w_in(w, tr):
    depth, d, n = w.shape
    assert n - (SEG_QI[1] + IDX_DIM + IDX_HEADS) == D_IN_ALIGNED - SEG_QC[0]
    w = jnp.pad(w.astype(BF16), ((0, 0), (0, 0), (0, D_IN_ALIGNED - n)))
    return pl.pallas_call(
        _align_w_in_body,
        grid=(depth, d // tr),
        in_specs=[pl.BlockSpec((None, tr, D_IN_ALIGNED), lambda l, r: (l, r, 0))],
        out_specs=pl.BlockSpec((None, tr, D_IN_ALIGNED), lambda l, r: (l, r, 0)),
        out_shape=jax.ShapeDtypeStruct((depth, d, D_IN_ALIGNED), BF16),
        compiler_params=_params("parallel", "parallel"),
        name="align_w_in",
    )(w)


def _tile(n, want):
    while n % want:
        want //= 2
    return want


def kernel(x, positions, norm_ffn1, ffn1_gate, ffn1_up, ffn1_down, norm_mix, w_in, conv_w, w_out,
           norm_ffn2, ffn2_gate, ffn2_up, ffn2_down, norm_final):
    b, s, d = x.shape
    t = b * s
    depth = w_in.shape[0]
    tm_big = _tile(t, 1024)
    tm_proj = _tile(s, 512)
    tq_dsa = _tile(s, 256)

    def ffn(xf, g, weights, layer):
        wg, wu, wd = weights
        h = _ffn_up(xf, g.reshape(depth, 1, d), wg, wu, layer, tm_big, 512)
        return _ffn_down(h, wd, xf, tm_big, 512)

    ffn1_stacks, ffn2_stacks = (ffn1_gate, ffn1_up, ffn1_down), (ffn2_gate, ffn2_up, ffn2_down)
    pos = positions.astype(F32).reshape(t, 1)
    tabs, w1 = _rope_tables(pos, _tile(t, 256), [(w, 0) for w in ffn1_stacks])
    w_in_al = _align_w_in(w_in, _tile(d, 256))
    xf = x.reshape(t, d)
    for i in range(depth):
        xf = ffn(xf, norm_ffn1, w1, i)
        ya, qb, kb, vb, qi, ki, wi, qc, kc, vc = _in_proj(
            xf, norm_mix.reshape(depth, 1, d), w_in_al, conv_w, tabs, i, s, tm_proj)
        r3 = lambda a: a.reshape(b, s, a.shape[-1])
        to_cast = [(w, i) for w in ffn2_stacks] + ([(w, i + 1) for w in ffn1_stacks] if i + 1 < depth else [])
        yb, copies = _dsa(qb, r3(kb), r3(vb), qi, r3(ki), wi, tq_dsa, to_cast)
        w2, w1 = copies[:3], copies[3:]
        ycs = _dilated(r3(qc), r3(kc), r3(vc))
        parts = [ya, yb.reshape(t, -1)] + [yc.reshape(t, -1) for yc in ycs]
        xf = _out_proj(parts, w_out, xf, i, tm_big, 1024)
        xf = ffn(xf, norm_ffn2, w2, i)
    return _final_norm(xf, norm_final.reshape(1, d), tm_big).reshape(b, s, d)
```

```python
import functools

import jax
import jax.numpy as jnp
from jax import lax
from jax.experimental import pallas as pl
from jax.experimental.pallas import tpu as pltpu

F32 = jnp.float32
BF16 = jnp.bfloat16
I32 = jnp.int32
I16 = jnp.int16
HALF16 = 1 << 15
ONES_ROWS = 16

HEAD_DIM = 128
CONV_CH = 512
CONV_WIDTH = 3
DSA_HEADS = 6
IDX_HEADS = 16
IDX_DIM = 64
TOPK_MAX = 256
DIL_PATTERNS = ((128, 1), (512, 4), (2048, 16))
DIL_HEADS_PER_GROUP = 2
DIL_HEADS = len(DIL_PATTERNS) * DIL_HEADS_PER_GROUP
ROPE_THETA = 10000.0
RMS_EPS = 1e-6

LANES = 128
SUBLANES = 8
VMEM_LIMIT = 56 * 1024 * 1024
NEG = -1e30
INT_MIN = -2 ** 31
LOG2E = 1.4426950408889634

SEG_CONV = (0, 3 * CONV_CH)
SEG_QB = (SEG_CONV[1], SEG_CONV[1] + DSA_HEADS * HEAD_DIM)
SEG_KV = (SEG_QB[1], SEG_QB[1] + 2 * HEAD_DIM)
SEG_QI = (SEG_KV[1], SEG_KV[1] + IDX_HEADS * IDX_DIM)
SEG_KI = (SEG_QI[1], SEG_QI[1] + LANES)
SEG_WI = (SEG_KI[1], SEG_KI[1] + LANES)
SEG_QC = (SEG_WI[1], SEG_WI[1] + DIL_HEADS * HEAD_DIM)
SEG_KC = (SEG_QC[1], SEG_QC[1] + DIL_HEADS * HEAD_DIM)
SEG_VC = (SEG_KC[1], SEG_KC[1] + DIL_HEADS * HEAD_DIM)
D_IN_ALIGNED = SEG_VC[1]


def _params(*sem):
    return pltpu.CompilerParams(dimension_semantics=sem, vmem_limit_bytes=VMEM_LIMIT)


def _rms(x, g):
    ms = jnp.mean(x * x, axis=-1, keepdims=True)
    return x * lax.rsqrt(ms + RMS_EPS) * g


def _ffn_up_body(x_ref, g_ref, wg_ref, wu_ref, h_ref, xn_ref):
    @pl.when(pl.program_id(1) == 0)
    def _():
        xn_ref[...] = _rms(x_ref[...], g_ref[...]).astype(BF16)

    xn = xn_ref[...]
    a = jnp.dot(xn, wg_ref[...], preferred_element_type=F32)
    b = jnp.dot(xn, wu_ref[...], preferred_element_type=F32)
    h_ref[...] = (a * jax.nn.sigmoid(a) * b).astype(BF16)


def _ffn_up(x, g, wg, wu, layer, tm, tn):
    t, d = x.shape
    f = wg.shape[1]
    return pl.pallas_call(
        _ffn_up_body,
        grid=(t // tm, f // tn),
        in_specs=[
            pl.BlockSpec((tm, d), lambda m, n: (m, 0)),
            pl.BlockSpec((None, 1, d), lambda m, n: (layer, 0, 0)),
            pl.BlockSpec((d, tn), lambda m, n: (0, n)),
            pl.BlockSpec((d, tn), lambda m, n: (0, n)),
        ],
        out_specs=pl.BlockSpec((tm, tn), lambda m, n: (m, n)),
        out_shape=jax.ShapeDtypeStruct((t, f), BF16),
        scratch_shapes=[pltpu.VMEM((tm, d), BF16)],
        compiler_params=_params("parallel", "arbitrary"),
        name="ffn_up",
    )(x, g, wg, wu)


def _ffn_down_body(h_ref, w_ref, x_ref, o_ref):
    y = jnp.dot(h_ref[...], w_ref[...], preferred_element_type=F32)
    o_ref[...] = x_ref[...] + 0.5 * y


def _ffn_down(h, wd, x, tm, tn):
    t, f = h.shape
    d = wd.shape[1]
    return pl.pallas_call(
        _ffn_down_body,
        grid=(t // tm, d // tn),
        in_specs=[
            pl.BlockSpec((tm, f), lambda m, n: (m, 0)),
            pl.BlockSpec((f, tn), lambda m, n: (0, n)),
            pl.BlockSpec((tm, tn), lambda m, n: (m, n)),
        ],
        out_specs=pl.BlockSpec((tm, tn), lambda m, n: (m, n)),
        out_shape=jax.ShapeDtypeStruct((t, d), F32),
        compiler_params=_params("parallel", "arbitrary"),
        name="ffn_down",
    )(h, wd, x)


def _cast_block_specs(w, layer, nsteps, step_of):
    _, rows, cols = w.shape
    nblk = nsteps
    while rows % nblk or (rows // nblk) % 16:
        nblk //= 2
    blk = rows // nblk
    return (pl.BlockSpec((None, blk, cols), lambda *g: (layer, step_of(*g) * nblk // nsteps, 0)),
            pl.BlockSpec((blk, cols), lambda *g: (step_of(*g) * nblk // nsteps, 0)),
            jax.ShapeDtypeStruct((rows, cols), BF16))


def _rope_table_body(ncast, pos_ref, inv_ref, *rest):
    cast_in, (ch_ref, sh_ref, ci_ref, si_ref, *cast_out) = rest[:ncast], rest[ncast:]
    for src_ref, dst_ref in zip(cast_in, cast_out):
        dst_ref[...] = src_ref[...].astype(BF16)

    ang = pos_ref[...] * inv_ref[...]
    cos, sin = jnp.cos(ang), jnp.sin(ang)
    hh, hi = HEAD_DIM // 2, IDX_DIM // 2
    ch_ref[...] = jnp.concatenate([cos[:, :hh]] * 2, axis=1)
    sh_ref[...] = jnp.concatenate([-sin[:, :hh], sin[:, :hh]], axis=1)
    ci_ref[...] = jnp.concatenate([cos[:, hh:hh + hi]] * (LANES // hi), axis=1)
    si_ref[...] = jnp.concatenate([-sin[:, hh:hh + hi], sin[:, hh:hh + hi]] * (LANES // IDX_DIM), axis=1)


def _rope_tables(pos, tm, to_cast):
    t = pos.shape[0]
    inv = lambda dim: 1.0 / (ROPE_THETA ** (jnp.arange(0, dim, 2, dtype=F32) / dim))
    inv_l = jnp.concatenate([inv(HEAD_DIM), inv(IDX_DIM), jnp.zeros((LANES - (HEAD_DIM + IDX_DIM) // 2,), F32)])
    tab = pl.BlockSpec((tm, LANES), lambda m: (m, 0))
    casts = [_cast_block_specs(w, layer, t // tm, lambda m: m) for w, layer in to_cast]
    *tabs, = pl.pallas_call(
        functools.partial(_rope_table_body, len(casts)),
        grid=(t // tm,),
        in_specs=[pl.BlockSpec((tm, 1), lambda m: (m, 0)), pl.BlockSpec((1, LANES), lambda m: (0, 0))]
        + [c[0] for c in casts],
        out_specs=[tab] * 4 + [c[1] for c in casts],
        out_shape=[jax.ShapeDtypeStruct((t, LANES), F32)] * 4 + [c[2] for c in casts],
        compiler_params=_params("parallel"),
        name="rope_tables",
    )(pos, inv_l.reshape(1, LANES), *[w for w, _ in to_cast])
    return tabs[:4], tabs[4:]


def _rope128(x, cos, sin):
    return x * cos + pltpu.roll(x, HEAD_DIM // 2, 1) * sin


def _rope64(x, cos, sin, lo_half):
    partner = jnp.where(lo_half, pltpu.roll(x, LANES - IDX_DIM // 2, 1), pltpu.roll(x, IDX_DIM // 2, 1))
    return x * cos + partner * sin


def _in_proj_body(seq_tiles, x_ref, g_ref, w_ref, cw_ref, ch_ref, sh_ref, ci_ref, si_ref,
                  ya_ref, qb_ref, kb_ref, vb_ref, qi_ref, ki_ref, wi_ref, qc_ref, kc_ref, vc_ref,
                  u_ref):
    tm = x_ref.shape[0]
    xn = _rms(x_ref[...], g_ref[...]).astype(BF16)

    def proj(seg):
        return jnp.dot(xn, w_ref[:, seg[0]:seg[1]], preferred_element_type=F32)

    ch, sh, ci, si = ch_ref[...], sh_ref[...], ci_ref[...], si_ref[...]
    scale = HEAD_DIM ** -0.5 * LOG2E

    def rope_heads(p, n, mul):
        return jnp.concatenate(
            [_rope128(p[:, j * LANES:(j + 1) * LANES], ch, sh) * mul for j in range(n)], axis=1)

    qb_ref[...] = rope_heads(proj(SEG_QB), DSA_HEADS, scale).T.astype(BF16)
    p = proj(SEG_KV)
    kb_ref[...] = _rope128(p[:, :HEAD_DIM], ch, sh).astype(BF16)
    vb_ref[...] = p[:, HEAD_DIM:].astype(BF16)

    lo_half = lax.broadcasted_iota(I32, (tm, LANES), 1) % IDX_DIM < IDX_DIM // 2
    p = proj(SEG_QI)
    qi_ref[...] = jnp.concatenate(
        [_rope64(p[:, j * LANES:(j + 1) * LANES], ci, si, lo_half) * (IDX_DIM ** -0.5)
         for j in range(IDX_HEADS * IDX_DIM // LANES)], axis=1).T.astype(BF16)
    ki_ref[...] = _rope64(proj(SEG_KI), ci, si, lo_half)[:, :IDX_DIM].astype(BF16)
    wi_ref[...] = (proj(SEG_WI) * (IDX_HEADS ** -0.5)).T[:IDX_HEADS, :]

    qc_ref[...] = rope_heads(proj(SEG_QC), DIL_HEADS, scale)
    kc_ref[...] = rope_heads(proj(SEG_KC), DIL_HEADS, 1.0)

    h, gate_b, gate_c = (proj((SEG_CONV[0] + j * CONV_CH, SEG_CONV[0] + (j + 1) * CONV_CH)) for j in range(3))
    u = gate_c * h

    carry = jnp.where(pl.program_id(0) % seq_tiles == 0, 0.0, u_ref[...])
    ext = jnp.concatenate([carry, u], axis=0)
    cw = cw_ref[...]
    y = (cw[2:3, :] * u + cw[1:2, :] * pltpu.roll(ext, 1, 0)[SUBLANES:, :]
         + cw[0:1, :] * pltpu.roll(ext, 2, 0)[SUBLANES:, :])
    u_ref[...] = u[tm - SUBLANES:, :]
    ya_ref[...] = (gate_b * y).astype(BF16)

    vc_ref[...] = proj(SEG_VC)


def _in_proj(x, g, w, cw, tabs, layer, seq, tm):
    t, d = x.shape
    n = w.shape[2]
    row = lambda width: pl.BlockSpec((tm, width), lambda m: (m, 0))
    col = lambda height: pl.BlockSpec((height, tm), lambda m: (0, m))
    const = lambda shape: pl.BlockSpec((None,) + shape, lambda m: (layer, 0, 0))
    outs = [(CONV_CH, BF16, True), (DSA_HEADS * HEAD_DIM, BF16, False), (HEAD_DIM, BF16, True),
            (HEAD_DIM, BF16, True), (IDX_HEADS * IDX_DIM, BF16, False), (IDX_DIM, BF16, True),
            (IDX_HEADS, F32, False), (DIL_HEADS * HEAD_DIM, F32, True), (DIL_HEADS * HEAD_DIM, F32, True),
            (DIL_HEADS * HEAD_DIM, F32, True)]
    return pl.pallas_call(
        functools.partial(_in_proj_body, seq // tm),
        grid=(t // tm,),
        in_specs=[row(d), const((1, d)),
                  pl.BlockSpec((None, d, n), lambda m: (layer, 0, 0), pipeline_mode=pl.Buffered(1)),
                  const((CONV_WIDTH, CONV_CH)), row(LANES), row(LANES), row(LANES), row(LANES)],
        out_specs=[row(wd) if tok else col(wd) for wd, _, tok in outs],
        out_shape=[jax.ShapeDtypeStruct((t, wd) if tok else (wd, t), dt) for wd, dt, tok in outs],
        scratch_shapes=[pltpu.VMEM((SUBLANES, CONV_CH), F32)],
        compiler_params=_params("arbitrary"),
        name="in_proj",
    )(x, g, w, cw, *tabs)


def _dsa_body(topk, ncast, qbt_in_ref, kb_ref, vb_ref, qt_ref, ki_ref, wt_ref, *rest):
    cast_in, (o_ref, *cast_out) = rest[:ncast], rest[ncast:2 * ncast + 1]
    keys_ref, hi_ref, lo_ref, lo2_ref, qbt_ref, vt_ref, acc_ref = rest[2 * ncast + 1:]
    tq = o_ref.shape[1]
    seq = kb_ref.shape[1]

    for src_ref, dst_ref in zip(cast_in, cast_out):
        dst_ref[...] = src_ref[...].astype(BF16)

    nh = DSA_HEADS
    i = pl.program_id(1)
    nch = i + 1

    def chunk(c):
        return pl.ds(pl.multiple_of(c * tq, tq), tq)

    def transpose_bf16(a):
        return a.astype(F32).T.astype(BF16)

    @pl.when(i == 0)
    def _():
        for c2 in range(seq // (2 * tq)):
            vt_ref[c2, :HEAD_DIM, :] = transpose_bf16(vb_ref[0, c2 * 2 * tq:(c2 + 1) * 2 * tq, :])
            vt_ref[c2, HEAD_DIM:, :] = jnp.ones((ONES_ROWS, 2 * tq), BF16)

    for h in range(nh):
        qbt_ref[:, h * tq:(h + 1) * tq] = qbt_in_ref[h * HEAD_DIM:(h + 1) * HEAD_DIM, :]
    wt = wt_ref[...]
    krow = lax.broadcasted_iota(I32, (tq, tq), 0)

    def pair(c2):
        return pl.ds(pl.multiple_of(c2 * (2 * tq), 2 * tq), 2 * tq)

    def score_rows(rows, n, first_key):
        kc = ki_ref[0, rows, :]
        acc = jnp.zeros((n, tq), F32)
        for h in range(IDX_HEADS):
            lg = jnp.dot(kc, qt_ref[h * IDX_DIM:(h + 1) * IDX_DIM, :], preferred_element_type=F32)
            acc = acc + jnp.maximum(lg, 0.0) * wt[h:h + 1, :]
        bits = pltpu.bitcast(acc, I32)
        key = bits ^ ((bits >> 31) & 0x7FFFFFFF)
        causal = lax.broadcasted_iota(I32, (n, tq), 0) + first_key <= lax.broadcasted_iota(I32, (n, tq), 1) + i * tq
        key = jnp.where(causal, key, INT_MIN)
        keys_ref[rows, :] = key
        hi_ref[rows, :] = (key >> 16).astype(I16)
        lo_ref[rows, :] = ((key & 0xFFFF) - HALF16).astype(I16)

    def score_pair(c2, carry):
        score_rows(pair(c2), 2 * tq, c2 * (2 * tq))
        return carry

    lax.fori_loop(0, nch // 2, score_pair, 0)

    npair = (nch + 1) // 2

    @pl.when(nch % 2 == 1)
    def _():
        score_rows(chunk(nch - 1), tq, (nch - 1) * tq)
        keys_ref[chunk(nch), :] = jnp.full((tq, tq), INT_MIN, I32)
        hi_ref[chunk(nch), :] = jnp.full((tq, tq), -HALF16, I16)
        lo_ref[chunk(nch), :] = jnp.full((tq, tq), -HALF16, I16)

    def count(pred):
        def body(c, acc):
            hit = pred(keys_ref[chunk(c), :], krow + c * tq).astype(I32)
            return acc + jnp.sum(hit.reshape(tq // 8, 8, tq), axis=0)
        acc = lax.fori_loop(0, nch, body, jnp.zeros((8, tq), I32))
        return jnp.sum(acc, axis=0, keepdims=True)

    def count16(ref, pred):
        def body(c2, acc):
            hit = jnp.where(pred(ref[pair(c2), :]), jnp.ones((), BF16), jnp.zeros((), BF16))
            parts = [hit[r * 16:(r + 1) * 16] for r in range(2 * tq // 16)]
            while len(parts) > 1:
                parts = [parts[j] + parts[j + 1] for j in range(0, len(parts), 2)]
            return acc + parts[0].astype(F32)
        acc = lax.fori_loop(0, npair, body, jnp.zeros((16, tq), F32))
        return jnp.sum(acc, axis=0, keepdims=True)

    def kth_largest16(ref, k):
        def search_bit(b, t_u):
            cand = t_u | (jnp.int32(1) << (15 - b))
            cand16 = (cand - HALF16).astype(I16)
            return jnp.where(count16(ref, lambda v: v >= cand16) >= k, cand, t_u)
        return lax.fori_loop(0, 16, search_bit, jnp.zeros((1, tq), I32)) - HALF16

    t_hi = kth_largest16(hi_ref, float(topk))
    t_hi16 = t_hi.astype(I16)
    n_gt_hi = count16(hi_ref, lambda v: v > t_hi16)
    need_lo = float(topk) - n_gt_hi

    def bucket(c2, carry):
        lo2_ref[pair(c2), :] = jnp.where(hi_ref[pair(c2), :] == t_hi16, lo_ref[pair(c2), :],
                                         jnp.full((), -HALF16, I16))
        return carry

    lax.fori_loop(0, npair, bucket, 0)
    t_lo = kth_largest16(lo2_ref, need_lo)
    thr = jnp.maximum((t_hi << 16) | (t_lo + HALF16), INT_MIN + 1)
    t_lo16 = t_lo.astype(I16)
    n_ge = jnp.where(t_hi > -HALF16, n_gt_hi + count16(lo2_ref, lambda v: v >= t_lo16), 0.0)

    idx_bits = seq.bit_length() - 1

    @pl.when(jnp.max(n_ge) > topk)
    def _():
        need = topk - count(lambda k, _: k > thr)
        def bit(b, j):
            cand = j | (jnp.int32(1) << (idx_bits - 1 - b))
            below = count(lambda k, idx: (k == thr) & (idx < cand))
            return jnp.where(below < need, cand, j)
        bound = lax.fori_loop(0, idx_bits, bit, jnp.zeros((1, tq), I32))
        bound = jnp.where(n_ge > topk, bound, seq)

        def demote(c, carry):
            k = keys_ref[chunk(c), :]
            keys_ref[chunk(c), :] = jnp.where((k == thr) & (krow + c * tq > bound), thr - 1, k)
            return carry

        lax.fori_loop(0, nch, demote, 0)

    acc_ref[...] = jnp.zeros(acc_ref.shape, F32)

    def attend(c2, m):
        bias = jnp.where(keys_ref[pair(c2), :] >= thr, 0.0, NEG)
        st = jnp.dot(kb_ref[0, pair(c2), :], qbt_ref[...], preferred_element_type=F32)
        st = st + jnp.concatenate([bias] * nh, axis=1)
        m_new = jnp.maximum(m, jnp.max(st, axis=0, keepdims=True))
        p = jnp.exp2(st - m_new).astype(BF16)
        acc_ref[...] = acc_ref[...] * jnp.exp2(m - m_new) + jnp.dot(vt_ref[c2], p, preferred_element_type=F32)
        return m_new

    lax.fori_loop(0, npair, attend, jnp.full((1, nh * tq), NEG, F32))
    out = acc_ref[:HEAD_DIM, :] / acc_ref[HEAD_DIM:HEAD_DIM + 1, :]
    for h in range(nh):
        o_ref[0, :, h * HEAD_DIM:(h + 1) * HEAD_DIM] = out[:, h * tq:(h + 1) * tq].T.astype(BF16)


def _dsa(qbt, kb, vb, qit, ki, wit, tq, to_cast):
    b, s, _ = kb.shape
    topk = min(TOPK_MAX, s // 4)
    assert s % (2 * tq) == 0
    nq = s // tq
    blk = lambda width: pl.BlockSpec((1, tq, width), lambda bb, i: (bb, i, 0))
    full = lambda width: pl.BlockSpec((1, s, width), lambda bb, i: (bb, 0, 0))
    qcol = lambda a: pl.BlockSpec((a.shape[0], tq), lambda bb, i: (0, bb * nq + i))
    casts = [_cast_block_specs(w, layer, b * nq, lambda bb, i: bb * nq + i) for w, layer in to_cast]
    out, *copies = pl.pallas_call(
        functools.partial(_dsa_body, topk, len(casts)),
        grid=(b, nq),
        in_specs=[qcol(qbt), full(HEAD_DIM), full(HEAD_DIM), qcol(qit), full(IDX_DIM), qcol(wit)]
        + [c[0] for c in casts],
        out_specs=[blk(DSA_HEADS * HEAD_DIM)] + [c[1] for c in casts],
        out_shape=[jax.ShapeDtypeStruct((b, s, DSA_HEADS * HEAD_DIM), BF16)] + [c[2] for c in casts],
        scratch_shapes=[pltpu.VMEM((s, tq), I32),
                        pltpu.VMEM((s, tq), I16),
                        pltpu.VMEM((s, tq), I16),
                        pltpu.VMEM((s, tq), I16),
                        pltpu.VMEM((HEAD_DIM, DSA_HEADS * tq), BF16),
                        pltpu.VMEM((s // (2 * tq), HEAD_DIM + ONES_ROWS, 2 * tq), BF16),
                        pltpu.VMEM((HEAD_DIM + ONES_ROWS, DSA_HEADS * tq), F32)],
        compiler_params=_params("parallel", "arbitrary"),
        name="dsa",
    )(qbt, kb, vb, qit, ki, wit, *[w for w, _ in to_cast])
    return out, copies


def _dilated_body(*refs):
    ng = len(DIL_PATTERNS)
    q_refs, k_refs, v_refs = refs[0:ng], refs[ng:2 * ng], refs[2 * ng:3 * ng]
    o_refs = refs[3 * ng:4 * ng]
    acc_ref, m_ref, l_ref = refs[4 * ng:]
    sb_tokens = q_refs[0].shape[1]
    t0 = pl.program_id(2) * sb_tokens
    blk = HEAD_DIM
    UNITS = 4
    rq = lax.broadcasted_iota(I32, (blk, blk), 0)
    ck = lax.broadcasted_iota(I32, (blk, blk), 1)
    bias_cur = jnp.where(ck <= rq, 0.0, NEG)
    bias_prev = jnp.where(ck >= rq, 0.0, NEG)
    nt = (((1,), (1,)), ((), ()))

    for g, (win, dil) in enumerate(DIL_PATTERNS):
        q_ref, k_ref, v_ref = q_refs[g], k_refs[g], v_refs[g]
        per_res = sb_tokens // dil // blk

        def rows(start, dil=dil):
            return pl.ds(start, blk, stride=dil) if dil > 1 else pl.ds(start, blk)

        def units(it, carry, g=g, dil=dil, per_res=per_res, rows=rows,
                  q_ref=q_ref, k_ref=k_ref, v_ref=v_ref):
            q0s, kcs, kps, hps = [], [], [], []
            for n in range(UNITS):
                idx = it * UNITS + n
                q0 = idx // per_res + (idx % per_res) * (blk * dil)
                k_cur = t0 + q0
                has_prev = k_cur >= blk * dil
                q0s.append(q0)
                kcs.append(k_cur)
                hps.append(has_prev)
                kps.append(jnp.where(has_prev, k_cur - blk * dil, k_cur))
            ss = []
            for q0, kc, kp, hp in zip(q0s, kcs, kps, hps):
                k2 = jnp.concatenate([k_ref[0, rows(kp), :], k_ref[0, rows(kc), :]], axis=0).astype(BF16)
                s = lax.dot_general(q_ref[0, rows(q0), :].astype(BF16), k2, nt, preferred_element_type=F32)
                ss.append(s + jnp.concatenate([jnp.where(hp, bias_prev, NEG), bias_cur], axis=1))
            ms = [jnp.max(jnp.maximum(s[:, :blk], s[:, blk:]), axis=-1, keepdims=True) for s in ss]
            ps = [jnp.exp2(s - m).astype(BF16) for s, m in zip(ss, ms)]
            for q0, kc, kp, m, p in zip(q0s, kcs, kps, ms, ps):
                v2 = jnp.concatenate([v_ref[0, rows(kp), :], v_ref[0, rows(kc), :]], axis=0).astype(BF16)
                av = jnp.dot(p, jnp.concatenate([v2, jnp.ones_like(v2)], axis=1), preferred_element_type=F32)
                acc_ref[g, rows(q0), :] = av[:, :HEAD_DIM]
                l_ref[g, rows(q0), :] = av[:, HEAD_DIM:]
                m_ref[g, rows(q0), :] = jnp.broadcast_to(m, (blk, LANES))
            return carry

        lax.fori_loop(0, dil * per_res // UNITS, units, 0)

    step = 256

    def merge(c, carry):
        sl = pl.ds(pl.multiple_of(c * step, step), step)
        ms = [m_ref[g, sl, :] for g in range(ng)]
        m_all = functools.reduce(jnp.maximum, ms)
        ws = [jnp.exp2(m - m_all) for m in ms]
        den = sum(w * l_ref[g, sl, :] for g, w in enumerate(ws))
        for g, w in enumerate(ws):
            o_refs[g][0, sl, :] = (acc_ref[g, sl, :] * (w / den)).astype(BF16)
        return carry

    lax.fori_loop(0, sb_tokens // step, merge, 0)


def _dilated(qc, kc, vc):
    b, s, _ = qc.shape
    ng, hg = len(DIL_PATTERNS), DIL_HEADS_PER_GROUP
    sb_tokens = HEAD_DIM * max(dil for _, dil in DIL_PATTERNS)
    assert s % sb_tokens == 0 and all(win == HEAD_DIM * dil for win, dil in DIL_PATTERNS)
    head = lambda g: (lambda bb, j, sb: (bb, sb, g * hg + j))
    head_full = lambda g: (lambda bb, j, sb: (bb, 0, g * hg + j))
    q_specs = [pl.BlockSpec((1, sb_tokens, HEAD_DIM), head(g)) for g in range(ng)]
    kv_specs = [pl.BlockSpec((1, s, HEAD_DIM), head_full(g)) for g in range(ng)]
    out_spec = pl.BlockSpec((1, sb_tokens, HEAD_DIM), lambda bb, j, sb: (bb, sb, j))
    return pl.pallas_call(
        _dilated_body,
        grid=(b, hg, s // sb_tokens),
        in_specs=q_specs + kv_specs + kv_specs,
        out_specs=[out_spec] * ng,
        out_shape=[jax.ShapeDtypeStruct((b, s, hg * HEAD_DIM), BF16)] * ng,
        scratch_shapes=[pltpu.VMEM((ng, sb_tokens, HEAD_DIM), F32)] * 3,
        compiler_params=_params("parallel", "parallel", "arbitrary"),
        name="dilated",
    )(*([qc] * ng + [kc] * ng + [vc] * ng))


def _out_proj_body(*refs):
    *y_refs, w_ref, x_ref, o_ref, wb_ref = refs

    @pl.when(pl.program_id(1) == 0)
    def _():
        wb_ref[...] = w_ref[...].astype(BF16)

    y = jnp.concatenate([y_ref[...] for y_ref in y_refs], axis=1)
    o_ref[...] = x_ref[...] + jnp.dot(y, wb_ref[...], preferred_element_type=F32)


def _out_proj(parts, w, x, layer, tm, tn):
    t, d = x.shape
    assert sum(a.shape[1] for a in parts) == w.shape[1]
    return pl.pallas_call(
        _out_proj_body,
        grid=(d // tn, t // tm),
        in_specs=[pl.BlockSpec((tm, a.shape[1]), lambda n, m: (m, 0)) for a in parts]
        + [pl.BlockSpec((None, w.shape[1], tn), lambda n, m: (layer, 0, n)),
           pl.BlockSpec((tm, tn), lambda n, m: (m, n))],
        out_specs=pl.BlockSpec((tm, tn), lambda n, m: (m, n)),
        out_shape=jax.ShapeDtypeStruct((t, d), F32),
        scratch_shapes=[pltpu.VMEM((w.shape[1], tn), BF16)],
        compiler_params=_params("parallel", "arbitrary"),
        name="out_proj",
    )(*parts, w, x)


def _final_norm_body(x_ref, g_ref, o_ref):
    o_ref[...] = _rms(x_ref[...], g_ref[...])


def _final_norm(x, g, tm):
    t, d = x.shape
    return pl.pallas_call(
        _final_norm_body,
        grid=(t // tm,),
        in_specs=[pl.BlockSpec((tm, d), lambda m: (m, 0)), pl.BlockSpec((1, d), lambda m: (0, 0))],
        out_specs=pl.BlockSpec((tm, d), lambda m: (m, 0)),
        out_shape=jax.ShapeDtypeStruct((t, d), F32),
        compiler_params=_params("parallel"),
        name="final_norm",
    )(x, g)


def _align_w_in_body(x_ref, o_ref):
    o_wi = SEG_QI[1] + IDX_DIM
    o_qc = o_wi + IDX_HEADS
    rows = x_ref.shape[0]
    o_ref[:, :o_wi] = x_ref[:, :o_wi]
    o_ref[:, o_wi:SEG_WI[0]] = jnp.zeros((rows, SEG_WI[0] - o_wi), BF16)
    o_ref[:, SEG_WI[0]:SEG_WI[0] + IDX_HEADS] = x_ref[:, o_wi:o_qc]
    o_ref[:, SEG_WI[0] + IDX_HEADS:SEG_QC[0]] = jnp.zeros((rows, LANES - IDX_HEADS), BF16)
    o_ref[:, SEG_QC[0]:] = x_ref[:, o_qc:o_qc + D_IN_ALIGNED - SEG_QC[0]]


def _align_w_in(w, tr):
    depth, d, n = w.shape
    assert n - (SEG_QI[1] + IDX_DIM + IDX_HEADS) == D_IN_ALIGNED - SEG_QC[0]
    w = jnp.pad(w.astype(BF16), ((0, 0), (0, 0), (0, D_IN_ALIGNED - n)))
    return pl.pallas_call(
        _align_w_in_body,
        grid=(depth, d // tr),
        in_specs=[pl.BlockSpec((None, tr, D_IN_ALIGNED), lambda l, r: (l, r, 0))],
        out_specs=pl.BlockSpec((None, tr, D_IN_ALIGNED), lambda l, r: (l, r, 0)),
        out_shape=jax.ShapeDtypeStruct((depth, d, D_IN_ALIGNED), BF16),
        compiler_params=_params("parallel", "parallel"),
        name="align_w_in",
    )(w)


def _tile(n, want):
    while n % want:
        want //= 2
    return want


def kernel(x, positions, norm_ffn1, ffn1_gate, ffn1_up, ffn1_down, norm_mix, w_in, conv_w, w_out,
           norm_ffn2, ffn2_gate, ffn2_up, ffn2_down, norm_final):
    b, s, d = x.shape
    t = b * s
    depth = w_in.shape[0]
    tm_big = _tile(t, 1024)
    tm_proj = _tile(s, 512)
    tq_dsa = _tile(s, 256)

    def ffn(xf, g, weights, layer):
        wg, wu, wd = weights
        h = _ffn_up(xf, g.reshape(depth, 1, d), wg, wu, layer, tm_big, 512)
        return _ffn_down(h, wd, xf, tm_big, 512)

    ffn1_stacks, ffn2_stacks = (ffn1_gate, ffn1_up, ffn1_down), (ffn2_gate, ffn2_up, ffn2_down)
    pos = positions.astype(F32).reshape(t, 1)
    tabs, w1 = _rope_tables(pos, _tile(t, 256), [(w, 0) for w in ffn1_stacks])
    w_in_al = _align_w_in(w_in, _tile(d, 256))
    xf = x.reshape(t, d)
    for i in range(depth):
        xf = ffn(xf, norm_ffn1, w1, i)
        ya, qb, kb, vb, qi, ki, wi, qc, kc, vc = _in_proj(
            xf, norm_mix.reshape(depth, 1, d), w_in_al, conv_w, tabs, i, s, tm_proj)
        r3 = lambda a: a.reshape(b, s, a.shape[-1])
        to_cast = [(w, i) for w in ffn2_stacks] + ([(w, i + 1) for w in ffn1_stacks] if i + 1 < depth else [])
        yb, copies = _dsa(qb, r3(kb), r3(vb), qi, r3(ki), wi, tq_dsa, to_cast)
        w2, w1 = copies[:3], copies[3:]
        ycs = _dilated(r3(qc), r3(kc), r3(vc))
        parts = [ya, yb.reshape(t, -1)] + [yc.reshape(t, -1) for yc in ycs]
        xf = _out_proj(parts, w_out, xf, i, tm_big, 1024)
        xf = ffn(xf, norm_ffn2, w2, i)
    return _final_norm(xf, norm_final.reshape(1, d), tm_big).reshape(b, s, d)
```

```python
import functools

import jax
import jax.numpy as jnp
from jax import lax
from jax.experimental import pallas as pl
from jax.experimental.pallas import tpu as pltpu

F32 = jnp.float32
BF16 = jnp.bfloat16
I32 = jnp.int32
I16 = jnp.int16
HALF16 = 1 << 15
ONES_ROWS = 16

HEAD_DIM = 128
CONV_CH = 512
CONV_WIDTH = 3
DSA_HEADS = 6
IDX_HEADS = 16
IDX_DIM = 64
TOPK_MAX = 256
DIL_PATTERNS = ((128, 1), (512, 4), (2048, 16))
DIL_HEADS_PER_GROUP = 2
DIL_HEADS = len(DIL_PATTERNS) * DIL_HEADS_PER_GROUP
ROPE_THETA = 10000.0
RMS_EPS = 1e-6

LANES = 128
SUBLANES = 8
VMEM_LIMIT = 56 * 1024 * 1024
NEG = -1e30
INT_MIN = -2 ** 31
LOG2E = 1.4426950408889634

SEG_CONV = (0, 3 * CONV_CH)
SEG_QB = (SEG_CONV[1], SEG_CONV[1] + DSA_HEADS * HEAD_DIM)
SEG_KV = (SEG_QB[1], SEG_QB[1] + 2 * HEAD_DIM)
SEG_QI = (SEG_KV[1], SEG_KV[1] + IDX_HEADS * IDX_DIM)
SEG_KI = (SEG_QI[1], SEG_QI[1] + LANES)
SEG_WI = (SEG_KI[1], SEG_KI[1] + LANES)
SEG_QC = (SEG_WI[1], SEG_WI[1] + DIL_HEADS * HEAD_DIM)
SEG_KC = (SEG_QC[1], SEG_QC[1] + DIL_HEADS * HEAD_DIM)
SEG_VC = (SEG_KC[1], SEG_KC[1] + DIL_HEADS * HEAD_DIM)
D_IN_ALIGNED = SEG_VC[1]


def _params(*sem):
    return pltpu.CompilerParams(dimension_semantics=sem, vmem_limit_bytes=VMEM_LIMIT)


def _rms(x, g):
    ms = jnp.mean(x * x, axis=-1, keepdims=True)
    return x * lax.rsqrt(ms + RMS_EPS) * g


def _ffn_up_body(x_ref, g_ref, wg_ref, wu_ref, h_ref, xn_ref):
    @pl.when(pl.program_id(1) == 0)
    def _():
        xn_ref[...] = _rms(x_ref[...], g_ref[...]).astype(BF16)

    xn = xn_ref[...]
    a = jnp.dot(xn, wg_ref[...], preferred_element_type=F32)
    b = jnp.dot(xn, wu_ref[...], preferred_element_type=F32)
    h_ref[...] = (a * jax.nn.sigmoid(a) * b).astype(BF16)


def _ffn_up(x, g, wg, wu, layer, tm, tn):
    t, d = x.shape
    f = wg.shape[1]
    return pl.pallas_call(
        _ffn_up_body,
        grid=(t // tm, f // tn),
        in_specs=[
            pl.BlockSpec((tm, d), lambda m, n: (m, 0)),
            pl.BlockSpec((None, 1, d), lambda m, n: (layer, 0, 0)),
            pl.BlockSpec((d, tn), lambda m, n: (0, n)),
            pl.BlockSpec((d, tn), lambda m, n: (0, n)),
        ],
        out_specs=pl.BlockSpec((tm, tn), lambda m, n: (m, n)),
        out_shape=jax.ShapeDtypeStruct((t, f), BF16),
        scratch_shapes=[pltpu.VMEM((tm, d), BF16)],
        compiler_params=_params("parallel", "arbitrary"),
        name="ffn_up",
    )(x, g, wg, wu)


def _ffn_down_body(h_ref, w_ref, x_ref, o_ref):
    y = jnp.dot(h_ref[...], w_ref[...], preferred_element_type=F32)
    o_ref[...] = x_ref[...] + 0.5 * y


def _ffn_down(h, wd, x, tm, tn):
    t, f = h.shape
    d = wd.shape[1]
    return pl.pallas_call(
        _ffn_down_body,
        grid=(t // tm, d // tn),
        in_specs=[
            pl.BlockSpec((tm, f), lambda m, n: (m, 0)),
            pl.BlockSpec((f, tn), lambda m, n: (0, n)),
            pl.BlockSpec((tm, tn), lambda m, n: (m, n)),
        ],
        out_specs=pl.BlockSpec((tm, tn), lambda m, n: (m, n)),
        out_shape=jax.ShapeDtypeStruct((t, d), F32),
        compiler_params=_params("parallel", "arbitrary"),
        name="ffn_down",
    )(h, wd, x)


def _cast_block_specs(w, layer, nsteps, step_of):
    _, rows, cols = w.shape
    nblk = nsteps
    while rows % nblk or (rows // nblk) % 16:
        nblk //= 2
    blk = rows // nblk
    return (pl.BlockSpec((None, blk, cols), lambda *g: (layer, step_of(*g) * nblk // nsteps, 0)),
            pl.BlockSpec((blk, cols), lambda *g: (step_of(*g) * nblk // nsteps, 0)),
            jax.ShapeDtypeStruct((rows, cols), BF16))


def _rope_table_body(ncast, pos_ref, inv_ref, *rest):
    cast_in, (ch_ref, sh_ref, ci_ref, si_ref, *cast_out) = rest[:ncast], rest[ncast:]
    for src_ref, dst_ref in zip(cast_in, cast_out):
        dst_ref[...] = src_ref[...].astype(BF16)

    ang = pos_ref[...] * inv_ref[...]
    cos, sin = jnp.cos(ang), jnp.sin(ang)
    hh, hi = HEAD_DIM // 2, IDX_DIM // 2
    ch_ref[...] = jnp.concatenate([cos[:, :hh]] * 2, axis=1)
    sh_ref[...] = jnp.concatenate([-sin[:, :hh], sin[:, :hh]], axis=1)
    ci_ref[...] = jnp.concatenate([cos[:, hh:hh + hi]] * (LANES // hi), axis=1)
    si_ref[...] = jnp.concatenate([-sin[:, hh:hh + hi], sin[:, hh:hh + hi]] * (LANES // IDX_DIM), axis=1)


def _rope_tables(pos, tm, to_cast):
    t = pos.shape[0]
    inv = lambda dim: 1.0 / (ROPE_THETA ** (jnp.arange(0, dim, 2, dtype=F32) / dim))
    inv_l = jnp.concatenate([inv(HEAD_DIM), inv(IDX_DIM), jnp.zeros((LANES - (HEAD_DIM + IDX_DIM) // 2,), F32)])
    tab = pl.BlockSpec((tm, LANES), lambda m: (m, 0))
    casts = [_cast_block_specs(w, layer, t // tm, lambda m: m) for w, layer in to_cast]
    *tabs, = pl.pallas_call(
        functools.partial(_rope_table_body, len(casts)),
        grid=(t // tm,),
        in_specs=[pl.BlockSpec((tm, 1), lambda m: (m, 0)), pl.BlockSpec((1, LANES), lambda m: (0, 0))]
        + [c[0] for c in casts],
        out_specs=[tab] * 4 + [c[1] for c in casts],
        out_shape=[jax.ShapeDtypeStruct((t, LANES), F32)] * 4 + [c[2] for c in casts],
        compiler_params=_params("parallel"),
        name="rope_tables",
    )(pos, inv_l.reshape(1, LANES), *[w for w, _ in to_cast])
    return tabs[:4], tabs[4:]


def _rope128(x, cos, sin):
    return x * cos + pltpu.roll(x, HEAD_DIM // 2, 1) * sin


def _rope64(x, cos, sin, lo_half):
    partner = jnp.where(lo_half, pltpu.roll(x, LANES - IDX_DIM // 2, 1), pltpu.roll(x, IDX_DIM // 2, 1))
    return x * cos + partner * sin


def _in_proj_body(seq_tiles, x_ref, g_ref, w_ref, cw_ref, ch_ref, sh_ref, ci_ref, si_ref,
                  ya_ref, qb_ref, kb_ref, vb_ref, qi_ref, ki_ref, wi_ref, qc_ref, kc_ref, vc_ref,
                  u_ref):
    tm = x_ref.shape[0]
    xn = _rms(x_ref[...], g_ref[...]).astype(BF16)

    def proj(seg):
        return jnp.dot(xn, w_ref[:, seg[0]:seg[1]], preferred_element_type=F32)

    ch, sh, ci, si = ch_ref[...], sh_ref[...], ci_ref[...], si_ref[...]
    scale = HEAD_DIM ** -0.5 * LOG2E

    def rope_heads(p, n, mul):
        return jnp.concatenate(
            [_rope128(p[:, j * LANES:(j + 1) * LANES], ch, sh) * mul for j in range(n)], axis=1)

    qb_ref[...] = rope_heads(proj(SEG_QB), DSA_HEADS, scale).T.astype(BF16)
    p = proj(SEG_KV)
    kb_ref[...] = _rope128(p[:, :HEAD_DIM], ch, sh).astype(BF16)
    vb_ref[...] = p[:, HEAD_DIM:].astype(BF16)

    lo_half = lax.broadcasted_iota(I32, (tm, LANES), 1) % IDX_DIM < IDX_DIM // 2
    p = proj(SEG_QI)
    qi_ref[...] = jnp.concatenate(
        [_rope64(p[:, j * LANES:(j + 1) * LANES], ci, si, lo_half) * (IDX_DIM ** -0.5)
         for j in range(IDX_HEADS * IDX_DIM // LANES)], axis=1).T.astype(BF16)
    ki_ref[...] = _rope64(proj(SEG_KI), ci, si, lo_half)[:, :IDX_DIM].astype(BF16)
    wi_ref[...] = (proj(SEG_WI) * (IDX_HEADS ** -0.5)).T[:IDX_HEADS, :]

    qc_ref[...] = rope_heads(proj(SEG_QC), DIL_HEADS, scale)
    kc_ref[...] = rope_heads(proj(SEG_KC), DIL_HEADS, 1.0)

    h, gate_b, gate_c = (proj((SEG_CONV[0] + j * CONV_CH, SEG_CONV[0] + (j + 1) * CONV_CH)) for j in range(3))
    u = gate_c * h

    carry = jnp.where(pl.program_id(0) % seq_tiles == 0, 0.0, u_ref[...])
    ext = jnp.concatenate([carry, u], axis=0)
    cw = cw_ref[...]
    y = (cw[2:3, :] * u + cw[1:2, :] * pltpu.roll(ext, 1, 0)[SUBLANES:, :]
         + cw[0:1, :] * pltpu.roll(ext, 2, 0)[SUBLANES:, :])
    u_ref[...] = u[tm - SUBLANES:, :]
    ya_ref[...] = (gate_b * y).astype(BF16)

    vc_ref[...] = proj(SEG_VC)


def _in_proj(x, g, w, cw, tabs, layer, seq, tm):
    t, d = x.shape
    n = w.shape[2]
    row = lambda width: pl.BlockSpec((tm, width), lambda m: (m, 0))
    col = lambda height: pl.BlockSpec((height, tm), lambda m: (0, m))
    const = lambda shape: pl.BlockSpec((None,) + shape, lambda m: (layer, 0, 0))
    outs = [(CONV_CH, BF16, True), (DSA_HEADS * HEAD_DIM, BF16, False), (HEAD_DIM, BF16, True),
            (HEAD_DIM, BF16, True), (IDX_HEADS * IDX_DIM, BF16, False), (IDX_DIM, BF16, True),
            (IDX_HEADS, F32, False), (DIL_HEADS * HEAD_DIM, F32, True), (DIL_HEADS * HEAD_DIM, F32, True),
            (DIL_HEADS * HEAD_DIM, F32, True)]
    return pl.pallas_call(
        functools.partial(_in_proj_body, seq // tm),
        grid=(t // tm,),
        in_specs=[row(d), const((1, d)),
                  pl.BlockSpec((None, d, n), lambda m: (layer, 0, 0), pipeline_mode=pl.Buffered(1)),
                  const((CONV_WIDTH, CONV_CH)), row(LANES), row(LANES), row(LANES), row(LANES)],
        out_specs=[row(wd) if tok else col(wd) for wd, _, tok in outs],
        out_shape=[jax.ShapeDtypeStruct((t, wd) if tok else (wd, t), dt) for wd, dt, tok in outs],
        scratch_shapes=[pltpu.VMEM((SUBLANES, CONV_CH), F32)],
        compiler_params=_params("arbitrary"),
        name="in_proj",
    )(x, g, w, cw, *tabs)


def _dsa_body(topk, ncast, qbt_in_ref, kb_ref, vb_ref, qt_ref, ki_ref, wt_ref, *rest):
    cast_in, (o_ref, *cast_out) = rest[:ncast], rest[ncast:2 * ncast + 1]
    keys_ref, hi_ref, lo_ref, lo2_ref, qbt_ref, vt_ref, acc_ref = rest[2 * ncast + 1:]
    tq = o_ref.shape[1]
    seq = kb_ref.shape[1]

    for src_ref, dst_ref in zip(cast_in, cast_out):
        dst_ref[...] = src_ref[...].astype(BF16)

    nh = DSA_HEADS
    i = pl.program_id(1)
    nch = i + 1

    def chunk(c):
        return pl.ds(pl.multiple_of(c * tq, tq), tq)

    def transpose_bf16(a):
        return a.astype(F32).T.astype(BF16)

    @pl.when(i == 0)
    def _():
        for c2 in range(seq // (2 * tq)):
            vt_ref[c2, :HEAD_DIM, :] = transpose_bf16(vb_ref[0, c2 * 2 * tq:(c2 + 1) * 2 * tq, :])
            vt_ref[c2, HEAD_DIM:, :] = jnp.ones((ONES_ROWS, 2 * tq), BF16)

    for h in range(nh):
        qbt_ref[:, h * tq:(h + 1) * tq] = qbt_in_ref[h * HEAD_DIM:(h + 1) * HEAD_DIM, :]
    wt = wt_ref[...]
    krow = lax.broadcasted_iota(I32, (tq, tq), 0)

    def pair(c2):
        return pl.ds(pl.multiple_of(c2 * (2 * tq), 2 * tq), 2 * tq)

    def score_rows(rows, n, first_key):
        kc = ki_ref[0, rows, :]
        acc = jnp.zeros((n, tq), F32)
        for h in range(IDX_HEADS):
            lg = jnp.dot(kc, qt_ref[h * IDX_DIM:(h + 1) * IDX_DIM, :], preferred_element_type=F32)
            acc = acc + jnp.maximum(lg, 0.0) * wt[h:h + 1, :]
        bits = pltpu.bitcast(acc, I32)
        key = bits ^ ((bits >> 31) & 0x7FFFFFFF)
        causal = lax.broadcasted_iota(I32, (n, tq), 0) + first_key <= lax.broadcasted_iota(I32, (n, tq), 1) + i * tq
        key = jnp.where(causal, key, INT_MIN)
        keys_ref[rows, :] = key
        hi_ref[rows, :] = (key >> 16).astype(I16)
        lo_ref[rows, :] = ((key & 0xFFFF) - HALF16).astype(I16)

    def score_quad(c4, carry):
        score_rows(pl.ds(pl.multiple_of(c4 * (4 * tq), 4 * tq), 4 * tq), 4 * tq, c4 * (4 * tq))
        return carry

    lax.fori_loop(0, nch // 4, score_quad, 0)

    @pl.when(nch % 4 >= 2)
    def _():
        score_rows(pair(nch // 4 * 2), 2 * tq, nch // 4 * (4 * tq))

    npair = (nch + 1) // 2

    @pl.when(nch % 2 == 1)
    def _():
        score_rows(chunk(nch - 1), tq, (nch - 1) * tq)
        keys_ref[chunk(nch), :] = jnp.full((tq, tq), INT_MIN, I32)
        hi_ref[chunk(nch), :] = jnp.full((tq, tq), -HALF16, I16)
        lo_ref[chunk(nch), :] = jnp.full((tq, tq), -HALF16, I16)

    def count(pred):
        def body(c, acc):
            hit = pred(keys_ref[chunk(c), :], krow + c * tq).astype(I32)
            return acc + jnp.sum(hit.reshape(tq // 8, 8, tq), axis=0)
        acc = lax.fori_loop(0, nch, body, jnp.zeros((8, tq), I32))
        return jnp.sum(acc, axis=0, keepdims=True)

    def count16(ref, pred):
        def body(c2, acc):
            hit = jnp.where(pred(ref[pair(c2), :]), jnp.ones((), BF16), jnp.zeros((), BF16))
            parts = [hit[r * 16:(r + 1) * 16] for r in range(2 * tq // 16)]
            while len(parts) > 1:
                parts = [parts[j] + parts[j + 1] for j in range(0, len(parts), 2)]
            return acc + parts[0].astype(F32)
        acc = lax.fori_loop(0, npair, body, jnp.zeros((16, tq), F32))
        return jnp.sum(acc, axis=0, keepdims=True)

    def kth_largest16(ref, k):
        def search_bit(b, t_u):
            cand = t_u | (jnp.int32(1) << (15 - b))
            cand16 = (cand - HALF16).astype(I16)
            return jnp.where(count16(ref, lambda v: v >= cand16) >= k, cand, t_u)
        return lax.fori_loop(0, 16, search_bit, jnp.zeros((1, tq), I32)) - HALF16

    t_hi = kth_largest16(hi_ref, float(topk))
    t_hi16 = t_hi.astype(I16)
    n_gt_hi = count16(hi_ref, lambda v: v > t_hi16)
    need_lo = float(topk) - n_gt_hi

    def bucket(c2, carry):
        lo2_ref[pair(c2), :] = jnp.where(hi_ref[pair(c2), :] == t_hi16, lo_ref[pair(c2), :],
                                         jnp.full((), -HALF16, I16))
        return carry

    lax.fori_loop(0, npair, bucket, 0)
    t_lo = kth_largest16(lo2_ref, need_lo)
    thr = jnp.maximum((t_hi << 16) | (t_lo + HALF16), INT_MIN + 1)
    t_lo16 = t_lo.astype(I16)
    n_ge = jnp.where(t_hi > -HALF16, n_gt_hi + count16(lo2_ref, lambda v: v >= t_lo16), 0.0)

    idx_bits = seq.bit_length() - 1

    @pl.when(jnp.max(n_ge) > topk)
    def _():
        need = topk - count(lambda k, _: k > thr)
        def bit(b, j):
            cand = j | (jnp.int32(1) << (idx_bits - 1 - b))
            below = count(lambda k, idx: (k == thr) & (idx < cand))
            return jnp.where(below < need, cand, j)
        bound = lax.fori_loop(0, idx_bits, bit, jnp.zeros((1, tq), I32))
        bound = jnp.where(n_ge > topk, bound, seq)

        def demote(c, carry):
            k = keys_ref[chunk(c), :]
            keys_ref[chunk(c), :] = jnp.where((k == thr) & (krow + c * tq > bound), thr - 1, k)
            return carry

        lax.fori_loop(0, nch, demote, 0)

    acc_ref[...] = jnp.zeros(acc_ref.shape, F32)

    def attend(first_pair, npairs, m):
        n = npairs * 2 * tq
        rows = pl.ds(pl.multiple_of(first_pair * (2 * tq), 2 * tq), n)
        bias = jnp.where(keys_ref[rows, :] >= thr, 0.0, NEG)
        st = jnp.dot(kb_ref[0, rows, :], qbt_ref[...], preferred_element_type=F32)
        st = st + jnp.concatenate([bias] * nh, axis=1)
        m_new = jnp.maximum(m, jnp.max(st, axis=0, keepdims=True))
        p = jnp.exp2(st - m_new).astype(BF16)
        pv = sum(jnp.dot(vt_ref[first_pair + j], p[j * 2 * tq:(j + 1) * 2 * tq], preferred_element_type=F32)
                 for j in range(npairs))
        acc_ref[...] = acc_ref[...] * jnp.exp2(m - m_new) + pv
        return m_new

    m = lax.fori_loop(0, npair // 2, lambda c4, m: attend(2 * c4, 2, m), jnp.full((1, nh * tq), NEG, F32))

    @pl.when(npair % 2 == 1)
    def _():
        attend(npair - 1, 1, m)

    out = acc_ref[:HEAD_DIM, :] / acc_ref[HEAD_DIM:HEAD_DIM + 1, :]
    for h in range(nh):
        o_ref[0, :, h * HEAD_DIM:(h + 1) * HEAD_DIM] = out[:, h * tq:(h + 1) * tq].T.astype(BF16)


def _dsa(qbt, kb, vb, qit, ki, wit, tq, to_cast):
    b, s, _ = kb.shape
    topk = min(TOPK_MAX, s // 4)
    assert s % (2 * tq) == 0
    nq = s // tq
    blk = lambda width: pl.BlockSpec((1, tq, width), lambda bb, i: (bb, i, 0))
    full = lambda width: pl.BlockSpec((1, s, width), lambda bb, i: (bb, 0, 0))
    qcol = lambda a: pl.BlockSpec((a.shape[0], tq), lambda bb, i: (0, bb * nq + i))
    casts = [_cast_block_specs(w, layer, b * nq, lambda bb, i: bb * nq + i) for w, layer in to_cast]
    out, *copies = pl.pallas_call(
        functools.partial(_dsa_body, topk, len(casts)),
        grid=(b, nq),
        in_specs=[qcol(qbt), full(HEAD_DIM), full(HEAD_DIM), qcol(qit), full(IDX_DIM), qcol(wit)]
        + [c[0] for c in casts],
        out_specs=[blk(DSA_HEADS * HEAD_DIM)] + [c[1] for c in casts],
        out_shape=[jax.ShapeDtypeStruct((b, s, DSA_HEADS * HEAD_DIM), BF16)] + [c[2] for c in casts],
        scratch_shapes=[pltpu.VMEM((s, tq), I32),
                        pltpu.VMEM((s, tq), I16),
                        pltpu.VMEM((s, tq), I16),
                        pltpu.VMEM((s, tq), I16),
                        pltpu.VMEM((HEAD_DIM, DSA_HEADS * tq), BF16),
                        pltpu.VMEM((s // (2 * tq), HEAD_DIM + ONES_ROWS, 2 * tq), BF16),
                        pltpu.VMEM((HEAD_DIM + ONES_ROWS, DSA_HEADS * tq), F32)],
        compiler_params=_params("parallel", "arbitrary"),
        name="dsa",
    )(qbt, kb, vb, qit, ki, wit, *[w for w, _ in to_cast])
    return out, copies


def _dilated_body(*refs):
    ng = len(DIL_PATTERNS)
    q_refs, k_refs, v_refs = refs[0:ng], refs[ng:2 * ng], refs[2 * ng:3 * ng]
    o_refs = refs[3 * ng:4 * ng]
    acc_ref, m_ref, l_ref = refs[4 * ng:]
    sb_tokens = q_refs[0].shape[1]
    t0 = pl.program_id(2) * sb_tokens
    blk = HEAD_DIM
    UNITS = 4
    rq = lax.broadcasted_iota(I32, (blk, blk), 0)
    ck = lax.broadcasted_iota(I32, (blk, blk), 1)
    bias_cur = jnp.where(ck <= rq, 0.0, NEG)
    bias_prev = jnp.where(ck >= rq, 0.0, NEG)
    nt = (((1,), (1,)), ((), ()))

    for g, (win, dil) in enumerate(DIL_PATTERNS):
        q_ref, k_ref, v_ref = q_refs[g], k_refs[g], v_refs[g]
        per_res = sb_tokens // dil // blk

        def rows(start, dil=dil):
            return pl.ds(start, blk, stride=dil) if dil > 1 else pl.ds(start, blk)

        def units(it, carry, g=g, dil=dil, per_res=per_res, rows=rows,
                  q_ref=q_ref, k_ref=k_ref, v_ref=v_ref):
            q0s, kcs, kps, hps = [], [], [], []
            for n in range(UNITS):
                idx = it * UNITS + n
                q0 = idx // per_res + (idx % per_res) * (blk * dil)
                k_cur = t0 + q0
                has_prev = k_cur >= blk * dil
                q0s.append(q0)
                kcs.append(k_cur)
                hps.append(has_prev)
                kps.append(jnp.where(has_prev, k_cur - blk * dil, k_cur))
            ss = []
            for q0, kc, kp, hp in zip(q0s, kcs, kps, hps):
                k2 = jnp.concatenate([k_ref[0, rows(kp), :], k_ref[0, rows(kc), :]], axis=0).astype(BF16)
                s = lax.dot_general(q_ref[0, rows(q0), :].astype(BF16), k2, nt, preferred_element_type=F32)
                ss.append(s + jnp.concatenate([jnp.where(hp, bias_prev, NEG), bias_cur], axis=1))
            ms = [jnp.max(jnp.maximum(s[:, :blk], s[:, blk:]), axis=-1, keepdims=True) for s in ss]
            ps = [jnp.exp2(s - m).astype(BF16) for s, m in zip(ss, ms)]
            for q0, kc, kp, m, p in zip(q0s, kcs, kps, ms, ps):
                v2 = jnp.concatenate([v_ref[0, rows(kp), :], v_ref[0, rows(kc), :]], axis=0).astype(BF16)
                av = jnp.dot(p, jnp.concatenate([v2, jnp.ones_like(v2)], axis=1), preferred_element_type=F32)
                acc_ref[g, rows(q0), :] = av[:, :HEAD_DIM]
                l_ref[g, rows(q0), :] = av[:, HEAD_DIM:]
                m_ref[g, rows(q0), :] = jnp.broadcast_to(m, (blk, LANES))
            return carry

        lax.fori_loop(0, dil * per_res // UNITS, units, 0)

    step = 256

    def merge(c, carry):
        sl = pl.ds(pl.multiple_of(c * step, step), step)
        ms = [m_ref[g, sl, :] for g in range(ng)]
        m_all = functools.reduce(jnp.maximum, ms)
        ws = [jnp.exp2(m - m_all) for m in ms]
        den = sum(w * l_ref[g, sl, :] for g, w in enumerate(ws))
        for g, w in enumerate(ws):
            o_refs[g][0, sl, :] = (acc_ref[g, sl, :] * (w / den)).astype(BF16)
        return carry

    lax.fori_loop(0, sb_tokens // step, merge, 0)


def _dilated(qc, kc, vc):
    b, s, _ = qc.shape
    ng, hg = len(DIL_PATTERNS), DIL_HEADS_PER_GROUP
    sb_tokens = HEAD_DIM * max(dil for _, dil in DIL_PATTERNS)
    assert s % sb_tokens == 0 and all(win == HEAD_DIM * dil for win, dil in DIL_PATTERNS)
    head = lambda g: (lambda bb, j, sb: (bb, sb, g * hg + j))
    head_full = lambda g: (lambda bb, j, sb: (bb, 0, g * hg + j))
    q_specs = [pl.BlockSpec((1, sb_tokens, HEAD_DIM), head(g)) for g in range(ng)]
    kv_specs = [pl.BlockSpec((1, s, HEAD_DIM), head_full(g)) for g in range(ng)]
    out_spec = pl.BlockSpec((1, sb_tokens, HEAD_DIM), lambda bb, j, sb: (bb, sb, j))
    return pl.pallas_call(
        _dilated_body,
        grid=(b, hg, s // sb_tokens),
        in_specs=q_specs + kv_specs + kv_specs,
        out_specs=[out_spec] * ng,
        out_shape=[jax.ShapeDtypeStruct((b, s, hg * HEAD_DIM), BF16)] * ng,
        scratch_shapes=[pltpu.VMEM((ng, sb_tokens, HEAD_DIM), F32)] * 3,
        compiler_params=_params("parallel", "parallel", "arbitrary"),
        name="dilated",
    )(*([qc] * ng + [kc] * ng + [vc] * ng))


def _out_proj_body(*refs):
    *y_refs, w_ref, x_ref, o_ref, wb_ref = refs

    @pl.when(pl.program_id(1) == 0)
    def _():
        wb_ref[...] = w_ref[...].astype(BF16)

    y = jnp.concatenate([y_ref[...] for y_ref in y_refs], axis=1)
    o_ref[...] = x_ref[...] + jnp.dot(y, wb_ref[...], preferred_element_type=F32)


def _out_proj(parts, w, x, layer, tm, tn):
    t, d = x.shape
    assert sum(a.shape[1] for a in parts) == w.shape[1]
    return pl.pallas_call(
        _out_proj_body,
        grid=(d // tn, t // tm),
        in_specs=[pl.BlockSpec((tm, a.shape[1]), lambda n, m: (m, 0)) for a in parts]
        + [pl.BlockSpec((None, w.shape[1], tn), lambda n, m: (layer, 0, n)),
           pl.BlockSpec((tm, tn), lambda n, m: (m, n))],
        out_specs=pl.BlockSpec((tm, tn), lambda n, m: (m, n)),
        out_shape=jax.ShapeDtypeStruct((t, d), F32),
        scratch_shapes=[pltpu.VMEM((w.shape[1], tn), BF16)],
        compiler_params=_params("parallel", "arbitrary"),
        name="out_proj",
    )(*parts, w, x)


def _final_norm_body(x_ref, g_ref, o_ref):
    o_ref[...] = _rms(x_ref[...], g_ref[...])


def _final_norm(x, g, tm):
    t, d = x.shape
    return pl.pallas_call(
        _final_norm_body,
        grid=(t // tm,),
        in_specs=[pl.BlockSpec((tm, d), lambda m: (m, 0)), pl.BlockSpec((1, d), lambda m: (0, 0))],
        out_specs=pl.BlockSpec((tm, d), lambda m: (m, 0)),
        out_shape=jax.ShapeDtypeStruct((t, d), F32),
        compiler_params=_params("parallel"),
        name="final_norm",
    )(x, g)


def _align_w_in_body(x_ref, o_ref):
    o_wi = SEG_QI[1] + IDX_DIM
    o_qc = o_wi + IDX_HEADS
    rows = x_ref.shape[0]
    o_ref[:, :o_wi] = x_ref[:, :o_wi]
    o_ref[:, o_wi:SEG_WI[0]] = jnp.zeros((rows, SEG_WI[0] - o_wi), BF16)
    o_ref[:, SEG_WI[0]:SEG_WI[0] + IDX_HEADS] = x_ref[:, o_wi:o_qc]
    o_ref[:, SEG_WI[0] + IDX_HEADS:SEG_QC[0]] = jnp.zeros((rows, LANES - IDX_HEADS), BF16)
    o_ref[:, SEG_QC[0]:] = x_ref[:, o_qc:o_qc + D_IN_ALIGNED - SEG_QC[0]]


def _align_w_in(w, tr):
    depth, d, n = w.shape
    assert n - (SEG_QI[1] + IDX_DIM + IDX_HEADS) == D_IN_ALIGNED - SEG_QC[0]
    w = jnp.pad(w.astype(BF16), ((0, 0), (0, 0), (0, D_IN_ALIGNED - n)))
    return pl.pallas_call(
        _align_w_in_body,
        grid=(depth, d // tr),
        in_specs=[pl.BlockSpec((None, tr, D_IN_ALIGNED), lambda l, r: (l, r, 0))],
        out_specs=pl.BlockSpec((None, tr, D_IN_ALIGNED), lambda l, r: (l, r, 0)),
        out_shape=jax.ShapeDtypeStruct((depth, d, D_IN_ALIGNED), BF16),
        compiler_params=_params("parallel", "parallel"),
        name="align_w_in",
    )(w)


def _tile(n, want):
    while n % want:
        want //= 2
    return want


def kernel(x, positions, norm_ffn1, ffn1_gate, ffn1_up, ffn1_down, norm_mix, w_in, conv_w, w_out,
           norm_ffn2, ffn2_gate, ffn2_up, ffn2_down, norm_final):
    b, s, d = x.shape
    t = b * s
    depth = w_in.shape[0]
    tm_big = _tile(t, 1024)
    tm_proj = _tile(s, 512)
    tq_dsa = _tile(s, 256)

    def ffn(xf, g, weights, layer):
        wg, wu, wd = weights
        h = _ffn_up(xf, g.reshape(depth, 1, d), wg, wu, layer, tm_big, 512)
        return _ffn_down(h, wd, xf, tm_big, 512)

    ffn1_stacks, ffn2_stacks = (ffn1_gate, ffn1_up, ffn1_down), (ffn2_gate, ffn2_up, ffn2_down)
    pos = positions.astype(F32).reshape(t, 1)
    tabs, w1 = _rope_tables(pos, _tile(t, 256), [(w, 0) for w in ffn1_stacks])
    w_in_al = _align_w_in(w_in, _tile(d, 256))
    xf = x.reshape(t, d)
    for i in range(depth):
        xf = ffn(xf, norm_ffn1, w1, i)
        ya, qb, kb, vb, qi, ki, wi, qc, kc, vc = _in_proj(
            xf, norm_mix.reshape(depth, 1, d), w_in_al, conv_w, tabs, i, s, tm_proj)
        r3 = lambda a: a.reshape(b, s, a.shape[-1])
        to_cast = [(w, i) for w in ffn2_stacks] + ([(w, i + 1) for w in ffn1_stacks] if i + 1 < depth else [])
        yb, copies = _dsa(qb, r3(kb), r3(vb), qi, r3(ki), wi, tq_dsa, to_cast)
        w2, w1 = copies[:3], copies[3:]
        ycs = _dilated(r3(qc), r3(kc), r3(vc))
        parts = [ya, yb.reshape(t, -1)] + [yc.reshape(t, -1) for yc in ycs]
        xf = _out_proj(parts, w_out, xf, i, tm_big, 1024)
        xf = ffn(xf, norm_ffn2, w2, i)
    return _final_norm(xf, norm_final.reshape(1, d), tm_big).reshape(b, s, d)
```

```python
import functools
from typing import NamedTuple

import jax
import jax.numpy as jnp
from jax import lax
from jax.experimental import pallas as pl
from jax.experimental.pallas import tpu as pltpu

F32 = jnp.float32
BF16 = jnp.bfloat16
I32 = jnp.int32
I16 = jnp.int16
HALF16 = 1 << 15
PACKED_ROWS = 16
ONES_ROWS = PACKED_ROWS

HEAD_DIM = 128
CONV_CH = 512
CONV_WIDTH = 3
DSA_HEADS = 6
IDX_HEADS = 16
IDX_DIM = 64
TOPK_MAX = 256
DIL_PATTERNS = ((128, 1), (512, 4), (2048, 16))
DIL_HEADS_PER_GROUP = 2
DIL_HEADS = len(DIL_PATTERNS) * DIL_HEADS_PER_GROUP
ROPE_THETA = 10000.0
RMS_EPS = 1e-6

LANES = 128
SUBLANES = 8
VMEM_LIMIT = 56 * 1024 * 1024
NEG = -1e30
INT_MIN = -2 ** 31
LOG2E = 1.4426950408889634

SEG_CONV = (0, 3 * CONV_CH)
SEG_QB = (SEG_CONV[1], SEG_CONV[1] + DSA_HEADS * HEAD_DIM)
SEG_KV = (SEG_QB[1], SEG_QB[1] + 2 * HEAD_DIM)
SEG_QI = (SEG_KV[1], SEG_KV[1] + IDX_HEADS * IDX_DIM)
SEG_KI = (SEG_QI[1], SEG_QI[1] + LANES)
SEG_WI = (SEG_KI[1], SEG_KI[1] + LANES)
SEG_QC = (SEG_WI[1], SEG_WI[1] + DIL_HEADS * HEAD_DIM)
SEG_KC = (SEG_QC[1], SEG_QC[1] + DIL_HEADS * HEAD_DIM)
SEG_VC = (SEG_KC[1], SEG_KC[1] + DIL_HEADS * HEAD_DIM)
D_IN_ALIGNED = SEG_VC[1]


def _params(*sem):
    return pltpu.CompilerParams(dimension_semantics=sem, vmem_limit_bytes=VMEM_LIMIT)


def _rms(x, g):
    ms = jnp.mean(x * x, axis=-1, keepdims=True)
    return x * lax.rsqrt(ms + RMS_EPS) * g


def _ffn_up_body(x_ref, g_ref, wg_ref, wu_ref, h_ref, xn_ref):
    @pl.when(pl.program_id(1) == 0)
    def _():
        xn_ref[...] = _rms(x_ref[...], g_ref[...]).astype(BF16)

    xn = xn_ref[...]
    a = jnp.dot(xn, wg_ref[...], preferred_element_type=F32)
    b = jnp.dot(xn, wu_ref[...], preferred_element_type=F32)
    h_ref[...] = (a * jax.nn.sigmoid(a) * b).astype(BF16)


def _ffn_up(x, g, wg, wu, layer, tm, tn):
    t, d = x.shape
    f = wg.shape[1]
    return pl.pallas_call(
        _ffn_up_body,
        grid=(t // tm, f // tn),
        in_specs=[
            pl.BlockSpec((tm, d), lambda m, n: (m, 0)),
            pl.BlockSpec((None, 1, d), lambda m, n: (layer, 0, 0)),
            pl.BlockSpec((d, tn), lambda m, n: (0, n)),
            pl.BlockSpec((d, tn), lambda m, n: (0, n)),
        ],
        out_specs=pl.BlockSpec((tm, tn), lambda m, n: (m, n)),
        out_shape=jax.ShapeDtypeStruct((t, f), BF16),
        scratch_shapes=[pltpu.VMEM((tm, d), BF16)],
        compiler_params=_params("parallel", "arbitrary"),
        name="ffn_up",
    )(x, g, wg, wu)


def _ffn_down_body(h_ref, w_ref, x_ref, o_ref):
    y = jnp.dot(h_ref[...], w_ref[...], preferred_element_type=F32)
    o_ref[...] = x_ref[...] + 0.5 * y


def _ffn_down(h, wd, x, tm, tn):
    t, f = h.shape
    d = wd.shape[1]
    return pl.pallas_call(
        _ffn_down_body,
        grid=(t // tm, d // tn),
        in_specs=[
            pl.BlockSpec((tm, f), lambda m, n: (m, 0)),
            pl.BlockSpec((f, tn), lambda m, n: (0, n)),
            pl.BlockSpec((tm, tn), lambda m, n: (m, n)),
        ],
        out_specs=pl.BlockSpec((tm, tn), lambda m, n: (m, n)),
        out_shape=jax.ShapeDtypeStruct((t, d), F32),
        compiler_params=_params("parallel", "arbitrary"),
        name="ffn_down",
    )(h, wd, x)


def _cast_block_specs(w, layer, nsteps, step_of):
    _, rows, cols = w.shape
    nblk = nsteps
    while rows % nblk or (rows // nblk) % PACKED_ROWS:
        nblk //= 2
    blk = rows // nblk
    return (pl.BlockSpec((None, blk, cols), lambda *g: (layer, step_of(*g) * nblk // nsteps, 0)),
            pl.BlockSpec((blk, cols), lambda *g: (step_of(*g) * nblk // nsteps, 0)),
            jax.ShapeDtypeStruct((rows, cols), BF16))


def _rope_table_body(ncast, pos_ref, inv_ref, *rest):
    cast_in, (ch_ref, sh_ref, ci_ref, si_ref, *cast_out) = rest[:ncast], rest[ncast:]
    for src_ref, dst_ref in zip(cast_in, cast_out):
        dst_ref[...] = src_ref[...].astype(BF16)

    ang = pos_ref[...] * inv_ref[...]
    cos, sin = jnp.cos(ang), jnp.sin(ang)
    hh, hi = HEAD_DIM // 2, IDX_DIM // 2
    ch_ref[...] = jnp.concatenate([cos[:, :hh]] * 2, axis=1)
    sh_ref[...] = jnp.concatenate([-sin[:, :hh], sin[:, :hh]], axis=1)
    ci_ref[...] = jnp.concatenate([cos[:, hh:hh + hi]] * (LANES // hi), axis=1)
    si_ref[...] = jnp.concatenate([-sin[:, hh:hh + hi], sin[:, hh:hh + hi]] * (LANES // IDX_DIM), axis=1)


def _rope_tables(pos, tm, to_cast):
    t = pos.shape[0]
    inv = lambda dim: 1.0 / (ROPE_THETA ** (jnp.arange(0, dim, 2, dtype=F32) / dim))
    inv_l = jnp.concatenate([inv(HEAD_DIM), inv(IDX_DIM), jnp.zeros((LANES - (HEAD_DIM + IDX_DIM) // 2,), F32)])
    tab = pl.BlockSpec((tm, LANES), lambda m: (m, 0))
    casts = [_cast_block_specs(w, layer, t // tm, lambda m: m) for w, layer in to_cast]
    *tabs, = pl.pallas_call(
        functools.partial(_rope_table_body, len(casts)),
        grid=(t // tm,),
        in_specs=[pl.BlockSpec((tm, 1), lambda m: (m, 0)), pl.BlockSpec((1, LANES), lambda m: (0, 0))]
        + [c[0] for c in casts],
        out_specs=[tab] * 4 + [c[1] for c in casts],
        out_shape=[jax.ShapeDtypeStruct((t, LANES), F32)] * 4 + [c[2] for c in casts],
        compiler_params=_params("parallel"),
        name="rope_tables",
    )(pos, inv_l.reshape(1, LANES), *[w for w, _ in to_cast])
    return tabs[:4], tabs[4:]


def _rope128(x, cos, sin):
    return x * cos + pltpu.roll(x, HEAD_DIM // 2, 1) * sin


def _rope64(x, cos, sin, lo_half):
    partner = jnp.where(lo_half, pltpu.roll(x, LANES - IDX_DIM // 2, 1), pltpu.roll(x, IDX_DIM // 2, 1))
    return x * cos + partner * sin


def _in_proj_body(seq_tiles, x_ref, g_ref, w_ref, cw_ref, ch_ref, sh_ref, ci_ref, si_ref,
                  ya_ref, qb_ref, kb_ref, vb_ref, qi_ref, ki_ref, wi_ref, qc_ref, kc_ref, vc_ref,
                  u_ref):
    tm = x_ref.shape[0]
    xn = _rms(x_ref[...], g_ref[...]).astype(BF16)

    def proj(seg):
        return jnp.dot(xn, w_ref[:, seg[0]:seg[1]], preferred_element_type=F32)

    ch, sh, ci, si = ch_ref[...], sh_ref[...], ci_ref[...], si_ref[...]
    scale = HEAD_DIM ** -0.5 * LOG2E

    def rope_heads(p, n, mul):
        return jnp.concatenate(
            [_rope128(p[:, j * LANES:(j + 1) * LANES], ch, sh) * mul for j in range(n)], axis=1)

    qb_ref[...] = rope_heads(proj(SEG_QB), DSA_HEADS, scale).T.astype(BF16)
    p = proj(SEG_KV)
    kb_ref[...] = _rope128(p[:, :HEAD_DIM], ch, sh).astype(BF16)
    vb_ref[...] = p[:, HEAD_DIM:].astype(BF16)

    lo_half = lax.broadcasted_iota(I32, (tm, LANES), 1) % IDX_DIM < IDX_DIM // 2
    p = proj(SEG_QI)
    qi_ref[...] = jnp.concatenate(
        [_rope64(p[:, j * LANES:(j + 1) * LANES], ci, si, lo_half) * (IDX_DIM ** -0.5)
         for j in range(IDX_HEADS * IDX_DIM // LANES)], axis=1).T.astype(BF16)
    ki_ref[...] = _rope64(proj(SEG_KI), ci, si, lo_half)[:, :IDX_DIM].astype(BF16)
    wi_ref[...] = (proj(SEG_WI) * (IDX_HEADS ** -0.5)).T[:IDX_HEADS, :]

    qc_ref[...] = rope_heads(proj(SEG_QC), DIL_HEADS, scale)
    kc_ref[...] = rope_heads(proj(SEG_KC), DIL_HEADS, 1.0)

    h, gate_b, gate_c = (proj((SEG_CONV[0] + j * CONV_CH, SEG_CONV[0] + (j + 1) * CONV_CH)) for j in range(3))
    u = gate_c * h

    carry = jnp.where(pl.program_id(0) % seq_tiles == 0, 0.0, u_ref[...])
    ext = jnp.concatenate([carry, u], axis=0)
    cw = cw_ref[...]
    y = (cw[2:3, :] * u + cw[1:2, :] * pltpu.roll(ext, 1, 0)[SUBLANES:, :]
         + cw[0:1, :] * pltpu.roll(ext, 2, 0)[SUBLANES:, :])
    u_ref[...] = u[tm - SUBLANES:, :]
    ya_ref[...] = (gate_b * y).astype(BF16)

    vc_ref[...] = proj(SEG_VC)


def _in_proj(x, g, w, cw, tabs, layer, seq, tm):
    t, d = x.shape
    n = w.shape[2]
    row = lambda width: pl.BlockSpec((tm, width), lambda m: (m, 0))
    col = lambda height: pl.BlockSpec((height, tm), lambda m: (0, m))
    const = lambda shape: pl.BlockSpec((None,) + shape, lambda m: (layer, 0, 0))
    outs = [(CONV_CH, BF16, True), (DSA_HEADS * HEAD_DIM, BF16, False), (HEAD_DIM, BF16, True),
            (HEAD_DIM, BF16, True), (IDX_HEADS * IDX_DIM, BF16, False), (IDX_DIM, BF16, True),
            (IDX_HEADS, F32, False), (DIL_HEADS * HEAD_DIM, F32, True), (DIL_HEADS * HEAD_DIM, F32, True),
            (DIL_HEADS * HEAD_DIM, F32, True)]
    return pl.pallas_call(
        functools.partial(_in_proj_body, seq // tm),
        grid=(t // tm,),
        in_specs=[row(d), const((1, d)),
                  pl.BlockSpec((None, d, n), lambda m: (layer, 0, 0), pipeline_mode=pl.Buffered(1)),
                  const((CONV_WIDTH, CONV_CH)), row(LANES), row(LANES), row(LANES), row(LANES)],
        out_specs=[row(wd) if tok else col(wd) for wd, _, tok in outs],
        out_shape=[jax.ShapeDtypeStruct((t, wd) if tok else (wd, t), dt) for wd, dt, tok in outs],
        scratch_shapes=[pltpu.VMEM((SUBLANES, CONV_CH), F32)],
        compiler_params=_params("arbitrary"),
        name="in_proj",
    )(x, g, w, cw, *tabs)


def _dsa_body(topk, ncast, qbt_in_ref, kb_ref, vb_ref, qt_ref, ki_ref, wt_ref, *rest):
    cast_in, (o_ref, *cast_out) = rest[:ncast], rest[ncast:2 * ncast + 1]
    keys_ref, hi_ref, lo_ref, lo2_ref, qbt_ref, vt_ref, acc_ref = rest[2 * ncast + 1:]
    tq = o_ref.shape[1]
    seq = kb_ref.shape[1]

    for src_ref, dst_ref in zip(cast_in, cast_out):
        dst_ref[...] = src_ref[...].astype(BF16)

    nh = DSA_HEADS
    i = pl.program_id(1)
    nch = i + 1

    def chunk(c):
        return pl.ds(pl.multiple_of(c * tq, tq), tq)

    def transpose_bf16(a):
        return a.astype(F32).T.astype(BF16)

    @pl.when(i == 0)
    def _():
        for c2 in range(seq // (2 * tq)):
            vt_ref[c2, :HEAD_DIM, :] = transpose_bf16(vb_ref[0, c2 * 2 * tq:(c2 + 1) * 2 * tq, :])
            vt_ref[c2, HEAD_DIM:, :] = jnp.ones((ONES_ROWS, 2 * tq), BF16)

    for h in range(nh):
        qbt_ref[:, h * tq:(h + 1) * tq] = qbt_in_ref[h * HEAD_DIM:(h + 1) * HEAD_DIM, :]
    wt = wt_ref[...]
    krow = lax.broadcasted_iota(I32, (tq, tq), 0)

    def pair(c2):
        return pl.ds(pl.multiple_of(c2 * (2 * tq), 2 * tq), 2 * tq)

    def score_rows(rows, n, first_key):
        kc = ki_ref[0, rows, :]
        acc = jnp.zeros((n, tq), F32)
        for h in range(IDX_HEADS):
            lg = jnp.dot(kc, qt_ref[h * IDX_DIM:(h + 1) * IDX_DIM, :], preferred_element_type=F32)
            acc = acc + jnp.maximum(lg, 0.0) * wt[h:h + 1, :]
        bits = pltpu.bitcast(acc, I32)
        key = bits ^ ((bits >> 31) & 0x7FFFFFFF)
        causal = lax.broadcasted_iota(I32, (n, tq), 0) + first_key <= lax.broadcasted_iota(I32, (n, tq), 1) + i * tq
        key = jnp.where(causal, key, INT_MIN)
        keys_ref[rows, :] = key
        hi_ref[rows, :] = (key >> 16).astype(I16)
        lo_ref[rows, :] = ((key & 0xFFFF) - HALF16).astype(I16)

    def score_quad(c4, carry):
        score_rows(pl.ds(pl.multiple_of(c4 * (4 * tq), 4 * tq), 4 * tq), 4 * tq, c4 * (4 * tq))
        return carry

    lax.fori_loop(0, nch // 4, score_quad, 0)

    @pl.when(nch % 4 >= 2)
    def _():
        score_rows(pair(nch // 4 * 2), 2 * tq, nch // 4 * (4 * tq))

    npair = (nch + 1) // 2

    @pl.when(nch % 2 == 1)
    def _():
        score_rows(chunk(nch - 1), tq, (nch - 1) * tq)
        keys_ref[chunk(nch), :] = jnp.full((tq, tq), INT_MIN, I32)
        hi_ref[chunk(nch), :] = jnp.full((tq, tq), -HALF16, I16)
        lo_ref[chunk(nch), :] = jnp.full((tq, tq), -HALF16, I16)

    def count(pred):
        def body(c, acc):
            hit = pred(keys_ref[chunk(c), :], krow + c * tq).astype(I32)
            return acc + jnp.sum(hit.reshape(tq // 8, 8, tq), axis=0)
        acc = lax.fori_loop(0, nch, body, jnp.zeros((8, tq), I32))
        return jnp.sum(acc, axis=0, keepdims=True)

    def count16(ref, pred):
        def body(c2, acc):
            hit = jnp.where(pred(ref[pair(c2), :]), jnp.ones((), BF16), jnp.zeros((), BF16))
            parts = [hit[r * PACKED_ROWS:(r + 1) * PACKED_ROWS] for r in range(2 * tq // PACKED_ROWS)]
            while len(parts) > 1:
                parts = [parts[j] + parts[j + 1] for j in range(0, len(parts), 2)]
            return acc + parts[0].astype(F32)
        acc = lax.fori_loop(0, npair, body, jnp.zeros((PACKED_ROWS, tq), F32))
        return jnp.sum(acc, axis=0, keepdims=True)

    def kth_largest16(ref, k):
        def search_bit(b, t_u):
            cand = t_u | (jnp.int32(1) << (15 - b))
            cand16 = (cand - HALF16).astype(I16)
            return jnp.where(count16(ref, lambda v: v >= cand16) >= k, cand, t_u)
        return lax.fori_loop(0, 16, search_bit, jnp.zeros((1, tq), I32)) - HALF16

    t_hi = kth_largest16(hi_ref, float(topk))
    t_hi16 = t_hi.astype(I16)
    n_gt_hi = count16(hi_ref, lambda v: v > t_hi16)
    need_lo = float(topk) - n_gt_hi

    def bucket(c2, carry):
        lo2_ref[pair(c2), :] = jnp.where(hi_ref[pair(c2), :] == t_hi16, lo_ref[pair(c2), :],
                                         jnp.full((), -HALF16, I16))
        return carry

    lax.fori_loop(0, npair, bucket, 0)
    t_lo = kth_largest16(lo2_ref, need_lo)
    thr = jnp.maximum((t_hi << 16) | (t_lo + HALF16), INT_MIN + 1)
    t_lo16 = t_lo.astype(I16)
    n_ge = jnp.where(t_hi > -HALF16, n_gt_hi + count16(lo2_ref, lambda v: v >= t_lo16), 0.0)

    idx_bits = seq.bit_length() - 1

    @pl.when(jnp.max(n_ge) > topk)
    def _():
        need = topk - count(lambda k, _: k > thr)
        def bit(b, j):
            cand = j | (jnp.int32(1) << (idx_bits - 1 - b))
            below = count(lambda k, idx: (k == thr) & (idx < cand))
            return jnp.where(below < need, cand, j)
        bound = lax.fori_loop(0, idx_bits, bit, jnp.zeros((1, tq), I32))
        bound = jnp.where(n_ge > topk, bound, seq)

        def demote(c, carry):
            k = keys_ref[chunk(c), :]
            keys_ref[chunk(c), :] = jnp.where((k == thr) & (krow + c * tq > bound), thr - 1, k)
            return carry

        lax.fori_loop(0, nch, demote, 0)

    acc_ref[...] = jnp.zeros(acc_ref.shape, F32)

    def attend(first_pair, npairs, m):
        n = npairs * 2 * tq
        rows = pl.ds(pl.multiple_of(first_pair * (2 * tq), 2 * tq), n)
        bias = jnp.where(keys_ref[rows, :] >= thr, 0.0, NEG)
        st = jnp.dot(kb_ref[0, rows, :], qbt_ref[...], preferred_element_type=F32)
        st = st + jnp.concatenate([bias] * nh, axis=1)
        m_new = jnp.maximum(m, jnp.max(st, axis=0, keepdims=True))
        p = jnp.exp2(st - m_new).astype(BF16)
        pv = sum(jnp.dot(vt_ref[first_pair + j], p[j * 2 * tq:(j + 1) * 2 * tq], preferred_element_type=F32)
                 for j in range(npairs))
        acc_ref[...] = acc_ref[...] * jnp.exp2(m - m_new) + pv
        return m_new

    m = lax.fori_loop(0, npair // 2, lambda c4, m: attend(2 * c4, 2, m), jnp.full((1, nh * tq), NEG, F32))

    @pl.when(npair % 2 == 1)
    def _():
        attend(npair - 1, 1, m)

    out = acc_ref[:HEAD_DIM, :] / acc_ref[HEAD_DIM:HEAD_DIM + 1, :]
    for h in range(nh):
        o_ref[0, :, h * HEAD_DIM:(h + 1) * HEAD_DIM] = out[:, h * tq:(h + 1) * tq].T.astype(BF16)


def _dsa(qbt, kb, vb, qit, ki, wit, tq, to_cast):
    b, s, _ = kb.shape
    topk = min(TOPK_MAX, s // 4)
    assert s % (2 * tq) == 0
    assert 2 * tq // PACKED_ROWS <= 256
    nq = s // tq
    blk = lambda width: pl.BlockSpec((1, tq, width), lambda bb, i: (bb, i, 0))
    full = lambda width: pl.BlockSpec((1, s, width), lambda bb, i: (bb, 0, 0))
    qcol = lambda a: pl.BlockSpec((a.shape[0], tq), lambda bb, i: (0, bb * nq + i))
    casts = [_cast_block_specs(w, layer, b * nq, lambda bb, i: bb * nq + i) for w, layer in to_cast]
    out, *copies = pl.pallas_call(
        functools.partial(_dsa_body, topk, len(casts)),
        grid=(b, nq),
        in_specs=[qcol(qbt), full(HEAD_DIM), full(HEAD_DIM), qcol(qit), full(IDX_DIM), qcol(wit)]
        + [c[0] for c in casts],
        out_specs=[blk(DSA_HEADS * HEAD_DIM)] + [c[1] for c in casts],
        out_shape=[jax.ShapeDtypeStruct((b, s, DSA_HEADS * HEAD_DIM), BF16)] + [c[2] for c in casts],
        scratch_shapes=[pltpu.VMEM((s, tq), I32),
                        pltpu.VMEM((s, tq), I16),
                        pltpu.VMEM((s, tq), I16),
                        pltpu.VMEM((s, tq), I16),
                        pltpu.VMEM((HEAD_DIM, DSA_HEADS * tq), BF16),
                        pltpu.VMEM((s // (2 * tq), HEAD_DIM + ONES_ROWS, 2 * tq), BF16),
                        pltpu.VMEM((HEAD_DIM + ONES_ROWS, DSA_HEADS * tq), F32)],
        compiler_params=_params("parallel", "arbitrary"),
        name="dsa",
    )(qbt, kb, vb, qit, ki, wit, *[w for w, _ in to_cast])
    return out, copies


def _dilated_body(*refs):
    ng = len(DIL_PATTERNS)
    q_refs, k_refs, v_refs = refs[0:ng], refs[ng:2 * ng], refs[2 * ng:3 * ng]
    o_refs = refs[3 * ng:4 * ng]
    acc_ref, m_ref, l_ref = refs[4 * ng:]
    sb_tokens = q_refs[0].shape[1]
    t0 = pl.program_id(2) * sb_tokens
    blk = HEAD_DIM
    UNITS = 4
    rq = lax.broadcasted_iota(I32, (blk, blk), 0)
    ck = lax.broadcasted_iota(I32, (blk, blk), 1)
    bias_cur = jnp.where(ck <= rq, 0.0, NEG)
    bias_prev = jnp.where(ck >= rq, 0.0, NEG)
    nt = (((1,), (1,)), ((), ()))

    for g, (win, dil) in enumerate(DIL_PATTERNS):
        q_ref, k_ref, v_ref = q_refs[g], k_refs[g], v_refs[g]
        per_res = sb_tokens // dil // blk

        def rows(start, dil=dil):
            return pl.ds(start, blk, stride=dil) if dil > 1 else pl.ds(start, blk)

        def units(it, carry, g=g, dil=dil, per_res=per_res, rows=rows,
                  q_ref=q_ref, k_ref=k_ref, v_ref=v_ref):
            q0s, kcs, kps, hps = [], [], [], []
            for n in range(UNITS):
                idx = it * UNITS + n
                q0 = idx // per_res + (idx % per_res) * (blk * dil)
                k_cur = t0 + q0
                has_prev = k_cur >= blk * dil
                q0s.append(q0)
                kcs.append(k_cur)
                hps.append(has_prev)
                kps.append(jnp.where(has_prev, k_cur - blk * dil, k_cur))
            ss = []
            for q0, kc, kp, hp in zip(q0s, kcs, kps, hps):
                k2 = jnp.concatenate([k_ref[0, rows(kp), :], k_ref[0, rows(kc), :]], axis=0).astype(BF16)
                s = lax.dot_general(q_ref[0, rows(q0), :].astype(BF16), k2, nt, preferred_element_type=F32)
                ss.append(s + jnp.concatenate([jnp.where(hp, bias_prev, NEG), bias_cur], axis=1))
            ms = [jnp.max(jnp.maximum(s[:, :blk], s[:, blk:]), axis=-1, keepdims=True) for s in ss]
            ps = [jnp.exp2(s - m).astype(BF16) for s, m in zip(ss, ms)]
            for q0, kc, kp, m, p in zip(q0s, kcs, kps, ms, ps):
                v2 = jnp.concatenate([v_ref[0, rows(kp), :], v_ref[0, rows(kc), :]], axis=0).astype(BF16)
                av = jnp.dot(p, jnp.concatenate([v2, jnp.ones_like(v2)], axis=1), preferred_element_type=F32)
                acc_ref[g, rows(q0), :] = av[:, :HEAD_DIM]
                l_ref[g, rows(q0), :] = av[:, HEAD_DIM:]
                m_ref[g, rows(q0), :] = jnp.broadcast_to(m, (blk, LANES))
            return carry

        lax.fori_loop(0, dil * per_res // UNITS, units, 0)

    step = 256

    def merge(c, carry):
        sl = pl.ds(pl.multiple_of(c * step, step), step)
        ms = [m_ref[g, sl, :] for g in range(ng)]
        m_all = functools.reduce(jnp.maximum, ms)
        ws = [jnp.exp2(m - m_all) for m in ms]
        den = sum(w * l_ref[g, sl, :] for g, w in enumerate(ws))
        for g, w in enumerate(ws):
            o_refs[g][0, sl, :] = (acc_ref[g, sl, :] * (w / den)).astype(BF16)
        return carry

    lax.fori_loop(0, sb_tokens // step, merge, 0)


def _dilated(qc, kc, vc):
    b, s, _ = qc.shape
    ng, hg = len(DIL_PATTERNS), DIL_HEADS_PER_GROUP
    sb_tokens = HEAD_DIM * max(dil for _, dil in DIL_PATTERNS)
    assert s % sb_tokens == 0 and all(win == HEAD_DIM * dil for win, dil in DIL_PATTERNS)
    head = lambda g: (lambda bb, j, sb: (bb, sb, g * hg + j))
    head_full = lambda g: (lambda bb, j, sb: (bb, 0, g * hg + j))
    q_specs = [pl.BlockSpec((1, sb_tokens, HEAD_DIM), head(g)) for g in range(ng)]
    kv_specs = [pl.BlockSpec((1, s, HEAD_DIM), head_full(g)) for g in range(ng)]
    out_spec = pl.BlockSpec((1, sb_tokens, HEAD_DIM), lambda bb, j, sb: (bb, sb, j))
    return pl.pallas_call(
        _dilated_body,
        grid=(b, hg, s // sb_tokens),
        in_specs=q_specs + kv_specs + kv_specs,
        out_specs=[out_spec] * ng,
        out_shape=[jax.ShapeDtypeStruct((b, s, hg * HEAD_DIM), BF16)] * ng,
        scratch_shapes=[pltpu.VMEM((ng, sb_tokens, HEAD_DIM), F32)] * 3,
        compiler_params=_params("parallel", "parallel", "arbitrary"),
        name="dilated",
    )(*([qc] * ng + [kc] * ng + [vc] * ng))


def _out_proj_body(*refs):
    *y_refs, w_ref, x_ref, o_ref, wb_ref = refs

    @pl.when(pl.program_id(1) == 0)
    def _():
        wb_ref[...] = w_ref[...].astype(BF16)

    y = jnp.concatenate([y_ref[...] for y_ref in y_refs], axis=1)
    o_ref[...] = x_ref[...] + jnp.dot(y, wb_ref[...], preferred_element_type=F32)


def _out_proj(parts, w, x, layer, tm, tn):
    t, d = x.shape
    assert sum(a.shape[1] for a in parts) == w.shape[1]
    return pl.pallas_call(
        _out_proj_body,
        grid=(d // tn, t // tm),
        in_specs=[pl.BlockSpec((tm, a.shape[1]), lambda n, m: (m, 0)) for a in parts]
        + [pl.BlockSpec((None, w.shape[1], tn), lambda n, m: (layer, 0, n)),
           pl.BlockSpec((tm, tn), lambda n, m: (m, n))],
        out_specs=pl.BlockSpec((tm, tn), lambda n, m: (m, n)),
        out_shape=jax.ShapeDtypeStruct((t, d), F32),
        scratch_shapes=[pltpu.VMEM((w.shape[1], tn), BF16)],
        compiler_params=_params("parallel", "arbitrary"),
        name="out_proj",
    )(*parts, w, x)


def _final_norm_body(x_ref, g_ref, o_ref):
    o_ref[...] = _rms(x_ref[...], g_ref[...])


def _final_norm(x, g, tm):
    t, d = x.shape
    return pl.pallas_call(
        _final_norm_body,
        grid=(t // tm,),
        in_specs=[pl.BlockSpec((tm, d), lambda m: (m, 0)), pl.BlockSpec((1, d), lambda m: (0, 0))],
        out_specs=pl.BlockSpec((tm, d), lambda m: (m, 0)),
        out_shape=jax.ShapeDtypeStruct((t, d), F32),
        compiler_params=_params("parallel"),
        name="final_norm",
    )(x, g)


def _align_w_in_body(x_ref, o_ref):
    o_wi = SEG_QI[1] + IDX_DIM
    o_qc = o_wi + IDX_HEADS
    rows = x_ref.shape[0]
    o_ref[:, :o_wi] = x_ref[:, :o_wi]
    o_ref[:, o_wi:SEG_WI[0]] = jnp.zeros((rows, SEG_WI[0] - o_wi), BF16)
    o_ref[:, SEG_WI[0]:SEG_WI[0] + IDX_HEADS] = x_ref[:, o_wi:o_qc]
    o_ref[:, SEG_WI[0] + IDX_HEADS:SEG_QC[0]] = jnp.zeros((rows, LANES - IDX_HEADS), BF16)
    o_ref[:, SEG_QC[0]:] = x_ref[:, o_qc:o_qc + D_IN_ALIGNED - SEG_QC[0]]


def _align_w_in(w, tr):
    depth, d, n = w.shape
    assert n - (SEG_QI[1] + IDX_DIM + IDX_HEADS) == D_IN_ALIGNED - SEG_QC[0]
    w = jnp.pad(w.astype(BF16), ((0, 0), (0, 0), (0, D_IN_ALIGNED - n)))
    return pl.pallas_call(
        _align_w_in_body,
        grid=(depth, d // tr),
        in_specs=[pl.BlockSpec((None, tr, D_IN_ALIGNED), lambda l, r: (l, r, 0))],
        out_specs=pl.BlockSpec((None, tr, D_IN_ALIGNED), lambda l, r: (l, r, 0)),
        out_shape=jax.ShapeDtypeStruct((depth, d, D_IN_ALIGNED), BF16),
        compiler_params=_params("parallel", "parallel"),
        name="align_w_in",
    )(w)


def _tile(n, want):
    while n % want:
        want //= 2
    return want


class _Tiles(NamedTuple):
    tm: int
    tn_ffn: int
    tn_out: int
    tm_proj: int
    tq: int
    tm_rope: int
    tr_align: int


def _tiles(t, s, d, f):
    return _Tiles(tm=_tile(t, 1024), tn_ffn=_tile(f, 512), tn_out=_tile(d, 1024), tm_proj=_tile(s, 512),
                  tq=_tile(s, 256), tm_rope=_tile(t, 256), tr_align=_tile(d, 256))


def kernel(x, positions, norm_ffn1, ffn1_gate, ffn1_up, ffn1_down, norm_mix, w_in, conv_w, w_out,
           norm_ffn2, ffn2_gate, ffn2_up, ffn2_down, norm_final):
    b, s, d = x.shape
    t = b * s
    depth = w_in.shape[0]
    tl = _tiles(t, s, d, ffn1_gate.shape[2])

    def ffn(xf, g, weights, layer):
        wg, wu, wd = weights
        h = _ffn_up(xf, g.reshape(depth, 1, d), wg, wu, layer, tl.tm, tl.tn_ffn)
        return _ffn_down(h, wd, xf, tl.tm, tl.tn_ffn)

    ffn1_stacks, ffn2_stacks = (ffn1_gate, ffn1_up, ffn1_down), (ffn2_gate, ffn2_up, ffn2_down)
    pos = positions.astype(F32).reshape(t, 1)
    tabs, w1 = _rope_tables(pos, tl.tm_rope, [(w, 0) for w in ffn1_stacks])
    w_in_al = _align_w_in(w_in, tl.tr_align)
    xf = x.reshape(t, d)
    for i in range(depth):
        xf = ffn(xf, norm_ffn1, w1, i)
        ya, qb, kb, vb, qi, ki, wi, qc, kc, vc = _in_proj(
            xf, norm_mix.reshape(depth, 1, d), w_in_al, conv_w, tabs, i, s, tl.tm_proj)
        r3 = lambda a: a.reshape(b, s, a.shape[-1])
        to_cast = [(w, i) for w in ffn2_stacks] + ([(w, i + 1) for w in ffn1_stacks] if i + 1 < depth else [])
        yb, copies = _dsa(qb, r3(kb), r3(vb), qi, r3(ki), wi, tl.tq, to_cast)
        w2, w1 = copies[:3], copies[3:]
        ycs = _dilated(r3(qc), r3(kc), r3(vc))
        parts = [ya, yb.reshape(t, -1)] + [yc.reshape(t, -1) for yc in ycs]
        xf = _out_proj(parts, w_out, xf, i, tl.tm, tl.tn_out)
        xf = ffn(xf, norm_ffn2, w2, i)
    return _final_norm(xf, norm_final.reshape(1, d), tl.tm).reshape(b, s, d)
```

```python
import functools
from typing import NamedTuple

import jax
import jax.numpy as jnp
from jax import lax
from jax.experimental import pallas as pl
from jax.experimental.pallas import tpu as pltpu

F32 = jnp.float32
BF16 = jnp.bfloat16
I32 = jnp.int32
I16 = jnp.int16
HALF16 = 1 << 15
PACKED_ROWS = 16
ONES_ROWS = PACKED_ROWS

HEAD_DIM = 128
CONV_CH = 512
CONV_WIDTH = 3
DSA_HEADS = 6
IDX_HEADS = 16
IDX_DIM = 64
TOPK_MAX = 256
DIL_PATTERNS = ((128, 1), (512, 4), (2048, 16))
DIL_HEADS_PER_GROUP = 2
DIL_HEADS = len(DIL_PATTERNS) * DIL_HEADS_PER_GROUP
ROPE_THETA = 10000.0
RMS_EPS = 1e-6

LANES = 128
SUBLANES = 8
VMEM_LIMIT = 56 * 1024 * 1024
NEG = -1e30
INT_MIN = -2 ** 31
LOG2E = 1.4426950408889634

SEG_CONV = (0, 3 * CONV_CH)
SEG_QB = (SEG_CONV[1], SEG_CONV[1] + DSA_HEADS * HEAD_DIM)
SEG_KV = (SEG_QB[1], SEG_QB[1] + 2 * HEAD_DIM)
SEG_QI = (SEG_KV[1], SEG_KV[1] + IDX_HEADS * IDX_DIM)
SEG_KI = (SEG_QI[1], SEG_QI[1] + LANES)
SEG_WI = (SEG_KI[1], SEG_KI[1] + LANES)
SEG_QC = (SEG_WI[1], SEG_WI[1] + DIL_HEADS * HEAD_DIM)
SEG_KC = (SEG_QC[1], SEG_QC[1] + DIL_HEADS * HEAD_DIM)
SEG_VC = (SEG_KC[1], SEG_KC[1] + DIL_HEADS * HEAD_DIM)
D_IN_ALIGNED = SEG_VC[1]


def _params(*sem):
    return pltpu.CompilerParams(dimension_semantics=sem, vmem_limit_bytes=VMEM_LIMIT)


def _rms(x, g):
    ms = jnp.mean(x * x, axis=-1, keepdims=True)
    return x * lax.rsqrt(ms + RMS_EPS) * g


def _ffn_up_body(x_ref, g_ref, wg_ref, wu_ref, h_ref, xn_ref):
    @pl.when(pl.program_id(1) == 0)
    def _():
        xn_ref[...] = _rms(x_ref[...], g_ref[...]).astype(BF16)

    xn = xn_ref[...]
    a = jnp.dot(xn, wg_ref[...], preferred_element_type=F32)
    b = jnp.dot(xn, wu_ref[...], preferred_element_type=F32)
    h_ref[...] = (a * jax.nn.sigmoid(a) * b).astype(BF16)


def _ffn_up(x, g, wg, wu, layer, tm, tn):
    t, d = x.shape
    f = wg.shape[1]
    return pl.pallas_call(
        _ffn_up_body,
        grid=(t // tm, f // tn),
        in_specs=[
            pl.BlockSpec((tm, d), lambda m, n: (m, 0)),
            pl.BlockSpec((None, 1, d), lambda m, n: (layer, 0, 0)),
            pl.BlockSpec((d, tn), lambda m, n: (0, n)),
            pl.BlockSpec((d, tn), lambda m, n: (0, n)),
        ],
        out_specs=pl.BlockSpec((tm, tn), lambda m, n: (m, n)),
        out_shape=jax.ShapeDtypeStruct((t, f), BF16),
        scratch_shapes=[pltpu.VMEM((tm, d), BF16)],
        compiler_params=_params("parallel", "arbitrary"),
        name="ffn_up",
    )(x, g, wg, wu)


def _ffn_down_body(h_ref, w_ref, x_ref, o_ref):
    y = jnp.dot(h_ref[...], w_ref[...], preferred_element_type=F32)
    o_ref[...] = x_ref[...] + 0.5 * y


def _ffn_down(h, wd, x, tm, tn):
    t, f = h.shape
    d = wd.shape[1]
    return pl.pallas_call(
        _ffn_down_body,
        grid=(t // tm, d // tn),
        in_specs=[
            pl.BlockSpec((tm, f), lambda m, n: (m, 0)),
            pl.BlockSpec((f, tn), lambda m, n: (0, n)),
            pl.BlockSpec((tm, tn), lambda m, n: (m, n)),
        ],
        out_specs=pl.BlockSpec((tm, tn), lambda m, n: (m, n)),
        out_shape=jax.ShapeDtypeStruct((t, d), F32),
        compiler_params=_params("parallel", "arbitrary"),
        name="ffn_down",
    )(h, wd, x)


def _cast_block_specs(w, layer, nsteps, step_of):
    _, rows, cols = w.shape
    nblk = nsteps
    while rows % nblk or (rows // nblk) % PACKED_ROWS:
        nblk //= 2
    blk = rows // nblk
    return (pl.BlockSpec((None, blk, cols), lambda *g: (layer, step_of(*g) * nblk // nsteps, 0)),
            pl.BlockSpec((blk, cols), lambda *g: (step_of(*g) * nblk // nsteps, 0)),
            jax.ShapeDtypeStruct((rows, cols), BF16))


def _rope_table_body(ncast, pos_ref, inv_ref, *rest):
    cast_in, (ch_ref, sh_ref, ci_ref, si_ref, *cast_out) = rest[:ncast], rest[ncast:]
    for src_ref, dst_ref in zip(cast_in, cast_out):
        dst_ref[...] = src_ref[...].astype(BF16)

    ang = pos_ref[...] * inv_ref[...]
    cos, sin = jnp.cos(ang), jnp.sin(ang)
    hh, hi = HEAD_DIM // 2, IDX_DIM // 2
    ch_ref[...] = jnp.concatenate([cos[:, :hh]] * 2, axis=1)
    sh_ref[...] = jnp.concatenate([-sin[:, :hh], sin[:, :hh]], axis=1)
    ci_ref[...] = jnp.concatenate([cos[:, hh:hh + hi]] * (LANES // hi), axis=1)
    si_ref[...] = jnp.concatenate([-sin[:, hh:hh + hi], sin[:, hh:hh + hi]] * (LANES // IDX_DIM), axis=1)


def _rope_tables(pos, tm, to_cast):
    t = pos.shape[0]
    inv = lambda dim: 1.0 / (ROPE_THETA ** (jnp.arange(0, dim, 2, dtype=F32) / dim))
    inv_l = jnp.concatenate([inv(HEAD_DIM), inv(IDX_DIM), jnp.zeros((LANES - (HEAD_DIM + IDX_DIM) // 2,), F32)])
    tab = pl.BlockSpec((tm, LANES), lambda m: (m, 0))
    casts = [_cast_block_specs(w, layer, t // tm, lambda m: m) for w, layer in to_cast]
    *tabs, = pl.pallas_call(
        functools.partial(_rope_table_body, len(casts)),
        grid=(t // tm,),
        in_specs=[pl.BlockSpec((tm, 1), lambda m: (m, 0)), pl.BlockSpec((1, LANES), lambda m: (0, 0))]
        + [c[0] for c in casts],
        out_specs=[tab] * 4 + [c[1] for c in casts],
        out_shape=[jax.ShapeDtypeStruct((t, LANES), F32)] * 4 + [c[2] for c in casts],
        compiler_params=_params("parallel"),
        name="rope_tables",
    )(pos, inv_l.reshape(1, LANES), *[w for w, _ in to_cast])
    return tabs[:4], tabs[4:]


def _rope128(x, cos, sin):
    return x * cos + pltpu.roll(x, HEAD_DIM // 2, 1) * sin


def _rope64(x, cos, sin, lo_half):
    partner = jnp.where(lo_half, pltpu.roll(x, LANES - IDX_DIM // 2, 1), pltpu.roll(x, IDX_DIM // 2, 1))
    return x * cos + partner * sin


def _in_proj_body(seq_tiles, x_ref, g_ref, w_ref, cw_ref, ch_ref, sh_ref, ci_ref, si_ref,
                  ya_ref, qb_ref, kb_ref, vb_ref, qi_ref, ki_ref, wi_ref, qc_ref, kc_ref, vc_ref,
                  u_ref):
    tm = x_ref.shape[0]
    xn = _rms(x_ref[...], g_ref[...]).astype(BF16)

    def proj(seg):
        return jnp.dot(xn, w_ref[:, seg[0]:seg[1]], preferred_element_type=F32)

    ch, sh, ci, si = ch_ref[...], sh_ref[...], ci_ref[...], si_ref[...]
    scale = HEAD_DIM ** -0.5 * LOG2E

    def rope_heads(p, n, mul):
        return jnp.concatenate(
            [_rope128(p[:, j * LANES:(j + 1) * LANES], ch, sh) * mul for j in range(n)], axis=1)

    qb_ref[...] = rope_heads(proj(SEG_QB), DSA_HEADS, scale).T.astype(BF16)
    p = proj(SEG_KV)
    kb_ref[...] = _rope128(p[:, :HEAD_DIM], ch, sh).astype(BF16)
    vb_ref[...] = p[:, HEAD_DIM:].astype(BF16)

    lo_half = lax.broadcasted_iota(I32, (tm, LANES), 1) % IDX_DIM < IDX_DIM // 2
    p = proj(SEG_QI)
    qi_ref[...] = jnp.concatenate(
        [_rope64(p[:, j * LANES:(j + 1) * LANES], ci, si, lo_half) * (IDX_DIM ** -0.5)
         for j in range(IDX_HEADS * IDX_DIM // LANES)], axis=1).T.astype(BF16)
    ki_ref[...] = _rope64(proj(SEG_KI), ci, si, lo_half)[:, :IDX_DIM].astype(BF16)
    wi_ref[...] = (proj(SEG_WI) * (IDX_HEADS ** -0.5)).T[:IDX_HEADS, :]

    qc_ref[...] = rope_heads(proj(SEG_QC), DIL_HEADS, scale)
    kc_ref[...] = rope_heads(proj(SEG_KC), DIL_HEADS, 1.0)

    h, gate_b, gate_c = (proj((SEG_CONV[0] + j * CONV_CH, SEG_CONV[0] + (j + 1) * CONV_CH)) for j in range(3))
    u = gate_c * h

    carry = jnp.where(pl.program_id(0) % seq_tiles == 0, 0.0, u_ref[...])
    ext = jnp.concatenate([carry, u], axis=0)
    cw = cw_ref[...]
    y = (cw[2:3, :] * u + cw[1:2, :] * pltpu.roll(ext, 1, 0)[SUBLANES:, :]
         + cw[0:1, :] * pltpu.roll(ext, 2, 0)[SUBLANES:, :])
    u_ref[...] = u[tm - SUBLANES:, :]
    ya_ref[...] = (gate_b * y).astype(BF16)

    vc_ref[...] = proj(SEG_VC)


def _in_proj(x, g, w, cw, tabs, layer, seq, tm):
    t, d = x.shape
    n = w.shape[2]
    row = lambda width: pl.BlockSpec((tm, width), lambda m: (m, 0))
    col = lambda height: pl.BlockSpec((height, tm), lambda m: (0, m))
    const = lambda shape: pl.BlockSpec((None,) + shape, lambda m: (layer, 0, 0))
    outs = [(CONV_CH, BF16, True), (DSA_HEADS * HEAD_DIM, BF16, False), (HEAD_DIM, BF16, True),
            (HEAD_DIM, BF16, True), (IDX_HEADS * IDX_DIM, BF16, False), (IDX_DIM, BF16, True),
            (IDX_HEADS, F32, False), (DIL_HEADS * HEAD_DIM, F32, True), (DIL_HEADS * HEAD_DIM, F32, True),
            (DIL_HEADS * HEAD_DIM, F32, True)]
    return pl.pallas_call(
        functools.partial(_in_proj_body, seq // tm),
        grid=(t // tm,),
        in_specs=[row(d), const((1, d)),
                  pl.BlockSpec((None, d, n), lambda m: (layer, 0, 0), pipeline_mode=pl.Buffered(1)),
                  const((CONV_WIDTH, CONV_CH)), row(LANES), row(LANES), row(LANES), row(LANES)],
        out_specs=[row(wd) if tok else col(wd) for wd, _, tok in outs],
        out_shape=[jax.ShapeDtypeStruct((t, wd) if tok else (wd, t), dt) for wd, dt, tok in outs],
        scratch_shapes=[pltpu.VMEM((SUBLANES, CONV_CH), F32)],
        compiler_params=_params("arbitrary"),
        name="in_proj",
    )(x, g, w, cw, *tabs)


def _dsa_body(topk, ncast, qbt_in_ref, kb_ref, vb_ref, qt_ref, ki_ref, wt_ref, *rest):
    cast_in, (o_ref, *cast_out) = rest[:ncast], rest[ncast:2 * ncast + 1]
    keys_ref, hi_ref, lo_ref, lo2_ref, qbt_ref, vt_ref, acc_ref = rest[2 * ncast + 1:]
    tq = o_ref.shape[1]
    seq = kb_ref.shape[1]

    for src_ref, dst_ref in zip(cast_in, cast_out):
        dst_ref[...] = src_ref[...].astype(BF16)

    nh = DSA_HEADS
    i = pl.program_id(1)
    nch = i + 1

    def chunk(c):
        return pl.ds(pl.multiple_of(c * tq, tq), tq)

    def transpose_bf16(a):
        return a.astype(F32).T.astype(BF16)

    @pl.when(i == 0)
    def _():
        for c2 in range(seq // (2 * tq)):
            vt_ref[c2, :HEAD_DIM, :] = transpose_bf16(vb_ref[0, c2 * 2 * tq:(c2 + 1) * 2 * tq, :])
            vt_ref[c2, HEAD_DIM:, :] = jnp.ones((ONES_ROWS, 2 * tq), BF16)

    for h in range(nh):
        qbt_ref[:, h * tq:(h + 1) * tq] = qbt_in_ref[h * HEAD_DIM:(h + 1) * HEAD_DIM, :]
    wt = wt_ref[...]
    krow = lax.broadcasted_iota(I32, (tq, tq), 0)

    def pair(c2):
        return pl.ds(pl.multiple_of(c2 * (2 * tq), 2 * tq), 2 * tq)

    def score_rows(rows, n, first_key):
        kc = ki_ref[0, rows, :]
        acc = jnp.zeros((n, tq), F32)
        for h in range(IDX_HEADS):
            lg = jnp.dot(kc, qt_ref[h * IDX_DIM:(h + 1) * IDX_DIM, :], preferred_element_type=F32)
            acc = acc + jnp.maximum(lg, 0.0) * wt[h:h + 1, :]
        bits = pltpu.bitcast(acc, I32)
        key = bits ^ ((bits >> 31) & 0x7FFFFFFF)
        causal = lax.broadcasted_iota(I32, (n, tq), 0) + first_key <= lax.broadcasted_iota(I32, (n, tq), 1) + i * tq
        key = jnp.where(causal, key, INT_MIN)
        keys_ref[rows, :] = key
        hi_ref[rows, :] = (key >> 16).astype(I16)
        lo_ref[rows, :] = ((key & 0xFFFF) - HALF16).astype(I16)

    def score_quad(c4, carry):
        score_rows(pl.ds(pl.multiple_of(c4 * (4 * tq), 4 * tq), 4 * tq), 4 * tq, c4 * (4 * tq))
        return carry

    lax.fori_loop(0, nch // 4, score_quad, 0)

    @pl.when(nch % 4 >= 2)
    def _():
        score_rows(pair(nch // 4 * 2), 2 * tq, nch // 4 * (4 * tq))

    npair = (nch + 1) // 2

    @pl.when(nch % 2 == 1)
    def _():
        score_rows(chunk(nch - 1), tq, (nch - 1) * tq)
        keys_ref[chunk(nch), :] = jnp.full((tq, tq), INT_MIN, I32)
        hi_ref[chunk(nch), :] = jnp.full((tq, tq), -HALF16, I16)
        lo_ref[chunk(nch), :] = jnp.full((tq, tq), -HALF16, I16)

    def count(pred):
        def body(c, acc):
            hit = pred(keys_ref[chunk(c), :], krow + c * tq).astype(I32)
            return acc + jnp.sum(hit.reshape(tq // 8, 8, tq), axis=0)
        acc = lax.fori_loop(0, nch, body, jnp.zeros((8, tq), I32))
        return jnp.sum(acc, axis=0, keepdims=True)

    def count16(ref, pred):
        def body(c2, acc):
            hit = jnp.where(pred(ref[pair(c2), :]), jnp.ones((), BF16), jnp.zeros((), BF16))
            parts = [hit[r * PACKED_ROWS:(r + 1) * PACKED_ROWS] for r in range(2 * tq // PACKED_ROWS)]
            while len(parts) > 1:
                parts = [parts[j] + parts[j + 1] for j in range(0, len(parts), 2)]
            return acc + parts[0].astype(F32)
        acc = lax.fori_loop(0, npair, body, jnp.zeros((PACKED_ROWS, tq), F32))
        return jnp.sum(acc, axis=0, keepdims=True)

    def kth_largest16(ref, k):
        def search_bit(b, carry):
            t_u, n_gt, n_ge = carry
            cand = t_u | (jnp.int32(1) << (15 - b))
            cand16 = (cand - HALF16).astype(I16)
            n = count16(ref, lambda v: v >= cand16)
            ok = n >= k
            return jnp.where(ok, cand, t_u), jnp.where(ok, n_gt, n), jnp.where(ok, n, n_ge)
        walked = (npair * (2 * tq)).astype(F32)
        init = (jnp.zeros((1, tq), I32), jnp.zeros((1, tq), F32), jnp.zeros((1, tq), F32) + walked)
        t_u, n_gt, n_ge = lax.fori_loop(0, 16, search_bit, init)
        return t_u - HALF16, n_gt, n_ge

    t_hi, n_gt_hi, _ = kth_largest16(hi_ref, float(topk))
    t_hi16 = t_hi.astype(I16)
    need_lo = float(topk) - n_gt_hi

    def bucket(c2, carry):
        lo2_ref[pair(c2), :] = jnp.where(hi_ref[pair(c2), :] == t_hi16, lo_ref[pair(c2), :],
                                         jnp.full((), -HALF16, I16))
        return carry

    lax.fori_loop(0, npair, bucket, 0)
    t_lo, _, n_ge_lo = kth_largest16(lo2_ref, need_lo)
    thr = jnp.maximum((t_hi << 16) | (t_lo + HALF16), INT_MIN + 1)
    n_ge = jnp.where(t_hi > -HALF16, n_gt_hi + n_ge_lo, 0.0)

    idx_bits = seq.bit_length() - 1

    @pl.when(jnp.max(n_ge) > topk)
    def _():
        need = topk - count(lambda k, _: k > thr)
        def bit(b, j):
            cand = j | (jnp.int32(1) << (idx_bits - 1 - b))
            below = count(lambda k, idx: (k == thr) & (idx < cand))
            return jnp.where(below < need, cand, j)
        bound = lax.fori_loop(0, idx_bits, bit, jnp.zeros((1, tq), I32))
        bound = jnp.where(n_ge > topk, bound, seq)

        def demote(c, carry):
            k = keys_ref[chunk(c), :]
            keys_ref[chunk(c), :] = jnp.where((k == thr) & (krow + c * tq > bound), thr - 1, k)
            return carry

        lax.fori_loop(0, nch, demote, 0)

    acc_ref[...] = jnp.zeros(acc_ref.shape, F32)

    def attend(first_pair, npairs, m):
        n = npairs * 2 * tq
        rows = pl.ds(pl.multiple_of(first_pair * (2 * tq), 2 * tq), n)
        bias = jnp.where(keys_ref[rows, :] >= thr, 0.0, NEG)
        st = jnp.dot(kb_ref[0, rows, :], qbt_ref[...], preferred_element_type=F32)
        st = st + jnp.concatenate([bias] * nh, axis=1)
        m_new = jnp.maximum(m, jnp.max(st, axis=0, keepdims=True))
        p = jnp.exp2(st - m_new).astype(BF16)
        pv = sum(jnp.dot(vt_ref[first_pair + j], p[j * 2 * tq:(j + 1) * 2 * tq], preferred_element_type=F32)
                 for j in range(npairs))
        acc_ref[...] = acc_ref[...] * jnp.exp2(m - m_new) + pv
        return m_new

    m = lax.fori_loop(0, npair // 2, lambda c4, m: attend(2 * c4, 2, m), jnp.full((1, nh * tq), NEG, F32))

    @pl.when(npair % 2 == 1)
    def _():
        attend(npair - 1, 1, m)

    out = acc_ref[:HEAD_DIM, :] / acc_ref[HEAD_DIM:HEAD_DIM + 1, :]
    for h in range(nh):
        o_ref[0, :, h * HEAD_DIM:(h + 1) * HEAD_DIM] = out[:, h * tq:(h + 1) * tq].T.astype(BF16)


def _dsa(qbt, kb, vb, qit, ki, wit, tq, to_cast):
    b, s, _ = kb.shape
    topk = min(TOPK_MAX, s // 4)
    assert s % (2 * tq) == 0
    assert 2 * tq // PACKED_ROWS <= 256
    nq = s // tq
    blk = lambda width: pl.BlockSpec((1, tq, width), lambda bb, i: (bb, i, 0))
    full = lambda width: pl.BlockSpec((1, s, width), lambda bb, i: (bb, 0, 0))
    qcol = lambda a: pl.BlockSpec((a.shape[0], tq), lambda bb, i: (0, bb * nq + i))
    casts = [_cast_block_specs(w, layer, b * nq, lambda bb, i: bb * nq + i) for w, layer in to_cast]
    out, *copies = pl.pallas_call(
        functools.partial(_dsa_body, topk, len(casts)),
        grid=(b, nq),
        in_specs=[qcol(qbt), full(HEAD_DIM), full(HEAD_DIM), qcol(qit), full(IDX_DIM), qcol(wit)]
        + [c[0] for c in casts],
        out_specs=[blk(DSA_HEADS * HEAD_DIM)] + [c[1] for c in casts],
        out_shape=[jax.ShapeDtypeStruct((b, s, DSA_HEADS * HEAD_DIM), BF16)] + [c[2] for c in casts],
        scratch_shapes=[pltpu.VMEM((s, tq), I32),
                        pltpu.VMEM((s, tq), I16),
                        pltpu.VMEM((s, tq), I16),
                        pltpu.VMEM((s, tq), I16),
                        pltpu.VMEM((HEAD_DIM, DSA_HEADS * tq), BF16),
                        pltpu.VMEM((s // (2 * tq), HEAD_DIM + ONES_ROWS, 2 * tq), BF16),
                        pltpu.VMEM((HEAD_DIM + ONES_ROWS, DSA_HEADS * tq), F32)],
        compiler_params=_params("parallel", "arbitrary"),
        name="dsa",
    )(qbt, kb, vb, qit, ki, wit, *[w for w, _ in to_cast])
    return out, copies


def _dilated_body(*refs):
    ng = len(DIL_PATTERNS)
    q_refs, k_refs, v_refs = refs[0:ng], refs[ng:2 * ng], refs[2 * ng:3 * ng]
    o_refs = refs[3 * ng:4 * ng]
    acc_ref, m_ref, l_ref = refs[4 * ng:]
    sb_tokens = q_refs[0].shape[1]
    t0 = pl.program_id(2) * sb_tokens
    blk = HEAD_DIM
    UNITS = 4
    rq = lax.broadcasted_iota(I32, (blk, blk), 0)
    ck = lax.broadcasted_iota(I32, (blk, blk), 1)
    bias_cur = jnp.where(ck <= rq, 0.0, NEG)
    bias_prev = jnp.where(ck >= rq, 0.0, NEG)
    nt = (((1,), (1,)), ((), ()))

    for g, (win, dil) in enumerate(DIL_PATTERNS):
        q_ref, k_ref, v_ref = q_refs[g], k_refs[g], v_refs[g]
        per_res = sb_tokens // dil // blk

        def rows(start, dil=dil):
            return pl.ds(start, blk, stride=dil) if dil > 1 else pl.ds(start, blk)

        def units(it, carry, g=g, dil=dil, per_res=per_res, rows=rows,
                  q_ref=q_ref, k_ref=k_ref, v_ref=v_ref):
            q0s, kcs, kps, hps = [], [], [], []
            for n in range(UNITS):
                idx = it * UNITS + n
                q0 = idx // per_res + (idx % per_res) * (blk * dil)
                k_cur = t0 + q0
                has_prev = k_cur >= blk * dil
                q0s.append(q0)
                kcs.append(k_cur)
                hps.append(has_prev)
                kps.append(jnp.where(has_prev, k_cur - blk * dil, k_cur))
            ss = []
            for q0, kc, kp, hp in zip(q0s, kcs, kps, hps):
                k2 = jnp.concatenate([k_ref[0, rows(kp), :], k_ref[0, rows(kc), :]], axis=0).astype(BF16)
                s = lax.dot_general(q_ref[0, rows(q0), :].astype(BF16), k2, nt, preferred_element_type=F32)
                ss.append(s + jnp.concatenate([jnp.where(hp, bias_prev, NEG), bias_cur], axis=1))
            ms = [jnp.max(jnp.maximum(s[:, :blk], s[:, blk:]), axis=-1, keepdims=True) for s in ss]
            ps = [jnp.exp2(s - m).astype(BF16) for s, m in zip(ss, ms)]
            for q0, kc, kp, m, p in zip(q0s, kcs, kps, ms, ps):
                v2 = jnp.concatenate([v_ref[0, rows(kp), :], v_ref[0, rows(kc), :]], axis=0).astype(BF16)
                av = jnp.dot(p, jnp.concatenate([v2, jnp.ones_like(v2)], axis=1), preferred_element_type=F32)
                acc_ref[g, rows(q0), :] = av[:, :HEAD_DIM]
                l_ref[g, rows(q0), :] = av[:, HEAD_DIM:]
                m_ref[g, rows(q0), :] = jnp.broadcast_to(m, (blk, LANES))
            return carry

        lax.fori_loop(0, dil * per_res // UNITS, units, 0)

    step = 256

    def merge(c, carry):
        sl = pl.ds(pl.multiple_of(c * step, step), step)
        ms = [m_ref[g, sl, :] for g in range(ng)]
        m_all = functools.reduce(jnp.maximum, ms)
        ws = [jnp.exp2(m - m_all) for m in ms]
        den = sum(w * l_ref[g, sl, :] for g, w in enumerate(ws))
        for g, w in enumerate(ws):
            o_refs[g][0, sl, :] = (acc_ref[g, sl, :] * (w / den)).astype(BF16)
        return carry

    lax.fori_loop(0, sb_tokens // step, merge, 0)


def _dilated(qc, kc, vc):
    b, s, _ = qc.shape
    ng, hg = len(DIL_PATTERNS), DIL_HEADS_PER_GROUP
    sb_tokens = HEAD_DIM * max(dil for _, dil in DIL_PATTERNS)
    assert s % sb_tokens == 0 and all(win == HEAD_DIM * dil for win, dil in DIL_PATTERNS)
    head = lambda g: (lambda bb, j, sb: (bb, sb, g * hg + j))
    head_full = lambda g: (lambda bb, j, sb: (bb, 0, g * hg + j))
    q_specs = [pl.BlockSpec((1, sb_tokens, HEAD_DIM), head(g)) for g in range(ng)]
    kv_specs = [pl.BlockSpec((1, s, HEAD_DIM), head_full(g)) for g in range(ng)]
    out_spec = pl.BlockSpec((1, sb_tokens, HEAD_DIM), lambda bb, j, sb: (bb, sb, j))
    return pl.pallas_call(
        _dilated_body,
        grid=(b, hg, s // sb_tokens),
        in_specs=q_specs + kv_specs + kv_specs,
        out_specs=[out_spec] * ng,
        out_shape=[jax.ShapeDtypeStruct((b, s, hg * HEAD_DIM), BF16)] * ng,
        scratch_shapes=[pltpu.VMEM((ng, sb_tokens, HEAD_DIM), F32)] * 3,
        compiler_params=_params("parallel", "parallel", "arbitrary"),
        name="dilated",
    )(*([qc] * ng + [kc] * ng + [vc] * ng))


def _out_proj_body(*refs):
    *y_refs, w_ref, x_ref, o_ref, wb_ref = refs

    @pl.when(pl.program_id(1) == 0)
    def _():
        wb_ref[...] = w_ref[...].astype(BF16)

    y = jnp.concatenate([y_ref[...] for y_ref in y_refs], axis=1)
    o_ref[...] = x_ref[...] + jnp.dot(y, wb_ref[...], preferred_element_type=F32)


def _out_proj(parts, w, x, layer, tm, tn):
    t, d = x.shape
    assert sum(a.shape[1] for a in parts) == w.shape[1]
    return pl.pallas_call(
        _out_proj_body,
        grid=(d // tn, t // tm),
        in_specs=[pl.BlockSpec((tm, a.shape[1]), lambda n, m: (m, 0)) for a in parts]
        + [pl.BlockSpec((None, w.shape[1], tn), lambda n, m: (layer, 0, n)),
           pl.BlockSpec((tm, tn), lambda n, m: (m, n))],
        out_specs=pl.BlockSpec((tm, tn), lambda n, m: (m, n)),
        out_shape=jax.ShapeDtypeStruct((t, d), F32),
        scratch_shapes=[pltpu.VMEM((w.shape[1], tn), BF16)],
        compiler_params=_params("parallel", "arbitrary"),
        name="out_proj",
    )(*parts, w, x)


def _final_norm_body(x_ref, g_ref, o_ref):
    o_ref[...] = _rms(x_ref[...], g_ref[...])


def _final_norm(x, g, tm):
    t, d = x.shape
    return pl.pallas_call(
        _final_norm_body,
        grid=(t // tm,),
        in_specs=[pl.BlockSpec((tm, d), lambda m: (m, 0)), pl.BlockSpec((1, d), lambda m: (0, 0))],
        out_specs=pl.BlockSpec((tm, d), lambda m: (m, 0)),
        out_shape=jax.ShapeDtypeStruct((t, d), F32),
        compiler_params=_params("parallel"),
        name="final_norm",
    )(x, g)


def _align_w_in_body(x_ref, o_ref):
    o_wi = SEG_QI[1] + IDX_DIM
    o_qc = o_wi + IDX_HEADS
    rows = x_ref.shape[0]
    o_ref[:, :o_wi] = x_ref[:, :o_wi]
    o_ref[:, o_wi:SEG_WI[0]] = jnp.zeros((rows, SEG_WI[0] - o_wi), BF16)
    o_ref[:, SEG_WI[0]:SEG_WI[0] + IDX_HEADS] = x_ref[:, o_wi:o_qc]
    o_ref[:, SEG_WI[0] + IDX_HEADS:SEG_QC[0]] = jnp.zeros((rows, LANES - IDX_HEADS), BF16)
    o_ref[:, SEG_QC[0]:] = x_ref[:, o_qc:o_qc + D_IN_ALIGNED - SEG_QC[0]]


def _align_w_in(w, tr):
    depth, d, n = w.shape
    assert n - (SEG_QI[1] + IDX_DIM + IDX_HEADS) == D_IN_ALIGNED - SEG_QC[0]
    w = jnp.pad(w.astype(BF16), ((0, 0), (0, 0), (0, D_IN_ALIGNED - n)))
    return pl.pallas_call(
        _align_w_in_body,
        grid=(depth, d // tr),
        in_specs=[pl.BlockSpec((None, tr, D_IN_ALIGNED), lambda l, r: (l, r, 0))],
        out_specs=pl.BlockSpec((None, tr, D_IN_ALIGNED), lambda l, r: (l, r, 0)),
        out_shape=jax.ShapeDtypeStruct((depth, d, D_IN_ALIGNED), BF16),
        compiler_params=_params("parallel", "parallel"),
        name="align_w_in",
    )(w)


def _tile(n, want):
    while n % want:
        want //= 2
    return want


class _Tiles(NamedTuple):
    tm: int
    tn_ffn: int
    tn_out: int
    tm_proj: int
    tq: int
    tm_rope: int
    tr_align: int


def _tiles(t, s, d, f):
    return _Tiles(tm=_tile(t, 1024), tn_ffn=_tile(f, 512), tn_out=_tile(d, 1024), tm_proj=_tile(s, 512),
                  tq=_tile(s, 256), tm_rope=_tile(t, 256), tr_align=_tile(d, 256))


def kernel(x, positions, norm_ffn1, ffn1_gate, ffn1_up, ffn1_down, norm_mix, w_in, conv_w, w_out,
           norm_ffn2, ffn2_gate, ffn2_up, ffn2_down, norm_final):
    b, s, d = x.shape
    t = b * s
    depth = w_in.shape[0]
    tl = _tiles(t, s, d, ffn1_gate.shape[2])

    def ffn(xf, g, weights, layer):
        wg, wu, wd = weights
        h = _ffn_up(xf, g.reshape(depth, 1, d), wg, wu, layer, tl.tm, tl.tn_ffn)
        return _ffn_down(h, wd, xf, tl.tm, tl.tn_ffn)

    ffn1_stacks, ffn2_stacks = (ffn1_gate, ffn1_up, ffn1_down), (ffn2_gate, ffn2_up, ffn2_down)
    pos = positions.astype(F32).reshape(t, 1)
    tabs, w1 = _rope_tables(pos, tl.tm_rope, [(w, 0) for w in ffn1_stacks])
    w_in_al = _align_w_in(w_in, tl.tr_align)
    xf = x.reshape(t, d)
    for i in range(depth):
        xf = ffn(xf, norm_ffn1, w1, i)
        ya, qb, kb, vb, qi, ki, wi, qc, kc, vc = _in_proj(
            xf, norm_mix.reshape(depth, 1, d), w_in_al, conv_w, tabs, i, s, tl.tm_proj)
        r3 = lambda a: a.reshape(b, s, a.shape[-1])
        to_cast = [(w, i) for w in ffn2_stacks] + ([(w, i + 1) for w in ffn1_stacks] if i + 1 < depth else [])
        yb, copies = _dsa(qb, r3(kb), r3(vb), qi, r3(ki), wi, tl.tq, to_cast)
        w2, w1 = copies[:3], copies[3:]
        ycs = _dilated(r3(qc), r3(kc), r3(vc))
        parts = [ya, yb.reshape(t, -1)] + [yc.reshape(t, -1) for yc in ycs]
        xf = _out_proj(parts, w_out, xf, i, tl.tm, tl.tn_out)
        xf = ffn(xf, norm_ffn2, w2, i)
    return _final_norm(xf, norm_final.reshape(1, d), tl.tm).reshape(b, s, d)
```

```python
import functools
from typing import NamedTuple

import jax
import jax.numpy as jnp
from jax import lax
from jax.experimental import pallas as pl
from jax.experimental.pallas import tpu as pltpu

F32 = jnp.float32
BF16 = jnp.bfloat16
I32 = jnp.int32
I16 = jnp.int16
HALF16 = 1 << 15
PACKED_ROWS = 16
ONES_ROWS = PACKED_ROWS

HEAD_DIM = 128
CONV_CH = 512
CONV_WIDTH = 3
DSA_HEADS = 6
IDX_HEADS = 16
IDX_DIM = 64
TOPK_MAX = 256
DIL_PATTERNS = ((128, 1), (512, 4), (2048, 16))
DIL_HEADS_PER_GROUP = 2
DIL_HEADS = len(DIL_PATTERNS) * DIL_HEADS_PER_GROUP
ROPE_THETA = 10000.0
RMS_EPS = 1e-6

LANES = 128
SUBLANES = 8
VMEM_LIMIT = 56 * 1024 * 1024
NEG = -1e30
INT_MIN = -2 ** 31
LOG2E = 1.4426950408889634

SEG_CONV = (0, 3 * CONV_CH)
SEG_QB = (SEG_CONV[1], SEG_CONV[1] + DSA_HEADS * HEAD_DIM)
SEG_KV = (SEG_QB[1], SEG_QB[1] + 2 * HEAD_DIM)
SEG_QI = (SEG_KV[1], SEG_KV[1] + IDX_HEADS * IDX_DIM)
SEG_KI = (SEG_QI[1], SEG_QI[1] + LANES)
SEG_WI = (SEG_KI[1], SEG_KI[1] + LANES)
SEG_QC = (SEG_WI[1], SEG_WI[1] + DIL_HEADS * HEAD_DIM)
SEG_KC = (SEG_QC[1], SEG_QC[1] + DIL_HEADS * HEAD_DIM)
SEG_VC = (SEG_KC[1], SEG_KC[1] + DIL_HEADS * HEAD_DIM)
D_IN_ALIGNED = SEG_VC[1]


def _params(*sem):
    return pltpu.CompilerParams(dimension_semantics=sem, vmem_limit_bytes=VMEM_LIMIT)


def _rms(x, g):
    ms = jnp.mean(x * x, axis=-1, keepdims=True)
    return x * lax.rsqrt(ms + RMS_EPS) * g


def _ffn_up_body(x_ref, g_ref, wg_ref, wu_ref, h_ref, xn_ref):
    @pl.when(pl.program_id(1) == 0)
    def _():
        xn_ref[...] = _rms(x_ref[...], g_ref[...]).astype(BF16)

    xn = xn_ref[...]
    a = jnp.dot(xn, wg_ref[...].astype(BF16), preferred_element_type=F32)
    b = jnp.dot(xn, wu_ref[...].astype(BF16), preferred_element_type=F32)
    h_ref[...] = (a * jax.nn.sigmoid(a) * b).astype(BF16)


def _weight_tile(w, tn):
    if isinstance(w, tuple):
        stack, layer = w
        return stack, pl.BlockSpec((None, stack.shape[1], tn), lambda m, n: (layer, 0, n))
    return w, pl.BlockSpec((w.shape[0], tn), lambda m, n: (0, n))


def _ffn_up(x, g, wg, wu, layer, tm, tn):
    t, d = x.shape
    (wg, wg_spec), (wu, wu_spec) = _weight_tile(wg, tn), _weight_tile(wu, tn)
    f = wg.shape[-1]
    return pl.pallas_call(
        _ffn_up_body,
        grid=(t // tm, f // tn),
        in_specs=[
            pl.BlockSpec((tm, d), lambda m, n: (m, 0)),
            pl.BlockSpec((None, 1, d), lambda m, n: (layer, 0, 0)),
            wg_spec,
            wu_spec,
        ],
        out_specs=pl.BlockSpec((tm, tn), lambda m, n: (m, n)),
        out_shape=jax.ShapeDtypeStruct((t, f), BF16),
        scratch_shapes=[pltpu.VMEM((tm, d), BF16)],
        compiler_params=_params("parallel", "arbitrary"),
        name="ffn_up",
    )(x, g, wg, wu)


def _ffn_down_body(h_ref, w_ref, x_ref, o_ref):
    y = jnp.dot(h_ref[...], w_ref[...].astype(BF16), preferred_element_type=F32)
    o_ref[...] = x_ref[...] + 0.5 * y


def _ffn_down(h, wd, x, tm, tn):
    t, f = h.shape
    wd, wd_spec = _weight_tile(wd, tn)
    d = wd.shape[-1]
    return pl.pallas_call(
        _ffn_down_body,
        grid=(t // tm, d // tn),
        in_specs=[
            pl.BlockSpec((tm, f), lambda m, n: (m, 0)),
            wd_spec,
            pl.BlockSpec((tm, tn), lambda m, n: (m, n)),
        ],
        out_specs=pl.BlockSpec((tm, tn), lambda m, n: (m, n)),
        out_shape=jax.ShapeDtypeStruct((t, d), F32),
        compiler_params=_params("parallel", "arbitrary"),
        name="ffn_down",
    )(h, wd, x)


def _cast_block_specs(w, layer, nsteps, step_of):
    _, rows, cols = w.shape
    nblk = nsteps
    while rows % nblk or (rows // nblk) % PACKED_ROWS:
        nblk //= 2
    blk = rows // nblk
    return (pl.BlockSpec((None, blk, cols), lambda *g: (layer, step_of(*g) * nblk // nsteps, 0)),
            pl.BlockSpec((blk, cols), lambda *g: (step_of(*g) * nblk // nsteps, 0)),
            jax.ShapeDtypeStruct((rows, cols), BF16))


def _rope_table_body(pos_ref, inv_ref, ch_ref, sh_ref, ci_ref, si_ref):
    ang = pos_ref[...] * inv_ref[...]
    cos, sin = jnp.cos(ang), jnp.sin(ang)
    hh, hi = HEAD_DIM // 2, IDX_DIM // 2
    ch_ref[...] = jnp.concatenate([cos[:, :hh]] * 2, axis=1)
    sh_ref[...] = jnp.concatenate([-sin[:, :hh], sin[:, :hh]], axis=1)
    ci_ref[...] = jnp.concatenate([cos[:, hh:hh + hi]] * (LANES // hi), axis=1)
    si_ref[...] = jnp.concatenate([-sin[:, hh:hh + hi], sin[:, hh:hh + hi]] * (LANES // IDX_DIM), axis=1)


def _rope_tables(pos, tm):
    t = pos.shape[0]
    inv = lambda dim: 1.0 / (ROPE_THETA ** (jnp.arange(0, dim, 2, dtype=F32) / dim))
    inv_l = jnp.concatenate([inv(HEAD_DIM), inv(IDX_DIM), jnp.zeros((LANES - (HEAD_DIM + IDX_DIM) // 2,), F32)])
    tab = pl.BlockSpec((tm, LANES), lambda m: (m, 0))
    return pl.pallas_call(
        _rope_table_body,
        grid=(t // tm,),
        in_specs=[pl.BlockSpec((tm, 1), lambda m: (m, 0)), pl.BlockSpec((1, LANES), lambda m: (0, 0))],
        out_specs=[tab] * 4,
        out_shape=[jax.ShapeDtypeStruct((t, LANES), F32)] * 4,
        compiler_params=_params("parallel"),
        name="rope_tables",
    )(pos, inv_l.reshape(1, LANES))


def _rope128(x, cos, sin):
    return x * cos + pltpu.roll(x, HEAD_DIM // 2, 1) * sin


def _rope64(x, cos, sin, lo_half):
    partner = jnp.where(lo_half, pltpu.roll(x, LANES - IDX_DIM // 2, 1), pltpu.roll(x, IDX_DIM // 2, 1))
    return x * cos + partner * sin


def _in_proj_body(seq_tiles, x_ref, g_ref, w_ref, cw_ref, ch_ref, sh_ref, ci_ref, si_ref,
                  ya_ref, qb_ref, kb_ref, vb_ref, qi_ref, ki_ref, wi_ref, qc_ref, kc_ref, vc_ref,
                  u_ref):
    tm = x_ref.shape[0]
    xn = _rms(x_ref[...], g_ref[...]).astype(BF16)

    def proj(seg):
        return jnp.dot(xn, w_ref[:, seg[0]:seg[1]], preferred_element_type=F32)

    ch, sh, ci, si = ch_ref[...], sh_ref[...], ci_ref[...], si_ref[...]
    scale = HEAD_DIM ** -0.5 * LOG2E

    def rope_heads(p, n, mul):
        return jnp.concatenate(
            [_rope128(p[:, j * LANES:(j + 1) * LANES], ch, sh) * mul for j in range(n)], axis=1)

    qb_ref[...] = rope_heads(proj(SEG_QB), DSA_HEADS, scale).T.astype(BF16)
    p = proj(SEG_KV)
    kb_ref[...] = _rope128(p[:, :HEAD_DIM], ch, sh).astype(BF16)
    vb_ref[...] = p[:, HEAD_DIM:].astype(BF16)

    lo_half = lax.broadcasted_iota(I32, (tm, LANES), 1) % IDX_DIM < IDX_DIM // 2
    p = proj(SEG_QI)
    qi_ref[...] = jnp.concatenate(
        [_rope64(p[:, j * LANES:(j + 1) * LANES], ci, si, lo_half) * (IDX_DIM ** -0.5)
         for j in range(IDX_HEADS * IDX_DIM // LANES)], axis=1).T.astype(BF16)
    ki_ref[...] = _rope64(proj(SEG_KI), ci, si, lo_half)[:, :IDX_DIM].astype(BF16)
    wi_ref[...] = (proj(SEG_WI) * (IDX_HEADS ** -0.5)).T[:IDX_HEADS, :]

    qc_ref[...] = rope_heads(proj(SEG_QC), DIL_HEADS, scale)
    kc_ref[...] = rope_heads(proj(SEG_KC), DIL_HEADS, 1.0)

    h, gate_b, gate_c = (proj((SEG_CONV[0] + j * CONV_CH, SEG_CONV[0] + (j + 1) * CONV_CH)) for j in range(3))
    u = gate_c * h

    carry = jnp.where(pl.program_id(0) % seq_tiles == 0, 0.0, u_ref[...])
    ext = jnp.concatenate([carry, u], axis=0)
    cw = cw_ref[...]
    y = (cw[2:3, :] * u + cw[1:2, :] * pltpu.roll(ext, 1, 0)[SUBLANES:, :]
         + cw[0:1, :] * pltpu.roll(ext, 2, 0)[SUBLANES:, :])
    u_ref[...] = u[tm - SUBLANES:, :]
    ya_ref[...] = (gate_b * y).astype(BF16)

    vc_ref[...] = proj(SEG_VC)


def _in_proj(x, g, w, cw, tabs, layer, seq, tm):
    t, d = x.shape
    n = w.shape[2]
    row = lambda width: pl.BlockSpec((tm, width), lambda m: (m, 0))
    col = lambda height: pl.BlockSpec((height, tm), lambda m: (0, m))
    const = lambda shape: pl.BlockSpec((None,) + shape, lambda m: (layer, 0, 0))
    outs = [(CONV_CH, BF16, True), (DSA_HEADS * HEAD_DIM, BF16, False), (HEAD_DIM, BF16, True),
            (HEAD_DIM, BF16, True), (IDX_HEADS * IDX_DIM, BF16, False), (IDX_DIM, BF16, True),
            (IDX_HEADS, F32, False), (DIL_HEADS * HEAD_DIM, F32, True), (DIL_HEADS * HEAD_DIM, F32, True),
            (DIL_HEADS * HEAD_DIM, F32, True)]
    return pl.pallas_call(
        functools.partial(_in_proj_body, seq // tm),
        grid=(t // tm,),
        in_specs=[row(d), const((1, d)),
                  pl.BlockSpec((None, d, n), lambda m: (layer, 0, 0), pipeline_mode=pl.Buffered(1)),
                  const((CONV_WIDTH, CONV_CH)), row(LANES), row(LANES), row(LANES), row(LANES)],
        out_specs=[row(wd) if tok else col(wd) for wd, _, tok in outs],
        out_shape=[jax.ShapeDtypeStruct((t, wd) if tok else (wd, t), dt) for wd, dt, tok in outs],
        scratch_shapes=[pltpu.VMEM((SUBLANES, CONV_CH), F32)],
        compiler_params=_params("arbitrary"),
        name="in_proj",
    )(x, g, w, cw, *tabs)


def _dsa_body(topk, ncast, qbt_in_ref, kb_ref, vb_ref, qt_ref, ki_ref, wt_ref, *rest):
    cast_in, (o_ref, *cast_out) = rest[:ncast], rest[ncast:2 * ncast + 1]
    keys_ref, hi_ref, lo_ref, lo2_ref, qbt_ref, vt_ref, acc_ref = rest[2 * ncast + 1:]
    tq = o_ref.shape[1]
    seq = kb_ref.shape[1]

    for src_ref, dst_ref in zip(cast_in, cast_out):
        dst_ref[...] = src_ref[...].astype(BF16)

    nh = DSA_HEADS
    i = pl.program_id(1)
    nch = i + 1

    def chunk(c):
        return pl.ds(pl.multiple_of(c * tq, tq), tq)

    def transpose_bf16(a):
        return a.astype(F32).T.astype(BF16)

    @pl.when(i == 0)
    def _():
        for c2 in range(seq // (2 * tq)):
            vt_ref[c2, :HEAD_DIM, :] = transpose_bf16(vb_ref[0, c2 * 2 * tq:(c2 + 1) * 2 * tq, :])
            vt_ref[c2, HEAD_DIM:, :] = jnp.ones((ONES_ROWS, 2 * tq), BF16)

    for h in range(nh):
        qbt_ref[:, h * tq:(h + 1) * tq] = qbt_in_ref[h * HEAD_DIM:(h + 1) * HEAD_DIM, :]
    wt = wt_ref[...]
    krow = lax.broadcasted_iota(I32, (tq, tq), 0)

    def pair(c2):
        return pl.ds(pl.multiple_of(c2 * (2 * tq), 2 * tq), 2 * tq)

    def score_rows(rows, n, first_key):
        kc = ki_ref[0, rows, :]
        acc = jnp.zeros((n, tq), F32)
        for h in range(IDX_HEADS):
            lg = jnp.dot(kc, qt_ref[h * IDX_DIM:(h + 1) * IDX_DIM, :], preferred_element_type=F32)
            acc = acc + jnp.maximum(lg, 0.0) * wt[h:h + 1, :]
        bits = pltpu.bitcast(acc, I32)
        key = bits ^ ((bits >> 31) & 0x7FFFFFFF)
        causal = lax.broadcasted_iota(I32, (n, tq), 0) + first_key <= lax.broadcasted_iota(I32, (n, tq), 1) + i * tq
        key = jnp.where(causal, key, INT_MIN)
        keys_ref[rows, :] = key
        hi_ref[rows, :] = (key >> 16).astype(I16)
        lo_ref[rows, :] = ((key & 0xFFFF) - HALF16).astype(I16)

    def score_quad(c4, carry):
        score_rows(pl.ds(pl.multiple_of(c4 * (4 * tq), 4 * tq), 4 * tq), 4 * tq, c4 * (4 * tq))
        return carry

    lax.fori_loop(0, nch // 4, score_quad, 0)

    @pl.when(nch % 4 >= 2)
    def _():
        score_rows(pair(nch // 4 * 2), 2 * tq, nch // 4 * (4 * tq))

    npair = (nch + 1) // 2

    @pl.when(nch % 2 == 1)
    def _():
        score_rows(chunk(nch - 1), tq, (nch - 1) * tq)
        keys_ref[chunk(nch), :] = jnp.full((tq, tq), INT_MIN, I32)
        hi_ref[chunk(nch), :] = jnp.full((tq, tq), -HALF16, I16)
        lo_ref[chunk(nch), :] = jnp.full((tq, tq), -HALF16, I16)

    def count(pred):
        def body(c, acc):
            hit = pred(keys_ref[chunk(c), :], krow + c * tq).astype(I32)
            return acc + jnp.sum(hit.reshape(tq // 8, 8, tq), axis=0)
        acc = lax.fori_loop(0, nch, body, jnp.zeros((8, tq), I32))
        return jnp.sum(acc, axis=0, keepdims=True)

    def count16(ref, pred):
        def body(c2, acc):
            hit = jnp.where(pred(ref[pair(c2), :]), jnp.ones((), BF16), jnp.zeros((), BF16))
            parts = [hit[r * PACKED_ROWS:(r + 1) * PACKED_ROWS] for r in range(2 * tq // PACKED_ROWS)]
            while len(parts) > 1:
                parts = [parts[j] + parts[j + 1] for j in range(0, len(parts), 2)]
            return acc + parts[0].astype(F32)
        acc = lax.fori_loop(0, npair, body, jnp.zeros((PACKED_ROWS, tq), F32))
        return jnp.sum(acc, axis=0, keepdims=True)

    def kth_largest16(ref, k):
        def search_bit(b, carry):
            t_u, n_gt, n_ge = carry
            cand = t_u | (jnp.int32(1) << (15 - b))
            cand16 = (cand - HALF16).astype(I16)
            n = count16(ref, lambda v: v >= cand16)
            ok = n >= k
            return jnp.where(ok, cand, t_u), jnp.where(ok, n_gt, n), jnp.where(ok, n, n_ge)
        walked = (npair * (2 * tq)).astype(F32)
        init = (jnp.zeros((1, tq), I32), jnp.zeros((1, tq), F32), jnp.zeros((1, tq), F32) + walked)
        t_u, n_gt, n_ge = lax.fori_loop(0, 16, search_bit, init)
        return t_u - HALF16, n_gt, n_ge

    t_hi, n_gt_hi, _ = kth_largest16(hi_ref, float(topk))
    t_hi16 = t_hi.astype(I16)
    need_lo = float(topk) - n_gt_hi

    def bucket(c2, carry):
        lo2_ref[pair(c2), :] = jnp.where(hi_ref[pair(c2), :] == t_hi16, lo_ref[pair(c2), :],
                                         jnp.full((), -HALF16, I16))
        return carry

    lax.fori_loop(0, npair, bucket, 0)
    t_lo, _, n_ge_lo = kth_largest16(lo2_ref, need_lo)
    thr = jnp.maximum((t_hi << 16) | (t_lo + HALF16), INT_MIN + 1)
    n_ge = jnp.where(t_hi > -HALF16, n_gt_hi + n_ge_lo, 0.0)

    idx_bits = seq.bit_length() - 1

    @pl.when(jnp.max(n_ge) > topk)
    def _():
        need = topk - count(lambda k, _: k > thr)
        def bit(b, j):
            cand = j | (jnp.int32(1) << (idx_bits - 1 - b))
            below = count(lambda k, idx: (k == thr) & (idx < cand))
            return jnp.where(below < need, cand, j)
        bound = lax.fori_loop(0, idx_bits, bit, jnp.zeros((1, tq), I32))
        bound = jnp.where(n_ge > topk, bound, seq)

        def demote(c, carry):
            k = keys_ref[chunk(c), :]
            keys_ref[chunk(c), :] = jnp.where((k == thr) & (krow + c * tq > bound), thr - 1, k)
            return carry

        lax.fori_loop(0, nch, demote, 0)

    acc_ref[...] = jnp.zeros(acc_ref.shape, F32)

    def attend(first_pair, npairs, m):
        n = npairs * 2 * tq
        rows = pl.ds(pl.multiple_of(first_pair * (2 * tq), 2 * tq), n)
        bias = jnp.where(keys_ref[rows, :] >= thr, 0.0, NEG)
        st = jnp.dot(kb_ref[0, rows, :], qbt_ref[...], preferred_element_type=F32)
        st = st + jnp.concatenate([bias] * nh, axis=1)
        m_new = jnp.maximum(m, jnp.max(st, axis=0, keepdims=True))
        p = jnp.exp2(st - m_new).astype(BF16)
        pv = sum(jnp.dot(vt_ref[first_pair + j], p[j * 2 * tq:(j + 1) * 2 * tq], preferred_element_type=F32)
                 for j in range(npairs))
        acc_ref[...] = acc_ref[...] * jnp.exp2(m - m_new) + pv
        return m_new

    m = lax.fori_loop(0, npair // 2, lambda c4, m: attend(2 * c4, 2, m), jnp.full((1, nh * tq), NEG, F32))

    @pl.when(npair % 2 == 1)
    def _():
        attend(npair - 1, 1, m)

    out = acc_ref[:HEAD_DIM, :] / acc_ref[HEAD_DIM:HEAD_DIM + 1, :]
    for h in range(nh):
        o_ref[0, :, h * HEAD_DIM:(h + 1) * HEAD_DIM] = out[:, h * tq:(h + 1) * tq].T.astype(BF16)


def _dsa(qbt, kb, vb, qit, ki, wit, tq, to_cast):
    b, s, _ = kb.shape
    topk = min(TOPK_MAX, s // 4)
    assert s % (2 * tq) == 0
    assert 2 * tq // PACKED_ROWS <= 256
    nq = s // tq
    blk = lambda width: pl.BlockSpec((1, tq, width), lambda bb, i: (bb, i, 0))
    full = lambda width: pl.BlockSpec((1, s, width), lambda bb, i: (bb, 0, 0))
    qcol = lambda a: pl.BlockSpec((a.shape[0], tq), lambda bb, i: (0, bb * nq + i))
    casts = [_cast_block_specs(w, layer, b * nq, lambda bb, i: bb * nq + i) for w, layer in to_cast]
    out, *copies = pl.pallas_call(
        functools.partial(_dsa_body, topk, len(casts)),
        grid=(b, nq),
        in_specs=[qcol(qbt), full(HEAD_DIM), full(HEAD_DIM), qcol(qit), full(IDX_DIM), qcol(wit)]
        + [c[0] for c in casts],
        out_specs=[blk(DSA_HEADS * HEAD_DIM)] + [c[1] for c in casts],
        out_shape=[jax.ShapeDtypeStruct((b, s, DSA_HEADS * HEAD_DIM), BF16)] + [c[2] for c in casts],
        scratch_shapes=[pltpu.VMEM((s, tq), I32),
                        pltpu.VMEM((s, tq), I16),
                        pltpu.VMEM((s, tq), I16),
                        pltpu.VMEM((s, tq), I16),
                        pltpu.VMEM((HEAD_DIM, DSA_HEADS * tq), BF16),
                        pltpu.VMEM((s // (2 * tq), HEAD_DIM + ONES_ROWS, 2 * tq), BF16),
                        pltpu.VMEM((HEAD_DIM + ONES_ROWS, DSA_HEADS * tq), F32)],
        compiler_params=_params("parallel", "arbitrary"),
        name="dsa",
    )(qbt, kb, vb, qit, ki, wit, *[w for w, _ in to_cast])
    return out, copies


def _dilated_body(*refs):
    ng = len(DIL_PATTERNS)
    q_refs, k_refs, v_refs = refs[0:ng], refs[ng:2 * ng], refs[2 * ng:3 * ng]
    o_refs = refs[3 * ng:4 * ng]
    acc_ref, m_ref, l_ref = refs[4 * ng:]
    sb_tokens = q_refs[0].shape[1]
    t0 = pl.program_id(2) * sb_tokens
    blk = HEAD_DIM
    UNITS = 4
    rq = lax.broadcasted_iota(I32, (blk, blk), 0)
    ck = lax.broadcasted_iota(I32, (blk, blk), 1)
    bias_cur = jnp.where(ck <= rq, 0.0, NEG)
    bias_prev = jnp.where(ck >= rq, 0.0, NEG)
    nt = (((1,), (1,)), ((), ()))

    for g, (win, dil) in enumerate(DIL_PATTERNS):
        q_ref, k_ref, v_ref = q_refs[g], k_refs[g], v_refs[g]
        per_res = sb_tokens // dil // blk

        def rows(start, dil=dil):
            return pl.ds(start, blk, stride=dil) if dil > 1 else pl.ds(start, blk)

        def units(it, carry, g=g, dil=dil, per_res=per_res, rows=rows,
                  q_ref=q_ref, k_ref=k_ref, v_ref=v_ref):
            q0s, kcs, kps, hps = [], [], [], []
            for n in range(UNITS):
                idx = it * UNITS + n
                q0 = idx // per_res + (idx % per_res) * (blk * dil)
                k_cur = t0 + q0
                has_prev = k_cur >= blk * dil
                q0s.append(q0)
                kcs.append(k_cur)
                hps.append(has_prev)
                kps.append(jnp.where(has_prev, k_cur - blk * dil, k_cur))
            ss = []
            for q0, kc, kp, hp in zip(q0s, kcs, kps, hps):
                k2 = jnp.concatenate([k_ref[0, rows(kp), :], k_ref[0, rows(kc), :]], axis=0).astype(BF16)
                s = lax.dot_general(q_ref[0, rows(q0), :].astype(BF16), k2, nt, preferred_element_type=F32)
                ss.append(s + jnp.concatenate([jnp.where(hp, bias_prev, NEG), bias_cur], axis=1))
            ms = [jnp.max(jnp.maximum(s[:, :blk], s[:, blk:]), axis=-1, keepdims=True) for s in ss]
            ps = [jnp.exp2(s - m).astype(BF16) for s, m in zip(ss, ms)]
            for q0, kc, kp, m, p in zip(q0s, kcs, kps, ms, ps):
                v2 = jnp.concatenate([v_ref[0, rows(kp), :], v_ref[0, rows(kc), :]], axis=0).astype(BF16)
                av = jnp.dot(p, jnp.concatenate([v2, jnp.ones_like(v2)], axis=1), preferred_element_type=F32)
                acc_ref[g, rows(q0), :] = av[:, :HEAD_DIM]
                l_ref[g, rows(q0), :] = av[:, HEAD_DIM:]
                m_ref[g, rows(q0), :] = jnp.broadcast_to(m, (blk, LANES))
            return carry

        lax.fori_loop(0, dil * per_res // UNITS, units, 0)

    step = 256

    def merge(c, carry):
        sl = pl.ds(pl.multiple_of(c * step, step), step)
        ms = [m_ref[g, sl, :] for g in range(ng)]
        m_all = functools.reduce(jnp.maximum, ms)
        ws = [jnp.exp2(m - m_all) for m in ms]
        den = sum(w * l_ref[g, sl, :] for g, w in enumerate(ws))
        for g, w in enumerate(ws):
            o_refs[g][0, sl, :] = (acc_ref[g, sl, :] * (w / den)).astype(BF16)
        return carry

    lax.fori_loop(0, sb_tokens // step, merge, 0)


def _dilated(qc, kc, vc):
    b, s, _ = qc.shape
    ng, hg = len(DIL_PATTERNS), DIL_HEADS_PER_GROUP
    sb_tokens = HEAD_DIM * max(dil for _, dil in DIL_PATTERNS)
    assert s % sb_tokens == 0 and all(win == HEAD_DIM * dil for win, dil in DIL_PATTERNS)
    head = lambda g: (lambda bb, j, sb: (bb, sb, g * hg + j))
    head_full = lambda g: (lambda bb, j, sb: (bb, 0, g * hg + j))
    q_specs = [pl.BlockSpec((1, sb_tokens, HEAD_DIM), head(g)) for g in range(ng)]
    kv_specs = [pl.BlockSpec((1, s, HEAD_DIM), head_full(g)) for g in range(ng)]
    out_spec = pl.BlockSpec((1, sb_tokens, HEAD_DIM), lambda bb, j, sb: (bb, sb, j))
    return pl.pallas_call(
        _dilated_body,
        grid=(b, hg, s // sb_tokens),
        in_specs=q_specs + kv_specs + kv_specs,
        out_specs=[out_spec] * ng,
        out_shape=[jax.ShapeDtypeStruct((b, s, hg * HEAD_DIM), BF16)] * ng,
        scratch_shapes=[pltpu.VMEM((ng, sb_tokens, HEAD_DIM), F32)] * 3,
        compiler_params=_params("parallel", "parallel", "arbitrary"),
        name="dilated",
    )(*([qc] * ng + [kc] * ng + [vc] * ng))


def _out_proj_body(*refs):
    *y_refs, w_ref, x_ref, o_ref, wb_ref = refs

    @pl.when(pl.program_id(1) == 0)
    def _():
        wb_ref[...] = w_ref[...].astype(BF16)

    y = jnp.concatenate([y_ref[...] for y_ref in y_refs], axis=1)
    o_ref[...] = x_ref[...] + jnp.dot(y, wb_ref[...], preferred_element_type=F32)


def _out_proj(parts, w, x, layer, tm, tn):
    t, d = x.shape
    assert sum(a.shape[1] for a in parts) == w.shape[1]
    return pl.pallas_call(
        _out_proj_body,
        grid=(d // tn, t // tm),
        in_specs=[pl.BlockSpec((tm, a.shape[1]), lambda n, m: (m, 0)) for a in parts]
        + [pl.BlockSpec((None, w.shape[1], tn), lambda n, m: (layer, 0, n)),
           pl.BlockSpec((tm, tn), lambda n, m: (m, n))],
        out_specs=pl.BlockSpec((tm, tn), lambda n, m: (m, n)),
        out_shape=jax.ShapeDtypeStruct((t, d), F32),
        scratch_shapes=[pltpu.VMEM((w.shape[1], tn), BF16)],
        compiler_params=_params("parallel", "arbitrary"),
        name="out_proj",
    )(*parts, w, x)


def _final_norm_body(x_ref, g_ref, o_ref):
    o_ref[...] = _rms(x_ref[...], g_ref[...])


def _final_norm(x, g, tm):
    t, d = x.shape
    return pl.pallas_call(
        _final_norm_body,
        grid=(t // tm,),
        in_specs=[pl.BlockSpec((tm, d), lambda m: (m, 0)), pl.BlockSpec((1, d), lambda m: (0, 0))],
        out_specs=pl.BlockSpec((tm, d), lambda m: (m, 0)),
        out_shape=jax.ShapeDtypeStruct((t, d), F32),
        compiler_params=_params("parallel"),
        name="final_norm",
    )(x, g)


def _align_w_in_body(x_ref, o_ref):
    o_wi = SEG_QI[1] + IDX_DIM
    o_qc = o_wi + IDX_HEADS
    rows = x_ref.shape[0]
    o_ref[:, :o_wi] = x_ref[:, :o_wi]
    o_ref[:, o_wi:SEG_WI[0]] = jnp.zeros((rows, SEG_WI[0] - o_wi), BF16)
    o_ref[:, SEG_WI[0]:SEG_WI[0] + IDX_HEADS] = x_ref[:, o_wi:o_qc]
    o_ref[:, SEG_WI[0] + IDX_HEADS:SEG_QC[0]] = jnp.zeros((rows, LANES - IDX_HEADS), BF16)
    o_ref[:, SEG_QC[0]:] = x_ref[:, o_qc:o_qc + D_IN_ALIGNED - SEG_QC[0]]


def _align_w_in(w, tr):
    depth, d, n = w.shape
    assert n - (SEG_QI[1] + IDX_DIM + IDX_HEADS) == D_IN_ALIGNED - SEG_QC[0]
    w = jnp.pad(w.astype(BF16), ((0, 0), (0, 0), (0, D_IN_ALIGNED - n)))
    return pl.pallas_call(
        _align_w_in_body,
        grid=(depth, d // tr),
        in_specs=[pl.BlockSpec((None, tr, D_IN_ALIGNED), lambda l, r: (l, r, 0))],
        out_specs=pl.BlockSpec((None, tr, D_IN_ALIGNED), lambda l, r: (l, r, 0)),
        out_shape=jax.ShapeDtypeStruct((depth, d, D_IN_ALIGNED), BF16),
        compiler_params=_params("parallel", "parallel"),
        name="align_w_in",
    )(w)


def _tile(n, want):
    while n % want:
        want //= 2
    return want


class _Tiles(NamedTuple):
    tm: int
    tn_ffn: int
    tn_out: int
    tm_proj: int
    tn_down_f32: int
    tq: int
    tr_align: int


def _tiles(t, s, d, f):
    return _Tiles(tm=_tile(t, 1024), tn_ffn=_tile(f, 512), tn_out=_tile(d, 1024), tm_proj=_tile(s, 512),
                  tn_down_f32=_tile(d, 256), tq=_tile(s, 256), tr_align=_tile(d, 256))


def kernel(x, positions, norm_ffn1, ffn1_gate, ffn1_up, ffn1_down, norm_mix, w_in, conv_w, w_out,
           norm_ffn2, ffn2_gate, ffn2_up, ffn2_down, norm_final):
    b, s, d = x.shape
    t = b * s
    depth = w_in.shape[0]
    tl = _tiles(t, s, d, ffn1_gate.shape[2])

    def ffn(xf, g, weights, layer):
        wg, wu, wd = weights
        h = _ffn_up(xf, g.reshape(depth, 1, d), wg, wu, layer, tl.tm, tl.tn_ffn)
        return _ffn_down(h, wd, xf, tl.tm, tl.tn_down_f32 if isinstance(wd, tuple) else tl.tn_ffn)

    ffn1_stacks, ffn2_stacks = (ffn1_gate, ffn1_up, ffn1_down), (ffn2_gate, ffn2_up, ffn2_down)
    w1 = [(w, 0) for w in ffn1_stacks]
    pos = positions.astype(F32).reshape(t, 1)
    tabs = _rope_tables(pos, tl.tm)
    w_in_al = _align_w_in(w_in, tl.tr_align)
    xf = x.reshape(t, d)
    for i in range(depth):
        xf = ffn(xf, norm_ffn1, w1, i)
        ya, qb, kb, vb, qi, ki, wi, qc, kc, vc = _in_proj(
            xf, norm_mix.reshape(depth, 1, d), w_in_al, conv_w, tabs, i, s, tl.tm_proj)
        r3 = lambda a: a.reshape(b, s, a.shape[-1])
        to_cast = [(w, i) for w in ffn2_stacks] + ([(w, i + 1) for w in ffn1_stacks] if i + 1 < depth else [])
        yb, copies = _dsa(qb, r3(kb), r3(vb), qi, r3(ki), wi, tl.tq, to_cast)
        w2, w1 = copies[:3], copies[3:]
        ycs = _dilated(r3(qc), r3(kc), r3(vc))
        parts = [ya, yb.reshape(t, -1)] + [yc.reshape(t, -1) for yc in ycs]
        xf = _out_proj(parts, w_out, xf, i, tl.tm, tl.tn_out)
        xf = ffn(xf, norm_ffn2, w2, i)
    return _final_norm(xf, norm_final.reshape(1, d), tl.tm).reshape(b, s, d)
```

```python
import functools
from typing import NamedTuple

import jax
import jax.numpy as jnp
from jax import lax
from jax.experimental import pallas as pl
from jax.experimental.pallas import tpu as pltpu

F32 = jnp.float32
BF16 = jnp.bfloat16
I32 = jnp.int32
I16 = jnp.int16
HALF16 = 1 << 15
PACKED_ROWS = 16
ONES_ROWS = PACKED_ROWS

HEAD_DIM = 128
CONV_CH = 512
CONV_WIDTH = 3
DSA_HEADS = 6
IDX_HEADS = 16
IDX_DIM = 64
TOPK_MAX = 256
DIL_PATTERNS = ((128, 1), (512, 4), (2048, 16))
DIL_HEADS_PER_GROUP = 2
DIL_HEADS = len(DIL_PATTERNS) * DIL_HEADS_PER_GROUP
ROPE_THETA = 10000.0
RMS_EPS = 1e-6

LANES = 128
SUBLANES = 8
VMEM_LIMIT = 56 * 1024 * 1024
NEG = -1e30
INT_MIN = -2 ** 31
LOG2E = 1.4426950408889634

SEG_CONV = (0, 3 * CONV_CH)
SEG_QB = (SEG_CONV[1], SEG_CONV[1] + DSA_HEADS * HEAD_DIM)
SEG_KV = (SEG_QB[1], SEG_QB[1] + 2 * HEAD_DIM)
SEG_QI = (SEG_KV[1], SEG_KV[1] + IDX_HEADS * IDX_DIM)
SEG_KI = (SEG_QI[1], SEG_QI[1] + LANES)
SEG_WI = (SEG_KI[1], SEG_KI[1] + LANES)
SEG_QC = (SEG_WI[1], SEG_WI[1] + DIL_HEADS * HEAD_DIM)
SEG_KC = (SEG_QC[1], SEG_QC[1] + DIL_HEADS * HEAD_DIM)
SEG_VC = (SEG_KC[1], SEG_KC[1] + DIL_HEADS * HEAD_DIM)
D_IN_ALIGNED = SEG_VC[1]


def _params(*sem):
    return pltpu.CompilerParams(dimension_semantics=sem, vmem_limit_bytes=VMEM_LIMIT)


def _rms(x, g):
    ms = jnp.mean(x * x, axis=-1, keepdims=True)
    return x * lax.rsqrt(ms + RMS_EPS) * g


def _ffn_up_body(x_ref, g_ref, wg_ref, wu_ref, h_ref, xn_ref):
    @pl.when(pl.program_id(1) == 0)
    def _():
        xn_ref[...] = _rms(x_ref[...], g_ref[...]).astype(BF16)

    xn = xn_ref[...]
    a = jnp.dot(xn, wg_ref[...].astype(BF16), preferred_element_type=F32)
    b = jnp.dot(xn, wu_ref[...].astype(BF16), preferred_element_type=F32)
    h_ref[...] = (a * jax.nn.sigmoid(a) * b).astype(BF16)


def _weight_tile(w, tn):
    if isinstance(w, tuple):
        stack, layer = w
        return stack, pl.BlockSpec((None, stack.shape[1], tn), lambda m, n: (layer, 0, n))
    return w, pl.BlockSpec((w.shape[0], tn), lambda m, n: (0, n))


def _ffn_up(x, g, wg, wu, layer, tm, tn):
    t, d = x.shape
    (wg, wg_spec), (wu, wu_spec) = _weight_tile(wg, tn), _weight_tile(wu, tn)
    f = wg.shape[-1]
    return pl.pallas_call(
        _ffn_up_body,
        grid=(t // tm, f // tn),
        in_specs=[
            pl.BlockSpec((tm, d), lambda m, n: (m, 0)),
            pl.BlockSpec((None, 1, d), lambda m, n: (layer, 0, 0)),
            wg_spec,
            wu_spec,
        ],
        out_specs=pl.BlockSpec((tm, tn), lambda m, n: (m, n)),
        out_shape=jax.ShapeDtypeStruct((t, f), BF16),
        scratch_shapes=[pltpu.VMEM((tm, d), BF16)],
        compiler_params=_params("parallel", "arbitrary"),
        name="ffn_up",
    )(x, g, wg, wu)


def _ffn_down_body(h_ref, w_ref, x_ref, o_ref):
    y = jnp.dot(h_ref[...], w_ref[...].astype(BF16), preferred_element_type=F32)
    o_ref[...] = x_ref[...] + 0.5 * y


def _ffn_down(h, wd, x, tm, tn):
    t, f = h.shape
    if not isinstance(wd, tuple):
        return _ffn_down_wide(h, wd, x, tm, tn)
    wd, wd_spec = _weight_tile(wd, tn)
    d = wd.shape[-1]
    return pl.pallas_call(
        _ffn_down_body,
        grid=(t // tm, d // tn),
        in_specs=[
            pl.BlockSpec((tm, f), lambda m, n: (m, 0)),
            wd_spec,
            pl.BlockSpec((tm, tn), lambda m, n: (m, n)),
        ],
        out_specs=pl.BlockSpec((tm, tn), lambda m, n: (m, n)),
        out_shape=jax.ShapeDtypeStruct((t, d), F32),
        compiler_params=_params("parallel", "arbitrary"),
        name="ffn_down",
    )(h, wd, x)


def _ffn_down_wide(h, wd, x, tm, tn):
    t, f = h.shape
    d = wd.shape[1]
    return pl.pallas_call(
        _ffn_down_body,
        grid=(d // tn, t // tm),
        in_specs=[
            pl.BlockSpec((tm, f), lambda n, m: (m, 0)),
            pl.BlockSpec((f, tn), lambda n, m: (0, n), pipeline_mode=pl.Buffered(1)),
            pl.BlockSpec((tm, tn), lambda n, m: (m, n)),
        ],
        out_specs=pl.BlockSpec((tm, tn), lambda n, m: (m, n)),
        out_shape=jax.ShapeDtypeStruct((t, d), F32),
        compiler_params=_params("parallel", "arbitrary"),
        name="ffn_down",
    )(h, wd, x)


def _cast_block_specs(w, layer, nsteps, step_of):
    _, rows, cols = w.shape
    nblk = nsteps
    while rows % nblk or (rows // nblk) % PACKED_ROWS:
        nblk //= 2
    blk = rows // nblk
    return (pl.BlockSpec((None, blk, cols), lambda *g: (layer, step_of(*g) * nblk // nsteps, 0)),
            pl.BlockSpec((blk, cols), lambda *g: (step_of(*g) * nblk // nsteps, 0)),
            jax.ShapeDtypeStruct((rows, cols), BF16))


def _rope_table_body(pos_ref, inv_ref, ch_ref, sh_ref, ci_ref, si_ref):
    ang = pos_ref[...] * inv_ref[...]
    cos, sin = jnp.cos(ang), jnp.sin(ang)
    hh, hi = HEAD_DIM // 2, IDX_DIM // 2
    ch_ref[...] = jnp.concatenate([cos[:, :hh]] * 2, axis=1)
    sh_ref[...] = jnp.concatenate([-sin[:, :hh], sin[:, :hh]], axis=1)
    ci_ref[...] = jnp.concatenate([cos[:, hh:hh + hi]] * (LANES // hi), axis=1)
    si_ref[...] = jnp.concatenate([-sin[:, hh:hh + hi], sin[:, hh:hh + hi]] * (LANES // IDX_DIM), axis=1)


def _rope_tables(pos, tm):
    t = pos.shape[0]
    inv = lambda dim: 1.0 / (ROPE_THETA ** (jnp.arange(0, dim, 2, dtype=F32) / dim))
    inv_l = jnp.concatenate([inv(HEAD_DIM), inv(IDX_DIM), jnp.zeros((LANES - (HEAD_DIM + IDX_DIM) // 2,), F32)])
    tab = pl.BlockSpec((tm, LANES), lambda m: (m, 0))
    return pl.pallas_call(
        _rope_table_body,
        grid=(t // tm,),
        in_specs=[pl.BlockSpec((tm, 1), lambda m: (m, 0)), pl.BlockSpec((1, LANES), lambda m: (0, 0))],
        out_specs=[tab] * 4,
        out_shape=[jax.ShapeDtypeStruct((t, LANES), F32)] * 4,
        compiler_params=_params("parallel"),
        name="rope_tables",
    )(pos, inv_l.reshape(1, LANES))


def _rope128(x, cos, sin):
    return x * cos + pltpu.roll(x, HEAD_DIM // 2, 1) * sin


def _rope64(x, cos, sin, lo_half):
    partner = jnp.where(lo_half, pltpu.roll(x, LANES - IDX_DIM // 2, 1), pltpu.roll(x, IDX_DIM // 2, 1))
    return x * cos + partner * sin


def _in_proj_body(seq_tiles, x_ref, g_ref, w_ref, cw_ref, ch_ref, sh_ref, ci_ref, si_ref,
                  ya_ref, qb_ref, kb_ref, vb_ref, qi_ref, ki_ref, wi_ref, qc_ref, kc_ref, vc_ref,
                  u_ref):
    tm = x_ref.shape[0]
    xn = _rms(x_ref[...], g_ref[...]).astype(BF16)

    def proj(seg):
        return jnp.dot(xn, w_ref[:, seg[0]:seg[1]], preferred_element_type=F32)

    ch, sh, ci, si = ch_ref[...], sh_ref[...], ci_ref[...], si_ref[...]
    scale = HEAD_DIM ** -0.5 * LOG2E

    def rope_heads(p, n, mul):
        return jnp.concatenate(
            [_rope128(p[:, j * LANES:(j + 1) * LANES], ch, sh) * mul for j in range(n)], axis=1)

    qb_ref[...] = rope_heads(proj(SEG_QB), DSA_HEADS, scale).T.astype(BF16)
    p = proj(SEG_KV)
    kb_ref[...] = _rope128(p[:, :HEAD_DIM], ch, sh).astype(BF16)
    vb_ref[...] = p[:, HEAD_DIM:].astype(BF16)

    lo_half = lax.broadcasted_iota(I32, (tm, LANES), 1) % IDX_DIM < IDX_DIM // 2
    p = proj(SEG_QI)
    qi_ref[...] = jnp.concatenate(
        [_rope64(p[:, j * LANES:(j + 1) * LANES], ci, si, lo_half) * (IDX_DIM ** -0.5)
         for j in range(IDX_HEADS * IDX_DIM // LANES)], axis=1).T.astype(BF16)
    ki_ref[...] = _rope64(proj(SEG_KI), ci, si, lo_half)[:, :IDX_DIM].astype(BF16)
    wi_ref[...] = (proj(SEG_WI) * (IDX_HEADS ** -0.5)).T[:IDX_HEADS, :]

    qc_ref[...] = rope_heads(proj(SEG_QC), DIL_HEADS, scale)
    kc_ref[...] = rope_heads(proj(SEG_KC), DIL_HEADS, 1.0)

    h, gate_b, gate_c = (proj((SEG_CONV[0] + j * CONV_CH, SEG_CONV[0] + (j + 1) * CONV_CH)) for j in range(3))
    u = gate_c * h

    carry = jnp.where(pl.program_id(0) % seq_tiles == 0, 0.0, u_ref[...])
    ext = jnp.concatenate([carry, u], axis=0)
    cw = cw_ref[...]
    y = (cw[2:3, :] * u + cw[1:2, :] * pltpu.roll(ext, 1, 0)[SUBLANES:, :]
         + cw[0:1, :] * pltpu.roll(ext, 2, 0)[SUBLANES:, :])
    u_ref[...] = u[tm - SUBLANES:, :]
    ya_ref[...] = (gate_b * y).astype(BF16)

    vc_ref[...] = proj(SEG_VC)


def _in_proj(x, g, w, cw, tabs, layer, seq, tm):
    t, d = x.shape
    n = w.shape[2]
    row = lambda width: pl.BlockSpec((tm, width), lambda m: (m, 0))
    col = lambda height: pl.BlockSpec((height, tm), lambda m: (0, m))
    const = lambda shape: pl.BlockSpec((None,) + shape, lambda m: (layer, 0, 0))
    outs = [(CONV_CH, BF16, True), (DSA_HEADS * HEAD_DIM, BF16, False), (HEAD_DIM, BF16, True),
            (HEAD_DIM, BF16, True), (IDX_HEADS * IDX_DIM, BF16, False), (IDX_DIM, BF16, True),
            (IDX_HEADS, F32, False), (DIL_HEADS * HEAD_DIM, F32, True), (DIL_HEADS * HEAD_DIM, F32, True),
            (DIL_HEADS * HEAD_DIM, F32, True)]
    return pl.pallas_call(
        functools.partial(_in_proj_body, seq // tm),
        grid=(t // tm,),
        in_specs=[row(d), const((1, d)),
                  pl.BlockSpec((None, d, n), lambda m: (layer, 0, 0), pipeline_mode=pl.Buffered(1)),
                  const((CONV_WIDTH, CONV_CH)), row(LANES), row(LANES), row(LANES), row(LANES)],
        out_specs=[row(wd) if tok else col(wd) for wd, _, tok in outs],
        out_shape=[jax.ShapeDtypeStruct((t, wd) if tok else (wd, t), dt) for wd, dt, tok in outs],
        scratch_shapes=[pltpu.VMEM((SUBLANES, CONV_CH), F32)],
        compiler_params=_params("arbitrary"),
        name="in_proj",
    )(x, g, w, cw, *tabs)


def _dsa_body(topk, ncast, qbt_in_ref, kb_ref, vb_ref, qt_ref, ki_ref, wt_ref, *rest):
    cast_in, (o_ref, *cast_out) = rest[:ncast], rest[ncast:2 * ncast + 1]
    keys_ref, hi_ref, lo_ref, lo2_ref, qbt_ref, vt_ref, acc_ref = rest[2 * ncast + 1:]
    tq = o_ref.shape[1]
    seq = kb_ref.shape[1]

    for src_ref, dst_ref in zip(cast_in, cast_out):
        dst_ref[...] = src_ref[...].astype(BF16)

    nh = DSA_HEADS
    i = pl.program_id(1)
    nch = i + 1

    def chunk(c):
        return pl.ds(pl.multiple_of(c * tq, tq), tq)

    def transpose_bf16(a):
        return a.astype(F32).T.astype(BF16)

    @pl.when(i == 0)
    def _():
        for c2 in range(seq // (2 * tq)):
            vt_ref[c2, :HEAD_DIM, :] = transpose_bf16(vb_ref[0, c2 * 2 * tq:(c2 + 1) * 2 * tq, :])
            vt_ref[c2, HEAD_DIM:, :] = jnp.ones((ONES_ROWS, 2 * tq), BF16)

    for h in range(nh):
        qbt_ref[:, h * tq:(h + 1) * tq] = qbt_in_ref[h * HEAD_DIM:(h + 1) * HEAD_DIM, :]
    wt = wt_ref[...]
    krow = lax.broadcasted_iota(I32, (tq, tq), 0)

    def pair(c2):
        return pl.ds(pl.multiple_of(c2 * (2 * tq), 2 * tq), 2 * tq)

    def score_rows(rows, n, first_key):
        kc = ki_ref[0, rows, :]
        acc = jnp.zeros((n, tq), F32)
        for h in range(IDX_HEADS):
            lg = jnp.dot(kc, qt_ref[h * IDX_DIM:(h + 1) * IDX_DIM, :], preferred_element_type=F32)
            acc = acc + jnp.maximum(lg, 0.0) * wt[h:h + 1, :]
        bits = pltpu.bitcast(acc, I32)
        key = bits ^ ((bits >> 31) & 0x7FFFFFFF)
        causal = lax.broadcasted_iota(I32, (n, tq), 0) + first_key <= lax.broadcasted_iota(I32, (n, tq), 1) + i * tq
        key = jnp.where(causal, key, INT_MIN)
        keys_ref[rows, :] = key
        hi_ref[rows, :] = (key >> 16).astype(I16)
        lo_ref[rows, :] = ((key & 0xFFFF) - HALF16).astype(I16)

    def score_quad(c4, carry):
        score_rows(pl.ds(pl.multiple_of(c4 * (4 * tq), 4 * tq), 4 * tq), 4 * tq, c4 * (4 * tq))
        return carry

    lax.fori_loop(0, nch // 4, score_quad, 0)

    @pl.when(nch % 4 >= 2)
    def _():
        score_rows(pair(nch // 4 * 2), 2 * tq, nch // 4 * (4 * tq))

    npair = (nch + 1) // 2

    @pl.when(nch % 2 == 1)
    def _():
        score_rows(chunk(nch - 1), tq, (nch - 1) * tq)
        keys_ref[chunk(nch), :] = jnp.full((tq, tq), INT_MIN, I32)
        hi_ref[chunk(nch), :] = jnp.full((tq, tq), -HALF16, I16)
        lo_ref[chunk(nch), :] = jnp.full((tq, tq), -HALF16, I16)

    def count(pred):
        def body(c, acc):
            hit = pred(keys_ref[chunk(c), :], krow + c * tq).astype(I32)
            return acc + jnp.sum(hit.reshape(tq // 8, 8, tq), axis=0)
        acc = lax.fori_loop(0, nch, body, jnp.zeros((8, tq), I32))
        return jnp.sum(acc, axis=0, keepdims=True)

    def count16(ref, pred):
        def body(c2, acc):
            hit = jnp.where(pred(ref[pair(c2), :]), jnp.ones((), BF16), jnp.zeros((), BF16))
            parts = [hit[r * PACKED_ROWS:(r + 1) * PACKED_ROWS] for r in range(2 * tq // PACKED_ROWS)]
            while len(parts) > 1:
                parts = [parts[j] + parts[j + 1] for j in range(0, len(parts), 2)]
            return acc + parts[0].astype(F32)
        acc = lax.fori_loop(0, npair, body, jnp.zeros((PACKED_ROWS, tq), F32))
        return jnp.sum(acc, axis=0, keepdims=True)

    def kth_largest16(ref, k):
        def search_bit(b, carry):
            t_u, n_gt, n_ge = carry
            cand = t_u | (jnp.int32(1) << (15 - b))
            cand16 = (cand - HALF16).astype(I16)
            n = count16(ref, lambda v: v >= cand16)
            ok = n >= k
            return jnp.where(ok, cand, t_u), jnp.where(ok, n_gt, n), jnp.where(ok, n, n_ge)
        walked = (npair * (2 * tq)).astype(F32)
        init = (jnp.zeros((1, tq), I32), jnp.zeros((1, tq), F32), jnp.zeros((1, tq), F32) + walked)
        t_u, n_gt, n_ge = lax.fori_loop(0, 16, search_bit, init)
        return t_u - HALF16, n_gt, n_ge

    t_hi, n_gt_hi, _ = kth_largest16(hi_ref, float(topk))
    t_hi16 = t_hi.astype(I16)
    need_lo = float(topk) - n_gt_hi

    def bucket(c2, carry):
        lo2_ref[pair(c2), :] = jnp.where(hi_ref[pair(c2), :] == t_hi16, lo_ref[pair(c2), :],
                                         jnp.full((), -HALF16, I16))
        return carry

    lax.fori_loop(0, npair, bucket, 0)
    t_lo, _, n_ge_lo = kth_largest16(lo2_ref, need_lo)
    thr = jnp.maximum((t_hi << 16) | (t_lo + HALF16), INT_MIN + 1)
    n_ge = jnp.where(t_hi > -HALF16, n_gt_hi + n_ge_lo, 0.0)

    idx_bits = seq.bit_length() - 1

    @pl.when(jnp.max(n_ge) > topk)
    def _():
        need = topk - count(lambda k, _: k > thr)
        def bit(b, j):
            cand = j | (jnp.int32(1) << (idx_bits - 1 - b))
            below = count(lambda k, idx: (k == thr) & (idx < cand))
            return jnp.where(below < need, cand, j)
        bound = lax.fori_loop(0, idx_bits, bit, jnp.zeros((1, tq), I32))
        bound = jnp.where(n_ge > topk, bound, seq)

        def demote(c, carry):
            k = keys_ref[chunk(c), :]
            keys_ref[chunk(c), :] = jnp.where((k == thr) & (krow + c * tq > bound), thr - 1, k)
            return carry

        lax.fori_loop(0, nch, demote, 0)

    acc_ref[...] = jnp.zeros(acc_ref.shape, F32)

    def attend(first_pair, npairs, m):
        n = npairs * 2 * tq
        rows = pl.ds(pl.multiple_of(first_pair * (2 * tq), 2 * tq), n)
        bias = jnp.where(keys_ref[rows, :] >= thr, 0.0, NEG)
        st = jnp.dot(kb_ref[0, rows, :], qbt_ref[...], preferred_element_type=F32)
        st = st + jnp.concatenate([bias] * nh, axis=1)
        m_new = jnp.maximum(m, jnp.max(st, axis=0, keepdims=True))
        p = jnp.exp2(st - m_new).astype(BF16)
        pv = sum(jnp.dot(vt_ref[first_pair + j], p[j * 2 * tq:(j + 1) * 2 * tq], preferred_element_type=F32)
                 for j in range(npairs))
        acc_ref[...] = acc_ref[...] * jnp.exp2(m - m_new) + pv
        return m_new

    m = lax.fori_loop(0, npair // 2, lambda c4, m: attend(2 * c4, 2, m), jnp.full((1, nh * tq), NEG, F32))

    @pl.when(npair % 2 == 1)
    def _():
        attend(npair - 1, 1, m)

    out = acc_ref[:HEAD_DIM, :] / acc_ref[HEAD_DIM:HEAD_DIM + 1, :]
    for h in range(nh):
        o_ref[0, :, h * HEAD_DIM:(h + 1) * HEAD_DIM] = out[:, h * tq:(h + 1) * tq].T.astype(BF16)


def _dsa(qbt, kb, vb, qit, ki, wit, tq, to_cast):
    b, s, _ = kb.shape
    topk = min(TOPK_MAX, s // 4)
    assert s % (2 * tq) == 0
    assert 2 * tq // PACKED_ROWS <= 256
    nq = s // tq
    blk = lambda width: pl.BlockSpec((1, tq, width), lambda bb, i: (bb, i, 0))
    full = lambda width: pl.BlockSpec((1, s, width), lambda bb, i: (bb, 0, 0))
    qcol = lambda a: pl.BlockSpec((a.shape[0], tq), lambda bb, i: (0, bb * nq + i))
    casts = [_cast_block_specs(w, layer, b * nq, lambda bb, i: bb * nq + i) for w, layer in to_cast]
    out, *copies = pl.pallas_call(
        functools.partial(_dsa_body, topk, len(casts)),
        grid=(b, nq),
        in_specs=[qcol(qbt), full(HEAD_DIM), full(HEAD_DIM), qcol(qit), full(IDX_DIM), qcol(wit)]
        + [c[0] for c in casts],
        out_specs=[blk(DSA_HEADS * HEAD_DIM)] + [c[1] for c in casts],
        out_shape=[jax.ShapeDtypeStruct((b, s, DSA_HEADS * HEAD_DIM), BF16)] + [c[2] for c in casts],
        scratch_shapes=[pltpu.VMEM((s, tq), I32),
                        pltpu.VMEM((s, tq), I16),
                        pltpu.VMEM((s, tq), I16),
                        pltpu.VMEM((s, tq), I16),
                        pltpu.VMEM((HEAD_DIM, DSA_HEADS * tq), BF16),
                        pltpu.VMEM((s // (2 * tq), HEAD_DIM + ONES_ROWS, 2 * tq), BF16),
                        pltpu.VMEM((HEAD_DIM + ONES_ROWS, DSA_HEADS * tq), F32)],
        compiler_params=_params("parallel", "arbitrary"),
        name="dsa",
    )(qbt, kb, vb, qit, ki, wit, *[w for w, _ in to_cast])
    return out, copies


def _dilated_body(*refs):
    ng = len(DIL_PATTERNS)
    q_refs, k_refs, v_refs = refs[0:ng], refs[ng:2 * ng], refs[2 * ng:3 * ng]
    o_refs = refs[3 * ng:4 * ng]
    acc_ref, m_ref, l_ref = refs[4 * ng:]
    sb_tokens = q_refs[0].shape[1]
    t0 = pl.program_id(2) * sb_tokens
    blk = HEAD_DIM
    UNITS = 4
    rq = lax.broadcasted_iota(I32, (blk, blk), 0)
    ck = lax.broadcasted_iota(I32, (blk, blk), 1)
    bias_cur = jnp.where(ck <= rq, 0.0, NEG)
    bias_prev = jnp.where(ck >= rq, 0.0, NEG)
    nt = (((1,), (1,)), ((), ()))

    for g, (win, dil) in enumerate(DIL_PATTERNS):
        q_ref, k_ref, v_ref = q_refs[g], k_refs[g], v_refs[g]
        per_res = sb_tokens // dil // blk

        def rows(start, dil=dil):
            return pl.ds(start, blk, stride=dil) if dil > 1 else pl.ds(start, blk)

        def units(it, carry, g=g, dil=dil, per_res=per_res, rows=rows,
                  q_ref=q_ref, k_ref=k_ref, v_ref=v_ref):
            q0s, kcs, kps, hps = [], [], [], []
            for n in range(UNITS):
                idx = it * UNITS + n
                q0 = idx // per_res + (idx % per_res) * (blk * dil)
                k_cur = t0 + q0
                has_prev = k_cur >= blk * dil
                q0s.append(q0)
                kcs.append(k_cur)
                hps.append(has_prev)
                kps.append(jnp.where(has_prev, k_cur - blk * dil, k_cur))
            ss = []
            for q0, kc, kp, hp in zip(q0s, kcs, kps, hps):
                k2 = jnp.concatenate([k_ref[0, rows(kp), :], k_ref[0, rows(kc), :]], axis=0).astype(BF16)
                s = lax.dot_general(q_ref[0, rows(q0), :].astype(BF16), k2, nt, preferred_element_type=F32)
                ss.append(s + jnp.concatenate([jnp.where(hp, bias_prev, NEG), bias_cur], axis=1))
            ms = [jnp.max(jnp.maximum(s[:, :blk], s[:, blk:]), axis=-1, keepdims=True) for s in ss]
            ps = [jnp.exp2(s - m).astype(BF16) for s, m in zip(ss, ms)]
            for q0, kc, kp, m, p in zip(q0s, kcs, kps, ms, ps):
                v2 = jnp.concatenate([v_ref[0, rows(kp), :], v_ref[0, rows(kc), :]], axis=0).astype(BF16)
                av = jnp.dot(p, jnp.concatenate([v2, jnp.ones_like(v2)], axis=1), preferred_element_type=F32)
                acc_ref[g, rows(q0), :] = av[:, :HEAD_DIM]
                l_ref[g, rows(q0), :] = av[:, HEAD_DIM:]
                m_ref[g, rows(q0), :] = jnp.broadcast_to(m, (blk, LANES))
            return carry

        lax.fori_loop(0, dil * per_res // UNITS, units, 0)

    step = 256

    def merge(c, carry):
        sl = pl.ds(pl.multiple_of(c * step, step), step)
        ms = [m_ref[g, sl, :] for g in range(ng)]
        m_all = functools.reduce(jnp.maximum, ms)
        ws = [jnp.exp2(m - m_all) for m in ms]
        den = sum(w * l_ref[g, sl, :] for g, w in enumerate(ws))
        for g, w in enumerate(ws):
            o_refs[g][0, sl, :] = (acc_ref[g, sl, :] * (w / den)).astype(BF16)
        return carry

    lax.fori_loop(0, sb_tokens // step, merge, 0)


def _dilated(qc, kc, vc):
    b, s, _ = qc.shape
    ng, hg = len(DIL_PATTERNS), DIL_HEADS_PER_GROUP
    sb_tokens = HEAD_DIM * max(dil for _, dil in DIL_PATTERNS)
    assert s % sb_tokens == 0 and all(win == HEAD_DIM * dil for win, dil in DIL_PATTERNS)
    head = lambda g: (lambda bb, j, sb: (bb, sb, g * hg + j))
    head_full = lambda g: (lambda bb, j, sb: (bb, 0, g * hg + j))
    q_specs = [pl.BlockSpec((1, sb_tokens, HEAD_DIM), head(g)) for g in range(ng)]
    kv_specs = [pl.BlockSpec((1, s, HEAD_DIM), head_full(g)) for g in range(ng)]
    out_spec = pl.BlockSpec((1, sb_tokens, HEAD_DIM), lambda bb, j, sb: (bb, sb, j))
    return pl.pallas_call(
        _dilated_body,
        grid=(b, hg, s // sb_tokens),
        in_specs=q_specs + kv_specs + kv_specs,
        out_specs=[out_spec] * ng,
        out_shape=[jax.ShapeDtypeStruct((b, s, hg * HEAD_DIM), BF16)] * ng,
        scratch_shapes=[pltpu.VMEM((ng, sb_tokens, HEAD_DIM), F32)] * 3,
        compiler_params=_params("parallel", "parallel", "arbitrary"),
        name="dilated",
    )(*([qc] * ng + [kc] * ng + [vc] * ng))


def _out_proj_body(*refs):
    *y_refs, w_ref, x_ref, o_ref, wb_ref = refs

    @pl.when(pl.program_id(1) == 0)
    def _():
        wb_ref[...] = w_ref[...].astype(BF16)

    y = jnp.concatenate([y_ref[...] for y_ref in y_refs], axis=1)
    o_ref[...] = x_ref[...] + jnp.dot(y, wb_ref[...], preferred_element_type=F32)


def _out_proj(parts, w, x, layer, tm, tn):
    t, d = x.shape
    assert sum(a.shape[1] for a in parts) == w.shape[1]
    return pl.pallas_call(
        _out_proj_body,
        grid=(d // tn, t // tm),
        in_specs=[pl.BlockSpec((tm, a.shape[1]), lambda n, m: (m, 0)) for a in parts]
        + [pl.BlockSpec((None, w.shape[1], tn), lambda n, m: (layer, 0, n)),
           pl.BlockSpec((tm, tn), lambda n, m: (m, n))],
        out_specs=pl.BlockSpec((tm, tn), lambda n, m: (m, n)),
        out_shape=jax.ShapeDtypeStruct((t, d), F32),
        scratch_shapes=[pltpu.VMEM((w.shape[1], tn), BF16)],
        compiler_params=_params("parallel", "arbitrary"),
        name="out_proj",
    )(*parts, w, x)


def _final_norm_body(x_ref, g_ref, o_ref):
    o_ref[...] = _rms(x_ref[...], g_ref[...])


def _final_norm(x, g, tm):
    t, d = x.shape
    return pl.pallas_call(
        _final_norm_body,
        grid=(t // tm,),
        in_specs=[pl.BlockSpec((tm, d), lambda m: (m, 0)), pl.BlockSpec((1, d), lambda m: (0, 0))],
        out_specs=pl.BlockSpec((tm, d), lambda m: (m, 0)),
        out_shape=jax.ShapeDtypeStruct((t, d), F32),
        compiler_params=_params("parallel"),
        name="final_norm",
    )(x, g)


def _align_w_in_body(x_ref, o_ref):
    o_wi = SEG_QI[1] + IDX_DIM
    o_qc = o_wi + IDX_HEADS
    rows = x_ref.shape[0]
    o_ref[:, :o_wi] = x_ref[:, :o_wi]
    o_ref[:, o_wi:SEG_WI[0]] = jnp.zeros((rows, SEG_WI[0] - o_wi), BF16)
    o_ref[:, SEG_WI[0]:SEG_WI[0] + IDX_HEADS] = x_ref[:, o_wi:o_qc]
    o_ref[:, SEG_WI[0] + IDX_HEADS:SEG_QC[0]] = jnp.zeros((rows, LANES - IDX_HEADS), BF16)
    o_ref[:, SEG_QC[0]:] = x_ref[:, o_qc:o_qc + D_IN_ALIGNED - SEG_QC[0]]


def _align_w_in(w, tr):
    depth, d, n = w.shape
    assert n - (SEG_QI[1] + IDX_DIM + IDX_HEADS) == D_IN_ALIGNED - SEG_QC[0]
    w = jnp.pad(w.astype(BF16), ((0, 0), (0, 0), (0, D_IN_ALIGNED - n)))
    return pl.pallas_call(
        _align_w_in_body,
        grid=(depth, d // tr),
        in_specs=[pl.BlockSpec((None, tr, D_IN_ALIGNED), lambda l, r: (l, r, 0))],
        out_specs=pl.BlockSpec((None, tr, D_IN_ALIGNED), lambda l, r: (l, r, 0)),
        out_shape=jax.ShapeDtypeStruct((depth, d, D_IN_ALIGNED), BF16),
        compiler_params=_params("parallel", "parallel"),
        name="align_w_in",
    )(w)


def _tile(n, want):
    while n % want:
        want //= 2
    return want


class _Tiles(NamedTuple):
    tm: int
    tn_ffn: int
    tn_out: int
    tm_proj: int
    tn_down_f32: int
    tq: int
    tr_align: int


def _tiles(t, s, d, f):
    return _Tiles(tm=_tile(t, 1024), tn_ffn=_tile(f, 512), tn_out=_tile(d, 1024), tm_proj=_tile(s, 512),
                  tn_down_f32=_tile(d, 256), tq=_tile(s, 256), tr_align=_tile(d, 256))


def kernel(x, positions, norm_ffn1, ffn1_gate, ffn1_up, ffn1_down, norm_mix, w_in, conv_w, w_out,
           norm_ffn2, ffn2_gate, ffn2_up, ffn2_down, norm_final):
    b, s, d = x.shape
    t = b * s
    depth = w_in.shape[0]
    tl = _tiles(t, s, d, ffn1_gate.shape[2])

    def ffn(xf, g, weights, layer):
        wg, wu, wd = weights
        h = _ffn_up(xf, g.reshape(depth, 1, d), wg, wu, layer, tl.tm, tl.tn_ffn)
        return _ffn_down(h, wd, xf, tl.tm, tl.tn_down_f32 if isinstance(wd, tuple) else tl.tn_out)

    ffn1_stacks, ffn2_stacks = (ffn1_gate, ffn1_up, ffn1_down), (ffn2_gate, ffn2_up, ffn2_down)
    w1 = [(w, 0) for w in ffn1_stacks]
    pos = positions.astype(F32).reshape(t, 1)
    tabs = _rope_tables(pos, tl.tm)
    w_in_al = _align_w_in(w_in, tl.tr_align)
    xf = x.reshape(t, d)
    for i in range(depth):
        xf = ffn(xf, norm_ffn1, w1, i)
        ya, qb, kb, vb, qi, ki, wi, qc, kc, vc = _in_proj(
            xf, norm_mix.reshape(depth, 1, d), w_in_al, conv_w, tabs, i, s, tl.tm_proj)
        r3 = lambda a: a.reshape(b, s, a.shape[-1])
        to_cast = [(w, i) for w in ffn2_stacks] + ([(w, i + 1) for w in ffn1_stacks] if i + 1 < depth else [])
        yb, copies = _dsa(qb, r3(kb), r3(vb), qi, r3(ki), wi, tl.tq, to_cast)
        w2, w1 = copies[:3], copies[3:]
        ycs = _dilated(r3(qc), r3(kc), r3(vc))
        parts = [ya, yb.reshape(t, -1)] + [yc.reshape(t, -1) for yc in ycs]
        xf = _out_proj(parts, w_out, xf, i, tl.tm, tl.tn_out)
        xf = ffn(xf, norm_ffn2, w2, i)
    return _final_norm(xf, norm_final.reshape(1, d), tl.tm).reshape(b, s, d)
```

```python
import functools
from typing import NamedTuple

import jax
import jax.numpy as jnp
from jax import lax
from jax.experimental import pallas as pl
from jax.experimental.pallas import tpu as pltpu

F32 = jnp.float32
BF16 = jnp.bfloat16
I32 = jnp.int32
I16 = jnp.int16
HALF16 = 1 << 15
PACKED_ROWS = 16
ONES_ROWS = PACKED_ROWS

HEAD_DIM = 128
CONV_CH = 512
CONV_WIDTH = 3
DSA_HEADS = 6
IDX_HEADS = 16
IDX_DIM = 64
TOPK_MAX = 256
DIL_PATTERNS = ((128, 1), (512, 4), (2048, 16))
DIL_HEADS_PER_GROUP = 2
DIL_HEADS = len(DIL_PATTERNS) * DIL_HEADS_PER_GROUP
ROPE_THETA = 10000.0
RMS_EPS = 1e-6

LANES = 128
SUBLANES = 8
VMEM_LIMIT = 56 * 1024 * 1024
NEG = -1e30
INT_MIN = -2 ** 31
LOG2E = 1.4426950408889634

SEG_CONV = (0, 3 * CONV_CH)
SEG_QB = (SEG_CONV[1], SEG_CONV[1] + DSA_HEADS * HEAD_DIM)
SEG_KV = (SEG_QB[1], SEG_QB[1] + 2 * HEAD_DIM)
SEG_QI = (SEG_KV[1], SEG_KV[1] + IDX_HEADS * IDX_DIM)
SEG_KI = (SEG_QI[1], SEG_QI[1] + LANES)
SEG_WI = (SEG_KI[1], SEG_KI[1] + LANES)
SEG_QC = (SEG_WI[1], SEG_WI[1] + DIL_HEADS * HEAD_DIM)
SEG_KC = (SEG_QC[1], SEG_QC[1] + DIL_HEADS * HEAD_DIM)
SEG_VC = (SEG_KC[1], SEG_KC[1] + DIL_HEADS * HEAD_DIM)
D_IN_ALIGNED = SEG_VC[1]


def _params(*sem):
    return pltpu.CompilerParams(dimension_semantics=sem, vmem_limit_bytes=VMEM_LIMIT)


def _rms(x, g):
    ms = jnp.mean(x * x, axis=-1, keepdims=True)
    return x * lax.rsqrt(ms + RMS_EPS) * g


def _ffn_up_body(x_ref, g_ref, wg_ref, wu_ref, h_ref, xn_ref):
    @pl.when(pl.program_id(1) == 0)
    def _():
        xn_ref[...] = _rms(x_ref[...], g_ref[...]).astype(BF16)

    xn = xn_ref[...]
    a = jnp.dot(xn, wg_ref[...].astype(BF16), preferred_element_type=F32)
    b = jnp.dot(xn, wu_ref[...].astype(BF16), preferred_element_type=F32)
    h_ref[...] = (a * jax.nn.sigmoid(a) * b).astype(BF16)


def _weight_tile(w, tn):
    if isinstance(w, tuple):
        stack, layer = w
        return stack, pl.BlockSpec((None, stack.shape[1], tn), lambda m, n: (layer, 0, n))
    return w, pl.BlockSpec((w.shape[0], tn), lambda m, n: (0, n))


def _ffn_up(x, g, wg, wu, layer, tm, tn):
    t, d = x.shape
    (wg, wg_spec), (wu, wu_spec) = _weight_tile(wg, tn), _weight_tile(wu, tn)
    f = wg.shape[-1]
    return pl.pallas_call(
        _ffn_up_body,
        grid=(t // tm, f // tn),
        in_specs=[
            pl.BlockSpec((tm, d), lambda m, n: (m, 0)),
            pl.BlockSpec((None, 1, d), lambda m, n: (layer, 0, 0)),
            wg_spec,
            wu_spec,
        ],
        out_specs=pl.BlockSpec((tm, tn), lambda m, n: (m, n)),
        out_shape=jax.ShapeDtypeStruct((t, f), BF16),
        scratch_shapes=[pltpu.VMEM((tm, d), BF16)],
        compiler_params=_params("parallel", "arbitrary"),
        name="ffn_up",
    )(x, g, wg, wu)


def _ffn_down_body(h_ref, w_ref, x_ref, o_ref):
    y = jnp.dot(h_ref[...], w_ref[...].astype(BF16), preferred_element_type=F32)
    o_ref[...] = x_ref[...] + 0.5 * y


def _ffn_down(h, wd, x, tm, tn):
    t, f = h.shape
    wd, wd_spec = _weight_tile(wd, tn)
    d = wd.shape[-1]
    return pl.pallas_call(
        _ffn_down_body,
        grid=(t // tm, d // tn),
        in_specs=[
            pl.BlockSpec((tm, f), lambda m, n: (m, 0)),
            wd_spec,
            pl.BlockSpec((tm, tn), lambda m, n: (m, n)),
        ],
        out_specs=pl.BlockSpec((tm, tn), lambda m, n: (m, n)),
        out_shape=jax.ShapeDtypeStruct((t, d), F32),
        compiler_params=_params("parallel", "arbitrary"),
        name="ffn_down",
    )(h, wd, x)


def _cast_block_specs(w, layer, nsteps, step_of):
    _, rows, cols = w.shape
    nblk = nsteps
    while rows % nblk or (rows // nblk) % PACKED_ROWS:
        nblk //= 2
    blk = rows // nblk
    return (pl.BlockSpec((None, blk, cols), lambda *g: (layer, step_of(*g) * nblk // nsteps, 0)),
            pl.BlockSpec((blk, cols), lambda *g: (step_of(*g) * nblk // nsteps, 0)),
            jax.ShapeDtypeStruct((rows, cols), BF16))


def _rope_table_body(pos_ref, inv_ref, ch_ref, sh_ref, ci_ref, si_ref):
    ang = pos_ref[...] * inv_ref[...]
    cos, sin = jnp.cos(ang), jnp.sin(ang)
    hh, hi = HEAD_DIM // 2, IDX_DIM // 2
    ch_ref[...] = jnp.concatenate([cos[:, :hh]] * 2, axis=1)
    sh_ref[...] = jnp.concatenate([-sin[:, :hh], sin[:, :hh]], axis=1)
    ci_ref[...] = jnp.concatenate([cos[:, hh:hh + hi]] * (LANES // hi), axis=1)
    si_ref[...] = jnp.concatenate([-sin[:, hh:hh + hi], sin[:, hh:hh + hi]] * (LANES // IDX_DIM), axis=1)


def _rope_tables(pos, tm):
    t = pos.shape[0]
    inv = lambda dim: 1.0 / (ROPE_THETA ** (jnp.arange(0, dim, 2, dtype=F32) / dim))
    inv_l = jnp.concatenate([inv(HEAD_DIM), inv(IDX_DIM), jnp.zeros((LANES - (HEAD_DIM + IDX_DIM) // 2,), F32)])
    tab = pl.BlockSpec((tm, LANES), lambda m: (m, 0))
    return pl.pallas_call(
        _rope_table_body,
        grid=(t // tm,),
        in_specs=[pl.BlockSpec((tm, 1), lambda m: (m, 0)), pl.BlockSpec((1, LANES), lambda m: (0, 0))],
        out_specs=[tab] * 4,
        out_shape=[jax.ShapeDtypeStruct((t, LANES), F32)] * 4,
        compiler_params=_params("parallel"),
        name="rope_tables",
    )(pos, inv_l.reshape(1, LANES))


def _rope128(x, cos, sin):
    return x * cos + pltpu.roll(x, HEAD_DIM // 2, 1) * sin


def _rope64(x, cos, sin, lo_half):
    partner = jnp.where(lo_half, pltpu.roll(x, LANES - IDX_DIM // 2, 1), pltpu.roll(x, IDX_DIM // 2, 1))
    return x * cos + partner * sin


def _in_proj_body(seq_tiles, x_ref, g_ref, w_ref, cw_ref, ch_ref, sh_ref, ci_ref, si_ref,
                  ya_ref, qb_ref, kb_ref, vb_ref, qi_ref, ki_ref, wi_ref, qc_ref, kc_ref, vc_ref,
                  u_ref):
    tm = x_ref.shape[0]
    xn = _rms(x_ref[...], g_ref[...]).astype(BF16)

    def proj(seg):
        return jnp.dot(xn, w_ref[:, seg[0]:seg[1]], preferred_element_type=F32)

    ch, sh, ci, si = ch_ref[...], sh_ref[...], ci_ref[...], si_ref[...]
    scale = HEAD_DIM ** -0.5 * LOG2E

    def rope_heads(p, n, mul):
        return jnp.concatenate(
            [_rope128(p[:, j * LANES:(j + 1) * LANES], ch, sh) * mul for j in range(n)], axis=1)

    qb_ref[...] = rope_heads(proj(SEG_QB), DSA_HEADS, scale).T.astype(BF16)
    p = proj(SEG_KV)
    kb_ref[...] = _rope128(p[:, :HEAD_DIM], ch, sh).astype(BF16)
    vb_ref[...] = p[:, HEAD_DIM:].astype(BF16)

    lo_half = lax.broadcasted_iota(I32, (tm, LANES), 1) % IDX_DIM < IDX_DIM // 2
    p = proj(SEG_QI)
    qi_ref[...] = jnp.concatenate(
        [_rope64(p[:, j * LANES:(j + 1) * LANES], ci, si, lo_half) * (IDX_DIM ** -0.5)
         for j in range(IDX_HEADS * IDX_DIM // LANES)], axis=1).T.astype(BF16)
    ki_ref[...] = _rope64(proj(SEG_KI), ci, si, lo_half)[:, :IDX_DIM].astype(BF16)
    wi_ref[...] = (proj(SEG_WI) * (IDX_HEADS ** -0.5)).T[:IDX_HEADS, :]

    qc_ref[...] = rope_heads(proj(SEG_QC), DIL_HEADS, scale)
    kc_ref[...] = rope_heads(proj(SEG_KC), DIL_HEADS, 1.0)

    h, gate_b, gate_c = (proj((SEG_CONV[0] + j * CONV_CH, SEG_CONV[0] + (j + 1) * CONV_CH)) for j in range(3))
    u = gate_c * h

    carry = jnp.where(pl.program_id(0) % seq_tiles == 0, 0.0, u_ref[...])
    ext = jnp.concatenate([carry, u], axis=0)
    cw = cw_ref[...]
    y = (cw[2:3, :] * u + cw[1:2, :] * pltpu.roll(ext, 1, 0)[SUBLANES:, :]
         + cw[0:1, :] * pltpu.roll(ext, 2, 0)[SUBLANES:, :])
    u_ref[...] = u[tm - SUBLANES:, :]
    ya_ref[...] = (gate_b * y).astype(BF16)

    vc_ref[...] = proj(SEG_VC)


def _in_proj(x, g, w, cw, tabs, layer, seq, tm):
    t, d = x.shape
    n = w.shape[2]
    row = lambda width: pl.BlockSpec((tm, width), lambda m: (m, 0))
    col = lambda height: pl.BlockSpec((height, tm), lambda m: (0, m))
    const = lambda shape: pl.BlockSpec((None,) + shape, lambda m: (layer, 0, 0))
    outs = [(CONV_CH, BF16, True), (DSA_HEADS * HEAD_DIM, BF16, False), (HEAD_DIM, BF16, True),
            (HEAD_DIM, BF16, True), (IDX_HEADS * IDX_DIM, BF16, False), (IDX_DIM, BF16, True),
            (IDX_HEADS, F32, False), (DIL_HEADS * HEAD_DIM, F32, True), (DIL_HEADS * HEAD_DIM, F32, True),
            (DIL_HEADS * HEAD_DIM, F32, True)]
    return pl.pallas_call(
        functools.partial(_in_proj_body, seq // tm),
        grid=(t // tm,),
        in_specs=[row(d), const((1, d)),
                  pl.BlockSpec((None, d, n), lambda m: (layer, 0, 0), pipeline_mode=pl.Buffered(1)),
                  const((CONV_WIDTH, CONV_CH)), row(LANES), row(LANES), row(LANES), row(LANES)],
        out_specs=[row(wd) if tok else col(wd) for wd, _, tok in outs],
        out_shape=[jax.ShapeDtypeStruct((t, wd) if tok else (wd, t), dt) for wd, dt, tok in outs],
        scratch_shapes=[pltpu.VMEM((SUBLANES, CONV_CH), F32)],
        compiler_params=_params("arbitrary"),
        name="in_proj",
    )(x, g, w, cw, *tabs)


def _dsa_body(topk, ncast, qbt_in_ref, kb_ref, vb_ref, qt_ref, ki_ref, wt_ref, *rest):
    cast_in, (o_ref, *cast_out) = rest[:ncast], rest[ncast:2 * ncast + 1]
    keys_ref, hi_ref, lo_ref, lo2_ref, qbt_ref, vt_ref, acc_ref = rest[2 * ncast + 1:]
    tq = o_ref.shape[1]
    seq = kb_ref.shape[1]

    for src_ref, dst_ref in zip(cast_in, cast_out):
        dst_ref[...] = src_ref[...].astype(BF16)

    nh = DSA_HEADS
    i = pl.program_id(1)
    nch = i + 1

    def chunk(c):
        return pl.ds(pl.multiple_of(c * tq, tq), tq)

    def transpose_bf16(a):
        return a.astype(F32).T.astype(BF16)

    @pl.when(i == 0)
    def _():
        for c2 in range(seq // (2 * tq)):
            vt_ref[c2, :HEAD_DIM, :] = transpose_bf16(vb_ref[0, c2 * 2 * tq:(c2 + 1) * 2 * tq, :])
            vt_ref[c2, HEAD_DIM:, :] = jnp.ones((ONES_ROWS, 2 * tq), BF16)

    for h in range(nh):
        qbt_ref[:, h * tq:(h + 1) * tq] = qbt_in_ref[h * HEAD_DIM:(h + 1) * HEAD_DIM, :]
    wt = wt_ref[...]
    krow = lax.broadcasted_iota(I32, (tq, tq), 0)

    def pair(c2):
        return pl.ds(pl.multiple_of(c2 * (2 * tq), 2 * tq), 2 * tq)

    def score_rows(rows, n, first_key):
        kc = ki_ref[0, rows, :]
        acc = jnp.zeros((n, tq), F32)
        for h in range(IDX_HEADS):
            lg = jnp.dot(kc, qt_ref[h * IDX_DIM:(h + 1) * IDX_DIM, :], preferred_element_type=F32)
            acc = acc + jnp.maximum(lg, 0.0) * wt[h:h + 1, :]
        bits = pltpu.bitcast(acc, I32)
        key = bits ^ ((bits >> 31) & 0x7FFFFFFF)
        causal = lax.broadcasted_iota(I32, (n, tq), 0) + first_key <= lax.broadcasted_iota(I32, (n, tq), 1) + i * tq
        key = jnp.where(causal, key, INT_MIN)
        keys_ref[rows, :] = key
        hi_ref[rows, :] = (key >> 16).astype(I16)
        lo_ref[rows, :] = ((key & 0xFFFF) - HALF16).astype(I16)

    def score_quad(c4, carry):
        score_rows(pl.ds(pl.multiple_of(c4 * (4 * tq), 4 * tq), 4 * tq), 4 * tq, c4 * (4 * tq))
        return carry

    lax.fori_loop(0, nch // 4, score_quad, 0)

    @pl.when(nch % 4 >= 2)
    def _():
        score_rows(pair(nch // 4 * 2), 2 * tq, nch // 4 * (4 * tq))

    npair = (nch + 1) // 2

    @pl.when(nch % 2 == 1)
    def _():
        score_rows(chunk(nch - 1), tq, (nch - 1) * tq)
        keys_ref[chunk(nch), :] = jnp.full((tq, tq), INT_MIN, I32)
        hi_ref[chunk(nch), :] = jnp.full((tq, tq), -HALF16, I16)
        lo_ref[chunk(nch), :] = jnp.full((tq, tq), -HALF16, I16)

    def count(pred):
        def body(c, acc):
            hit = pred(keys_ref[chunk(c), :], krow + c * tq).astype(I32)
            return acc + jnp.sum(hit.reshape(tq // 8, 8, tq), axis=0)
        acc = lax.fori_loop(0, nch, body, jnp.zeros((8, tq), I32))
        return jnp.sum(acc, axis=0, keepdims=True)

    def count16(ref, pred):
        def body(c2, acc):
            hit = jnp.where(pred(ref[pair(c2), :]), jnp.ones((), BF16), jnp.zeros((), BF16))
            parts = [hit[r * PACKED_ROWS:(r + 1) * PACKED_ROWS] for r in range(2 * tq // PACKED_ROWS)]
            while len(parts) > 1:
                parts = [parts[j] + parts[j + 1] for j in range(0, len(parts), 2)]
            return acc + parts[0].astype(F32)
        acc = lax.fori_loop(0, npair, body, jnp.zeros((PACKED_ROWS, tq), F32))
        return jnp.sum(acc, axis=0, keepdims=True)

    def kth_largest16(ref, k):
        def search_bit(b, carry):
            t_u, n_gt, n_ge = carry
            cand = t_u | (jnp.int32(1) << (15 - b))
            cand16 = (cand - HALF16).astype(I16)
            n = count16(ref, lambda v: v >= cand16)
            ok = n >= k
            return jnp.where(ok, cand, t_u), jnp.where(ok, n_gt, n), jnp.where(ok, n, n_ge)
        walked = (npair * (2 * tq)).astype(F32)
        init = (jnp.zeros((1, tq), I32), jnp.zeros((1, tq), F32), jnp.zeros((1, tq), F32) + walked)
        t_u, n_gt, n_ge = lax.fori_loop(0, 16, search_bit, init)
        return t_u - HALF16, n_gt, n_ge

    t_hi, n_gt_hi, _ = kth_largest16(hi_ref, float(topk))
    t_hi16 = t_hi.astype(I16)
    need_lo = float(topk) - n_gt_hi

    def bucket(c2, carry):
        lo2_ref[pair(c2), :] = jnp.where(hi_ref[pair(c2), :] == t_hi16, lo_ref[pair(c2), :],
                                         jnp.full((), -HALF16, I16))
        return carry

    lax.fori_loop(0, npair, bucket, 0)
    t_lo, _, n_ge_lo = kth_largest16(lo2_ref, need_lo)
    thr = jnp.maximum((t_hi << 16) | (t_lo + HALF16), INT_MIN + 1)
    n_ge = jnp.where(t_hi > -HALF16, n_gt_hi + n_ge_lo, 0.0)

    idx_bits = seq.bit_length() - 1

    @pl.when(jnp.max(n_ge) > topk)
    def _():
        need = topk - count(lambda k, _: k > thr)
        def bit(b, j):
            cand = j | (jnp.int32(1) << (idx_bits - 1 - b))
            below = count(lambda k, idx: (k == thr) & (idx < cand))
            return jnp.where(below < need, cand, j)
        bound = lax.fori_loop(0, idx_bits, bit, jnp.zeros((1, tq), I32))
        bound = jnp.where(n_ge > topk, bound, seq)

        def demote(c, carry):
            k = keys_ref[chunk(c), :]
            keys_ref[chunk(c), :] = jnp.where((k == thr) & (krow + c * tq > bound), thr - 1, k)
            return carry

        lax.fori_loop(0, nch, demote, 0)

    acc_ref[...] = jnp.zeros(acc_ref.shape, F32)

    def attend(first_pair, npairs, m):
        n = npairs * 2 * tq
        rows = pl.ds(pl.multiple_of(first_pair * (2 * tq), 2 * tq), n)
        bias = jnp.where(keys_ref[rows, :] >= thr, 0.0, NEG)
        st = jnp.dot(kb_ref[0, rows, :], qbt_ref[...], preferred_element_type=F32)
        st = st + jnp.concatenate([bias] * nh, axis=1)
        m_new = jnp.maximum(m, jnp.max(st, axis=0, keepdims=True))
        p = jnp.exp2(st - m_new).astype(BF16)
        pv = sum(jnp.dot(vt_ref[first_pair + j], p[j * 2 * tq:(j + 1) * 2 * tq], preferred_element_type=F32)
                 for j in range(npairs))
        acc_ref[...] = acc_ref[...] * jnp.exp2(m - m_new) + pv
        return m_new

    m = lax.fori_loop(0, npair // 2, lambda c4, m: attend(2 * c4, 2, m), jnp.full((1, nh * tq), NEG, F32))

    @pl.when(npair % 2 == 1)
    def _():
        attend(npair - 1, 1, m)

    out = acc_ref[:HEAD_DIM, :] / acc_ref[HEAD_DIM:HEAD_DIM + 1, :]
    for h in range(nh):
        o_ref[0, :, h * HEAD_DIM:(h + 1) * HEAD_DIM] = out[:, h * tq:(h + 1) * tq].T.astype(BF16)


def _dsa(qbt, kb, vb, qit, ki, wit, tq, to_cast):
    b, s, _ = kb.shape
    topk = min(TOPK_MAX, s // 4)
    assert s % (2 * tq) == 0
    assert 2 * tq // PACKED_ROWS <= 256
    nq = s // tq
    blk = lambda width: pl.BlockSpec((1, tq, width), lambda bb, i: (bb, i, 0))
    full = lambda width: pl.BlockSpec((1, s, width), lambda bb, i: (bb, 0, 0))
    qcol = lambda a: pl.BlockSpec((a.shape[0], tq), lambda bb, i: (0, bb * nq + i))
    casts = [_cast_block_specs(w, layer, b * nq, lambda bb, i: bb * nq + i) for w, layer in to_cast]
    out, *copies = pl.pallas_call(
        functools.partial(_dsa_body, topk, len(casts)),
        grid=(b, nq),
        in_specs=[qcol(qbt), full(HEAD_DIM), full(HEAD_DIM), qcol(qit), full(IDX_DIM), qcol(wit)]
        + [c[0] for c in casts],
        out_specs=[blk(DSA_HEADS * HEAD_DIM)] + [c[1] for c in casts],
        out_shape=[jax.ShapeDtypeStruct((b, s, DSA_HEADS * HEAD_DIM), BF16)] + [c[2] for c in casts],
        scratch_shapes=[pltpu.VMEM((s, tq), I32),
                        pltpu.VMEM((s, tq), I16),
                        pltpu.VMEM((s, tq), I16),
                        pltpu.VMEM((s, tq), I16),
                        pltpu.VMEM((HEAD_DIM, DSA_HEADS * tq), BF16),
                        pltpu.VMEM((s // (2 * tq), HEAD_DIM + ONES_ROWS, 2 * tq), BF16),
                        pltpu.VMEM((HEAD_DIM + ONES_ROWS, DSA_HEADS * tq), F32)],
        compiler_params=_params("parallel", "arbitrary"),
        name="dsa",
    )(qbt, kb, vb, qit, ki, wit, *[w for w, _ in to_cast])
    return out, copies


def _dilated_body(*refs):
    ng = len(DIL_PATTERNS)
    q_refs, k_refs, v_refs = refs[0:ng], refs[ng:2 * ng], refs[2 * ng:3 * ng]
    o_refs = refs[3 * ng:4 * ng]
    acc_ref, m_ref, l_ref = refs[4 * ng:]
    sb_tokens = q_refs[0].shape[1]
    t0 = pl.program_id(2) * sb_tokens
    blk = HEAD_DIM
    UNITS = 16
    rq = lax.broadcasted_iota(I32, (blk, blk), 0)
    ck = lax.broadcasted_iota(I32, (blk, blk), 1)
    bias_cur = jnp.where(ck <= rq, 0.0, NEG)
    bias_prev = jnp.where(ck >= rq, 0.0, NEG)
    nt = (((1,), (1,)), ((), ()))

    for g, (win, dil) in enumerate(DIL_PATTERNS):
        q_ref, k_ref, v_ref = q_refs[g], k_refs[g], v_refs[g]
        per_res = sb_tokens // dil // blk

        def rows(start, dil=dil):
            return pl.ds(start, blk, stride=dil) if dil > 1 else pl.ds(start, blk)

        def units(it, carry, g=g, dil=dil, per_res=per_res, rows=rows,
                  q_ref=q_ref, k_ref=k_ref, v_ref=v_ref):
            q0s, kcs, kps, hps = [], [], [], []
            for n in range(UNITS):
                idx = it * UNITS + n
                q0 = idx // per_res + (idx % per_res) * (blk * dil)
                k_cur = t0 + q0
                has_prev = k_cur >= blk * dil
                q0s.append(q0)
                kcs.append(k_cur)
                hps.append(has_prev)
                kps.append(jnp.where(has_prev, k_cur - blk * dil, k_cur))
            ss = []
            for q0, kc, kp, hp in zip(q0s, kcs, kps, hps):
                k2 = jnp.concatenate([k_ref[0, rows(kp), :], k_ref[0, rows(kc), :]], axis=0).astype(BF16)
                s = lax.dot_general(q_ref[0, rows(q0), :].astype(BF16), k2, nt, preferred_element_type=F32)
                ss.append(s + jnp.concatenate([jnp.where(hp, bias_prev, NEG), bias_cur], axis=1))
            ms = [jnp.max(jnp.maximum(s[:, :blk], s[:, blk:]), axis=-1, keepdims=True) for s in ss]
            ps = [jnp.exp2(s - m).astype(BF16) for s, m in zip(ss, ms)]
            for q0, kc, kp, m, p in zip(q0s, kcs, kps, ms, ps):
                v2 = jnp.concatenate([v_ref[0, rows(kp), :], v_ref[0, rows(kc), :]], axis=0).astype(BF16)
                av = jnp.dot(p, jnp.concatenate([v2, jnp.ones_like(v2)], axis=1), preferred_element_type=F32)
                acc_ref[g, rows(q0), :] = av[:, :HEAD_DIM]
                l_ref[g, rows(q0), :] = av[:, HEAD_DIM:]
                m_ref[g, rows(q0), :] = jnp.broadcast_to(m, (blk, LANES))
            return carry

        lax.fori_loop(0, dil * per_res // UNITS, units, 0)

    step = 256

    def merge(c, carry):
        sl = pl.ds(pl.multiple_of(c * step, step), step)
        ms = [m_ref[g, sl, :] for g in range(ng)]
        m_all = functools.reduce(jnp.maximum, ms)
        ws = [jnp.exp2(m - m_all) for m in ms]
        den = sum(w * l_ref[g, sl, :] for g, w in enumerate(ws))
        for g, w in enumerate(ws):
            o_refs[g][0, sl, :] = (acc_ref[g, sl, :] * (w / den)).astype(BF16)
        return carry

    lax.fori_loop(0, sb_tokens // step, merge, 0)


def _dilated(qc, kc, vc):
    b, s, _ = qc.shape
    ng, hg = len(DIL_PATTERNS), DIL_HEADS_PER_GROUP
    sb_tokens = HEAD_DIM * max(dil for _, dil in DIL_PATTERNS)
    assert s % sb_tokens == 0 and all(win == HEAD_DIM * dil for win, dil in DIL_PATTERNS)
    head = lambda g: (lambda bb, j, sb: (bb, sb, g * hg + j))
    head_full = lambda g: (lambda bb, j, sb: (bb, 0, g * hg + j))
    q_specs = [pl.BlockSpec((1, sb_tokens, HEAD_DIM), head(g)) for g in range(ng)]
    kv_specs = [pl.BlockSpec((1, s, HEAD_DIM), head_full(g)) for g in range(ng)]
    out_spec = pl.BlockSpec((1, sb_tokens, HEAD_DIM), lambda bb, j, sb: (bb, sb, j))
    return pl.pallas_call(
        _dilated_body,
        grid=(b, hg, s // sb_tokens),
        in_specs=q_specs + kv_specs + kv_specs,
        out_specs=[out_spec] * ng,
        out_shape=[jax.ShapeDtypeStruct((b, s, hg * HEAD_DIM), BF16)] * ng,
        scratch_shapes=[pltpu.VMEM((ng, sb_tokens, HEAD_DIM), F32)] * 3,
        compiler_params=_params("parallel", "parallel", "arbitrary"),
        name="dilated",
    )(*([qc] * ng + [kc] * ng + [vc] * ng))


def _out_proj_body(*refs):
    *y_refs, w_ref, x_ref, o_ref, wb_ref = refs

    @pl.when(pl.program_id(1) == 0)
    def _():
        wb_ref[...] = w_ref[...].astype(BF16)

    y = jnp.concatenate([y_ref[...] for y_ref in y_refs], axis=1)
    o_ref[...] = x_ref[...] + jnp.dot(y, wb_ref[...], preferred_element_type=F32)


def _out_proj(parts, w, x, layer, tm, tn):
    t, d = x.shape
    assert sum(a.shape[1] for a in parts) == w.shape[1]
    return pl.pallas_call(
        _out_proj_body,
        grid=(d // tn, t // tm),
        in_specs=[pl.BlockSpec((tm, a.shape[1]), lambda n, m: (m, 0)) for a in parts]
        + [pl.BlockSpec((None, w.shape[1], tn), lambda n, m: (layer, 0, n)),
           pl.BlockSpec((tm, tn), lambda n, m: (m, n))],
        out_specs=pl.BlockSpec((tm, tn), lambda n, m: (m, n)),
        out_shape=jax.ShapeDtypeStruct((t, d), F32),
        scratch_shapes=[pltpu.VMEM((w.shape[1], tn), BF16)],
        compiler_params=_params("parallel", "arbitrary"),
        name="out_proj",
    )(*parts, w, x)


def _final_norm_body(x_ref, g_ref, o_ref):
    o_ref[...] = _rms(x_ref[...], g_ref[...])


def _final_norm(x, g, tm):
    t, d = x.shape
    return pl.pallas_call(
        _final_norm_body,
        grid=(t // tm,),
        in_specs=[pl.BlockSpec((tm, d), lambda m: (m, 0)), pl.BlockSpec((1, d), lambda m: (0, 0))],
        out_specs=pl.BlockSpec((tm, d), lambda m: (m, 0)),
        out_shape=jax.ShapeDtypeStruct((t, d), F32),
        compiler_params=_params("parallel"),
        name="final_norm",
    )(x, g)


def _align_w_in_body(x_ref, o_ref):
    o_wi = SEG_QI[1] + IDX_DIM
    o_qc = o_wi + IDX_HEADS
    rows = x_ref.shape[0]
    o_ref[:, :o_wi] = x_ref[:, :o_wi]
    o_ref[:, o_wi:SEG_WI[0]] = jnp.zeros((rows, SEG_WI[0] - o_wi), BF16)
    o_ref[:, SEG_WI[0]:SEG_WI[0] + IDX_HEADS] = x_ref[:, o_wi:o_qc]
    o_ref[:, SEG_WI[0] + IDX_HEADS:SEG_QC[0]] = jnp.zeros((rows, LANES - IDX_HEADS), BF16)
    o_ref[:, SEG_QC[0]:] = x_ref[:, o_qc:o_qc + D_IN_ALIGNED - SEG_QC[0]]


def _align_w_in(w, tr):
    depth, d, n = w.shape
    assert n - (SEG_QI[1] + IDX_DIM + IDX_HEADS) == D_IN_ALIGNED - SEG_QC[0]
    w = jnp.pad(w.astype(BF16), ((0, 0), (0, 0), (0, D_IN_ALIGNED - n)))
    return pl.pallas_call(
        _align_w_in_body,
        grid=(depth, d // tr),
        in_specs=[pl.BlockSpec((None, tr, D_IN_ALIGNED), lambda l, r: (l, r, 0))],
        out_specs=pl.BlockSpec((None, tr, D_IN_ALIGNED), lambda l, r: (l, r, 0)),
        out_shape=jax.ShapeDtypeStruct((depth, d, D_IN_ALIGNED), BF16),
        compiler_params=_params("parallel", "parallel"),
        name="align_w_in",
    )(w)


def _tile(n, want):
    while n % want:
        want //= 2
    return want


class _Tiles(NamedTuple):
    tm: int
    tn_ffn: int
    tn_out: int
    tm_proj: int
    tn_down_f32: int
    tq: int
    tr_align: int


def _tiles(t, s, d, f):
    return _Tiles(tm=_tile(t, 1024), tn_ffn=_tile(f, 512), tn_out=_tile(d, 1024), tm_proj=_tile(s, 512),
                  tn_down_f32=_tile(d, 256), tq=_tile(s, 256), tr_align=_tile(d, 256))


def kernel(x, positions, norm_ffn1, ffn1_gate, ffn1_up, ffn1_down, norm_mix, w_in, conv_w, w_out,
           norm_ffn2, ffn2_gate, ffn2_up, ffn2_down, norm_final):
    b, s, d = x.shape
    t = b * s
    depth = w_in.shape[0]
    tl = _tiles(t, s, d, ffn1_gate.shape[2])

    def ffn(xf, g, weights, layer):
        wg, wu, wd = weights
        h = _ffn_up(xf, g.reshape(depth, 1, d), wg, wu, layer, tl.tm, tl.tn_ffn)
        return _ffn_down(h, wd, xf, tl.tm, tl.tn_down_f32 if isinstance(wd, tuple) else tl.tn_ffn)

    ffn1_stacks, ffn2_stacks = (ffn1_gate, ffn1_up, ffn1_down), (ffn2_gate, ffn2_up, ffn2_down)
    w1 = [(w, 0) for w in ffn1_stacks]
    pos = positions.astype(F32).reshape(t, 1)
    tabs = _rope_tables(pos, tl.tm)
    w_in_al = _align_w_in(w_in, tl.tr_align)
    xf = x.reshape(t, d)
    for i in range(depth):
        xf = ffn(xf, norm_ffn1, w1, i)
        ya, qb, kb, vb, qi, ki, wi, qc, kc, vc = _in_proj(
            xf, norm_mix.reshape(depth, 1, d), w_in_al, conv_w, tabs, i, s, tl.tm_proj)
        r3 = lambda a: a.reshape(b, s, a.shape[-1])
        to_cast = [(w, i) for w in ffn2_stacks] + ([(w, i + 1) for w in ffn1_stacks] if i + 1 < depth else [])
        yb, copies = _dsa(qb, r3(kb), r3(vb), qi, r3(ki), wi, tl.tq, to_cast)
        w2, w1 = copies[:3], copies[3:]
        ycs = _dilated(r3(qc), r3(kc), r3(vc))
        parts = [ya, yb.reshape(t, -1)] + [yc.reshape(t, -1) for yc in ycs]
        xf = _out_proj(parts, w_out, xf, i, tl.tm, tl.tn_out)
        xf = ffn(xf, norm_ffn2, w2, i)
    return _final_norm(xf, norm_final.reshape(1, d), tl.tm).reshape(b, s, d)
```

```python
import functools
from typing import NamedTuple

import jax
import jax.numpy as jnp
from jax import lax
from jax.experimental import pallas as pl
from jax.experimental.pallas import tpu as pltpu

F32 = jnp.float32
BF16 = jnp.bfloat16
I32 = jnp.int32
I16 = jnp.int16
HALF16 = 1 << 15
PACKED_ROWS = 16
ONES_ROWS = PACKED_ROWS

HEAD_DIM = 128
CONV_CH = 512
CONV_WIDTH = 3
DSA_HEADS = 6
IDX_HEADS = 16
IDX_DIM = 64
TOPK_MAX = 256
DIL_PATTERNS = ((128, 1), (512, 4), (2048, 16))
DIL_HEADS_PER_GROUP = 2
DIL_HEADS = len(DIL_PATTERNS) * DIL_HEADS_PER_GROUP
ROPE_THETA = 10000.0
RMS_EPS = 1e-6

LANES = 128
SUBLANES = 8
VMEM_LIMIT = 56 * 1024 * 1024
NEG = -1e30
INT_MIN = -2 ** 31
LOG2E = 1.4426950408889634

SEG_CONV = (0, 3 * CONV_CH)
SEG_QB = (SEG_CONV[1], SEG_CONV[1] + DSA_HEADS * HEAD_DIM)
SEG_KV = (SEG_QB[1], SEG_QB[1] + 2 * HEAD_DIM)
SEG_QI = (SEG_KV[1], SEG_KV[1] + IDX_HEADS * IDX_DIM)
SEG_KI = (SEG_QI[1], SEG_QI[1] + LANES)
SEG_WI = (SEG_KI[1], SEG_KI[1] + LANES)
SEG_QC = (SEG_WI[1], SEG_WI[1] + DIL_HEADS * HEAD_DIM)
SEG_KC = (SEG_QC[1], SEG_QC[1] + DIL_HEADS * HEAD_DIM)
SEG_VC = (SEG_KC[1], SEG_KC[1] + DIL_HEADS * HEAD_DIM)
D_IN_ALIGNED = SEG_VC[1]


def _params(*sem):
    return pltpu.CompilerParams(dimension_semantics=sem, vmem_limit_bytes=VMEM_LIMIT)


def _rms(x, g):
    ms = jnp.mean(x * x, axis=-1, keepdims=True)
    return x * lax.rsqrt(ms + RMS_EPS) * g


def _ffn_up_body(x_ref, g_ref, wg_ref, wu_ref, h_ref, xn_ref):
    @pl.when(pl.program_id(1) == 0)
    def _():
        xn_ref[...] = _rms(x_ref[...], g_ref[...]).astype(BF16)

    xn = xn_ref[...]
    a = jnp.dot(xn, wg_ref[...].astype(BF16), preferred_element_type=F32)
    b = jnp.dot(xn, wu_ref[...].astype(BF16), preferred_element_type=F32)
    h_ref[...] = (a * jax.nn.sigmoid(a) * b).astype(BF16)


def _weight_tile(w, tn):
    if isinstance(w, tuple):
        stack, layer = w
        return stack, pl.BlockSpec((None, stack.shape[1], tn), lambda m, n: (layer, 0, n))
    return w, pl.BlockSpec((w.shape[0], tn), lambda m, n: (0, n))


def _ffn_up(x, g, wg, wu, layer, tm, tn):
    t, d = x.shape
    (wg, wg_spec), (wu, wu_spec) = _weight_tile(wg, tn), _weight_tile(wu, tn)
    f = wg.shape[-1]
    return pl.pallas_call(
        _ffn_up_body,
        grid=(t // tm, f // tn),
        in_specs=[
            pl.BlockSpec((tm, d), lambda m, n: (m, 0)),
            pl.BlockSpec((None, 1, d), lambda m, n: (layer, 0, 0)),
            wg_spec,
            wu_spec,
        ],
        out_specs=pl.BlockSpec((tm, tn), lambda m, n: (m, n)),
        out_shape=jax.ShapeDtypeStruct((t, f), BF16),
        scratch_shapes=[pltpu.VMEM((tm, d), BF16)],
        compiler_params=_params("parallel", "arbitrary"),
        name="ffn_up",
    )(x, g, wg, wu)


def _ffn_down_body(h_ref, w_ref, x_ref, o_ref):
    y = jnp.dot(h_ref[...], w_ref[...].astype(BF16), preferred_element_type=F32)
    o_ref[...] = x_ref[...] + 0.5 * y


def _ffn_down(h, wd, x, tm, tn):
    t, f = h.shape
    wd, wd_spec = _weight_tile(wd, tn)
    d = wd.shape[-1]
    return pl.pallas_call(
        _ffn_down_body,
        grid=(t // tm, d // tn),
        in_specs=[
            pl.BlockSpec((tm, f), lambda m, n: (m, 0)),
            wd_spec,
            pl.BlockSpec((tm, tn), lambda m, n: (m, n)),
        ],
        out_specs=pl.BlockSpec((tm, tn), lambda m, n: (m, n)),
        out_shape=jax.ShapeDtypeStruct((t, d), F32),
        compiler_params=_params("parallel", "arbitrary"),
        name="ffn_down",
    )(h, wd, x)


def _cast_block_specs(w, layer, nsteps, step_of):
    _, rows, cols = w.shape
    nblk = nsteps
    while rows % nblk or (rows // nblk) % PACKED_ROWS:
        nblk //= 2
    blk = rows // nblk
    return (pl.BlockSpec((None, blk, cols), lambda *g: (layer, step_of(*g) * nblk // nsteps, 0)),
            pl.BlockSpec((blk, cols), lambda *g: (step_of(*g) * nblk // nsteps, 0)),
            jax.ShapeDtypeStruct((rows, cols), BF16))


def _rope_table_body(pos_ref, inv_ref, ch_ref, sh_ref, ci_ref, si_ref):
    ang = pos_ref[...] * inv_ref[...]
    cos, sin = jnp.cos(ang), jnp.sin(ang)
    hh, hi = HEAD_DIM // 2, IDX_DIM // 2
    ch_ref[...] = jnp.concatenate([cos[:, :hh]] * 2, axis=1)
    sh_ref[...] = jnp.concatenate([-sin[:, :hh], sin[:, :hh]], axis=1)
    ci_ref[...] = jnp.concatenate([cos[:, hh:hh + hi]] * (LANES // hi), axis=1)
    si_ref[...] = jnp.concatenate([-sin[:, hh:hh + hi], sin[:, hh:hh + hi]] * (LANES // IDX_DIM), axis=1)


def _rope_tables(pos, tm):
    t = pos.shape[0]
    inv = lambda dim: 1.0 / (ROPE_THETA ** (jnp.arange(0, dim, 2, dtype=F32) / dim))
    inv_l = jnp.concatenate([inv(HEAD_DIM), inv(IDX_DIM), jnp.zeros((LANES - (HEAD_DIM + IDX_DIM) // 2,), F32)])
    tab = pl.BlockSpec((tm, LANES), lambda m: (m, 0))
    return pl.pallas_call(
        _rope_table_body,
        grid=(t // tm,),
        in_specs=[pl.BlockSpec((tm, 1), lambda m: (m, 0)), pl.BlockSpec((1, LANES), lambda m: (0, 0))],
        out_specs=[tab] * 4,
        out_shape=[jax.ShapeDtypeStruct((t, LANES), F32)] * 4,
        compiler_params=_params("parallel"),
        name="rope_tables",
    )(pos, inv_l.reshape(1, LANES))


def _rope128(x, cos, sin):
    return x * cos + pltpu.roll(x, HEAD_DIM // 2, 1) * sin


def _rope64(x, cos, sin, lo_half):
    partner = jnp.where(lo_half, pltpu.roll(x, LANES - IDX_DIM // 2, 1), pltpu.roll(x, IDX_DIM // 2, 1))
    return x * cos + partner * sin


def _in_proj_body(seq_tiles, x_ref, g_ref, w_ref, cw_ref, ch_ref, sh_ref, ci_ref, si_ref,
                  ya_ref, qb_ref, kb_ref, vb_ref, qi_ref, ki_ref, wi_ref, qc_ref, kc_ref, vc_ref,
                  u_ref):
    tm = x_ref.shape[0]
    xn = _rms(x_ref[...], g_ref[...]).astype(BF16)

    def proj(seg):
        return jnp.dot(xn, w_ref[:, seg[0]:seg[1]], preferred_element_type=F32)

    ch, sh, ci, si = ch_ref[...], sh_ref[...], ci_ref[...], si_ref[...]
    scale = HEAD_DIM ** -0.5 * LOG2E

    def rope_heads(p, n, mul):
        return jnp.concatenate(
            [_rope128(p[:, j * LANES:(j + 1) * LANES], ch, sh) * mul for j in range(n)], axis=1)

    qb_ref[...] = rope_heads(proj(SEG_QB), DSA_HEADS, scale).T.astype(BF16)
    p = proj(SEG_KV)
    kb_ref[...] = _rope128(p[:, :HEAD_DIM], ch, sh).astype(BF16)
    vb_ref[...] = p[:, HEAD_DIM:].astype(BF16)

    lo_half = lax.broadcasted_iota(I32, (tm, LANES), 1) % IDX_DIM < IDX_DIM // 2
    p = proj(SEG_QI)
    qi_ref[...] = jnp.concatenate(
        [_rope64(p[:, j * LANES:(j + 1) * LANES], ci, si, lo_half) * (IDX_DIM ** -0.5)
         for j in range(IDX_HEADS * IDX_DIM // LANES)], axis=1).T.astype(BF16)
    ki_ref[...] = _rope64(proj(SEG_KI), ci, si, lo_half)[:, :IDX_DIM].astype(BF16)
    wi_ref[...] = (proj(SEG_WI) * (IDX_HEADS ** -0.5)).T[:IDX_HEADS, :]

    qc_ref[...] = rope_heads(proj(SEG_QC), DIL_HEADS, scale)
    kc_ref[...] = rope_heads(proj(SEG_KC), DIL_HEADS, 1.0)

    h, gate_b, gate_c = (proj((SEG_CONV[0] + j * CONV_CH, SEG_CONV[0] + (j + 1) * CONV_CH)) for j in range(3))
    u = gate_c * h

    carry = jnp.where(pl.program_id(0) % seq_tiles == 0, 0.0, u_ref[...])
    ext = jnp.concatenate([carry, u], axis=0)
    cw = cw_ref[...]
    y = (cw[2:3, :] * u + cw[1:2, :] * pltpu.roll(ext, 1, 0)[SUBLANES:, :]
         + cw[0:1, :] * pltpu.roll(ext, 2, 0)[SUBLANES:, :])
    u_ref[...] = u[tm - SUBLANES:, :]
    ya_ref[...] = (gate_b * y).astype(BF16)

    vc_ref[...] = proj(SEG_VC)


def _in_proj(x, g, w, cw, tabs, layer, seq, tm):
    t, d = x.shape
    n = w.shape[2]
    row = lambda width: pl.BlockSpec((tm, width), lambda m: (m, 0))
    col = lambda height: pl.BlockSpec((height, tm), lambda m: (0, m))
    const = lambda shape: pl.BlockSpec((None,) + shape, lambda m: (layer, 0, 0))
    outs = [(CONV_CH, BF16, True), (DSA_HEADS * HEAD_DIM, BF16, False), (HEAD_DIM, BF16, True),
            (HEAD_DIM, BF16, True), (IDX_HEADS * IDX_DIM, BF16, False), (IDX_DIM, BF16, True),
            (IDX_HEADS, F32, False), (DIL_HEADS * HEAD_DIM, F32, True), (DIL_HEADS * HEAD_DIM, F32, True),
            (DIL_HEADS * HEAD_DIM, F32, True)]
    return pl.pallas_call(
        functools.partial(_in_proj_body, seq // tm),
        grid=(t // tm,),
        in_specs=[row(d), const((1, d)),
                  pl.BlockSpec((None, d, n), lambda m: (layer, 0, 0), pipeline_mode=pl.Buffered(1)),
                  const((CONV_WIDTH, CONV_CH)), row(LANES), row(LANES), row(LANES), row(LANES)],
        out_specs=[row(wd) if tok else col(wd) for wd, _, tok in outs],
        out_shape=[jax.ShapeDtypeStruct((t, wd) if tok else (wd, t), dt) for wd, dt, tok in outs],
        scratch_shapes=[pltpu.VMEM((SUBLANES, CONV_CH), F32)],
        compiler_params=_params("arbitrary"),
        name="in_proj",
    )(x, g, w, cw, *tabs)


def _dsa_body(topk, ncast, qbt_in_ref, kb_ref, vb_ref, qt_ref, ki_ref, wt_ref, *rest):
    cast_in, (o_ref, *cast_out) = rest[:ncast], rest[ncast:2 * ncast + 1]
    keys_ref, hi_ref, lo_ref, lo2_ref, qbt_ref, vt_ref, acc_ref = rest[2 * ncast + 1:]
    tq = o_ref.shape[1]
    seq = kb_ref.shape[1]

    for src_ref, dst_ref in zip(cast_in, cast_out):
        dst_ref[...] = src_ref[...].astype(BF16)

    nh = DSA_HEADS
    i = pl.program_id(1)
    nch = i + 1

    def chunk(c):
        return pl.ds(pl.multiple_of(c * tq, tq), tq)

    def transpose_bf16(a):
        return a.astype(F32).T.astype(BF16)

    @pl.when(i == 0)
    def _():
        for c2 in range(seq // (2 * tq)):
            vt_ref[c2, :HEAD_DIM, :] = transpose_bf16(vb_ref[0, c2 * 2 * tq:(c2 + 1) * 2 * tq, :])
            vt_ref[c2, HEAD_DIM:, :] = jnp.ones((ONES_ROWS, 2 * tq), BF16)

    for h in range(nh):
        qbt_ref[:, h * tq:(h + 1) * tq] = qbt_in_ref[h * HEAD_DIM:(h + 1) * HEAD_DIM, :]
    wt = wt_ref[...]
    krow = lax.broadcasted_iota(I32, (tq, tq), 0)

    def pair(c2):
        return pl.ds(pl.multiple_of(c2 * (2 * tq), 2 * tq), 2 * tq)

    def score_rows(rows, n, first_key):
        kc = ki_ref[0, rows, :]
        acc = jnp.zeros((n, tq), F32)
        for h in range(IDX_HEADS):
            lg = jnp.dot(kc, qt_ref[h * IDX_DIM:(h + 1) * IDX_DIM, :], preferred_element_type=F32)
            acc = acc + jnp.maximum(lg, 0.0) * wt[h:h + 1, :]
        bits = pltpu.bitcast(acc, I32)
        key = bits ^ ((bits >> 31) & 0x7FFFFFFF)
        causal = lax.broadcasted_iota(I32, (n, tq), 0) + first_key <= lax.broadcasted_iota(I32, (n, tq), 1) + i * tq
        key = jnp.where(causal, key, INT_MIN)
        keys_ref[rows, :] = key
        hi_ref[rows, :] = (key >> 16).astype(I16)
        lo_ref[rows, :] = ((key & 0xFFFF) - HALF16).astype(I16)

    def score_quad(c4, carry):
        score_rows(pl.ds(pl.multiple_of(c4 * (4 * tq), 4 * tq), 4 * tq), 4 * tq, c4 * (4 * tq))
        return carry

    lax.fori_loop(0, nch // 4, score_quad, 0)

    @pl.when(nch % 4 >= 2)
    def _():
        score_rows(pair(nch // 4 * 2), 2 * tq, nch // 4 * (4 * tq))

    npair = (nch + 1) // 2

    @pl.when(nch % 2 == 1)
    def _():
        score_rows(chunk(nch - 1), tq, (nch - 1) * tq)
        keys_ref[chunk(nch), :] = jnp.full((tq, tq), INT_MIN, I32)
        hi_ref[chunk(nch), :] = jnp.full((tq, tq), -HALF16, I16)
        lo_ref[chunk(nch), :] = jnp.full((tq, tq), -HALF16, I16)

    def count(pred):
        def body(c, acc):
            hit = pred(keys_ref[chunk(c), :], krow + c * tq).astype(I32)
            return acc + jnp.sum(hit.reshape(tq // 8, 8, tq), axis=0)
        acc = lax.fori_loop(0, nch, body, jnp.zeros((8, tq), I32))
        return jnp.sum(acc, axis=0, keepdims=True)

    def count16(ref, pred):
        def body(c2, acc):
            hit = jnp.where(pred(ref[pair(c2), :]), jnp.ones((), BF16), jnp.zeros((), BF16))
            parts = [hit[r * PACKED_ROWS:(r + 1) * PACKED_ROWS] for r in range(2 * tq // PACKED_ROWS)]
            while len(parts) > 1:
                parts = [parts[j] + parts[j + 1] for j in range(0, len(parts), 2)]
            return acc + parts[0].astype(F32)
        acc = lax.fori_loop(0, npair, body, jnp.zeros((PACKED_ROWS, tq), F32))
        return jnp.sum(acc, axis=0, keepdims=True)

    def kth_largest16(ref, k):
        def search_bit(b, carry):
            t_u, n_gt, n_ge = carry
            cand = t_u | (jnp.int32(1) << (15 - b))
            cand16 = (cand - HALF16).astype(I16)
            n = count16(ref, lambda v: v >= cand16)
            ok = n >= k
            return jnp.where(ok, cand, t_u), jnp.where(ok, n_gt, n), jnp.where(ok, n, n_ge)
        walked = (npair * (2 * tq)).astype(F32)
        init = (jnp.zeros((1, tq), I32), jnp.zeros((1, tq), F32), jnp.zeros((1, tq), F32) + walked)
        t_u, n_gt, n_ge = lax.fori_loop(0, 16, search_bit, init)
        return t_u - HALF16, n_gt, n_ge

    t_hi, n_gt_hi, _ = kth_largest16(hi_ref, float(topk))
    t_hi16 = t_hi.astype(I16)
    need_lo = float(topk) - n_gt_hi

    def bucket(c2, carry):
        lo2_ref[pair(c2), :] = jnp.where(hi_ref[pair(c2), :] == t_hi16, lo_ref[pair(c2), :],
                                         jnp.full((), -HALF16, I16))
        return carry

    lax.fori_loop(0, npair, bucket, 0)
    t_lo, _, n_ge_lo = kth_largest16(lo2_ref, need_lo)
    thr = jnp.maximum((t_hi << 16) | (t_lo + HALF16), INT_MIN + 1)
    n_ge = jnp.where(t_hi > -HALF16, n_gt_hi + n_ge_lo, 0.0)

    idx_bits = seq.bit_length() - 1

    @pl.when(jnp.max(n_ge) > topk)
    def _():
        need = topk - count(lambda k, _: k > thr)
        def bit(b, j):
            cand = j | (jnp.int32(1) << (idx_bits - 1 - b))
            below = count(lambda k, idx: (k == thr) & (idx < cand))
            return jnp.where(below < need, cand, j)
        bound = lax.fori_loop(0, idx_bits, bit, jnp.zeros((1, tq), I32))
        bound = jnp.where(n_ge > topk, bound, seq)

        def demote(c, carry):
            k = keys_ref[chunk(c), :]
            keys_ref[chunk(c), :] = jnp.where((k == thr) & (krow + c * tq > bound), thr - 1, k)
            return carry

        lax.fori_loop(0, nch, demote, 0)

    acc_ref[...] = jnp.zeros(acc_ref.shape, F32)

    def attend(first_pair, npairs, m):
        n = npairs * 2 * tq
        rows = pl.ds(pl.multiple_of(first_pair * (2 * tq), 2 * tq), n)
        bias = jnp.where(keys_ref[rows, :] >= thr, 0.0, NEG)
        st = jnp.dot(kb_ref[0, rows, :], qbt_ref[...], preferred_element_type=F32)
        st = st + jnp.concatenate([bias] * nh, axis=1)
        m_new = jnp.maximum(m, jnp.max(st, axis=0, keepdims=True))
        p = jnp.exp2(st - m_new).astype(BF16)
        pv = sum(jnp.dot(vt_ref[first_pair + j], p[j * 2 * tq:(j + 1) * 2 * tq], preferred_element_type=F32)
                 for j in range(npairs))
        acc_ref[...] = acc_ref[...] * jnp.exp2(m - m_new) + pv
        return m_new

    m = lax.fori_loop(0, npair // 2, lambda c4, m: attend(2 * c4, 2, m), jnp.full((1, nh * tq), NEG, F32))

    @pl.when(npair % 2 == 1)
    def _():
        attend(npair - 1, 1, m)

    out = acc_ref[:HEAD_DIM, :] / acc_ref[HEAD_DIM:HEAD_DIM + 1, :]
    for h in range(nh):
        o_ref[0, :, h * HEAD_DIM:(h + 1) * HEAD_DIM] = out[:, h * tq:(h + 1) * tq].T.astype(BF16)


def _dsa(qbt, kb, vb, qit, ki, wit, tq, to_cast):
    b, s, _ = kb.shape
    topk = min(TOPK_MAX, s // 4)
    assert s % (2 * tq) == 0
    assert 2 * tq // PACKED_ROWS <= 256
    nq = s // tq
    blk = lambda width: pl.BlockSpec((1, tq, width), lambda bb, i: (bb, i, 0))
    full = lambda width: pl.BlockSpec((1, s, width), lambda bb, i: (bb, 0, 0))
    qcol = lambda a: pl.BlockSpec((a.shape[0], tq), lambda bb, i: (0, bb * nq + i))
    casts = [_cast_block_specs(w, layer, b * nq, lambda bb, i: bb * nq + i) for w, layer in to_cast]
    out, *copies = pl.pallas_call(
        functools.partial(_dsa_body, topk, len(casts)),
        grid=(b, nq),
        in_specs=[qcol(qbt), full(HEAD_DIM), full(HEAD_DIM), qcol(qit), full(IDX_DIM), qcol(wit)]
        + [c[0] for c in casts],
        out_specs=[blk(DSA_HEADS * HEAD_DIM)] + [c[1] for c in casts],
        out_shape=[jax.ShapeDtypeStruct((b, s, DSA_HEADS * HEAD_DIM), BF16)] + [c[2] for c in casts],
        scratch_shapes=[pltpu.VMEM((s, tq), I32),
                        pltpu.VMEM((s, tq), I16),
                        pltpu.VMEM((s, tq), I16),
                        pltpu.VMEM((s, tq), I16),
                        pltpu.VMEM((HEAD_DIM, DSA_HEADS * tq), BF16),
                        pltpu.VMEM((s // (2 * tq), HEAD_DIM + ONES_ROWS, 2 * tq), BF16),
                        pltpu.VMEM((HEAD_DIM + ONES_ROWS, DSA_HEADS * tq), F32)],
        compiler_params=_params("parallel", "arbitrary"),
        name="dsa",
    )(qbt, kb, vb, qit, ki, wit, *[w for w, _ in to_cast])
    return out, copies


def _dilated_body(*refs):
    ng = len(DIL_PATTERNS)
    q_refs, k_refs, v_refs = refs[0:ng], refs[ng:2 * ng], refs[2 * ng:3 * ng]
    o_refs = refs[3 * ng:4 * ng]
    acc_ref, m_ref, l_ref = refs[4 * ng:]
    sb_tokens = q_refs[0].shape[1]
    t0 = pl.program_id(2) * sb_tokens
    blk = HEAD_DIM
    UNITS = 16
    rq = lax.broadcasted_iota(I32, (blk, blk), 0)
    ck = lax.broadcasted_iota(I32, (blk, blk), 1)
    bias_cur = jnp.where(ck <= rq, 0.0, NEG)
    bias_prev = jnp.where(ck >= rq, 0.0, NEG)
    nt = (((1,), (1,)), ((), ()))

    for g, (win, dil) in enumerate(DIL_PATTERNS):
        q_ref, k_ref, v_ref = q_refs[g], k_refs[g], v_refs[g]
        per_res = sb_tokens // dil // blk

        def rows(start, dil=dil):
            return pl.ds(start, blk, stride=dil) if dil > 1 else pl.ds(start, blk)

        def units(it, carry, g=g, dil=dil, per_res=per_res, rows=rows,
                  q_ref=q_ref, k_ref=k_ref, v_ref=v_ref):
            q0s, kcs, kps, hps = [], [], [], []
            for n in range(UNITS):
                idx = it * UNITS + n
                q0 = idx // per_res + (idx % per_res) * (blk * dil)
                k_cur = t0 + q0
                has_prev = k_cur >= blk * dil
                q0s.append(q0)
                kcs.append(k_cur)
                hps.append(has_prev)
                kps.append(jnp.where(has_prev, k_cur - blk * dil, k_cur))
            ss = []
            for q0, kc, kp, hp in zip(q0s, kcs, kps, hps):
                k2 = jnp.concatenate([k_ref[0, rows(kp), :], k_ref[0, rows(kc), :]], axis=0).astype(BF16)
                s = lax.dot_general(q_ref[0, rows(q0), :].astype(BF16), k2, nt, preferred_element_type=F32)
                ss.append(s + jnp.concatenate([jnp.where(hp, bias_prev, NEG), bias_cur], axis=1))
            ms = [jnp.max(jnp.maximum(s[:, :blk], s[:, blk:]), axis=-1, keepdims=True) for s in ss]
            ps = [jnp.exp2(s - m).astype(BF16) for s, m in zip(ss, ms)]
            for q0, kc, kp, m, p in zip(q0s, kcs, kps, ms, ps):
                v2 = jnp.concatenate([v_ref[0, rows(kp), :], v_ref[0, rows(kc), :]], axis=0).astype(BF16)
                av = jnp.dot(p, jnp.concatenate([v2, jnp.ones_like(v2)], axis=1), preferred_element_type=F32)
                acc_ref[g, rows(q0), :] = av[:, :HEAD_DIM]
                l_ref[g, rows(q0), :] = av[:, HEAD_DIM:]
                m_ref[g, rows(q0), :] = jnp.broadcast_to(m, (blk, LANES))
            return carry

        lax.fori_loop(0, dil * per_res // UNITS, units, 0)

    step = 256

    def merge(c, carry):
        sl = pl.ds(pl.multiple_of(c * step, step), step)
        ms = [m_ref[g, sl, :] for g in range(ng)]
        m_all = functools.reduce(jnp.maximum, ms)
        ws = [jnp.exp2(m - m_all) for m in ms]
        den = sum(w * l_ref[g, sl, :] for g, w in enumerate(ws))
        for g, w in enumerate(ws):
            o_refs[g][0, sl, :] = (acc_ref[g, sl, :] * (w / den)).astype(BF16)
        return carry

    lax.fori_loop(0, sb_tokens // step, merge, 0)


def _dilated(qc, kc, vc):
    b, s, _ = qc.shape
    ng, hg = len(DIL_PATTERNS), DIL_HEADS_PER_GROUP
    sb_tokens = HEAD_DIM * max(dil for _, dil in DIL_PATTERNS)
    assert s % sb_tokens == 0 and all(win == HEAD_DIM * dil for win, dil in DIL_PATTERNS)
    head = lambda g: (lambda bb, j, sb: (bb, sb, g * hg + j))
    head_full = lambda g: (lambda bb, j, sb: (bb, 0, g * hg + j))
    q_specs = [pl.BlockSpec((1, sb_tokens, HEAD_DIM), head(g)) for g in range(ng)]
    kv_specs = [pl.BlockSpec((1, s, HEAD_DIM), head_full(g)) for g in range(ng)]
    out_spec = pl.BlockSpec((1, sb_tokens, HEAD_DIM), lambda bb, j, sb: (bb, sb, j))
    return pl.pallas_call(
        _dilated_body,
        grid=(b, hg, s // sb_tokens),
        in_specs=q_specs + kv_specs + kv_specs,
        out_specs=[out_spec] * ng,
        out_shape=[jax.ShapeDtypeStruct((b, s, hg * HEAD_DIM), BF16)] * ng,
        scratch_shapes=[pltpu.VMEM((ng, sb_tokens, HEAD_DIM), F32)] * 3,
        compiler_params=_params("parallel", "parallel", "arbitrary"),
        name="dilated",
    )(*([qc] * ng + [kc] * ng + [vc] * ng))


def _out_proj_body(*refs):
    *y_refs, w_ref, x_ref, o_ref = refs
    y = jnp.concatenate([y_ref[...] for y_ref in y_refs], axis=1)
    o_ref[...] = x_ref[...] + jnp.dot(y, w_ref[...], preferred_element_type=F32)


def _out_proj(parts, w, x, tm):
    t, d = x.shape
    assert sum(a.shape[1] for a in parts) == w.shape[0]
    return pl.pallas_call(
        _out_proj_body,
        grid=(t // tm,),
        in_specs=[pl.BlockSpec((tm, a.shape[1]), lambda m: (m, 0)) for a in parts]
        + [pl.BlockSpec(w.shape, lambda m: (0, 0), pipeline_mode=pl.Buffered(1)),
           pl.BlockSpec((tm, d), lambda m: (m, 0))],
        out_specs=pl.BlockSpec((tm, d), lambda m: (m, 0)),
        out_shape=jax.ShapeDtypeStruct((t, d), F32),
        compiler_params=_params("parallel"),
        name="out_proj",
    )(*parts, w, x)


def _final_norm_body(x_ref, g_ref, o_ref):
    o_ref[...] = _rms(x_ref[...], g_ref[...])


def _final_norm(x, g, tm):
    t, d = x.shape
    return pl.pallas_call(
        _final_norm_body,
        grid=(t // tm,),
        in_specs=[pl.BlockSpec((tm, d), lambda m: (m, 0)), pl.BlockSpec((1, d), lambda m: (0, 0))],
        out_specs=pl.BlockSpec((tm, d), lambda m: (m, 0)),
        out_shape=jax.ShapeDtypeStruct((t, d), F32),
        compiler_params=_params("parallel"),
        name="final_norm",
    )(x, g)


def _align_w_in_body(x_ref, o_ref):
    o_wi = SEG_QI[1] + IDX_DIM
    o_qc = o_wi + IDX_HEADS
    rows = x_ref.shape[0]
    o_ref[:, :o_wi] = x_ref[:, :o_wi]
    o_ref[:, o_wi:SEG_WI[0]] = jnp.zeros((rows, SEG_WI[0] - o_wi), BF16)
    o_ref[:, SEG_WI[0]:SEG_WI[0] + IDX_HEADS] = x_ref[:, o_wi:o_qc]
    o_ref[:, SEG_WI[0] + IDX_HEADS:SEG_QC[0]] = jnp.zeros((rows, LANES - IDX_HEADS), BF16)
    o_ref[:, SEG_QC[0]:] = x_ref[:, o_qc:o_qc + D_IN_ALIGNED - SEG_QC[0]]


def _align_w_in(w, tr):
    depth, d, n = w.shape
    assert n - (SEG_QI[1] + IDX_DIM + IDX_HEADS) == D_IN_ALIGNED - SEG_QC[0]
    w = jnp.pad(w.astype(BF16), ((0, 0), (0, 0), (0, D_IN_ALIGNED - n)))
    return pl.pallas_call(
        _align_w_in_body,
        grid=(depth, d // tr),
        in_specs=[pl.BlockSpec((None, tr, D_IN_ALIGNED), lambda l, r: (l, r, 0))],
        out_specs=pl.BlockSpec((None, tr, D_IN_ALIGNED), lambda l, r: (l, r, 0)),
        out_shape=jax.ShapeDtypeStruct((depth, d, D_IN_ALIGNED), BF16),
        compiler_params=_params("parallel", "parallel"),
        name="align_w_in",
    )(w)


def _tile(n, want):
    while n % want:
        want //= 2
    return want


class _Tiles(NamedTuple):
    tm: int
    tn_ffn: int
    tn_out: int
    tm_proj: int
    tn_down_f32: int
    tq: int
    tr_align: int


def _tiles(t, s, d, f):
    return _Tiles(tm=_tile(t, 1024), tn_ffn=_tile(f, 512), tn_out=_tile(d, 1024), tm_proj=_tile(s, 512),
                  tn_down_f32=_tile(d, 256), tq=_tile(s, 256), tr_align=_tile(d, 256))


def kernel(x, positions, norm_ffn1, ffn1_gate, ffn1_up, ffn1_down, norm_mix, w_in, conv_w, w_out,
           norm_ffn2, ffn2_gate, ffn2_up, ffn2_down, norm_final):
    b, s, d = x.shape
    t = b * s
    depth = w_in.shape[0]
    tl = _tiles(t, s, d, ffn1_gate.shape[2])

    def ffn(xf, g, weights, layer):
        wg, wu, wd = weights
        h = _ffn_up(xf, g.reshape(depth, 1, d), wg, wu, layer, tl.tm, tl.tn_ffn)
        return _ffn_down(h, wd, xf, tl.tm, tl.tn_down_f32 if isinstance(wd, tuple) else tl.tn_ffn)

    ffn1_stacks, ffn2_stacks = (ffn1_gate, ffn1_up, ffn1_down), (ffn2_gate, ffn2_up, ffn2_down)
    w1 = [(w, 0) for w in ffn1_stacks]
    pos = positions.astype(F32).reshape(t, 1)
    tabs = _rope_tables(pos, tl.tm)
    w_in_al = _align_w_in(w_in, tl.tr_align)
    xf = x.reshape(t, d)
    for i in range(depth):
        xf = ffn(xf, norm_ffn1, w1, i)
        ya, qb, kb, vb, qi, ki, wi, qc, kc, vc = _in_proj(
            xf, norm_mix.reshape(depth, 1, d), w_in_al, conv_w, tabs, i, s, tl.tm_proj)
        r3 = lambda a: a.reshape(b, s, a.shape[-1])
        to_cast = ([(w_out, i)] + [(w, i) for w in ffn2_stacks]
                   + ([(w, i + 1) for w in ffn1_stacks] if i + 1 < depth else []))
        yb, copies = _dsa(qb, r3(kb), r3(vb), qi, r3(ki), wi, tl.tq, to_cast)
        w_out_bf, w2, w1 = copies[0], copies[1:4], copies[4:]
        ycs = _dilated(r3(qc), r3(kc), r3(vc))
        parts = [ya, yb.reshape(t, -1)] + [yc.reshape(t, -1) for yc in ycs]
        xf = _out_proj(parts, w_out_bf, xf, tl.tm)
        xf = ffn(xf, norm_ffn2, w2, i)
    return _final_norm(xf, norm_final.reshape(1, d), tl.tm).reshape(b, s, d)
```
